```python
import math
import jax, jax.numpy as jnp
from jax import lax
import numpy as np

D_MODEL = 1024
BATCH = 16
SEQ = 256
DEPTH = 4
DEC_BATCH = 8
DEC_SEQ = 2048
PAST_LEN = 256

GRID_W = 64
HEAD_DIM = 64
NA_HEADS = D_MODEL // 128
NA_WIDTH = NA_HEADS * HEAD_DIM
DIFF_HEADS = D_MODEL // 256
DIFF_WIDTH = DIFF_HEADS * 2 * HEAD_DIM
MIX_WIDTH = NA_WIDTH + DIFF_WIDTH
IN_WIDTH = 3 * NA_WIDTH + 3 * DIFF_WIDTH
NA_WIN_H = 8
NA_WIN_W = 16
ROPE_THETA = 10000.0
D_FF = 2816
N_EXPERTS = 8
TOP_K = 2
D_FF_EXPERT = 3584
N_DENSE = (DEPTH + 1) // 2
N_MOE = DEPTH // 2
EPS = 1e-6
SUBLN_EPS = 1e-5
QBLOCK = 128
ATTN_SCALE = HEAD_DIM ** -0.5

kernel_name = "hybrid_na_diffattn_prefix_dit_step"


def rmsnorm(x, g, eps=EPS):
    xf = x.astype(jnp.float32)
    xf = xf * lax.rsqrt(jnp.mean(xf * xf, axis=-1, keepdims=True) + eps)
    return xf.astype(x.dtype) * g


def modulation(cvec, w, b):
    m = jax.nn.silu(cvec) @ w + b
    return jnp.split(m[:, None, :], 6, axis=-1)


def modulate(x, g, shift, scale):
    return rmsnorm(x, g) * (1.0 + scale) + shift


def split_proj(h, w):
    p = h @ w
    B, L = h.shape[:2]
    cuts = [NA_WIDTH, 2 * NA_WIDTH, 3 * NA_WIDTH, 3 * NA_WIDTH + DIFF_WIDTH, 3 * NA_WIDTH + 2 * DIFF_WIDTH]
    na_q, na_k, na_v, d_q, d_k, d_v = jnp.split(p, cuts, axis=-1)
    na = lambda t: t.reshape(B, L, NA_HEADS, HEAD_DIM)
    dqk = lambda t: t.reshape(B, L, DIFF_HEADS, 2, HEAD_DIM)
    return (na(na_q), na(na_k), na(na_v), dqk(d_q), dqk(d_k),
            d_v.reshape(B, L, DIFF_HEADS, 2 * HEAD_DIM))


def axial_rope_tables(n, dtype):
    t = jnp.arange(n)
    row = (t // GRID_W).astype(jnp.float32)
    col = (t % GRID_W).astype(jnp.float32)
    half = HEAD_DIM // 2
    inv = 1.0 / (ROPE_THETA ** (jnp.arange(0, half, 2, dtype=jnp.float32) / half))
    ar = row[:, None] * inv[None]
    ac = col[:, None] * inv[None]
    ang = jnp.concatenate([ar, ar, ac, ac], axis=-1)
    return jnp.cos(ang).astype(dtype), jnp.sin(ang).astype(dtype)


def rope_rotate(x):
    xr = x.reshape(x.shape[:-1] + (2, 2, HEAD_DIM // 4))
    return jnp.concatenate([-xr[..., 1:, :], xr[..., :1, :]], axis=-2).reshape(x.shape)


def apply_rope(x, cos, sin):
    c = cos[None, :, None, None, :]
    s = sin[None, :, None, None, :]
    return x * c + rope_rotate(x) * s


def diff_lambda(lq1, lk1, lq2, lk2, lam_init):
    return (jnp.exp(jnp.sum((lq1 * lk1).astype(jnp.float32)))
            - jnp.exp(jnp.sum((lq2 * lk2).astype(jnp.float32))) + lam_init)


def diff_finish(o, g, lam_init):
    B, L = o.shape[:2]
    return (rmsnorm(o, g, SUBLN_EPS) * (1.0 - lam_init)).reshape(B, L, DIFF_WIDTH)


def context_attn(q, k, v):
    s = jnp.einsum('blhd,bmhd->bhlm', q, k).astype(jnp.float32) * ATTN_SCALE
    p = jax.nn.softmax(s, axis=-1).astype(v.dtype)
    return jnp.einsum('bhlm,bmhd->blhd', p, v)


def context_diff_attn(q, k, v, lam):
    s = jnp.einsum('blhjd,bmhjd->bhjlm', q, k).astype(jnp.float32) * ATTN_SCALE
    p = jax.nn.softmax(s, axis=-1)
    a = (p[:, :, 0] - lam * p[:, :, 1]).astype(v.dtype)
    return jnp.einsum('bhlm,bmhe->blhe', a, v)


def latent_na_attn(q, k, v, kc, vc, rpb):
    B, N, H, dh = q.shape
    rows = N // GRID_W
    wh = min(NA_WIN_H, rows)
    r = jnp.arange(rows)
    rs = jnp.clip(r - wh // 2, 0, rows - wh)
    row_idx = rs[:, None] + jnp.arange(wh)[None, :]
    cc = jnp.arange(GRID_W)
    cs = jnp.clip(cc - NA_WIN_W // 2, 0, GRID_W - NA_WIN_W)
    col_mask = (cc[None, :] >= cs[:, None]) & (cc[None, :] < cs[:, None] + NA_WIN_W)
    qg = q.reshape(B, rows, GRID_W, H, dh)
    k_rows = k.reshape(B, rows, GRID_W, H, dh)[:, row_idx]
    v_rows = v.reshape(B, rows, GRID_W, H, dh)[:, row_idx]
    s_win = jnp.einsum('brqhd,brikhd->bhrqik', qg, k_rows).astype(jnp.float32) * ATTN_SCALE
    dr_idx = row_idx - r[:, None] + (NA_WIN_H - 1)
    dc_idx = jnp.clip(cc[None, :] - cc[:, None] + NA_WIN_W - 1, 0, 2 * NA_WIN_W - 2)
    bias = rpb[:, dr_idx[:, None, :, None], dc_idx[None, :, None, :]]
    s_win = s_win + bias[None].astype(jnp.float32)
    s_win = jnp.where(col_mask[None, None, None, :, None, :], s_win, -jnp.inf)
    s_win = s_win.reshape(B, H, rows, GRID_W, wh * GRID_W)
    s_ctx = jnp.einsum('brqhd,bchd->bhrqc', qg, kc).astype(jnp.float32) * ATTN_SCALE
    p = jax.nn.softmax(jnp.concatenate([s_win, s_ctx], axis=-1), axis=-1).astype(v.dtype)
    p_win = p[..., :wh * GRID_W].reshape(B, H, rows, GRID_W, wh, GRID_W)
    p_ctx = p[..., wh * GRID_W:]
    out = (jnp.einsum('bhrqik,brikhd->brqhd', p_win, v_rows)
           + jnp.einsum('bhrqc,bchd->brqhd', p_ctx, vc))
    return out.reshape(B, N, H * dh)


def latent_diff_attn(q, k, v, kc, vc, lam):
    B, N = q.shape[:2]
    nb = N // QBLOCK
    qb = q.reshape((B, nb, QBLOCK) + q.shape[2:]).transpose(1, 0, 2, 3, 4, 5)

    def block(qi):
        s_lat = jnp.einsum('bqhjd,bnhjd->bhjqn', qi, k)
        s_ctx = jnp.einsum('bqhjd,bchjd->bhjqc', qi, kc)
        s = jnp.concatenate([s_lat, s_ctx], axis=-1).astype(jnp.float32) * ATTN_SCALE
        p = jax.nn.softmax(s, axis=-1)
        a = (p[:, :, 0] - lam * p[:, :, 1]).astype(v.dtype)
        return (jnp.einsum('bhqn,bnhe->bqhe', a[..., :N], v)
                + jnp.einsum('bhqc,bche->bqhe', a[..., N:], vc))

    out = lax.map(block, qb)
    return out.transpose(1, 0, 2, 3, 4).reshape(B, N, DIFF_HEADS, 2 * HEAD_DIM)


def swiglu(h, wg, wu, wd):
    return (jax.nn.silu(h @ wg) * (h @ wu)) @ wd


def moe_swiglu(h, w_router, wg, wu, wd):
    logits = (h @ w_router).astype(jnp.float32)
    top_vals, top_idx = lax.top_k(logits, TOP_K)
    gates = jax.nn.softmax(top_vals, axis=-1).astype(h.dtype)
    combine = jnp.sum(jax.nn.one_hot(top_idx, N_EXPERTS, dtype=h.dtype) * gates[..., None], axis=-2)
    out = jnp.zeros_like(h)
    for e in range(N_EXPERTS):
        out = out + combine[..., e:e + 1] * swiglu(h, wg[e], wu[e], wd[e])
    return out


def channel_mixer(h, l, w_ffn_gate, w_ffn_up, w_ffn_down, w_router, w_moe_gate, w_moe_up, w_moe_down):
    if l % 2 == 0:
        i = l // 2
        return swiglu(h, w_ffn_gate[i], w_ffn_up[i], w_ffn_down[i])
    i = l // 2
    return moe_swiglu(h, w_router[i], w_moe_gate[i], w_moe_up[i], w_moe_down[i])


def setup_inputs(seed: int = 0) -> dict:
    key = jax.random.key(seed)
    ks = jax.random.split(key, 32)
    f32 = jnp.float32
    nrm = lambda k, shape, s: jax.random.normal(k, shape, f32) * s
    D = D_MODEL
    return {
        "x_prompt": nrm(ks[0], (BATCH, SEQ, D), 1.0),
        "x_sample": nrm(ks[1], (DEC_BATCH, DEC_SEQ, D), 1.0),
        "c": nrm(ks[2], (DEC_BATCH, D), 1.0),
        "cache_na_k": nrm(ks[3], (DEC_BATCH, DEPTH, PAST_LEN, NA_HEADS, HEAD_DIM), 1.0),
        "cache_na_v": nrm(ks[4], (DEC_BATCH, DEPTH, PAST_LEN, NA_HEADS, HEAD_DIM), 1.0),
        "cache_diff_k": nrm(ks[5], (DEC_BATCH, DEPTH, PAST_LEN, DIFF_HEADS, 2, HEAD_DIM), 1.0),
        "cache_diff_v": nrm(ks[6], (DEC_BATCH, DEPTH, PAST_LEN, DIFF_HEADS, 2 * HEAD_DIM), 1.0),
        "c_ctx": nrm(ks[7], (D,), 1.0),
        "w_ada": nrm(ks[8], (DEPTH, D, 6 * D), 0.5 * D ** -0.5),
        "b_ada": nrm(ks[9], (DEPTH, 6 * D), 0.02),
        "g_mix": 1.0 + nrm(ks[10], (DEPTH, D), 0.02),
        "w_in": nrm(ks[11], (DEPTH, D, IN_WIDTH), D ** -0.5),
        "rpb": nrm(ks[12], (DEPTH, NA_HEADS, 2 * NA_WIN_H - 1, 2 * NA_WIN_W - 1), 0.1),
        "lam_q1": nrm(ks[13], (DEPTH, HEAD_DIM), 0.1),
        "lam_k1": nrm(ks[14], (DEPTH, HEAD_DIM), 0.1),
        "lam_q2": nrm(ks[15], (DEPTH, HEAD_DIM), 0.1),
        "lam_k2": nrm(ks[16], (DEPTH, HEAD_DIM), 0.1),
        "g_subln": 1.0 + nrm(ks[17], (DEPTH, 2 * HEAD_DIM), 0.02),
        "w_out": nrm(ks[18], (DEPTH, MIX_WIDTH, D), MIX_WIDTH ** -0.5),
        "g_ffn": 1.0 + nrm(ks[19], (DEPTH, D), 0.02),
        "w_ffn_gate": nrm(ks[20], (N_DENSE, D, D_FF), D ** -0.5),
        "w_ffn_up": nrm(ks[21], (N_DENSE, D, D_FF), D ** -0.5),
        "w_ffn_down": nrm(ks[22], (N_DENSE, D_FF, D), D_FF ** -0.5),
        "w_router": nrm(ks[23], (N_MOE, D, N_EXPERTS), D ** -0.5),
        "w_moe_gate": nrm(ks[24], (N_MOE, N_EXPERTS, D, D_FF_EXPERT), D ** -0.5),
        "w_moe_up": nrm(ks[25], (N_MOE, N_EXPERTS, D, D_FF_EXPERT), D ** -0.5),
        "w_moe_down": nrm(ks[26], (N_MOE, N_EXPERTS, D_FF_EXPERT, D), D_FF_EXPERT ** -0.5),
        "g_final": 1.0 + nrm(ks[27], (D,), 0.02),
    }


def reference(x_prompt, x_sample, c, cache_na_k, cache_na_v, cache_diff_k, cache_diff_v,
              c_ctx, w_ada, b_ada, g_mix, w_in, rpb, lam_q1, lam_k1, lam_q2, lam_k2,
              g_subln, w_out, g_ffn, w_ffn_gate, w_ffn_up, w_ffn_down, w_router,
              w_moe_gate, w_moe_up, w_moe_down, g_final):
    ffn_args = (w_ffn_gate, w_ffn_up, w_ffn_down, w_router, w_moe_gate, w_moe_up, w_moe_down)
    xp = x_prompt
    xs = x_sample
    n_lat = x_sample.shape[1]
    cos, sin = axial_rope_tables(n_lat, x_sample.dtype)
    c_ctx_row = c_ctx[None, :]
    na_k_list, na_v_list, d_k_list, d_v_list = [], [], [], []
    for l in range(DEPTH):
        lam_init = 0.8 - 0.6 * math.exp(-0.3 * l)
        lam = diff_lambda(lam_q1[l], lam_k1[l], lam_q2[l], lam_k2[l], lam_init)

        sh1, sc1, ga1, sh2, sc2, ga2 = modulation(c_ctx_row, w_ada[l], b_ada[l])
        h = modulate(xp, g_mix[l], sh1, sc1)
        nq, nk, nv, dq, dk, dv = split_proj(h, w_in[l])
        o_na = context_attn(nq, nk, nv).reshape(xp.shape[0], xp.shape[1], NA_WIDTH)
        o_d = diff_finish(context_diff_attn(dq, dk, dv, lam), g_subln[l], lam_init)
        xp = xp + ga1 * (jnp.concatenate([o_na, o_d], axis=-1) @ w_out[l])
        h = modulate(xp, g_ffn[l], sh2, sc2)
        xp = xp + ga2 * channel_mixer(h, l, *ffn_args)
        na_k_list.append(nk)
        na_v_list.append(nv)
        d_k_list.append(dk)
        d_v_list.append(dv)

        sh1, sc1, ga1, sh2, sc2, ga2 = modulation(c, w_ada[l], b_ada[l])
        h = modulate(xs, g_mix[l], sh1, sc1)
        nq, nk, nv, dq, dk, dv = split_proj(h, w_in[l])
        o_na = latent_na_attn(nq, nk, nv, cache_na_k[:, l], cache_na_v[:, l], rpb[l])
        dq = apply_rope(dq, cos, sin)
        dk = apply_rope(dk, cos, sin)
        o_d = latent_diff_attn(dq, dk, dv, cache_diff_k[:, l], cache_diff_v[:, l], lam)
        o_d = diff_finish(o_d, g_subln[l], lam_init)
        xs = xs + ga1 * (jnp.concatenate([o_na, o_d], axis=-1) @ w_out[l])
        h = modulate(xs, g_ffn[l], sh2, sc2)
        xs = xs + ga2 * channel_mixer(h, l, *ffn_args)

    y_prompt = rmsnorm(xp, g_final)
    y_sample = rmsnorm(xs, g_final)
    new_na_k = jnp.stack(na_k_list, axis=1)
    new_na_v = jnp.stack(na_v_list, axis=1)
    new_diff_k = jnp.stack(d_k_list, axis=1)
    new_diff_v = jnp.stack(d_v_list, axis=1)
    return (y_prompt, y_sample, new_na_k, new_na_v, new_diff_k, new_diff_v)
```

```python
import functools
import math

import jax
import jax.numpy as jnp
from jax import lax
from jax.experimental import pallas as pl
from jax.experimental.pallas import tpu as pltpu

F32 = jnp.float32
BF16 = jnp.bfloat16

D_MODEL = 1024
DEPTH = 4
BATCH = 16
SEQ = 256
DEC_BATCH = 8
DEC_SEQ = 2048
PAST_LEN = 256
GRID_W = 64
GRID_ROWS = DEC_SEQ // GRID_W
HEAD_DIM = 64
NA_HEADS = 8
NA_WIDTH = 512
DIFF_HEADS = 4
DIFF_WIDTH = 512
IN_WIDTH = 3072
NA_WIN_H = 8
NA_WIN_W = 16
ROPE_THETA = 10000.0
D_FF = 2816
N_EXPERTS = 8
D_FF_EXPERT = 3584
EPS = 1e-6
SUBLN_EPS = 1e-5
ATTN_SCALE = HEAD_DIM ** -0.5

LANES = 128
N_CTX_TOK = BATCH * SEQ
N_LAT_TOK = DEC_BATCH * DEC_SEQ
MOD_ROWS = 16
MASK_VALUE = -1e30

COL_NA_Q, COL_NA_K, COL_NA_V = 0, 512, 1024
COL_D_Q, COL_D_K, COL_D_V = 1536, 2048, 2560

TM_PROJ = 512
FF_TILE_DENSE = 1408
FF_TILE_MOE = 512
TQ_DIFF = 256
NA_ROWS_PER_STEP = 4
VMEM_LIMIT = 56 * 1024 * 1024


def _cparams(*sem):
    return pltpu.CompilerParams(dimension_semantics=sem, vmem_limit_bytes=VMEM_LIMIT)


def _dot(a, b):
    return jnp.dot(a, b, preferred_element_type=F32)


def _dot_nt(a, b):
    return lax.dot_general(a, b, (((1,), (1,)), ((), ())), preferred_element_type=F32)


def _modulated_norm(x, g, shift, scale):
    xn = x * lax.rsqrt(jnp.mean(x * x, axis=-1, keepdims=True) + EPS)
    return (xn * g) * (1.0 + scale) + shift


def _mod_spec(chunk, row_of_tile):
    return pl.BlockSpec((1, 1, D_MODEL), lambda i, *_: (row_of_tile(i) * 6 + chunk, 0, 0))


def _mod_kernel(c_ref, w_ref, b_ref, o_ref):
    cv = c_ref[...]
    s = cv * jax.nn.sigmoid(cv)
    o_ref[...] = jnp.dot(s, w_ref[...], preferred_element_type=F32,
                         precision=lax.Precision.HIGHEST) + b_ref[...]


def _modulation(cvec, w_ada, b_ada):
    tn = 1536
    n = 6 * D_MODEL
    return pl.pallas_call(
        _mod_kernel,
        grid=(DEPTH, n // tn),
        in_specs=[
            pl.BlockSpec((MOD_ROWS, D_MODEL), lambda l, j: (0, 0)),
            pl.BlockSpec((None, D_MODEL, tn), lambda l, j: (l, 0, j)),
            pl.BlockSpec((None, 1, tn), lambda l, j: (l, 0, j)),
        ],
        out_specs=pl.BlockSpec((None, MOD_ROWS, tn), lambda l, j: (l, 0, j)),
        out_shape=jax.ShapeDtypeStruct((DEPTH, MOD_ROWS, n), F32),
        compiler_params=_cparams("parallel", "parallel"),
        name="adaln_modulation",
    )(cvec, w_ada, b_ada.reshape(DEPTH, 1, n))


def _inproj_kernel(*refs, rope, emit_kv):
    x_ref, g_ref, sh_ref, sc_ref, w_ref = refs[:5]
    refs = refs[5:]
    if rope:
        cos_ref, sina_ref, sinb_ref = refs[:3]
        refs = refs[3:]
    o_ref = refs[0]
    kv_ref = refs[1] if emit_kv else None

    h = _modulated_norm(x_ref[...], g_ref[...], sh_ref[0], sc_ref[0]).astype(BF16)
    chunk = 512
    for c in range(IN_WIDTH // chunk):
        col = c * chunk
        acc = _dot(h, w_ref[:, col:col + chunk])
        if rope and COL_D_Q <= col < COL_D_V:
            parts = []
            for j in range(chunk // LANES):
                blk = acc[:, j * LANES:(j + 1) * LANES]
                parts.append(blk * cos_ref[...]
                             + pltpu.roll(blk, LANES - 16, 1) * sina_ref[...]
                             + pltpu.roll(blk, 16, 1) * sinb_ref[...])
            acc = jnp.concatenate(parts, axis=1)
        o_ref[:, col:col + chunk] = acc.astype(o_ref.dtype)
        if emit_kv:
            for k_i, src in enumerate((COL_NA_K, COL_NA_V, COL_D_K, COL_D_V)):
                if src == col:
                    kv_ref[:, k_i * chunk:(k_i + 1) * chunk] = acc


def _inproj(x, g, mod, w_bf16, row_of_tile, rope_tabs=None, emit_kv=False):
    n_tok = x.shape[0]
    tm = TM_PROJ
    in_specs = [
        pl.BlockSpec((tm, D_MODEL), lambda i: (i, 0)),
        pl.BlockSpec((1, D_MODEL), lambda i: (0, 0)),
        _mod_spec(0, row_of_tile),
        _mod_spec(1, row_of_tile),
        pl.BlockSpec((D_MODEL, IN_WIDTH), lambda i: (0, 0)),
    ]
    args = [x, g, mod, mod, w_bf16]
    if rope_tabs is not None:
        per_seq = DEC_SEQ // tm
        for t in rope_tabs:
            in_specs.append(pl.BlockSpec((tm, LANES), lambda i: (i % per_seq, 0)))
            args.append(t)
    out_specs = [pl.BlockSpec((tm, IN_WIDTH), lambda i: (i, 0))]
    out_shape = [jax.ShapeDtypeStruct((n_tok, IN_WIDTH), BF16)]
    if emit_kv:
        out_specs.append(pl.BlockSpec((tm, 2048), lambda i: (i, 0)))
        out_shape.append(jax.ShapeDtypeStruct((n_tok, 2048), F32))
    return pl.pallas_call(
        functools.partial(_inproj_kernel, rope=rope_tabs is not None, emit_kv=emit_kv),
        grid=(n_tok // tm,),
        in_specs=in_specs,
        out_specs=out_specs,
        out_shape=out_shape,
        compiler_params=_cparams("parallel"),
        name="inproj_rope" if rope_tabs is not None else "inproj_ctx",
    )(*args)


def _lane_half_mask(shape, half):
    lane = lax.broadcasted_iota(jnp.int32, shape, len(shape) - 1)
    return (lane < HEAD_DIM) if half == 0 else (lane >= HEAD_DIM)


def _lambda_value(lq1, lk1, lq2, lk2, lam_init):
    a = jnp.sum(lq1 * lk1, axis=-1, keepdims=True)
    b = jnp.sum(lq2 * lk2, axis=-1, keepdims=True)
    return jnp.exp(a) - jnp.exp(b) + lam_init


def _subln(o, g, lam_init):
    on = o * lax.rsqrt(jnp.mean(o * o, axis=-1, keepdims=True) + SUBLN_EPS)
    return (on * g) * (1.0 - lam_init)


def _ctx_attn_kernel(p_ref, lq1, lk1, lq2, lk2, gs_ref, o_ref, *, lam_init):
    lam = _lambda_value(lq1[...], lk1[...], lq2[...], lk2[...], lam_init)
    for hp in range(NA_HEADS // 2):
        q = p_ref[:, COL_NA_Q + hp * LANES:COL_NA_Q + (hp + 1) * LANES]
        k = p_ref[:, COL_NA_K + hp * LANES:COL_NA_K + (hp + 1) * LANES]
        v = p_ref[:, COL_NA_V + hp * LANES:COL_NA_V + (hp + 1) * LANES]
        outs = []
        for half in range(2):
            qm = jnp.where(_lane_half_mask(q.shape, half), q, jnp.zeros_like(q))
            s = _dot_nt(qm, k)
            m = jnp.max(s, axis=-1, keepdims=True)
            e = jnp.exp(s - m)
            p = e / jnp.sum(e, axis=-1, keepdims=True)
            outs.append(_dot(p.astype(BF16), v))
        o = jnp.where(_lane_half_mask(outs[0].shape, 0), outs[0], outs[1])
        o_ref[:, hp * LANES:(hp + 1) * LANES] = o.astype(o_ref.dtype)
    for h in range(DIFF_HEADS):
        q = p_ref[:, COL_D_Q + h * LANES:COL_D_Q + (h + 1) * LANES]
        k = p_ref[:, COL_D_K + h * LANES:COL_D_K + (h + 1) * LANES]
        v = p_ref[:, COL_D_V + h * LANES:COL_D_V + (h + 1) * LANES]
        ps = []
        for half in range(2):
            qm = jnp.where(_lane_half_mask(q.shape, half), q, jnp.zeros_like(q))
            s = _dot_nt(qm, k)
            m = jnp.max(s, axis=-1, keepdims=True)
            e = jnp.exp(s - m)
            ps.append(e / jnp.sum(e, axis=-1, keepdims=True))
        a = (ps[0] - lam * ps[1]).astype(BF16)
        o = _subln(_dot(a, v), gs_ref[...], lam_init)
        o_ref[:, NA_WIDTH + h * LANES:NA_WIDTH + (h + 1) * LANES] = o.astype(o_ref.dtype)


def _ctx_attention(p_ctx, lam_params, g_subln, lam_init):
    vec = pl.BlockSpec((1, HEAD_DIM), lambda b: (0, 0))
    return pl.pallas_call(
        functools.partial(_ctx_attn_kernel, lam_init=lam_init),
        grid=(BATCH,),
        in_specs=[pl.BlockSpec((SEQ, IN_WIDTH), lambda b: (b, 0)), vec, vec, vec, vec,
                  pl.BlockSpec((1, 2 * HEAD_DIM), lambda b: (0, 0))],
        out_specs=pl.BlockSpec((SEQ, D_MODEL), lambda b: (b, 0)),
        out_shape=jax.ShapeDtypeStruct((N_CTX_TOK, D_MODEL), BF16),
        compiler_params=_cparams("parallel"),
        name="ctx_attention",
    )(p_ctx, *lam_params, g_subln)


def _na_attn_kernel(q_ref, k_ref, v_ref, kc_ref, vc_ref, bias_ref, o_ref):
    jblk = pl.program_id(1)
    for jr in range(NA_ROWS_PER_STEP):
        r = jblk * NA_ROWS_PER_STEP + jr
        rs = jnp.clip(r - NA_WIN_H // 2, 0, GRID_ROWS - NA_WIN_H)
        ds = rs - r + (NA_WIN_H - 1)
        k0 = pl.multiple_of(rs * GRID_W, GRID_W)
        for hp in range(NA_HEADS // 2):
            lanes = slice(hp * LANES, (hp + 1) * LANES)
            q = q_ref[jr * GRID_W:(jr + 1) * GRID_W, lanes]
            kw = k_ref[pl.ds(k0, NA_WIN_H * GRID_W), lanes]
            vw = v_ref[pl.ds(k0, NA_WIN_H * GRID_W), lanes]
            kc = kc_ref[:, lanes].astype(BF16)
            vc = vc_ref[:, lanes].astype(BF16)
            outs = []
            for half in range(2):
                qm = jnp.where(_lane_half_mask(q.shape, half), q, jnp.zeros_like(q))
                s_w = _dot_nt(qm, kw) + bias_ref[2 * hp + half, ds]
                s_c = _dot_nt(qm, kc)
                m = jnp.maximum(jnp.max(s_w, axis=-1, keepdims=True),
                                jnp.max(s_c, axis=-1, keepdims=True))
                e_w = jnp.exp(s_w - m)
                e_c = jnp.exp(s_c - m)
                inv = 1.0 / (jnp.sum(e_w, axis=-1, keepdims=True)
                             + jnp.sum(e_c, axis=-1, keepdims=True))
                outs.append(_dot((e_w * inv).astype(BF16), vw)
                            + _dot((e_c * inv).astype(BF16), vc))
            o = jnp.where(_lane_half_mask(outs[0].shape, 0), outs[0], outs[1])
            o_ref[jr * GRID_W:(jr + 1) * GRID_W, lanes] = o.astype(o_ref.dtype)


def _na_bias_table(rpb_l):
    cc = jnp.arange(GRID_W)
    cs = jnp.clip(cc - NA_WIN_W // 2, 0, GRID_W - NA_WIN_W)
    col_mask = (cc[None, :] >= cs[:, None]) & (cc[None, :] < cs[:, None] + NA_WIN_W)
    dc_idx = jnp.clip(cc[None, :] - cc[:, None] + NA_WIN_W - 1, 0, 2 * NA_WIN_W - 2)
    dr_idx = jnp.arange(NA_WIN_H)[:, None] + jnp.arange(NA_WIN_H)[None, :]
    t = rpb_l[:, dr_idx[:, None, :, None], dc_idx[None, :, None, :]]
    t = jnp.where(col_mask[None, None, :, None, :], t, MASK_VALUE)
    return t.reshape(NA_HEADS, NA_WIN_H, GRID_W, NA_WIN_H * GRID_W).astype(F32)


def _na_attention(p_lat, cache_k, cache_v, bias_tab, layer):
    q_rows = NA_ROWS_PER_STEP * GRID_W
    steps = DEC_SEQ // q_rows
    cache_spec = pl.BlockSpec((None, None, PAST_LEN, NA_WIDTH), lambda b, j: (b, layer, 0, 0))
    return pl.pallas_call(
        _na_attn_kernel,
        grid=(DEC_BATCH, steps),
        in_specs=[
            pl.BlockSpec((q_rows, NA_WIDTH), lambda b, j: (b * steps + j, COL_NA_Q // NA_WIDTH)),
            pl.BlockSpec((DEC_SEQ, NA_WIDTH), lambda b, j: (b, COL_NA_K // NA_WIDTH)),
            pl.BlockSpec((DEC_SEQ, NA_WIDTH), lambda b, j: (b, COL_NA_V // NA_WIDTH)),
            cache_spec, cache_spec,
            pl.BlockSpec((NA_HEADS, NA_WIN_H, GRID_W, NA_WIN_H * GRID_W), lambda b, j: (0, 0, 0, 0)),
        ],
        out_specs=pl.BlockSpec((q_rows, NA_WIDTH), lambda b, j: (b * steps + j, 0)),
        out_shape=jax.ShapeDtypeStruct((N_LAT_TOK, NA_WIDTH), BF16),
        compiler_params=_cparams("parallel", "arbitrary"),
        name="latent_na_attention",
    )(p_lat, p_lat, p_lat, cache_k, cache_v, bias_tab)


def _diff_attn_kernel(q_ref, k_ref, v_ref, kc_ref, vc_ref, lq1, lk1, lq2, lk2, gs_ref, o_ref,
                      *, lam_init):
    lam = _lambda_value(lq1[...], lk1[...], lq2[...], lk2[...], lam_init)
    q = q_ref[...]
    k = k_ref[...]
    kc = kc_ref[...].astype(BF16)
    probs = []
    for half in range(2):
        qm = jnp.where(_lane_half_mask(q.shape, half), q, jnp.zeros_like(q))
        s_l = _dot_nt(qm, k)
        s_c = _dot_nt(qm, kc)
        m = jnp.maximum(jnp.max(s_l, axis=-1, keepdims=True), jnp.max(s_c, axis=-1, keepdims=True))
        e_l = jnp.exp(s_l - m)
        e_c = jnp.exp(s_c - m)
        inv = 1.0 / (jnp.sum(e_l, axis=-1, keepdims=True) + jnp.sum(e_c, axis=-1, keepdims=True))
        probs.append((e_l, e_c, inv))
    w1 = probs[0][2]
    w2 = lam * probs[1][2]
    a_l = (probs[0][0] * w1 - probs[1][0] * w2).astype(BF16)
    a_c = (probs[0][1] * w1 - probs[1][1] * w2).astype(BF16)
    o = _dot(a_l, v_ref[...]) + _dot(a_c, vc_ref[...].astype(BF16))
    o_ref[...] = _subln(o, gs_ref[...], lam_init).astype(o_ref.dtype)


def _diff_attention(p_lat, cache_k, cache_v, lam_params, g_subln, lam_init, layer):
    steps = DEC_SEQ // TQ_DIFF
    vec = pl.BlockSpec((1, HEAD_DIM), lambda b, h, j: (0, 0))
    cache_spec = pl.BlockSpec((None, None, PAST_LEN, LANES), lambda b, h, j: (b, layer, 0, h))
    return pl.pallas_call(
        functools.partial(_diff_attn_kernel, lam_init=lam_init),
        grid=(DEC_BATCH, DIFF_HEADS, steps),
        in_specs=[
            pl.BlockSpec((TQ_DIFF, LANES), lambda b, h, j: (b * steps + j, COL_D_Q // LANES + h)),
            pl.BlockSpec((DEC_SEQ, LANES), lambda b, h, j: (b, COL_D_K // LANES + h)),
            pl.BlockSpec((DEC_SEQ, LANES), lambda b, h, j: (b, COL_D_V // LANES + h)),
            cache_spec, cache_spec, vec, vec, vec, vec,
            pl.BlockSpec((1, 2 * HEAD_DIM), lambda b, h, j: (0, 0)),
        ],
        out_specs=pl.BlockSpec((TQ_DIFF, LANES), lambda b, h, j: (b * steps + j, h)),
        out_shape=jax.ShapeDtypeStruct((N_LAT_TOK, DIFF_WIDTH), BF16),
        compiler_params=_cparams("parallel", "parallel", "arbitrary"),
        name="latent_diff_attention",
    )(p_lat, p_lat, p_lat, cache_k, cache_v, *lam_params, g_subln)


def _outproj_kernel(*refs, n_parts):
    x_ref = refs[0]
    o_parts = refs[1:1 + n_parts]
    ga_ref, w_ref, y_ref = refs[1 + n_parts:]
    width = D_MODEL // n_parts
    acc = None
    for i, part in enumerate(o_parts):
        t = _dot(part[...], w_ref[i * width:(i + 1) * width, :])
        acc = t if acc is None else acc + t
    y_ref[...] = x_ref[...] + ga_ref[0] * acc


def _outproj(x, o_parts, mod, w_bf16, row_of_tile):
    n_tok = x.shape[0]
    tm = TM_PROJ
    n_parts = len(o_parts)
    width = D_MODEL // n_parts
    return pl.pallas_call(
        functools.partial(_outproj_kernel, n_parts=n_parts),
        grid=(n_tok // tm,),
        in_specs=[pl.BlockSpec((tm, D_MODEL), lambda i: (i, 0))]
        + [pl.BlockSpec((tm, width), lambda i: (i, 0)) for _ in o_parts]
        + [_mod_spec(2, row_of_tile), pl.BlockSpec((D_MODEL, D_MODEL), lambda i: (0, 0))],
        out_specs=pl.BlockSpec((tm, D_MODEL), lambda i: (i, 0)),
        out_shape=jax.ShapeDtypeStruct((n_tok, D_MODEL), F32),
        compiler_params=_cparams("parallel"),
        name="outproj_residual",
    )(x, *o_parts, mod, w_bf16)


def _swiglu_tile(hb, wg, wu):
    g = _dot(hb, wg)
    u = _dot(hb, wu)
    return (g * jax.nn.sigmoid(g)) * u


def _ffn_kernel(x_ref, g_ref, sh_ref, sc_ref, ga_ref, wg_ref, wu_ref, wd_ref, y_ref, h_scr, acc_scr):
    f = pl.program_id(1)

    @pl.when(f == 0)
    def _():
        h = _modulated_norm(x_ref[...], g_ref[...], sh_ref[0], sc_ref[0])
        h_scr[...] = h.astype(BF16)
        acc_scr[...] = jnp.zeros_like(acc_scr)

    a = _swiglu_tile(h_scr[...], wg_ref[...], wu_ref[...])
    acc_scr[...] += _dot(a.astype(BF16), wd_ref[...])

    @pl.when(f == pl.num_programs(1) - 1)
    def _():
        y_ref[...] = x_ref[...] + ga_ref[0] * acc_scr[...]


def _ffn_dense(x, g, mod, wg, wu, wd, row_of_tile):
    n_tok = x.shape[0]
    tm = TM_PROJ
    tf = FF_TILE_DENSE
    return pl.pallas_call(
        _ffn_kernel,
        grid=(n_tok // tm, D_FF // tf),
        in_specs=[
            pl.BlockSpec((tm, D_MODEL), lambda i, f: (i, 0)),
            pl.BlockSpec((1, D_MODEL), lambda i, f: (0, 0)),
            _mod_spec(3, row_of_tile), _mod_spec(4, row_of_tile), _mod_spec(5, row_of_tile),
            pl.BlockSpec((D_MODEL, tf), lambda i, f: (0, f)),
            pl.BlockSpec((D_MODEL, tf), lambda i, f: (0, f)),
            pl.BlockSpec((tf, D_MODEL), lambda i, f: (f, 0)),
        ],
        out_specs=pl.BlockSpec((tm, D_MODEL), lambda i, f: (i, 0)),
        out_shape=jax.ShapeDtypeStruct((n_tok, D_MODEL), F32),
        scratch_shapes=[pltpu.VMEM((tm, D_MODEL), BF16), pltpu.VMEM((tm, D_MODEL), F32)],
        compiler_params=_cparams("parallel", "arbitrary"),
        name="ffn_dense",
    )(x, g, mod, mod, mod, wg, wu, wd)


def _router_kernel(x_ref, g_ref, sh_ref, sc_ref, wr_ref, h_ref, comb_ref):
    h = _modulated_norm(x_ref[...], g_ref[...], sh_ref[0], sc_ref[0])
    h_ref[...] = h.astype(BF16)
    logits = jnp.dot(h, wr_ref[...], preferred_element_type=F32, precision=lax.Precision.HIGHEST)
    lane = lax.broadcasted_iota(jnp.int32, logits.shape, 1)
    logits = jnp.where(lane < N_EXPERTS, logits, MASK_VALUE)
    m1 = jnp.max(logits, axis=-1, keepdims=True)
    i1 = jnp.min(jnp.where(logits == m1, lane, LANES), axis=-1, keepdims=True)
    rest = jnp.where(lane == i1, MASK_VALUE, logits)
    m2 = jnp.max(rest, axis=-1, keepdims=True)
    i2 = jnp.min(jnp.where(rest == m2, lane, LANES), axis=-1, keepdims=True)
    e2 = jnp.exp(m2 - m1)
    g1 = 1.0 / (1.0 + e2)
    g2 = e2 / (1.0 + e2)
    comb_ref[...] = jnp.where(lane == i1, g1, 0.0) + jnp.where(lane == i2, g2, 0.0)


def _router(x, g, mod, w_router_pad, row_of_tile):
    n_tok = x.shape[0]
    tm = TM_PROJ
    return pl.pallas_call(
        _router_kernel,
        grid=(n_tok // tm,),
        in_specs=[
            pl.BlockSpec((tm, D_MODEL), lambda i: (i, 0)),
            pl.BlockSpec((1, D_MODEL), lambda i: (0, 0)),
            _mod_spec(3, row_of_tile), _mod_spec(4, row_of_tile),
            pl.BlockSpec((D_MODEL, LANES), lambda i: (0, 0)),
        ],
        out_specs=[pl.BlockSpec((tm, D_MODEL), lambda i: (i, 0)),
                   pl.BlockSpec((tm, LANES), lambda i: (i, 0))],
        out_shape=[jax.ShapeDtypeStruct((n_tok, D_MODEL), BF16),
                   jax.ShapeDtypeStruct((n_tok, LANES), F32)],
        compiler_params=_cparams("parallel"),
        name="moe_router",
    )(x, g, mod, mod, w_router_pad)


def _moe_kernel(x_ref, h_ref, comb_ref, ga_ref, wg_ref, wu_ref, wd_ref, y_ref, acc_scr):
    e = pl.program_id(1)
    f = pl.program_id(2)

    @pl.when((e == 0) & (f == 0))
    def _():
        acc_scr[...] = jnp.zeros_like(acc_scr)

    comb = comb_ref[...]
    lane = lax.broadcasted_iota(jnp.int32, comb.shape, 1)
    c_e = jnp.sum(jnp.where(lane == e, comb, 0.0), axis=-1, keepdims=True)
    a = _swiglu_tile(h_ref[...], wg_ref[...], wu_ref[...]) * c_e
    acc_scr[...] += _dot(a.astype(BF16), wd_ref[...])

    @pl.when((e == pl.num_programs(1) - 1) & (f == pl.num_programs(2) - 1))
    def _():
        y_ref[...] = x_ref[...] + ga_ref[0] * acc_scr[...]


def _moe_dense(x, h, comb, mod, wg, wu, wd, row_of_tile):
    n_tok = x.shape[0]
    tm = TM_PROJ
    tf = FF_TILE_MOE
    return pl.pallas_call(
        _moe_kernel,
        grid=(n_tok // tm, N_EXPERTS, D_FF_EXPERT // tf),
        in_specs=[
            pl.BlockSpec((tm, D_MODEL), lambda i, e, f: (i, 0)),
            pl.BlockSpec((tm, D_MODEL), lambda i, e, f: (i, 0)),
            pl.BlockSpec((tm, LANES), lambda i, e, f: (i, 0)),
            _mod_spec(5, row_of_tile),
            pl.BlockSpec((None, D_MODEL, tf), lambda i, e, f: (e, 0, f)),
            pl.BlockSpec((None, D_MODEL, tf), lambda i, e, f: (e, 0, f)),
            pl.BlockSpec((None, tf, D_MODEL), lambda i, e, f: (e, f, 0)),
        ],
        out_specs=pl.BlockSpec((tm, D_MODEL), lambda i, e, f: (i, 0)),
        out_shape=jax.ShapeDtypeStruct((n_tok, D_MODEL), F32),
        scratch_shapes=[pltpu.VMEM((tm, D_MODEL), F32)],
        compiler_params=_cparams("parallel", "arbitrary", "arbitrary"),
        name="moe_experts",
    )(x, h, comb, mod, wg, wu, wd)


def _final_norm_kernel(x_ref, g_ref, y_ref):
    x = x_ref[...]
    y_ref[...] = (x * lax.rsqrt(jnp.mean(x * x, axis=-1, keepdims=True) + EPS)) * g_ref[...]


def _final_norm(x, g):
    n_tok = x.shape[0]
    tm = TM_PROJ
    return pl.pallas_call(
        _final_norm_kernel,
        grid=(n_tok // tm,),
        in_specs=[pl.BlockSpec((tm, D_MODEL), lambda i: (i, 0)),
                  pl.BlockSpec((1, D_MODEL), lambda i: (0, 0))],
        out_specs=pl.BlockSpec((tm, D_MODEL), lambda i: (i, 0)),
        out_shape=jax.ShapeDtypeStruct((n_tok, D_MODEL), F32),
        compiler_params=_cparams("parallel"),
        name="final_norm",
    )(x, g)


def _rope_tables():
    t = jnp.arange(DEC_SEQ)
    row = (t // GRID_W).astype(F32)
    col = (t % GRID_W).astype(F32)
    half = HEAD_DIM // 2
    inv = 1.0 / (ROPE_THETA ** (jnp.arange(0, half, 2, dtype=F32) / half))
    ar = row[:, None] * inv[None]
    ac = col[:, None] * inv[None]
    ang = jnp.concatenate([ar, ar, ac, ac], axis=-1)
    cos = jnp.tile(jnp.cos(ang), (1, LANES // HEAD_DIM))
    sin = jnp.tile(jnp.sin(ang), (1, LANES // HEAD_DIM))
    first_half = (jnp.arange(LANES) % 32) < 16
    sin_a = jnp.where(first_half[None, :], -sin, 0.0)
    sin_b = jnp.where(first_half[None, :], 0.0, sin)
    return cos, sin_a, sin_b


def kernel(x_prompt, x_sample, c, cache_na_k, cache_na_v, cache_diff_k, cache_diff_v, c_ctx, w_ada, b_ada, g_mix, w_in, rpb, lam_q1, lam_k1, lam_q2, lam_k2, g_subln, w_out, g_ffn, w_ffn_gate, w_ffn_up, w_ffn_down, w_router, w_moe_gate, w_moe_up, w_moe_down, g_final):
    xp = x_prompt.reshape(N_CTX_TOK, D_MODEL)
    xs = x_sample.reshape(N_LAT_TOK, D_MODEL)

    cvec = jnp.zeros((MOD_ROWS, D_MODEL), F32).at[0].set(c_ctx).at[1:1 + DEC_BATCH].set(c)
    mod_all = _modulation(cvec, w_ada, b_ada).reshape(DEPTH, MOD_ROWS * 6, 1, D_MODEL)

    col = jnp.arange(IN_WIDTH)
    is_q = (col < COL_NA_K) | ((col >= COL_D_Q) & (col < COL_D_K))
    q_scale = jnp.where(is_q, ATTN_SCALE, 1.0).astype(F32)
    w_in_b = (w_in * q_scale[None, None, :]).astype(BF16)
    w_out_b = w_out.astype(BF16)
    w_fg, w_fu, w_fd = (w.astype(BF16) for w in (w_ffn_gate, w_ffn_up, w_ffn_down))
    w_mg, w_mu, w_md = (w.astype(BF16) for w in (w_moe_gate, w_moe_up, w_moe_down))
    w_router_pad = jnp.pad(w_router, ((0, 0), (0, 0), (0, LANES - N_EXPERTS)))

    rope_tabs = _rope_tables()
    cna_k = cache_na_k.reshape(DEC_BATCH, DEPTH, PAST_LEN, NA_WIDTH)
    cna_v = cache_na_v.reshape(DEC_BATCH, DEPTH, PAST_LEN, NA_WIDTH)
    cd_k = cache_diff_k.reshape(DEC_BATCH, DEPTH, PAST_LEN, DIFF_WIDTH)
    cd_v = cache_diff_v.reshape(DEC_BATCH, DEPTH, PAST_LEN, DIFF_WIDTH)

    ctx_row = lambda i: 0
    lat_row = lambda i: 1 + (i * TM_PROJ) // DEC_SEQ

    kv_layers = []
    for l in range(DEPTH):
        lam_init = 0.8 - 0.6 * math.exp(-0.3 * l)
        mod = mod_all[l]
        g_mix_l = g_mix[l][None, :]
        g_ffn_l = g_ffn[l][None, :]
        g_sub_l = g_subln[l][None, :]
        lam_params = tuple(p[l][None, :] for p in (lam_q1, lam_k1, lam_q2, lam_k2))

        p_ctx, kv_ctx = _inproj(xp, g_mix_l, mod, w_in_b[l], ctx_row, emit_kv=True)
        kv_layers.append(kv_ctx)
        o_ctx = _ctx_attention(p_ctx, lam_params, g_sub_l, lam_init)
        xp = _outproj(xp, [o_ctx], mod, w_out_b[l], ctx_row)

        p_lat = _inproj(xs, g_mix_l, mod, w_in_b[l], lat_row, rope_tabs=rope_tabs)[0]
        o_na = _na_attention(p_lat, cna_k, cna_v, _na_bias_table(rpb[l]), l)
        o_d = _diff_attention(p_lat, cd_k, cd_v, lam_params, g_sub_l, lam_init, l)
        xs = _outproj(xs, [o_na, o_d], mod, w_out_b[l], lat_row)

        i = l // 2
        if l % 2 == 0:
            xp = _ffn_dense(xp, g_ffn_l, mod, w_fg[i], w_fu[i], w_fd[i], ctx_row)
            xs = _ffn_dense(xs, g_ffn_l, mod, w_fg[i], w_fu[i], w_fd[i], lat_row)
        else:
            wr = w_router_pad[i]
            h_p, comb_p = _router(xp, g_ffn_l, mod, wr, ctx_row)
            xp = _moe_dense(xp, h_p, comb_p, mod, w_mg[i], w_mu[i], w_md[i], ctx_row)
            h_s, comb_s = _router(xs, g_ffn_l, mod, wr, lat_row)
            xs = _moe_dense(xs, h_s, comb_s, mod, w_mg[i], w_mu[i], w_md[i], lat_row)

    g_fin = g_final[None, :]
    y_prompt = _final_norm(xp, g_fin).reshape(BATCH, SEQ, D_MODEL)
    y_sample = _final_norm(xs, g_fin).reshape(DEC_BATCH, DEC_SEQ, D_MODEL)

    kv = jnp.stack(kv_layers, axis=0).reshape(DEPTH, BATCH, SEQ, 4, 512)
    kv = kv.transpose(3, 1, 0, 2, 4)
    new_na_k = kv[0].reshape(BATCH, DEPTH, SEQ, NA_HEADS, HEAD_DIM)
    new_na_v = kv[1].reshape(BATCH, DEPTH, SEQ, NA_HEADS, HEAD_DIM)
    new_diff_k = kv[2].reshape(BATCH, DEPTH, SEQ, DIFF_HEADS, 2, HEAD_DIM)
    new_diff_v = kv[3].reshape(BATCH, DEPTH, SEQ, DIFF_HEADS, 2 * HEAD_DIM)
    return (y_prompt, y_sample, new_na_k, new_na_v, new_diff_k, new_diff_v)
```

```python
import functools
import math

import numpy as np
import jax
import jax.numpy as jnp
from jax import lax
from jax.experimental import pallas as pl
from jax.experimental.pallas import tpu as pltpu

F32 = jnp.float32
BF16 = jnp.bfloat16
I32 = jnp.int32

D_MODEL = 1024
DEPTH = 4
BATCH = 16
SEQ = 256
DEC_BATCH = 8
DEC_SEQ = 2048
PAST_LEN = 256
GRID_W = 64
GRID_ROWS = DEC_SEQ // GRID_W
HEAD_DIM = 64
NA_HEADS = 8
NA_WIDTH = 512
DIFF_HEADS = 4
DIFF_WIDTH = 512
IN_WIDTH = 3072
NA_WIN_H = 8
NA_WIN_W = 16
ROPE_THETA = 10000.0
D_FF = 2816
N_EXPERTS = 8
D_FF_EXPERT = 3584
EPS = 1e-6
SUBLN_EPS = 1e-5
ATTN_SCALE = HEAD_DIM ** -0.5

LANES = 128
N_CTX_TOK = BATCH * SEQ
N_LAT_TOK = DEC_BATCH * DEC_SEQ
N_TOK = N_CTX_TOK + N_LAT_TOK
MOD_ROWS = 16
MASK_VALUE = -1e30

COL_NA_Q, COL_NA_K, COL_NA_V = 0, 512, 1024
COL_D_Q, COL_D_K, COL_D_V = 1536, 2048, 2560

TM = 512
N_TILES = N_TOK // TM
CTX_TILES = N_CTX_TOK // TM
TILES_PER_SEQ = DEC_SEQ // TM
FF_TILE_DENSE = 1408
FF_TILE_MOE = 512
TQ_DIFF = 256
NA_R = 4
NA_WIN_ROWS = 12
NA_STEPS = GRID_ROWS // NA_R
MOE_SLOTS = 2 * N_TOK
MOE_TILES = MOE_SLOTS // TM + N_EXPERTS
VMEM_LIMIT = 56 * 1024 * 1024


def _cparams(*sem):
    return pltpu.CompilerParams(dimension_semantics=sem, vmem_limit_bytes=VMEM_LIMIT)


def _dot(a, b):
    return jnp.dot(a, b, preferred_element_type=F32)


def _dot_nt(a, b):
    return lax.dot_general(a, b, (((1,), (1,)), ((), ())), preferred_element_type=F32)


def _modulated_norm(x, g, shift, scale):
    xn = x * lax.rsqrt(jnp.mean(x * x, axis=-1, keepdims=True) + EPS)
    return (xn * g) * (1.0 + scale) + shift


def _mod_row(i):
    return jnp.maximum(i // TILES_PER_SEQ - CTX_TILES // TILES_PER_SEQ + 1, 0)


def _mod_spec(chunk):
    return pl.BlockSpec((1, 1, D_MODEL), lambda i, *_: (_mod_row(i) * 6 + chunk, 0, 0))


def _mod_kernel(c_ref, w_ref, b_ref, o_ref):
    cv = c_ref[...]
    s = cv * jax.nn.sigmoid(cv)
    o_ref[...] = jnp.dot(s, w_ref[...], preferred_element_type=F32,
                         precision=lax.Precision.HIGHEST) + b_ref[...]


def _modulation(cvec, w_ada, b_ada):
    tn = 1536
    n = 6 * D_MODEL
    return pl.pallas_call(
        _mod_kernel,
        grid=(DEPTH, n // tn),
        in_specs=[
            pl.BlockSpec((MOD_ROWS, D_MODEL), lambda l, j: (0, 0)),
            pl.BlockSpec((None, D_MODEL, tn), lambda l, j: (l, 0, j)),
            pl.BlockSpec((None, 1, tn), lambda l, j: (l, 0, j)),
        ],
        out_specs=pl.BlockSpec((None, MOD_ROWS, tn), lambda l, j: (l, 0, j)),
        out_shape=jax.ShapeDtypeStruct((DEPTH, MOD_ROWS, n), F32),
        compiler_params=_cparams("parallel", "parallel"),
        name="adaln_modulation",
    )(cvec, w_ada, b_ada.reshape(DEPTH, 1, n))


def _inproj_kernel(x_ref, g_ref, sh_ref, sc_ref, w_ref, cos_ref, sina_ref, sinb_ref, o_ref, kv_ref):
    i = pl.program_id(0)
    h = _modulated_norm(x_ref[...], g_ref[...], sh_ref[0], sc_ref[0]).astype(BF16)
    chunk = 512
    for c in range(IN_WIDTH // chunk):
        col = c * chunk
        acc = _dot(h, w_ref[:, col:col + chunk])
        for k_i, src in enumerate((COL_NA_K, COL_NA_V, COL_D_K, COL_D_V)):
            if src == col:
                @pl.when(i < CTX_TILES)
                def _(acc=acc, k_i=k_i):
                    kv_ref[:, k_i * chunk:(k_i + 1) * chunk] = acc
        if COL_D_Q <= col < COL_D_V:
            parts = []
            for j in range(chunk // LANES):
                blk = acc[:, j * LANES:(j + 1) * LANES]
                parts.append(blk * cos_ref[...]
                             + pltpu.roll(blk, LANES - 16, 1) * sina_ref[...]
                             + pltpu.roll(blk, 16, 1) * sinb_ref[...])
            acc = jnp.concatenate(parts, axis=1)
        o_ref[:, col:col + chunk] = acc.astype(o_ref.dtype)


def _inproj(x, g, mod, w_bf16, rope_tabs):
    rope_spec = pl.BlockSpec(
        (TM, LANES),
        lambda i: (jnp.where(i < CTX_TILES, TILES_PER_SEQ, (i - CTX_TILES) % TILES_PER_SEQ), 0))
    return pl.pallas_call(
        _inproj_kernel,
        grid=(N_TILES,),
        in_specs=[
            pl.BlockSpec((TM, D_MODEL), lambda i: (i, 0)),
            pl.BlockSpec((1, D_MODEL), lambda i: (0, 0)),
            _mod_spec(0), _mod_spec(1),
            pl.BlockSpec((D_MODEL, IN_WIDTH), lambda i: (0, 0)),
            rope_spec, rope_spec, rope_spec,
        ],
        out_specs=[
            pl.BlockSpec((TM, IN_WIDTH), lambda i: (i, 0)),
            pl.BlockSpec((TM, 2048), lambda i: (jnp.minimum(i, CTX_TILES - 1), 0)),
        ],
        out_shape=[jax.ShapeDtypeStruct((N_TOK, IN_WIDTH), BF16),
                   jax.ShapeDtypeStruct((N_CTX_TOK, 2048), F32)],
        compiler_params=_cparams("arbitrary"),
        name="inproj",
    )(x, g, mod, mod, w_bf16, *rope_tabs)


def _lane_half_mask(shape, half):
    lane = lax.broadcasted_iota(I32, shape, len(shape) - 1)
    return (lane < HEAD_DIM) if half == 0 else (lane >= HEAD_DIM)


def _lambda_value(lq1, lk1, lq2, lk2, lam_init):
    a = jnp.sum(lq1 * lk1, axis=-1, keepdims=True)
    b = jnp.sum(lq2 * lk2, axis=-1, keepdims=True)
    return jnp.exp(a) - jnp.exp(b) + lam_init


def _subln(o, g, lam_init):
    on = o * lax.rsqrt(jnp.mean(o * o, axis=-1, keepdims=True) + SUBLN_EPS)
    return (on * g) * (1.0 - lam_init)


def _ctx_attn_kernel(p_ref, lq1, lk1, lq2, lk2, gs_ref, o_ref, *, lam_init):
    lam = _lambda_value(lq1[...], lk1[...], lq2[...], lk2[...], lam_init)
    for hp in range(NA_HEADS // 2):
        q = p_ref[:, COL_NA_Q + hp * LANES:COL_NA_Q + (hp + 1) * LANES]
        k = p_ref[:, COL_NA_K + hp * LANES:COL_NA_K + (hp + 1) * LANES]
        v = p_ref[:, COL_NA_V + hp * LANES:COL_NA_V + (hp + 1) * LANES]
        outs = []
        for half in range(2):
            qm = jnp.where(_lane_half_mask(q.shape, half), q, jnp.zeros_like(q))
            s = _dot_nt(qm, k)
            m = jnp.max(s, axis=-1, keepdims=True)
            e = jnp.exp(s - m)
            inv = 1.0 / jnp.sum(e, axis=-1, keepdims=True)
            outs.append(_dot(e.astype(BF16), v) * inv)
        o = jnp.where(_lane_half_mask(outs[0].shape, 0), outs[0], outs[1])
        o_ref[:, hp * LANES:(hp + 1) * LANES] = o.astype(o_ref.dtype)
    for h in range(DIFF_HEADS):
        q = p_ref[:, COL_D_Q + h * LANES:COL_D_Q + (h + 1) * LANES]
        k = p_ref[:, COL_D_K + h * LANES:COL_D_K + (h + 1) * LANES]
        v = p_ref[:, COL_D_V + h * LANES:COL_D_V + (h + 1) * LANES]
        ps = []
        for half in range(2):
            qm = jnp.where(_lane_half_mask(q.shape, half), q, jnp.zeros_like(q))
            s = _dot_nt(qm, k)
            m = jnp.max(s, axis=-1, keepdims=True)
            e = jnp.exp(s - m)
            ps.append(e / jnp.sum(e, axis=-1, keepdims=True))
        a = (ps[0] - lam * ps[1]).astype(BF16)
        o = _subln(_dot(a, v), gs_ref[...], lam_init)
        o_ref[:, NA_WIDTH + h * LANES:NA_WIDTH + (h + 1) * LANES] = o.astype(o_ref.dtype)


def _ctx_attention(p_all, lam_params, g_subln, lam_init):
    vec = pl.BlockSpec((1, HEAD_DIM), lambda b: (0, 0))
    return pl.pallas_call(
        functools.partial(_ctx_attn_kernel, lam_init=lam_init),
        grid=(BATCH,),
        in_specs=[pl.BlockSpec((SEQ, IN_WIDTH), lambda b: (b, 0)), vec, vec, vec, vec,
                  pl.BlockSpec((1, 2 * HEAD_DIM), lambda b: (0, 0))],
        out_specs=pl.BlockSpec((SEQ, D_MODEL), lambda b: (b, 0)),
        out_shape=jax.ShapeDtypeStruct((N_CTX_TOK, D_MODEL), BF16),
        compiler_params=_cparams("parallel"),
        name="ctx_attention",
    )(p_all, *lam_params, g_subln)


def _bias_expand_kernel(rpb_ref, o_ref):
    n = GRID_W * GRID_W
    c = lax.broadcasted_iota(I32, (LANES, n), 0)
    j = lax.broadcasted_iota(I32, (LANES, n), 1)
    qc = jnp.right_shift(j, 6)
    kc = jnp.bitwise_and(j, GRID_W - 1)
    dc = jnp.clip(kc - qc + (NA_WIN_W - 1), 0, 2 * NA_WIN_W - 2)
    onehot = jnp.where(c == dc, 1.0, 0.0).astype(F32)
    t = jnp.dot(rpb_ref[...], onehot, preferred_element_type=F32, precision=lax.Precision.HIGHEST)
    j1 = lax.broadcasted_iota(I32, t.shape, 1)
    qc1 = jnp.right_shift(j1, 6)
    kc1 = jnp.bitwise_and(j1, GRID_W - 1)
    cs = jnp.clip(qc1 - NA_WIN_W // 2, 0, GRID_W - NA_WIN_W)
    valid = (kc1 >= cs) & (kc1 < cs + NA_WIN_W)
    o_ref[...] = jnp.where(valid, t, MASK_VALUE)


def _na_window_plan():
    dr = np.zeros((3, NA_R, NA_WIN_ROWS), np.int32)
    ok = np.zeros((3, NA_R, NA_WIN_ROWS), bool)
    for t, r0 in enumerate((0, NA_R, GRID_ROWS - NA_R)):
        lo = min(max(r0 - NA_WIN_H // 2, 0), GRID_ROWS - NA_WIN_ROWS)
        for jr in range(NA_R):
            r = r0 + jr
            rs = min(max(r - NA_WIN_H // 2, 0), GRID_ROWS - NA_WIN_H)
            for i in range(NA_WIN_ROWS):
                key_row = lo + i
                if rs <= key_row < rs + NA_WIN_H:
                    ok[t, jr, i] = True
                    dr[t, jr, i] = key_row - r + NA_WIN_H - 1
    return dr, ok


def _na_bias_tables(rpb):
    n_dr = 2 * NA_WIN_H - 1
    rows = DEPTH * NA_HEADS * n_dr
    rpb_pad = jnp.pad(rpb.reshape(rows, 2 * NA_WIN_W - 1), ((0, 0), (0, LANES - (2 * NA_WIN_W - 1))))
    tc = pl.pallas_call(
        _bias_expand_kernel,
        out_shape=jax.ShapeDtypeStruct((rows, GRID_W * GRID_W), F32),
        compiler_params=pltpu.CompilerParams(vmem_limit_bytes=VMEM_LIMIT),
        name="na_bias_expand",
    )(rpb_pad)
    tc = tc.reshape(DEPTH, NA_HEADS, n_dr, GRID_W, GRID_W)
    dr, ok = _na_window_plan()
    t = jnp.take(tc, jnp.asarray(dr.reshape(-1)), axis=2)
    t = t.reshape(DEPTH, NA_HEADS, 3, NA_R, NA_WIN_ROWS, GRID_W, GRID_W)
    t = jnp.where(jnp.asarray(ok)[None, None, :, :, :, None, None], t, MASK_VALUE)
    t = t.transpose(0, 2, 1, 3, 5, 4, 6)
    return t.reshape(DEPTH, 3, NA_HEADS, NA_R * GRID_W, NA_WIN_ROWS * GRID_W).astype(BF16)


def _na_attn_kernel(q_ref, k_ref, v_ref, kc_ref, vc_ref, bias_ref, o_ref):
    j = pl.program_id(1)
    lo = jnp.clip(j * NA_R - NA_WIN_H // 2, 0, GRID_ROWS - NA_WIN_ROWS)
    k0 = pl.multiple_of(lo * GRID_W, GRID_W)
    step_type = jnp.where(j == 0, 0, jnp.where(j == NA_STEPS - 1, 2, 1))
    n_keys = NA_WIN_ROWS * GRID_W
    for hp in range(NA_HEADS // 2):
        lanes = slice(hp * LANES, (hp + 1) * LANES)
        q = q_ref[:, lanes]
        kw = k_ref[pl.ds(k0, n_keys), lanes]
        vw = v_ref[pl.ds(k0, n_keys), lanes]
        kc = kc_ref[:, lanes].astype(BF16)
        vc = vc_ref[:, lanes].astype(BF16)
        outs = []
        for half in range(2):
            qm = jnp.where(_lane_half_mask(q.shape, half), q, jnp.zeros_like(q))
            s_w = _dot_nt(qm, kw) + bias_ref[step_type, 2 * hp + half].astype(F32)
            s_c = _dot_nt(qm, kc)
            m = jnp.maximum(jnp.max(s_w, axis=-1, keepdims=True),
                            jnp.max(s_c, axis=-1, keepdims=True))
            e_w = jnp.exp(s_w - m)
            e_c = jnp.exp(s_c - m)
            inv = 1.0 / (jnp.sum(e_w, axis=-1, keepdims=True)
                         + jnp.sum(e_c, axis=-1, keepdims=True))
            outs.append((_dot(e_w.astype(BF16), vw) + _dot(e_c.astype(BF16), vc)) * inv)
        o = jnp.where(_lane_half_mask(outs[0].shape, 0), outs[0], outs[1])
        o_ref[:, lanes] = o.astype(o_ref.dtype)


def _na_attention(p_all, cache_k, cache_v, bias_tab, layer):
    q_rows = NA_R * GRID_W
    q_blk0 = N_CTX_TOK // q_rows
    seq_blk0 = N_CTX_TOK // DEC_SEQ
    cache_spec = pl.BlockSpec((None, None, PAST_LEN, NA_WIDTH), lambda b, j: (b, layer, 0, 0))
    return pl.pallas_call(
        _na_attn_kernel,
        grid=(DEC_BATCH, NA_STEPS),
        in_specs=[
            pl.BlockSpec((q_rows, NA_WIDTH), lambda b, j: (q_blk0 + b * NA_STEPS + j, COL_NA_Q // NA_WIDTH)),
            pl.BlockSpec((DEC_SEQ, NA_WIDTH), lambda b, j: (seq_blk0 + b, COL_NA_K // NA_WIDTH)),
            pl.BlockSpec((DEC_SEQ, NA_WIDTH), lambda b, j: (seq_blk0 + b, COL_NA_V // NA_WIDTH)),
            cache_spec, cache_spec,
            pl.BlockSpec((None, 3, NA_HEADS, q_rows, NA_WIN_ROWS * GRID_W), lambda b, j: (layer, 0, 0, 0, 0)),
        ],
        out_specs=pl.BlockSpec((q_rows, NA_WIDTH), lambda b, j: (b * NA_STEPS + j, 0)),
        out_shape=jax.ShapeDtypeStruct((N_LAT_TOK, NA_WIDTH), BF16),
        compiler_params=_cparams("parallel", "arbitrary"),
        name="latent_na_attention",
    )(p_all, p_all, p_all, cache_k, cache_v, bias_tab)


def _diff_attn_kernel(q_ref, k_ref, v_ref, kc_ref, vc_ref, lq1, lk1, lq2, lk2, gs_ref, o_ref,
                      *, lam_init):
    lam = _lambda_value(lq1[...], lk1[...], lq2[...], lk2[...], lam_init)
    q = q_ref[...]
    k = k_ref[...]
    kc = kc_ref[...].astype(BF16)
    probs = []
    for half in range(2):
        qm = jnp.where(_lane_half_mask(q.shape, half), q, jnp.zeros_like(q))
        s_l = _dot_nt(qm, k)
        s_c = _dot_nt(qm, kc)
        m = jnp.maximum(jnp.max(s_l, axis=-1, keepdims=True), jnp.max(s_c, axis=-1, keepdims=True))
        e_l = jnp.exp(s_l - m)
        e_c = jnp.exp(s_c - m)
        inv = 1.0 / (jnp.sum(e_l, axis=-1, keepdims=True) + jnp.sum(e_c, axis=-1, keepdims=True))
        probs.append((e_l, e_c, inv))
    w1 = probs[0][2]
    w2 = lam * probs[1][2]
    a_l = (probs[0][0] * w1 - probs[1][0] * w2).astype(BF16)
    a_c = (probs[0][1] * w1 - probs[1][1] * w2).astype(BF16)
    o = _dot(a_l, v_ref[...]) + _dot(a_c, vc_ref[...].astype(BF16))
    o_ref[...] = _subln(o, gs_ref[...], lam_init).astype(o_ref.dtype)


def _diff_attention(p_all, cache_k, cache_v, lam_params, g_subln, lam_init, layer):
    steps = DEC_SEQ // TQ_DIFF
    q_blk0 = N_CTX_TOK // TQ_DIFF
    seq_blk0 = N_CTX_TOK // DEC_SEQ
    vec = pl.BlockSpec((1, HEAD_DIM), lambda b, h, j: (0, 0))
    cache_spec = pl.BlockSpec((None, None, PAST_LEN, LANES), lambda b, h, j: (b, layer, 0, h))
    return pl.pallas_call(
        functools.partial(_diff_attn_kernel, lam_init=lam_init),
        grid=(DEC_BATCH, DIFF_HEADS, steps),
        in_specs=[
            pl.BlockSpec((TQ_DIFF, LANES), lambda b, h, j: (q_blk0 + b * steps + j, COL_D_Q // LANES + h)),
            pl.BlockSpec((DEC_SEQ, LANES), lambda b, h, j: (seq_blk0 + b, COL_D_K // LANES + h)),
            pl.BlockSpec((DEC_SEQ, LANES), lambda b, h, j: (seq_blk0 + b, COL_D_V // LANES + h)),
            cache_spec, cache_spec, vec, vec, vec, vec,
            pl.BlockSpec((1, 2 * HEAD_DIM), lambda b, h, j: (0, 0)),
        ],
        out_specs=pl.BlockSpec((TQ_DIFF, LANES), lambda b, h, j: (b * steps + j, h)),
        out_shape=jax.ShapeDtypeStruct((N_LAT_TOK, DIFF_WIDTH), BF16),
        compiler_params=_cparams("parallel", "parallel", "arbitrary"),
        name="latent_diff_attention",
    )(p_all, p_all, p_all, cache_k, cache_v, *lam_params, g_subln)


def _outproj_kernel(x_ref, oc_ref, ona_ref, od_ref, ga_ref, w_ref, y_ref):
    i = pl.program_id(0)

    @pl.when(i < CTX_TILES)
    def _():
        y_ref[...] = x_ref[...] + ga_ref[0] * _dot(oc_ref[...], w_ref[...])

    @pl.when(i >= CTX_TILES)
    def _():
        acc = _dot(ona_ref[...], w_ref[:NA_WIDTH, :]) + _dot(od_ref[...], w_ref[NA_WIDTH:, :])
        y_ref[...] = x_ref[...] + ga_ref[0] * acc


def _outproj(x, o_ctx, o_na, o_d, mod, w_bf16):
    lat_idx = lambda i: (jnp.maximum(i - CTX_TILES, 0), 0)
    return pl.pallas_call(
        _outproj_kernel,
        grid=(N_TILES,),
        in_specs=[
            pl.BlockSpec((TM, D_MODEL), lambda i: (i, 0)),
            pl.BlockSpec((TM, D_MODEL), lambda i: (jnp.minimum(i, CTX_TILES - 1), 0)),
            pl.BlockSpec((TM, NA_WIDTH), lat_idx),
            pl.BlockSpec((TM, DIFF_WIDTH), lat_idx),
            _mod_spec(2),
            pl.BlockSpec((D_MODEL, D_MODEL), lambda i: (0, 0)),
        ],
        out_specs=pl.BlockSpec((TM, D_MODEL), lambda i: (i, 0)),
        out_shape=jax.ShapeDtypeStruct((N_TOK, D_MODEL), F32),
        compiler_params=_cparams("parallel"),
        name="outproj_residual",
    )(x, o_ctx, o_na, o_d, mod, w_bf16)


def _swiglu_tile(hb, wg, wu):
    g = _dot(hb, wg)
    u = _dot(hb, wu)
    return (g * jax.nn.sigmoid(g)) * u


def _ffn_kernel(x_ref, g_ref, sh_ref, sc_ref, ga_ref, wg_ref, wu_ref, wd_ref, y_ref, h_scr, acc_scr):
    f = pl.program_id(1)

    @pl.when(f == 0)
    def _():
        h = _modulated_norm(x_ref[...], g_ref[...], sh_ref[0], sc_ref[0])
        h_scr[...] = h.astype(BF16)
        acc_scr[...] = jnp.zeros_like(acc_scr)

    a = _swiglu_tile(h_scr[...], wg_ref[...], wu_ref[...])
    acc_scr[...] += _dot(a.astype(BF16), wd_ref[...])

    @pl.when(f == pl.num_programs(1) - 1)
    def _():
        y_ref[...] = x_ref[...] + ga_ref[0] * acc_scr[...]


def _ffn_dense(x, g, mod, wg, wu, wd):
    tf = FF_TILE_DENSE
    return pl.pallas_call(
        _ffn_kernel,
        grid=(N_TILES, D_FF // tf),
        in_specs=[
            pl.BlockSpec((TM, D_MODEL), lambda i, f: (i, 0)),
            pl.BlockSpec((1, D_MODEL), lambda i, f: (0, 0)),
            _mod_spec(3), _mod_spec(4), _mod_spec(5),
            pl.BlockSpec((D_MODEL, tf), lambda i, f: (0, f)),
            pl.BlockSpec((D_MODEL, tf), lambda i, f: (0, f)),
            pl.BlockSpec((tf, D_MODEL), lambda i, f: (f, 0)),
        ],
        out_specs=pl.BlockSpec((TM, D_MODEL), lambda i, f: (i, 0)),
        out_shape=jax.ShapeDtypeStruct((N_TOK, D_MODEL), F32),
        scratch_shapes=[pltpu.VMEM((TM, D_MODEL), BF16), pltpu.VMEM((TM, D_MODEL), F32)],
        compiler_params=_cparams("parallel", "arbitrary"),
        name="ffn_dense",
    )(x, g, mod, mod, mod, wg, wu, wd)


ROUTE_E1, ROUTE_E2, ROUTE_G1, ROUTE_G2, ROUTE_R1, ROUTE_R2 = range(6)


def _router_kernel(x_ref, g_ref, sh_ref, sc_ref, wr_ref, h_ref, route_ref, cnt_ref, carry_scr):
    i = pl.program_id(0)

    @pl.when(i == 0)
    def _():
        carry_scr[...] = jnp.zeros_like(carry_scr)

    h = _modulated_norm(x_ref[...], g_ref[...], sh_ref[0], sc_ref[0])
    h_ref[...] = h
    logits = jnp.dot(h, wr_ref[...], preferred_element_type=F32, precision=lax.Precision.HIGHEST)
    lane = lax.broadcasted_iota(I32, logits.shape, 1)
    logits = jnp.where(lane < N_EXPERTS, logits, MASK_VALUE)
    m1 = jnp.max(logits, axis=-1, keepdims=True)
    i1 = jnp.min(jnp.where(logits == m1, lane, LANES), axis=-1, keepdims=True)
    rest = jnp.where(lane == i1, MASK_VALUE, logits)
    m2 = jnp.max(rest, axis=-1, keepdims=True)
    i2 = jnp.min(jnp.where(rest == m2, lane, LANES), axis=-1, keepdims=True)
    e2 = jnp.exp(m2 - m1)
    g1 = 1.0 / (1.0 + e2)
    g2 = e2 / (1.0 + e2)

    hit1 = lane == i1
    hit2 = lane == i2
    onehot = jnp.where(hit1 | hit2, 1.0, 0.0)
    row = lax.broadcasted_iota(I32, (TM, TM), 0)
    col = lax.broadcasted_iota(I32, (TM, TM), 1)
    lower = jnp.where(row > col, 1.0, 0.0).astype(BF16)
    before = _dot(lower, onehot.astype(BF16)) + carry_scr[...]
    r1 = jnp.sum(jnp.where(hit1, before, 0.0), axis=-1, keepdims=True)
    r2 = jnp.sum(jnp.where(hit2, before, 0.0), axis=-1, keepdims=True)
    carry_scr[...] += jnp.sum(onehot, axis=0, keepdims=True)

    out = jnp.zeros(logits.shape, F32)
    for slot, val in ((ROUTE_E1, i1.astype(F32)), (ROUTE_E2, i2.astype(F32)), (ROUTE_G1, g1),
                      (ROUTE_G2, g2), (ROUTE_R1, r1), (ROUTE_R2, r2)):
        out = jnp.where(lane == slot, val, out)
    route_ref[...] = out

    @pl.when(i == pl.num_programs(0) - 1)
    def _():
        cnt_ref[...] = jnp.broadcast_to(carry_scr[...], cnt_ref.shape)


def _router(x, g, mod, w_router_pad):
    return pl.pallas_call(
        _router_kernel,
        grid=(N_TILES,),
        in_specs=[
            pl.BlockSpec((TM, D_MODEL), lambda i: (i, 0)),
            pl.BlockSpec((1, D_MODEL), lambda i: (0, 0)),
            _mod_spec(3), _mod_spec(4),
            pl.BlockSpec((D_MODEL, LANES), lambda i: (0, 0)),
        ],
        out_specs=[pl.BlockSpec((TM, D_MODEL), lambda i: (i, 0)),
                   pl.BlockSpec((TM, LANES), lambda i: (i, 0)),
                   pl.BlockSpec((8, LANES), lambda i: (0, 0))],
        out_shape=[jax.ShapeDtypeStruct((N_TOK, D_MODEL), F32),
                   jax.ShapeDtypeStruct((N_TOK, LANES), F32),
                   jax.ShapeDtypeStruct((8, LANES), F32)],
        scratch_shapes=[pltpu.VMEM((1, LANES), F32)],
        compiler_params=_cparams("arbitrary"),
        name="moe_router",
    )(x, g, mod, mod, w_router_pad)


def _moe_plan(route, counts):
    cnt = counts[0, :N_EXPERTS].astype(I32)
    tiles = (cnt + TM - 1) // TM
    tile_end = jnp.cumsum(tiles)
    group_start = (tile_end - tiles) * TM
    e = route[:, ROUTE_E1:ROUTE_E2 + 1].astype(I32)
    r = route[:, ROUTE_R1:ROUTE_R2 + 1].astype(I32)
    slot = (group_start[e] + r).T.reshape(-1)
    tok = jnp.tile(jnp.arange(N_TOK, dtype=I32), 2)
    src_tok = jnp.zeros((MOE_TILES * TM,), I32).at[slot].set(tok)
    n_valid = tile_end[-1:]
    t = jnp.arange(MOE_TILES, dtype=I32)
    tile_expert = jnp.minimum(jnp.sum((t[:, None] >= tile_end[None, :]).astype(I32), axis=1),
                              N_EXPERTS - 1)
    last_expert = tile_expert[jnp.maximum(n_valid[0] - 1, 0)]
    tile_expert = jnp.where(t < n_valid[0], tile_expert, last_expert)
    return tile_expert, n_valid, src_tok, slot


def _moe_kernel(te_ref, nv_ref, src_ref, h_hbm, wg_ref, wu_ref, wd_ref, y_ref,
                hrow_scr, hb_scr, acc_scr, sem):
    t = pl.program_id(0)
    f = pl.program_id(1)
    valid = t < nv_ref[0]
    last_f = pl.num_programs(1) - 1

    @pl.when(valid & (f == 0))
    def _():
        base = t * TM

        def issue(j, carry):
            tok = src_ref[base + j]
            pltpu.make_async_copy(h_hbm.at[pl.ds(tok, 1), :], hrow_scr.at[pl.ds(j, 1), :], sem).start()
            return carry

        lax.fori_loop(0, TM, issue, 0, unroll=8)
        pltpu.make_async_copy(h_hbm.at[pl.ds(0, TM), :], hrow_scr, sem).wait()
        hb_scr[...] = hrow_scr[...].astype(BF16)
        acc_scr[...] = jnp.zeros_like(acc_scr)

    @pl.when(valid)
    def _():
        a = _swiglu_tile(hb_scr[...], wg_ref[...], wu_ref[...])
        acc_scr[...] += _dot(a.astype(BF16), wd_ref[...])

    @pl.when(valid & (f == last_f))
    def _():
        y_ref[...] = acc_scr[...]

    @pl.when(jnp.logical_not(valid) & (f == last_f))
    def _():
        y_ref[...] = jnp.zeros_like(y_ref)


def _moe_experts(tile_expert, n_valid, src_tok, h, wg, wu, wd):
    tf = FF_TILE_MOE
    n_f = D_FF_EXPERT // tf

    def f_eff(t, f, nv):
        return jnp.where(t < nv[0], f, n_f - 1)

    grid_spec = pltpu.PrefetchScalarGridSpec(
        num_scalar_prefetch=3,
        grid=(MOE_TILES, n_f),
        in_specs=[
            pl.BlockSpec(memory_space=pl.ANY),
            pl.BlockSpec((None, D_MODEL, tf), lambda t, f, te, nv, st: (te[t], 0, f_eff(t, f, nv))),
            pl.BlockSpec((None, D_MODEL, tf), lambda t, f, te, nv, st: (te[t], 0, f_eff(t, f, nv))),
            pl.BlockSpec((None, tf, D_MODEL), lambda t, f, te, nv, st: (te[t], f_eff(t, f, nv), 0)),
        ],
        out_specs=pl.BlockSpec((TM, D_MODEL), lambda t, f, te, nv, st: (t, 0)),
        scratch_shapes=[pltpu.VMEM((TM, D_MODEL), F32), pltpu.VMEM((TM, D_MODEL), BF16),
                        pltpu.VMEM((TM, D_MODEL), F32), pltpu.SemaphoreType.DMA],
    )
    return pl.pallas_call(
        _moe_kernel,
        grid_spec=grid_spec,
        out_shape=jax.ShapeDtypeStruct((MOE_TILES * TM, D_MODEL), F32),
        compiler_params=_cparams("arbitrary", "arbitrary"),
        name="moe_experts",
    )(tile_expert, n_valid, src_tok, h, wg, wu, wd)


def _moe_combine_kernel(slot_ref, x_ref, route_ref, ga_ref, y_hbm, o_ref, rows_scr, sem):
    i = pl.program_id(0)
    for c in range(2):
        base = c * N_TOK + i * TM

        def issue(j, carry, base=base, c=c):
            s = slot_ref[base + j]
            pltpu.make_async_copy(y_hbm.at[pl.ds(s, 1), :], rows_scr.at[c, pl.ds(j, 1), :], sem).start()
            return carry

        lax.fori_loop(0, TM, issue, 0, unroll=8)
    for c in range(2):
        pltpu.make_async_copy(y_hbm.at[pl.ds(0, TM), :], rows_scr.at[c], sem).wait()
    route = route_ref[...]
    g1 = route[:, ROUTE_G1:ROUTE_G1 + 1]
    g2 = route[:, ROUTE_G2:ROUTE_G2 + 1]
    o_ref[...] = x_ref[...] + ga_ref[0] * (g1 * rows_scr[0] + g2 * rows_scr[1])


def _moe_combine(slot, x, route, mod, y):
    grid_spec = pltpu.PrefetchScalarGridSpec(
        num_scalar_prefetch=1,
        grid=(N_TILES,),
        in_specs=[
            pl.BlockSpec((TM, D_MODEL), lambda i, s: (i, 0)),
            pl.BlockSpec((TM, LANES), lambda i, s: (i, 0)),
            _mod_spec(5),
            pl.BlockSpec(memory_space=pl.ANY),
        ],
        out_specs=pl.BlockSpec((TM, D_MODEL), lambda i, s: (i, 0)),
        scratch_shapes=[pltpu.VMEM((2, TM, D_MODEL), F32), pltpu.SemaphoreType.DMA],
    )
    return pl.pallas_call(
        _moe_combine_kernel,
        grid_spec=grid_spec,
        out_shape=jax.ShapeDtypeStruct((N_TOK, D_MODEL), F32),
        compiler_params=_cparams("arbitrary"),
        name="moe_combine",
    )(slot, x, route, mod, y)


def _final_norm_kernel(x_ref, g_ref, y_ref):
    x = x_ref[...]
    y_ref[...] = (x * lax.rsqrt(jnp.mean(x * x, axis=-1, keepdims=True) + EPS)) * g_ref[...]


def _final_norm(x, g, tile0, n_tok):
    return pl.pallas_call(
        _final_norm_kernel,
        grid=(n_tok // TM,),
        in_specs=[pl.BlockSpec((TM, D_MODEL), lambda i: (tile0 + i, 0)),
                  pl.BlockSpec((1, D_MODEL), lambda i: (0, 0))],
        out_specs=pl.BlockSpec((TM, D_MODEL), lambda i: (i, 0)),
        out_shape=jax.ShapeDtypeStruct((n_tok, D_MODEL), F32),
        compiler_params=_cparams("parallel"),
        name="final_norm",
    )(x, g)


def _rope_tables():
    t = jnp.arange(DEC_SEQ)
    row = (t // GRID_W).astype(F32)
    col = (t % GRID_W).astype(F32)
    half = HEAD_DIM // 2
    inv = 1.0 / (ROPE_THETA ** (jnp.arange(0, half, 2, dtype=F32) / half))
    ar = row[:, None] * inv[None]
    ac = col[:, None] * inv[None]
    ang = jnp.concatenate([ar, ar, ac, ac], axis=-1)
    ang = jnp.concatenate([ang, jnp.zeros((TM, HEAD_DIM), F32)], axis=0)
    cos = jnp.tile(jnp.cos(ang), (1, LANES // HEAD_DIM))
    sin = jnp.tile(jnp.sin(ang), (1, LANES // HEAD_DIM))
    first_half = (jnp.arange(LANES) % 32) < 16
    sin_a = jnp.where(first_half[None, :], -sin, 0.0)
    sin_b = jnp.where(first_half[None, :], 0.0, sin)
    return cos, sin_a, sin_b


def kernel(x_prompt, x_sample, c, cache_na_k, cache_na_v, cache_diff_k, cache_diff_v, c_ctx, w_ada, b_ada, g_mix, w_in, rpb, lam_q1, lam_k1, lam_q2, lam_k2, g_subln, w_out, g_ffn, w_ffn_gate, w_ffn_up, w_ffn_down, w_router, w_moe_gate, w_moe_up, w_moe_down, g_final):
    x = jnp.concatenate([x_prompt.reshape(N_CTX_TOK, D_MODEL), x_sample.reshape(N_LAT_TOK, D_MODEL)])

    cvec = jnp.zeros((MOD_ROWS, D_MODEL), F32).at[0].set(c_ctx).at[1:1 + DEC_BATCH].set(c)
    mod_all = _modulation(cvec, w_ada, b_ada).reshape(DEPTH, MOD_ROWS * 6, 1, D_MODEL)

    col = jnp.arange(IN_WIDTH)
    is_q = (col < COL_NA_K) | ((col >= COL_D_Q) & (col < COL_D_K))
    q_scale = jnp.where(is_q, ATTN_SCALE, 1.0).astype(F32)
    w_in_b = (w_in * q_scale[None, None, :]).astype(BF16)
    w_out_b = w_out.astype(BF16)
    w_fg, w_fu, w_fd = (w.astype(BF16) for w in (w_ffn_gate, w_ffn_up, w_ffn_down))
    w_mg, w_mu, w_md = (w.astype(BF16) for w in (w_moe_gate, w_moe_up, w_moe_down))
    w_router_pad = jnp.pad(w_router, ((0, 0), (0, 0), (0, LANES - N_EXPERTS)))

    rope_tabs = _rope_tables()
    bias_tabs = _na_bias_tables(rpb)
    cna_k = cache_na_k.reshape(DEC_BATCH, DEPTH, PAST_LEN, NA_WIDTH)
    cna_v = cache_na_v.reshape(DEC_BATCH, DEPTH, PAST_LEN, NA_WIDTH)
    cd_k = cache_diff_k.reshape(DEC_BATCH, DEPTH, PAST_LEN, DIFF_WIDTH)
    cd_v = cache_diff_v.reshape(DEC_BATCH, DEPTH, PAST_LEN, DIFF_WIDTH)

    kv_layers = []
    for l in range(DEPTH):
        lam_init = 0.8 - 0.6 * math.exp(-0.3 * l)
        mod = mod_all[l]
        g_mix_l = g_mix[l][None, :]
        g_ffn_l = g_ffn[l][None, :]
        g_sub_l = g_subln[l][None, :]
        lam_params = tuple(p[l][None, :] for p in (lam_q1, lam_k1, lam_q2, lam_k2))

        p_all, kv_ctx = _inproj(x, g_mix_l, mod, w_in_b[l], rope_tabs)
        kv_layers.append(kv_ctx)
        o_ctx = _ctx_attention(p_all, lam_params, g_sub_l, lam_init)
        o_na = _na_attention(p_all, cna_k, cna_v, bias_tabs, l)
        o_d = _diff_attention(p_all, cd_k, cd_v, lam_params, g_sub_l, lam_init, l)
        x = _outproj(x, o_ctx, o_na, o_d, mod, w_out_b[l])

        i = l // 2
        if l % 2 == 0:
            x = _ffn_dense(x, g_ffn_l, mod, w_fg[i], w_fu[i], w_fd[i])
        else:
            h, route, counts = _router(x, g_ffn_l, mod, w_router_pad[i])
            tile_expert, n_valid, src_tok, slot = _moe_plan(route, counts)
            y = _moe_experts(tile_expert, n_valid, src_tok, h, w_mg[i], w_mu[i], w_md[i])
            x = _moe_combine(slot, x, route, mod, y)

    g_fin = g_final[None, :]
    y_prompt = _final_norm(x, g_fin, 0, N_CTX_TOK).reshape(BATCH, SEQ, D_MODEL)
    y_sample = _final_norm(x, g_fin, CTX_TILES, N_LAT_TOK).reshape(DEC_BATCH, DEC_SEQ, D_MODEL)

    kv = jnp.stack(kv_layers, axis=0).reshape(DEPTH, BATCH, SEQ, 4, 512)
    kv = kv.transpose(3, 1, 0, 2, 4)
    new_na_k = kv[0].reshape(BATCH, DEPTH, SEQ, NA_HEADS, HEAD_DIM)
    new_na_v = kv[1].reshape(BATCH, DEPTH, SEQ, NA_HEADS, HEAD_DIM)
    new_diff_k = kv[2].reshape(BATCH, DEPTH, SEQ, DIFF_HEADS, 2, HEAD_DIM)
    new_diff_v = kv[3].reshape(BATCH, DEPTH, SEQ, DIFF_HEADS, 2 * HEAD_DIM)
    return (y_prompt, y_sample, new_na_k, new_na_v, new_diff_k, new_diff_v)
```

```python
import functools
import math

import numpy as np
import jax
import jax.numpy as jnp
from jax import lax
from jax.experimental import pallas as pl
from jax.experimental.pallas import tpu as pltpu

F32 = jnp.float32
BF16 = jnp.bfloat16
I32 = jnp.int32

D_MODEL = 1024
DEPTH = 4
BATCH = 16
SEQ = 256
DEC_BATCH = 8
DEC_SEQ = 2048
PAST_LEN = 256
GRID_W = 64
GRID_ROWS = DEC_SEQ // GRID_W
HEAD_DIM = 64
NA_HEADS = 8
NA_WIDTH = 512
DIFF_HEADS = 4
DIFF_WIDTH = 512
IN_WIDTH = 3072
NA_WIN_H = 8
NA_WIN_W = 16
ROPE_THETA = 10000.0
D_FF = 2816
N_EXPERTS = 8
D_FF_EXPERT = 3584
EPS = 1e-6
SUBLN_EPS = 1e-5
ATTN_SCALE = HEAD_DIM ** -0.5

LANES = 128
N_CTX_TOK = BATCH * SEQ
N_LAT_TOK = DEC_BATCH * DEC_SEQ
N_TOK = N_CTX_TOK + N_LAT_TOK
MOD_ROWS = 16
MASK_VALUE = -1e30

COL_NA_Q, COL_NA_K, COL_NA_V = 0, 512, 1024
COL_D_Q, COL_D_K, COL_D_V = 1536, 2048, 2560

TM = 512
N_TILES = N_TOK // TM
CTX_TILES = N_CTX_TOK // TM
TILES_PER_SEQ = DEC_SEQ // TM
FF_TILE_DENSE = 1408
FF_TILE_MOE = 512
N_FF_MOE = D_FF_EXPERT // FF_TILE_MOE
TM_MOE = 1024
TQ_DIFF = 512
DIFF_SUB = 128
NA_R = 4
NA_WIN_ROWS = 12
NA_STEPS = GRID_ROWS // NA_R
MOE_SLOTS = 2 * N_TOK
MOE_TILES = MOE_SLOTS // TM_MOE + N_EXPERTS
GATHER_CHUNK = -(-TM_MOE // N_FF_MOE)
GATHER_ROWS = GATHER_CHUNK * N_FF_MOE
LOG2E = 1.4426950408889634
VMEM_LIMIT = 56 * 1024 * 1024


def _cparams(*sem):
    return pltpu.CompilerParams(dimension_semantics=sem, vmem_limit_bytes=VMEM_LIMIT)


def _dot(a, b):
    return jnp.dot(a, b, preferred_element_type=F32)


def _dot_nt(a, b):
    return lax.dot_general(a, b, (((1,), (1,)), ((), ())), preferred_element_type=F32)


def _modulated_norm(x, g, shift, scale):
    xn = x * lax.rsqrt(jnp.mean(x * x, axis=-1, keepdims=True) + EPS)
    return (xn * g) * (1.0 + scale) + shift


def _mod_row(i):
    return jnp.maximum(i // TILES_PER_SEQ - CTX_TILES // TILES_PER_SEQ + 1, 0)


def _mod_spec(chunk):
    return pl.BlockSpec((1, 1, D_MODEL), lambda i, *_: (_mod_row(i) * 6 + chunk, 0, 0))


def _mod_kernel(c_ref, w_ref, b_ref, o_ref):
    cv = c_ref[...]
    s = cv * jax.nn.sigmoid(cv)
    o_ref[...] = jnp.dot(s, w_ref[...], preferred_element_type=F32,
                         precision=lax.Precision.HIGHEST) + b_ref[...]


def _modulation(cvec, w_ada, b_ada):
    tn = 1536
    n = 6 * D_MODEL
    return pl.pallas_call(
        _mod_kernel,
        grid=(DEPTH, n // tn),
        in_specs=[
            pl.BlockSpec((MOD_ROWS, D_MODEL), lambda l, j: (0, 0)),
            pl.BlockSpec((None, D_MODEL, tn), lambda l, j: (l, 0, j)),
            pl.BlockSpec((None, 1, tn), lambda l, j: (l, 0, j)),
        ],
        out_specs=pl.BlockSpec((None, MOD_ROWS, tn), lambda l, j: (l, 0, j)),
        out_shape=jax.ShapeDtypeStruct((DEPTH, MOD_ROWS, n), F32),
        compiler_params=_cparams("parallel", "parallel"),
        name="adaln_modulation",
    )(cvec, w_ada, b_ada.reshape(DEPTH, 1, n))


def _inproj_kernel(x_ref, g_ref, sh_ref, sc_ref, w_ref, cos_ref, sina_ref, sinb_ref, o_ref, *kv_refs):
    i = pl.program_id(0)
    h = _modulated_norm(x_ref[...], g_ref[...], sh_ref[0], sc_ref[0]).astype(BF16)
    chunk = 512
    for c in range(IN_WIDTH // chunk):
        col = c * chunk
        acc = _dot(h, w_ref[:, col:col + chunk])
        for k_i, src in enumerate((COL_NA_K, COL_NA_V, COL_D_K, COL_D_V)):
            if src == col:
                @pl.when(i < CTX_TILES)
                def _(acc=acc, k_i=k_i):
                    for b in range(TM // SEQ):
                        kv_refs[k_i][b] = acc[b * SEQ:(b + 1) * SEQ, :]
        if COL_D_Q <= col < COL_D_V:
            parts = []
            for j in range(chunk // LANES):
                blk = acc[:, j * LANES:(j + 1) * LANES]
                parts.append(blk * cos_ref[...]
                             + pltpu.roll(blk, LANES - 16, 1) * sina_ref[...]
                             + pltpu.roll(blk, 16, 1) * sinb_ref[...])
            acc = jnp.concatenate(parts, axis=1)
        o_ref[:, col:col + chunk] = acc.astype(o_ref.dtype)


def _inproj(x, g, mod, w_bf16, rope_tabs):
    rope_spec = pl.BlockSpec(
        (TM, LANES),
        lambda i: (jnp.where(i < CTX_TILES, TILES_PER_SEQ, (i - CTX_TILES) % TILES_PER_SEQ), 0))
    return pl.pallas_call(
        _inproj_kernel,
        grid=(N_TILES,),
        in_specs=[
            pl.BlockSpec((TM, D_MODEL), lambda i: (i, 0)),
            pl.BlockSpec((1, D_MODEL), lambda i: (0, 0)),
            _mod_spec(0), _mod_spec(1),
            pl.BlockSpec((D_MODEL, IN_WIDTH), lambda i: (0, 0)),
            rope_spec, rope_spec, rope_spec,
        ],
        out_specs=[pl.BlockSpec((TM, IN_WIDTH), lambda i: (i, 0))]
        + [pl.BlockSpec((TM // SEQ, SEQ, 512), lambda i: (jnp.minimum(i, CTX_TILES - 1), 0, 0))] * 4,
        out_shape=[jax.ShapeDtypeStruct((N_TOK, IN_WIDTH), BF16)]
        + [jax.ShapeDtypeStruct((BATCH, SEQ, 512), F32)] * 4,
        compiler_params=_cparams("arbitrary"),
        name="inproj",
    )(x, g, mod, mod, w_bf16, *rope_tabs)


def _lane_half_mask(shape, half):
    lane = lax.broadcasted_iota(I32, shape, len(shape) - 1)
    return (lane < HEAD_DIM) if half == 0 else (lane >= HEAD_DIM)


def _lambda_value(lq1, lk1, lq2, lk2, lam_init):
    a = jnp.sum(lq1 * lk1, axis=-1, keepdims=True)
    b = jnp.sum(lq2 * lk2, axis=-1, keepdims=True)
    return jnp.exp(a) - jnp.exp(b) + lam_init


def _subln(o, g, lam_init):
    on = o * lax.rsqrt(jnp.mean(o * o, axis=-1, keepdims=True) + SUBLN_EPS)
    return (on * g) * (1.0 - lam_init)


def _ctx_attn_kernel(p_ref, lq1, lk1, lq2, lk2, gs_ref, o_ref, *, lam_init):
    lam = _lambda_value(lq1[...], lk1[...], lq2[...], lk2[...], lam_init)
    for hp in range(NA_HEADS // 2):
        q = p_ref[:, COL_NA_Q + hp * LANES:COL_NA_Q + (hp + 1) * LANES]
        k = p_ref[:, COL_NA_K + hp * LANES:COL_NA_K + (hp + 1) * LANES]
        v = p_ref[:, COL_NA_V + hp * LANES:COL_NA_V + (hp + 1) * LANES]
        outs = []
        for half in range(2):
            qm = jnp.where(_lane_half_mask(q.shape, half), q, jnp.zeros_like(q))
            s = _dot_nt(qm, k)
            m = jnp.max(s, axis=-1, keepdims=True)
            e = jnp.exp2(s - m)
            inv = 1.0 / jnp.sum(e, axis=-1, keepdims=True)
            outs.append(_dot(e.astype(BF16), v) * inv)
        o = jnp.where(_lane_half_mask(outs[0].shape, 0), outs[0], outs[1])
        o_ref[:, hp * LANES:(hp + 1) * LANES] = o.astype(o_ref.dtype)
    for h in range(DIFF_HEADS):
        q = p_ref[:, COL_D_Q + h * LANES:COL_D_Q + (h + 1) * LANES]
        k = p_ref[:, COL_D_K + h * LANES:COL_D_K + (h + 1) * LANES]
        v = p_ref[:, COL_D_V + h * LANES:COL_D_V + (h + 1) * LANES]
        ps = []
        for half in range(2):
            qm = jnp.where(_lane_half_mask(q.shape, half), q, jnp.zeros_like(q))
            s = _dot_nt(qm, k)
            m = jnp.max(s, axis=-1, keepdims=True)
            e = jnp.exp2(s - m)
            ps.append(e / jnp.sum(e, axis=-1, keepdims=True))
        a = (ps[0] - lam * ps[1]).astype(BF16)
        o = _subln(_dot(a, v), gs_ref[...], lam_init)
        o_ref[:, NA_WIDTH + h * LANES:NA_WIDTH + (h + 1) * LANES] = o.astype(o_ref.dtype)


def _ctx_attention(p_all, lam_params, g_subln, lam_init):
    vec = pl.BlockSpec((1, HEAD_DIM), lambda b: (0, 0))
    return pl.pallas_call(
        functools.partial(_ctx_attn_kernel, lam_init=lam_init),
        grid=(BATCH,),
        in_specs=[pl.BlockSpec((SEQ, IN_WIDTH), lambda b: (b, 0)), vec, vec, vec, vec,
                  pl.BlockSpec((1, 2 * HEAD_DIM), lambda b: (0, 0))],
        out_specs=pl.BlockSpec((SEQ, D_MODEL), lambda b: (b, 0)),
        out_shape=jax.ShapeDtypeStruct((N_CTX_TOK, D_MODEL), BF16),
        compiler_params=_cparams("parallel"),
        name="ctx_attention",
    )(p_all, *lam_params, g_subln)


def _bias_expand_kernel(rpb_ref, o_ref):
    n = GRID_W * GRID_W
    c = lax.broadcasted_iota(I32, (LANES, n), 0)
    j = lax.broadcasted_iota(I32, (LANES, n), 1)
    qc = jnp.right_shift(j, 6)
    kc = jnp.bitwise_and(j, GRID_W - 1)
    dc = jnp.clip(kc - qc + (NA_WIN_W - 1), 0, 2 * NA_WIN_W - 2)
    onehot = jnp.where(c == dc, 1.0, 0.0).astype(F32)
    t = jnp.dot(rpb_ref[...], onehot, preferred_element_type=F32, precision=lax.Precision.HIGHEST)
    j1 = lax.broadcasted_iota(I32, t.shape, 1)
    qc1 = jnp.right_shift(j1, 6)
    kc1 = jnp.bitwise_and(j1, GRID_W - 1)
    cs = jnp.clip(qc1 - NA_WIN_W // 2, 0, GRID_W - NA_WIN_W)
    valid = (kc1 >= cs) & (kc1 < cs + NA_WIN_W)
    o_ref[...] = jnp.where(valid, t, MASK_VALUE)


def _na_window_plan():
    dr = np.zeros((3, NA_R, NA_WIN_ROWS), np.int32)
    ok = np.zeros((3, NA_R, NA_WIN_ROWS), bool)
    for t, r0 in enumerate((0, NA_R, GRID_ROWS - NA_R)):
        lo = min(max(r0 - NA_WIN_H // 2, 0), GRID_ROWS - NA_WIN_ROWS)
        for jr in range(NA_R):
            r = r0 + jr
            rs = min(max(r - NA_WIN_H // 2, 0), GRID_ROWS - NA_WIN_H)
            for i in range(NA_WIN_ROWS):
                key_row = lo + i
                if rs <= key_row < rs + NA_WIN_H:
                    ok[t, jr, i] = True
                    dr[t, jr, i] = key_row - r + NA_WIN_H - 1
    return dr, ok


def _na_bias_tables(rpb):
    n_dr = 2 * NA_WIN_H - 1
    rows = DEPTH * NA_HEADS * n_dr
    rpb_pad = jnp.pad(rpb.reshape(rows, 2 * NA_WIN_W - 1), ((0, 0), (0, LANES - (2 * NA_WIN_W - 1))))
    tc = pl.pallas_call(
        _bias_expand_kernel,
        out_shape=jax.ShapeDtypeStruct((rows, GRID_W * GRID_W), F32),
        compiler_params=pltpu.CompilerParams(vmem_limit_bytes=VMEM_LIMIT),
        name="na_bias_expand",
    )(rpb_pad)
    tc = tc.reshape(DEPTH, NA_HEADS, n_dr, GRID_W, GRID_W)
    dr, ok = _na_window_plan()
    t = jnp.take(tc, jnp.asarray(dr.reshape(-1)), axis=2)
    t = t.reshape(DEPTH, NA_HEADS, 3, NA_R, NA_WIN_ROWS, GRID_W, GRID_W)
    t = jnp.where(jnp.asarray(ok)[None, None, :, :, :, None, None], t * LOG2E, MASK_VALUE)
    t = t.transpose(0, 2, 1, 3, 5, 4, 6)
    return t.reshape(DEPTH, 3, NA_HEADS, NA_R * GRID_W, NA_WIN_ROWS * GRID_W).astype(BF16)


def _na_attn_kernel(q_ref, k_ref, v_ref, kc_ref, vc_ref, bias_ref, o_ref):
    j = pl.program_id(1)
    lo = jnp.clip(j * NA_R - NA_WIN_H // 2, 0, GRID_ROWS - NA_WIN_ROWS)
    k0 = pl.multiple_of(lo * GRID_W, GRID_W)
    step_type = jnp.where(j == 0, 0, jnp.where(j == NA_STEPS - 1, 2, 1))
    n_keys = NA_WIN_ROWS * GRID_W

    def scores(head):
        lanes = slice((head // 2) * LANES, (head // 2 + 1) * LANES)
        q = q_ref[:, lanes]
        qm = jnp.where(_lane_half_mask(q.shape, head % 2), q, jnp.zeros_like(q))
        s_w = _dot_nt(qm, k_ref[pl.ds(k0, n_keys), lanes]) + bias_ref[step_type, head].astype(F32)
        s_c = _dot_nt(qm, kc_ref[:, lanes].astype(BF16))
        return s_w, s_c

    nxt = scores(0)
    outs = []
    for head in range(NA_HEADS):
        s_w, s_c = nxt
        if head + 1 < NA_HEADS:
            nxt = scores(head + 1)
        lanes = slice((head // 2) * LANES, (head // 2 + 1) * LANES)
        m = jnp.maximum(jnp.max(s_w, axis=-1, keepdims=True), jnp.max(s_c, axis=-1, keepdims=True))
        e_w = jnp.exp2(s_w - m)
        e_c = jnp.exp2(s_c - m)
        inv = 1.0 / (jnp.sum(e_w, axis=-1, keepdims=True) + jnp.sum(e_c, axis=-1, keepdims=True))
        pv = (_dot(e_w.astype(BF16), v_ref[pl.ds(k0, n_keys), lanes])
              + _dot(e_c.astype(BF16), vc_ref[:, lanes].astype(BF16)))
        outs.append(pv * inv)
        if head % 2 == 1:
            o = jnp.where(_lane_half_mask(outs[0].shape, 0), outs[0], outs[1])
            o_ref[:, lanes] = o.astype(o_ref.dtype)
            outs = []


def _na_attention(p_all, cache_k, cache_v, bias_tab, layer):
    q_rows = NA_R * GRID_W
    q_blk0 = N_CTX_TOK // q_rows
    seq_blk0 = N_CTX_TOK // DEC_SEQ
    cache_spec = pl.BlockSpec((None, None, PAST_LEN, NA_WIDTH), lambda b, j: (b, layer, 0, 0))
    return pl.pallas_call(
        _na_attn_kernel,
        grid=(DEC_BATCH, NA_STEPS),
        in_specs=[
            pl.BlockSpec((q_rows, NA_WIDTH), lambda b, j: (q_blk0 + b * NA_STEPS + j, COL_NA_Q // NA_WIDTH)),
            pl.BlockSpec((DEC_SEQ, NA_WIDTH), lambda b, j: (seq_blk0 + b, COL_NA_K // NA_WIDTH)),
            pl.BlockSpec((DEC_SEQ, NA_WIDTH), lambda b, j: (seq_blk0 + b, COL_NA_V // NA_WIDTH)),
            cache_spec, cache_spec,
            pl.BlockSpec((None, 3, NA_HEADS, q_rows, NA_WIN_ROWS * GRID_W), lambda b, j: (layer, 0, 0, 0, 0)),
        ],
        out_specs=pl.BlockSpec((q_rows, NA_WIDTH), lambda b, j: (b * NA_STEPS + j, 0)),
        out_shape=jax.ShapeDtypeStruct((N_LAT_TOK, NA_WIDTH), BF16),
        compiler_params=_cparams("parallel", "arbitrary"),
        name="latent_na_attention",
    )(p_all, p_all, p_all, cache_k, cache_v, bias_tab)


def _diff_attn_kernel(q_ref, k_ref, v_ref, kc_ref, vc_ref, lq1, lk1, lq2, lk2, gs_ref, o_ref,
                      *, lam_init):
    lam = _lambda_value(lq1[...], lk1[...], lq2[...], lk2[...], lam_init)
    k = k_ref[...]
    v = v_ref[...]
    kc = kc_ref[...].astype(BF16)
    vc = vc_ref[...].astype(BF16)
    n_sub = TQ_DIFF // DIFF_SUB

    def scores(i):
        q = q_ref[i * DIFF_SUB:(i + 1) * DIFF_SUB, :]
        out = []
        for half in range(2):
            qm = jnp.where(_lane_half_mask(q.shape, half), q, jnp.zeros_like(q))
            out.append((_dot_nt(qm, k), _dot_nt(qm, kc)))
        return out

    nxt = scores(0)
    for i in range(n_sub):
        cur = nxt
        if i + 1 < n_sub:
            nxt = scores(i + 1)
        probs = []
        for s_l, s_c in cur:
            m = jnp.maximum(jnp.max(s_l, axis=-1, keepdims=True), jnp.max(s_c, axis=-1, keepdims=True))
            e_l = jnp.exp2(s_l - m)
            e_c = jnp.exp2(s_c - m)
            inv = 1.0 / (jnp.sum(e_l, axis=-1, keepdims=True) + jnp.sum(e_c, axis=-1, keepdims=True))
            probs.append((e_l, e_c, inv))
        w1 = probs[0][2]
        w2 = lam * probs[1][2]
        a_l = (probs[0][0] * w1 - probs[1][0] * w2).astype(BF16)
        a_c = (probs[0][1] * w1 - probs[1][1] * w2).astype(BF16)
        o = _dot(a_l, v) + _dot(a_c, vc)
        o_ref[i * DIFF_SUB:(i + 1) * DIFF_SUB, :] = _subln(o, gs_ref[...], lam_init).astype(o_ref.dtype)


def _diff_attention(p_all, cache_k, cache_v, lam_params, g_subln, lam_init, layer):
    steps = DEC_SEQ // TQ_DIFF
    q_blk0 = N_CTX_TOK // TQ_DIFF
    seq_blk0 = N_CTX_TOK // DEC_SEQ
    vec = pl.BlockSpec((1, HEAD_DIM), lambda b, h, j: (0, 0))
    cache_spec = pl.BlockSpec((None, None, PAST_LEN, LANES), lambda b, h, j: (b, layer, 0, h))
    return pl.pallas_call(
        functools.partial(_diff_attn_kernel, lam_init=lam_init),
        grid=(DEC_BATCH, DIFF_HEADS, steps),
        in_specs=[
            pl.BlockSpec((TQ_DIFF, LANES), lambda b, h, j: (q_blk0 + b * steps + j, COL_D_Q // LANES + h)),
            pl.BlockSpec((DEC_SEQ, LANES), lambda b, h, j: (seq_blk0 + b, COL_D_K // LANES + h)),
            pl.BlockSpec((DEC_SEQ, LANES), lambda b, h, j: (seq_blk0 + b, COL_D_V // LANES + h)),
            cache_spec, cache_spec, vec, vec, vec, vec,
            pl.BlockSpec((1, 2 * HEAD_DIM), lambda b, h, j: (0, 0)),
        ],
        out_specs=pl.BlockSpec((TQ_DIFF, LANES), lambda b, h, j: (b * steps + j, h)),
        out_shape=jax.ShapeDtypeStruct((N_LAT_TOK, DIFF_WIDTH), BF16),
        compiler_params=_cparams("parallel", "parallel", "arbitrary"),
        name="latent_diff_attention",
    )(p_all, p_all, p_all, cache_k, cache_v, *lam_params, g_subln)


def _outproj_kernel(x_ref, oc_ref, ona_ref, od_ref, ga_ref, w_ref, y_ref):
    i = pl.program_id(0)

    @pl.when(i < CTX_TILES)
    def _():
        y_ref[...] = x_ref[...] + ga_ref[0] * _dot(oc_ref[...], w_ref[...])

    @pl.when(i >= CTX_TILES)
    def _():
        acc = _dot(ona_ref[...], w_ref[:NA_WIDTH, :]) + _dot(od_ref[...], w_ref[NA_WIDTH:, :])
        y_ref[...] = x_ref[...] + ga_ref[0] * acc


def _outproj(x, o_ctx, o_na, o_d, mod, w_bf16):
    lat_idx = lambda i: (jnp.maximum(i - CTX_TILES, 0), 0)
    return pl.pallas_call(
        _outproj_kernel,
        grid=(N_TILES,),
        in_specs=[
            pl.BlockSpec((TM, D_MODEL), lambda i: (i, 0)),
            pl.BlockSpec((TM, D_MODEL), lambda i: (jnp.minimum(i, CTX_TILES - 1), 0)),
            pl.BlockSpec((TM, NA_WIDTH), lat_idx),
            pl.BlockSpec((TM, DIFF_WIDTH), lat_idx),
            _mod_spec(2),
            pl.BlockSpec((D_MODEL, D_MODEL), lambda i: (0, 0)),
        ],
        out_specs=pl.BlockSpec((TM, D_MODEL), lambda i: (i, 0)),
        out_shape=jax.ShapeDtypeStruct((N_TOK, D_MODEL), F32),
        compiler_params=_cparams("parallel"),
        name="outproj_residual",
    )(x, o_ctx, o_na, o_d, mod, w_bf16)


def _swiglu_tile(hb, wg, wu):
    g = _dot(hb, wg)
    u = _dot(hb, wu)
    return (g * jax.nn.sigmoid(g)) * u


def _ffn_kernel(x_ref, g_ref, sh_ref, sc_ref, ga_ref, wg_ref, wu_ref, wd_ref, y_ref, h_scr, acc_scr):
    f = pl.program_id(1)

    @pl.when(f == 0)
    def _():
        h = _modulated_norm(x_ref[...], g_ref[...], sh_ref[0], sc_ref[0])
        h_scr[...] = h.astype(BF16)
        acc_scr[...] = jnp.zeros_like(acc_scr)

    a = _swiglu_tile(h_scr[...], wg_ref[...], wu_ref[...])
    acc_scr[...] += _dot(a.astype(BF16), wd_ref[...])

    @pl.when(f == pl.num_programs(1) - 1)
    def _():
        y_ref[...] = x_ref[...] + ga_ref[0] * acc_scr[...]


def _ffn_dense(x, g, mod, wg, wu, wd):
    tf = FF_TILE_DENSE
    return pl.pallas_call(
        _ffn_kernel,
        grid=(N_TILES, D_FF // tf),
        in_specs=[
            pl.BlockSpec((TM, D_MODEL), lambda i, f: (i, 0)),
            pl.BlockSpec((1, D_MODEL), lambda i, f: (0, 0)),
            _mod_spec(3), _mod_spec(4), _mod_spec(5),
            pl.BlockSpec((D_MODEL, tf), lambda i, f: (0, f)),
            pl.BlockSpec((D_MODEL, tf), lambda i, f: (0, f)),
            pl.BlockSpec((tf, D_MODEL), lambda i, f: (f, 0)),
        ],
        out_specs=pl.BlockSpec((TM, D_MODEL), lambda i, f: (i, 0)),
        out_shape=jax.ShapeDtypeStruct((N_TOK, D_MODEL), F32),
        scratch_shapes=[pltpu.VMEM((TM, D_MODEL), BF16), pltpu.VMEM((TM, D_MODEL), F32)],
        compiler_params=_cparams("parallel", "arbitrary"),
        name="ffn_dense",
    )(x, g, mod, mod, mod, wg, wu, wd)


ROUTE_E1, ROUTE_E2, ROUTE_G1, ROUTE_G2, ROUTE_R1, ROUTE_R2 = range(6)


def _router_kernel(x_ref, g_ref, sh_ref, sc_ref, wr_ref, h_ref, route_ref, cnt_ref, carry_scr):
    i = pl.program_id(0)

    @pl.when(i == 0)
    def _():
        carry_scr[...] = jnp.zeros_like(carry_scr)

    h = _modulated_norm(x_ref[...], g_ref[...], sh_ref[0], sc_ref[0])
    h_ref[...] = h
    logits = jnp.dot(h, wr_ref[...], preferred_element_type=F32, precision=lax.Precision.HIGHEST)
    lane = lax.broadcasted_iota(I32, logits.shape, 1)
    logits = jnp.where(lane < N_EXPERTS, logits, MASK_VALUE)
    m1 = jnp.max(logits, axis=-1, keepdims=True)
    i1 = jnp.min(jnp.where(logits == m1, lane, LANES), axis=-1, keepdims=True)
    rest = jnp.where(lane == i1, MASK_VALUE, logits)
    m2 = jnp.max(rest, axis=-1, keepdims=True)
    i2 = jnp.min(jnp.where(rest == m2, lane, LANES), axis=-1, keepdims=True)
    e2 = jnp.exp(m2 - m1)
    g1 = 1.0 / (1.0 + e2)
    g2 = e2 / (1.0 + e2)

    hit1 = lane == i1
    hit2 = lane == i2
    onehot = jnp.where(hit1 | hit2, 1.0, 0.0)
    row = lax.broadcasted_iota(I32, (TM, TM), 0)
    col = lax.broadcasted_iota(I32, (TM, TM), 1)
    lower = jnp.where(row > col, 1.0, 0.0).astype(BF16)
    before = _dot(lower, onehot.astype(BF16)) + carry_scr[...]
    r1 = jnp.sum(jnp.where(hit1, before, 0.0), axis=-1, keepdims=True)
    r2 = jnp.sum(jnp.where(hit2, before, 0.0), axis=-1, keepdims=True)
    carry_scr[...] += jnp.sum(onehot, axis=0, keepdims=True)

    out = jnp.zeros(logits.shape, F32)
    for slot, val in ((ROUTE_E1, i1.astype(F32)), (ROUTE_E2, i2.astype(F32)), (ROUTE_G1, g1),
                      (ROUTE_G2, g2), (ROUTE_R1, r1), (ROUTE_R2, r2)):
        out = jnp.where(lane == slot, val, out)
    route_ref[...] = out

    @pl.when(i == pl.num_programs(0) - 1)
    def _():
        cnt_ref[...] = jnp.broadcast_to(carry_scr[...], cnt_ref.shape)


def _router(x, g, mod, w_router_pad):
    return pl.pallas_call(
        _router_kernel,
        grid=(N_TILES,),
        in_specs=[
            pl.BlockSpec((TM, D_MODEL), lambda i: (i, 0)),
            pl.BlockSpec((1, D_MODEL), lambda i: (0, 0)),
            _mod_spec(3), _mod_spec(4),
            pl.BlockSpec((D_MODEL, LANES), lambda i: (0, 0)),
        ],
        out_specs=[pl.BlockSpec((TM, D_MODEL), lambda i: (i, 0)),
                   pl.BlockSpec((TM, LANES), lambda i: (i, 0)),
                   pl.BlockSpec((8, LANES), lambda i: (0, 0))],
        out_shape=[jax.ShapeDtypeStruct((N_TOK, D_MODEL), F32),
                   jax.ShapeDtypeStruct((N_TOK, LANES), F32),
                   jax.ShapeDtypeStruct((8, LANES), F32)],
        scratch_shapes=[pltpu.VMEM((1, LANES), F32)],
        compiler_params=_cparams("arbitrary"),
        name="moe_router",
    )(x, g, mod, mod, w_router_pad)


def _moe_plan(route, counts):
    cnt = counts[0, :N_EXPERTS].astype(I32)
    tiles = (cnt + TM_MOE - 1) // TM_MOE
    tile_end = jnp.cumsum(tiles)
    group_start = (tile_end - tiles) * TM_MOE
    e = route[:, ROUTE_E1:ROUTE_E2 + 1].astype(I32)
    r = route[:, ROUTE_R1:ROUTE_R2 + 1].astype(I32)
    slot = (group_start[e] + r).T.reshape(-1)
    tok = jnp.tile(jnp.arange(N_TOK, dtype=I32), 2)
    src_tok = jnp.zeros(((MOE_TILES + 2) * TM_MOE,), I32).at[slot].set(tok)
    n_valid = tile_end[-1:]
    t = jnp.arange(MOE_TILES, dtype=I32)
    tile_expert = jnp.minimum(jnp.sum((t[:, None] >= tile_end[None, :]).astype(I32), axis=1),
                              N_EXPERTS - 1)
    last_expert = tile_expert[jnp.maximum(n_valid[0] - 1, 0)]
    tile_expert = jnp.where(t < n_valid[0], tile_expert, last_expert)
    return tile_expert, n_valid, src_tok, slot


def _moe_kernel(te_ref, nv_ref, src_ref, h_hbm, wg_ref, wu_ref, wd_ref, y_ref, hrow_scr, hb_scr, sems):
    t = pl.program_id(0)
    f = pl.program_id(1)
    valid = t < nv_ref[0]
    last_t = pl.num_programs(0) - 1
    last_f = N_FF_MOE - 1

    def row_copy(tile, row, buf):
        tok = src_ref[tile * TM_MOE + row]
        return pltpu.make_async_copy(h_hbm.at[pl.ds(tok, 1), :], hrow_scr.at[buf, pl.ds(row, 1), :],
                                     sems.at[buf])

    def wait_tile(buf):
        pltpu.make_async_copy(h_hbm.at[pl.ds(0, TM_MOE), :],
                              hrow_scr.at[buf, pl.ds(0, TM_MOE), :], sems.at[buf]).wait()
        for k in range(TM_MOE, GATHER_ROWS):
            pltpu.make_async_copy(h_hbm.at[pl.ds(0, 1), :],
                                  hrow_scr.at[buf, pl.ds(k, 1), :], sems.at[buf]).wait()

    def issue_next_chunk():
        for j in range(GATHER_CHUNK):
            row_copy(t + 1, f * GATHER_CHUNK + j, (t + 1) % 2).start()

    @pl.when((t == 0) & (f == 0))
    def _():
        def issue(j, carry):
            row_copy(0, j, 0).start()
            return carry

        lax.fori_loop(0, GATHER_ROWS, issue, 0, unroll=7)

    @pl.when(f == 0)
    def _():
        wait_tile(t % 2)
        hb_scr[...] = hrow_scr[t % 2, :TM_MOE, :].astype(BF16)
        y_ref[...] = jnp.zeros_like(y_ref)

    @pl.when(valid)
    def _():
        issue_next_chunk()
        a = _swiglu_tile(hb_scr[...], wg_ref[...].astype(BF16), wu_ref[...].astype(BF16))
        y_ref[...] += _dot(a.astype(BF16), wd_ref[...].astype(BF16))

    @pl.when(jnp.logical_not(valid))
    def _():
        issue_next_chunk()

    @pl.when((t == last_t) & (f == last_f))
    def _():
        wait_tile((t + 1) % 2)


def _moe_experts(tile_expert, n_valid, src_tok, h, wg, wu, wd, layer):
    tf = FF_TILE_MOE
    buf_rows = -(-GATHER_ROWS // 8) * 8

    def f_eff(t, f, nv):
        return jnp.where(t < nv[0], f, N_FF_MOE - 1)

    grid_spec = pltpu.PrefetchScalarGridSpec(
        num_scalar_prefetch=3,
        grid=(MOE_TILES, N_FF_MOE),
        in_specs=[
            pl.BlockSpec(memory_space=pl.ANY),
            pl.BlockSpec((None, None, D_MODEL, tf), lambda t, f, te, nv, st: (layer, te[t], 0, f_eff(t, f, nv))),
            pl.BlockSpec((None, None, D_MODEL, tf), lambda t, f, te, nv, st: (layer, te[t], 0, f_eff(t, f, nv))),
            pl.BlockSpec((None, None, tf, D_MODEL), lambda t, f, te, nv, st: (layer, te[t], f_eff(t, f, nv), 0)),
        ],
        out_specs=pl.BlockSpec((TM_MOE, D_MODEL), lambda t, f, te, nv, st: (t, 0)),
        scratch_shapes=[pltpu.VMEM((2, buf_rows, D_MODEL), F32), pltpu.VMEM((TM_MOE, D_MODEL), BF16),
                        pltpu.SemaphoreType.DMA((2,))],
    )
    return pl.pallas_call(
        _moe_kernel,
        grid_spec=grid_spec,
        out_shape=jax.ShapeDtypeStruct((MOE_TILES * TM_MOE, D_MODEL), F32),
        compiler_params=_cparams("arbitrary", "arbitrary"),
        name="moe_experts",
    )(tile_expert, n_valid, src_tok, h, wg, wu, wd)


def _moe_combine_kernel(slot_ref, x_ref, route_ref, ga_ref, y_hbm, o_ref, rows_scr, sem):
    i = pl.program_id(0)
    for c in range(2):
        base = c * N_TOK + i * TM

        def issue(j, carry, base=base, c=c):
            s = slot_ref[base + j]
            pltpu.make_async_copy(y_hbm.at[pl.ds(s, 1), :], rows_scr.at[c, pl.ds(j, 1), :], sem).start()
            return carry

        lax.fori_loop(0, TM, issue, 0, unroll=8)
    for c in range(2):
        pltpu.make_async_copy(y_hbm.at[pl.ds(0, TM), :], rows_scr.at[c], sem).wait()
    route = route_ref[...]
    g1 = route[:, ROUTE_G1:ROUTE_G1 + 1]
    g2 = route[:, ROUTE_G2:ROUTE_G2 + 1]
    o_ref[...] = x_ref[...] + ga_ref[0] * (g1 * rows_scr[0] + g2 * rows_scr[1])


def _moe_combine(slot, x, route, mod, y):
    grid_spec = pltpu.PrefetchScalarGridSpec(
        num_scalar_prefetch=1,
        grid=(N_TILES,),
        in_specs=[
            pl.BlockSpec((TM, D_MODEL), lambda i, s: (i, 0)),
            pl.BlockSpec((TM, LANES), lambda i, s: (i, 0)),
            _mod_spec(5),
            pl.BlockSpec(memory_space=pl.ANY),
        ],
        out_specs=pl.BlockSpec((TM, D_MODEL), lambda i, s: (i, 0)),
        scratch_shapes=[pltpu.VMEM((2, TM, D_MODEL), F32), pltpu.SemaphoreType.DMA],
    )
    return pl.pallas_call(
        _moe_combine_kernel,
        grid_spec=grid_spec,
        out_shape=jax.ShapeDtypeStruct((N_TOK, D_MODEL), F32),
        compiler_params=_cparams("arbitrary"),
        name="moe_combine",
    )(slot, x, route, mod, y)


def _final_norm_kernel(x_ref, g_ref, y_ref):
    x = x_ref[...]
    y_ref[...] = (x * lax.rsqrt(jnp.mean(x * x, axis=-1, keepdims=True) + EPS)) * g_ref[...]


def _final_norm(x, g, tile0, n_tok):
    return pl.pallas_call(
        _final_norm_kernel,
        grid=(n_tok // TM,),
        in_specs=[pl.BlockSpec((TM, D_MODEL), lambda i: (tile0 + i, 0)),
                  pl.BlockSpec((1, D_MODEL), lambda i: (0, 0))],
        out_specs=pl.BlockSpec((TM, D_MODEL), lambda i: (i, 0)),
        out_shape=jax.ShapeDtypeStruct((n_tok, D_MODEL), F32),
        compiler_params=_cparams("parallel"),
        name="final_norm",
    )(x, g)


def _rope_tables():
    t = jnp.arange(DEC_SEQ)
    row = (t // GRID_W).astype(F32)
    col = (t % GRID_W).astype(F32)
    half = HEAD_DIM // 2
    inv = 1.0 / (ROPE_THETA ** (jnp.arange(0, half, 2, dtype=F32) / half))
    ar = row[:, None] * inv[None]
    ac = col[:, None] * inv[None]
    ang = jnp.concatenate([ar, ar, ac, ac], axis=-1)
    ang = jnp.concatenate([ang, jnp.zeros((TM, HEAD_DIM), F32)], axis=0)
    cos = jnp.tile(jnp.cos(ang), (1, LANES // HEAD_DIM))
    sin = jnp.tile(jnp.sin(ang), (1, LANES // HEAD_DIM))
    first_half = (jnp.arange(LANES) % 32) < 16
    sin_a = jnp.where(first_half[None, :], -sin, 0.0)
    sin_b = jnp.where(first_half[None, :], 0.0, sin)
    return cos, sin_a, sin_b


def kernel(x_prompt, x_sample, c, cache_na_k, cache_na_v, cache_diff_k, cache_diff_v, c_ctx, w_ada, b_ada, g_mix, w_in, rpb, lam_q1, lam_k1, lam_q2, lam_k2, g_subln, w_out, g_ffn, w_ffn_gate, w_ffn_up, w_ffn_down, w_router, w_moe_gate, w_moe_up, w_moe_down, g_final):
    x = jnp.concatenate([x_prompt.reshape(N_CTX_TOK, D_MODEL), x_sample.reshape(N_LAT_TOK, D_MODEL)])

    cvec = jnp.zeros((MOD_ROWS, D_MODEL), F32).at[0].set(c_ctx).at[1:1 + DEC_BATCH].set(c)
    mod_all = _modulation(cvec, w_ada, b_ada).reshape(DEPTH, MOD_ROWS * 6, 1, D_MODEL)

    col = jnp.arange(IN_WIDTH)
    is_q = (col < COL_NA_K) | ((col >= COL_D_Q) & (col < COL_D_K))
    q_scale = jnp.where(is_q, ATTN_SCALE * LOG2E, 1.0).astype(F32)
    w_in_b = (w_in * q_scale[None, None, :]).astype(BF16)
    w_out_b = w_out.astype(BF16)
    w_fg, w_fu, w_fd = (w.astype(BF16) for w in (w_ffn_gate, w_ffn_up, w_ffn_down))
    w_router_pad = jnp.pad(w_router, ((0, 0), (0, 0), (0, LANES - N_EXPERTS)))

    rope_tabs = _rope_tables()
    bias_tabs = _na_bias_tables(rpb)
    cna_k = cache_na_k.reshape(DEC_BATCH, DEPTH, PAST_LEN, NA_WIDTH)
    cna_v = cache_na_v.reshape(DEC_BATCH, DEPTH, PAST_LEN, NA_WIDTH)
    cd_k = cache_diff_k.reshape(DEC_BATCH, DEPTH, PAST_LEN, DIFF_WIDTH)
    cd_v = cache_diff_v.reshape(DEC_BATCH, DEPTH, PAST_LEN, DIFF_WIDTH)

    kv_layers = []
    for l in range(DEPTH):
        lam_init = 0.8 - 0.6 * math.exp(-0.3 * l)
        mod = mod_all[l]
        g_mix_l = g_mix[l][None, :]
        g_ffn_l = g_ffn[l][None, :]
        g_sub_l = g_subln[l][None, :]
        lam_params = tuple(p[l][None, :] for p in (lam_q1, lam_k1, lam_q2, lam_k2))

        p_all, *kv_ctx = _inproj(x, g_mix_l, mod, w_in_b[l], rope_tabs)
        kv_layers.append(kv_ctx)
        o_ctx = _ctx_attention(p_all, lam_params, g_sub_l, lam_init)
        o_na = _na_attention(p_all, cna_k, cna_v, bias_tabs, l)
        o_d = _diff_attention(p_all, cd_k, cd_v, lam_params, g_sub_l, lam_init, l)
        x = _outproj(x, o_ctx, o_na, o_d, mod, w_out_b[l])

        i = l // 2
        if l % 2 == 0:
            x = _ffn_dense(x, g_ffn_l, mod, w_fg[i], w_fu[i], w_fd[i])
        else:
            h, route, counts = _router(x, g_ffn_l, mod, w_router_pad[i])
            tile_expert, n_valid, src_tok, slot = _moe_plan(route, counts)
            y = _moe_experts(tile_expert, n_valid, src_tok, h, w_moe_gate, w_moe_up, w_moe_down, i)
            x = _moe_combine(slot, x, route, mod, y)

    g_fin = g_final[None, :]
    y_prompt = _final_norm(x, g_fin, 0, N_CTX_TOK).reshape(BATCH, SEQ, D_MODEL)
    y_sample = _final_norm(x, g_fin, CTX_TILES, N_LAT_TOK).reshape(DEC_BATCH, DEC_SEQ, D_MODEL)

    kv = [jnp.stack([layer_kv[k] for layer_kv in kv_layers], axis=1) for k in range(4)]
    new_na_k = kv[0].reshape(BATCH, DEPTH, SEQ, NA_HEADS, HEAD_DIM)
    new_na_v = kv[1].reshape(BATCH, DEPTH, SEQ, NA_HEADS, HEAD_DIM)
    new_diff_k = kv[2].reshape(BATCH, DEPTH, SEQ, DIFF_HEADS, 2, HEAD_DIM)
    new_diff_v = kv[3].reshape(BATCH, DEPTH, SEQ, DIFF_HEADS, 2 * HEAD_DIM)
    return (y_prompt, y_sample, new_na_k, new_na_v, new_diff_k, new_diff_v)
```

```python
import functools
import math

import numpy as np
import jax
import jax.numpy as jnp
from jax import lax
from jax.experimental import pallas as pl
from jax.experimental.pallas import tpu as pltpu

F32 = jnp.float32
BF16 = jnp.bfloat16
I32 = jnp.int32

D_MODEL = 1024
DEPTH = 4
BATCH = 16
SEQ = 256
DEC_BATCH = 8
DEC_SEQ = 2048
PAST_LEN = 256
GRID_W = 64
GRID_ROWS = DEC_SEQ // GRID_W
HEAD_DIM = 64
NA_HEADS = 8
NA_WIDTH = 512
DIFF_HEADS = 4
DIFF_WIDTH = 512
IN_WIDTH = 3072
NA_WIN_H = 8
NA_WIN_W = 16
ROPE_THETA = 10000.0
D_FF = 2816
N_EXPERTS = 8
D_FF_EXPERT = 3584
EPS = 1e-6
SUBLN_EPS = 1e-5
ATTN_SCALE = HEAD_DIM ** -0.5

LANES = 128
N_CTX_TOK = BATCH * SEQ
N_LAT_TOK = DEC_BATCH * DEC_SEQ
N_TOK = N_CTX_TOK + N_LAT_TOK
MOD_ROWS = 16
MASK_VALUE = -1e30

COL_NA_Q, COL_NA_K, COL_NA_V = 0, 512, 1024
COL_D_Q, COL_D_K, COL_D_V = 1536, 2048, 2560

TM = 512
N_TILES = N_TOK // TM
CTX_TILES = N_CTX_TOK // TM
TILES_PER_SEQ = DEC_SEQ // TM
FF_TILE_DENSE = 1408
FF_TILE_MOE = 512
N_FF_MOE = D_FF_EXPERT // FF_TILE_MOE
TM_MOE = 1024
MOE_W_SPLIT = 2
TQ_DIFF = 512
DIFF_SUB = 128
NA_R = 4
NA_WIN_ROWS = 12
NA_STEPS = GRID_ROWS // NA_R
MOE_SLOTS = 2 * N_TOK
MOE_TILES = MOE_SLOTS // TM_MOE + N_EXPERTS
GATHER_CHUNK = -(-TM_MOE // N_FF_MOE)
GATHER_ROWS = GATHER_CHUNK * N_FF_MOE
LOG2E = 1.4426950408889634
VMEM_LIMIT = 56 * 1024 * 1024


def _cparams(*sem):
    return pltpu.CompilerParams(dimension_semantics=sem, vmem_limit_bytes=VMEM_LIMIT)


def _dot(a, b):
    return jnp.dot(a, b, preferred_element_type=F32)


def _dot_nt(a, b):
    return lax.dot_general(a, b, (((1,), (1,)), ((), ())), preferred_element_type=F32)


def _modulated_norm(x, g, shift, scale):
    xn = x * lax.rsqrt(jnp.mean(x * x, axis=-1, keepdims=True) + EPS)
    return (xn * g) * (1.0 + scale) + shift


def _mod_row(i):
    return jnp.maximum(i // TILES_PER_SEQ - CTX_TILES // TILES_PER_SEQ + 1, 0)


def _mod_spec(chunk):
    return pl.BlockSpec((1, 1, D_MODEL), lambda i, *_: (_mod_row(i) * 6 + chunk, 0, 0))


def _mod_kernel(c_ref, w_ref, b_ref, o_ref):
    cv = c_ref[...]
    s = cv * jax.nn.sigmoid(cv)
    o_ref[...] = jnp.dot(s, w_ref[...], preferred_element_type=F32,
                         precision=lax.Precision.HIGHEST) + b_ref[...]


def _modulation(cvec, w_ada, b_ada):
    tn = 1536
    n = 6 * D_MODEL
    return pl.pallas_call(
        _mod_kernel,
        grid=(DEPTH, n // tn),
        in_specs=[
            pl.BlockSpec((MOD_ROWS, D_MODEL), lambda l, j: (0, 0)),
            pl.BlockSpec((None, D_MODEL, tn), lambda l, j: (l, 0, j)),
            pl.BlockSpec((None, 1, tn), lambda l, j: (l, 0, j)),
        ],
        out_specs=pl.BlockSpec((None, MOD_ROWS, tn), lambda l, j: (l, 0, j)),
        out_shape=jax.ShapeDtypeStruct((DEPTH, MOD_ROWS, n), F32),
        compiler_params=_cparams("parallel", "parallel"),
        name="adaln_modulation",
    )(cvec, w_ada, b_ada.reshape(DEPTH, 1, n))


def _inproj_kernel(x_ref, g_ref, sh_ref, sc_ref, w_ref, cos_ref, sina_ref, sinb_ref, o_ref, *kv_refs):
    i = pl.program_id(0)
    h = _modulated_norm(x_ref[...], g_ref[...], sh_ref[0], sc_ref[0]).astype(BF16)
    chunk = 512
    for c in range(IN_WIDTH // chunk):
        col = c * chunk
        acc = _dot(h, w_ref[:, col:col + chunk])
        for k_i, src in enumerate((COL_NA_K, COL_NA_V, COL_D_K, COL_D_V)):
            if src == col:
                @pl.when(i < CTX_TILES)
                def _(acc=acc, k_i=k_i):
                    for b in range(TM // SEQ):
                        kv_refs[k_i][b] = acc[b * SEQ:(b + 1) * SEQ, :]
        if COL_D_Q <= col < COL_D_V:
            parts = []
            for j in range(chunk // LANES):
                blk = acc[:, j * LANES:(j + 1) * LANES]
                parts.append(blk * cos_ref[...]
                             + pltpu.roll(blk, LANES - 16, 1) * sina_ref[...]
                             + pltpu.roll(blk, 16, 1) * sinb_ref[...])
            acc = jnp.concatenate(parts, axis=1)
        o_ref[:, col:col + chunk] = acc.astype(o_ref.dtype)


def _inproj(x, g, mod, w_bf16, rope_tabs):
    rope_spec = pl.BlockSpec(
        (TM, LANES),
        lambda i: (jnp.where(i < CTX_TILES, TILES_PER_SEQ, (i - CTX_TILES) % TILES_PER_SEQ), 0))
    return pl.pallas_call(
        _inproj_kernel,
        grid=(N_TILES,),
        in_specs=[
            pl.BlockSpec((TM, D_MODEL), lambda i: (i, 0)),
            pl.BlockSpec((1, D_MODEL), lambda i: (0, 0)),
            _mod_spec(0), _mod_spec(1),
            pl.BlockSpec((D_MODEL, IN_WIDTH), lambda i: (0, 0)),
            rope_spec, rope_spec, rope_spec,
        ],
        out_specs=[pl.BlockSpec((TM, IN_WIDTH), lambda i: (i, 0))]
        + [pl.BlockSpec((TM // SEQ, SEQ, 512), lambda i: (jnp.minimum(i, CTX_TILES - 1), 0, 0))] * 4,
        out_shape=[jax.ShapeDtypeStruct((N_TOK, IN_WIDTH), BF16)]
        + [jax.ShapeDtypeStruct((BATCH, SEQ, 512), F32)] * 4,
        compiler_params=_cparams("arbitrary"),
        name="inproj",
    )(x, g, mod, mod, w_bf16, *rope_tabs)


def _lane_half_mask(shape, half):
    lane = lax.broadcasted_iota(I32, shape, len(shape) - 1)
    return (lane < HEAD_DIM) if half == 0 else (lane >= HEAD_DIM)


def _lambda_value(lq1, lk1, lq2, lk2, lam_init):
    a = jnp.sum(lq1 * lk1, axis=-1, keepdims=True)
    b = jnp.sum(lq2 * lk2, axis=-1, keepdims=True)
    return jnp.exp(a) - jnp.exp(b) + lam_init


def _subln(o, g, lam_init):
    on = o * lax.rsqrt(jnp.mean(o * o, axis=-1, keepdims=True) + SUBLN_EPS)
    return (on * g) * (1.0 - lam_init)


def _ctx_attn_kernel(p_ref, lq1, lk1, lq2, lk2, gs_ref, o_ref, *, lam_init):
    lam = _lambda_value(lq1[...], lk1[...], lq2[...], lk2[...], lam_init)
    for hp in range(NA_HEADS // 2):
        q = p_ref[:, COL_NA_Q + hp * LANES:COL_NA_Q + (hp + 1) * LANES]
        k = p_ref[:, COL_NA_K + hp * LANES:COL_NA_K + (hp + 1) * LANES]
        v = p_ref[:, COL_NA_V + hp * LANES:COL_NA_V + (hp + 1) * LANES]
        outs = []
        for half in range(2):
            qm = jnp.where(_lane_half_mask(q.shape, half), q, jnp.zeros_like(q))
            s = _dot_nt(qm, k)
            m = jnp.max(s, axis=-1, keepdims=True)
            e = jnp.exp2(s - m)
            inv = 1.0 / jnp.sum(e, axis=-1, keepdims=True)
            outs.append(_dot(e.astype(BF16), v) * inv)
        o = jnp.where(_lane_half_mask(outs[0].shape, 0), outs[0], outs[1])
        o_ref[:, hp * LANES:(hp + 1) * LANES] = o.astype(o_ref.dtype)
    for h in range(DIFF_HEADS):
        q = p_ref[:, COL_D_Q + h * LANES:COL_D_Q + (h + 1) * LANES]
        k = p_ref[:, COL_D_K + h * LANES:COL_D_K + (h + 1) * LANES]
        v = p_ref[:, COL_D_V + h * LANES:COL_D_V + (h + 1) * LANES]
        ps = []
        for half in range(2):
            qm = jnp.where(_lane_half_mask(q.shape, half), q, jnp.zeros_like(q))
            s = _dot_nt(qm, k)
            m = jnp.max(s, axis=-1, keepdims=True)
            e = jnp.exp2(s - m)
            ps.append(e / jnp.sum(e, axis=-1, keepdims=True))
        a = (ps[0] - lam * ps[1]).astype(BF16)
        o = _subln(_dot(a, v), gs_ref[...], lam_init)
        o_ref[:, NA_WIDTH + h * LANES:NA_WIDTH + (h + 1) * LANES] = o.astype(o_ref.dtype)


def _ctx_attention(p_all, lam_params, g_subln, lam_init):
    vec = pl.BlockSpec((1, HEAD_DIM), lambda b: (0, 0))
    return pl.pallas_call(
        functools.partial(_ctx_attn_kernel, lam_init=lam_init),
        grid=(BATCH,),
        in_specs=[pl.BlockSpec((SEQ, IN_WIDTH), lambda b: (b, 0)), vec, vec, vec, vec,
                  pl.BlockSpec((1, 2 * HEAD_DIM), lambda b: (0, 0))],
        out_specs=pl.BlockSpec((SEQ, D_MODEL), lambda b: (b, 0)),
        out_shape=jax.ShapeDtypeStruct((N_CTX_TOK, D_MODEL), BF16),
        compiler_params=_cparams("parallel"),
        name="ctx_attention",
    )(p_all, *lam_params, g_subln)


def _bias_table_kernel(rpb_ref, o_ref):
    dr_plan, ok_plan = _na_window_plan()
    shape = (GRID_W, LANES)
    qc = lax.broadcasted_iota(I32, shape, 0)
    lane = lax.broadcasted_iota(I32, shape, 1)
    kc = jnp.bitwise_and(lane, GRID_W - 1)
    cs = jnp.clip(qc - NA_WIN_W // 2, 0, GRID_W - NA_WIN_W)
    in_window = (kc >= cs) & (kc < cs + NA_WIN_W)
    low_half = lane < GRID_W
    masked = jnp.full(shape, MASK_VALUE, F32)

    pieces = {}

    def piece(dr, parity):
        if (dr, parity) not in pieces:
            base = jnp.broadcast_to(rpb_ref[dr:dr + 1, :], shape) * LOG2E
            shifted = pltpu.roll(base, 0, 1, stride=1, stride_axis=0)
            pieces[(dr, parity)] = pltpu.roll(shifted, (LANES - (NA_WIN_W - 1) + GRID_W * parity) % LANES, 1)
        return pieces[(dr, parity)]

    for t in range(3):
        for jr in range(NA_R):
            for p in range(NA_WIN_ROWS // 2):
                halves = []
                for parity in range(2):
                    i = 2 * p + parity
                    halves.append(piece(int(dr_plan[t, jr, i]), parity) if ok_plan[t, jr, i] else masked)
                blk = jnp.where(in_window, jnp.where(low_half, halves[0], halves[1]), MASK_VALUE)
                o_ref[t, jr * GRID_W:(jr + 1) * GRID_W, p * LANES:(p + 1) * LANES] = blk.astype(o_ref.dtype)


def _na_window_plan():
    dr = np.zeros((3, NA_R, NA_WIN_ROWS), np.int32)
    ok = np.zeros((3, NA_R, NA_WIN_ROWS), bool)
    for t, r0 in enumerate((0, NA_R, GRID_ROWS - NA_R)):
        lo = min(max(r0 - NA_WIN_H // 2, 0), GRID_ROWS - NA_WIN_ROWS)
        for jr in range(NA_R):
            r = r0 + jr
            rs = min(max(r - NA_WIN_H // 2, 0), GRID_ROWS - NA_WIN_H)
            for i in range(NA_WIN_ROWS):
                key_row = lo + i
                if rs <= key_row < rs + NA_WIN_H:
                    ok[t, jr, i] = True
                    dr[t, jr, i] = key_row - r + NA_WIN_H - 1
    return dr, ok


def _na_bias_tables(rpb):
    n_dr = 2 * NA_WIN_H - 1
    rpb_pad = jnp.pad(rpb.reshape(DEPTH * NA_HEADS, n_dr, 2 * NA_WIN_W - 1),
                      ((0, 0), (0, 16 - n_dr), (0, LANES - (2 * NA_WIN_W - 1))))
    return pl.pallas_call(
        _bias_table_kernel,
        grid=(DEPTH * NA_HEADS,),
        in_specs=[pl.BlockSpec((None, 16, LANES), lambda g: (g, 0, 0))],
        out_specs=pl.BlockSpec((None, 3, None, NA_R * GRID_W, NA_WIN_ROWS * GRID_W),
                               lambda g: (g // NA_HEADS, 0, g % NA_HEADS, 0, 0)),
        out_shape=jax.ShapeDtypeStruct((DEPTH, 3, NA_HEADS, NA_R * GRID_W, NA_WIN_ROWS * GRID_W), BF16),
        compiler_params=_cparams("parallel"),
        name="na_bias_table",
    )(rpb_pad)


def _na_attn_kernel(q_ref, k_ref, v_ref, kc_ref, vc_ref, bias_ref, o_ref):
    j = pl.program_id(1)
    lo = jnp.clip(j * NA_R - NA_WIN_H // 2, 0, GRID_ROWS - NA_WIN_ROWS)
    k0 = pl.multiple_of(lo * GRID_W, GRID_W)
    step_type = jnp.where(j == 0, 0, jnp.where(j == NA_STEPS - 1, 2, 1))
    n_keys = NA_WIN_ROWS * GRID_W

    def scores(head):
        lanes = slice((head // 2) * LANES, (head // 2 + 1) * LANES)
        q = q_ref[:, lanes]
        qm = jnp.where(_lane_half_mask(q.shape, head % 2), q, jnp.zeros_like(q))
        s_w = _dot_nt(qm, k_ref[pl.ds(k0, n_keys), lanes]) + bias_ref[step_type, head].astype(F32)
        s_c = _dot_nt(qm, kc_ref[:, lanes].astype(BF16))
        return s_w, s_c

    nxt = scores(0)
    outs = []
    for head in range(NA_HEADS):
        s_w, s_c = nxt
        if head + 1 < NA_HEADS:
            nxt = scores(head + 1)
        lanes = slice((head // 2) * LANES, (head // 2 + 1) * LANES)
        m = jnp.maximum(jnp.max(s_w, axis=-1, keepdims=True), jnp.max(s_c, axis=-1, keepdims=True))
        e_w = jnp.exp2(s_w - m)
        e_c = jnp.exp2(s_c - m)
        inv = 1.0 / (jnp.sum(e_w, axis=-1, keepdims=True) + jnp.sum(e_c, axis=-1, keepdims=True))
        pv = (_dot(e_w.astype(BF16), v_ref[pl.ds(k0, n_keys), lanes])
              + _dot(e_c.astype(BF16), vc_ref[:, lanes].astype(BF16)))
        outs.append(pv * inv)
        if head % 2 == 1:
            o = jnp.where(_lane_half_mask(outs[0].shape, 0), outs[0], outs[1])
            o_ref[:, lanes] = o.astype(o_ref.dtype)
            outs = []


def _na_attention(p_all, cache_k, cache_v, bias_tab, layer):
    q_rows = NA_R * GRID_W
    q_blk0 = N_CTX_TOK // q_rows
    seq_blk0 = N_CTX_TOK // DEC_SEQ
    cache_spec = pl.BlockSpec((None, None, PAST_LEN, NA_WIDTH), lambda b, j: (b, layer, 0, 0))
    return pl.pallas_call(
        _na_attn_kernel,
        grid=(DEC_BATCH, NA_STEPS),
        in_specs=[
            pl.BlockSpec((q_rows, NA_WIDTH), lambda b, j: (q_blk0 + b * NA_STEPS + j, COL_NA_Q // NA_WIDTH)),
            pl.BlockSpec((DEC_SEQ, NA_WIDTH), lambda b, j: (seq_blk0 + b, COL_NA_K // NA_WIDTH)),
            pl.BlockSpec((DEC_SEQ, NA_WIDTH), lambda b, j: (seq_blk0 + b, COL_NA_V // NA_WIDTH)),
            cache_spec, cache_spec,
            pl.BlockSpec((None, 3, NA_HEADS, q_rows, NA_WIN_ROWS * GRID_W), lambda b, j: (layer, 0, 0, 0, 0)),
        ],
        out_specs=pl.BlockSpec((q_rows, NA_WIDTH), lambda b, j: (b * NA_STEPS + j, 0)),
        out_shape=jax.ShapeDtypeStruct((N_LAT_TOK, NA_WIDTH), BF16),
        compiler_params=_cparams("parallel", "arbitrary"),
        name="latent_na_attention",
    )(p_all, p_all, p_all, cache_k, cache_v, bias_tab)


def _diff_attn_kernel(q_ref, k_ref, v_ref, kc_ref, vc_ref, lq1, lk1, lq2, lk2, gs_ref, o_ref,
                      *, lam_init):
    lam = _lambda_value(lq1[...], lk1[...], lq2[...], lk2[...], lam_init)
    k = k_ref[...]
    v = v_ref[...]
    kc = kc_ref[...].astype(BF16)
    vc = vc_ref[...].astype(BF16)
    n_sub = TQ_DIFF // DIFF_SUB

    def scores(i):
        q = q_ref[i * DIFF_SUB:(i + 1) * DIFF_SUB, :]
        out = []
        for half in range(2):
            qm = jnp.where(_lane_half_mask(q.shape, half), q, jnp.zeros_like(q))
            out.append((_dot_nt(qm, k), _dot_nt(qm, kc)))
        return out

    nxt = scores(0)
    for i in range(n_sub):
        cur = nxt
        if i + 1 < n_sub:
            nxt = scores(i + 1)
        probs = []
        for s_l, s_c in cur:
            m = jnp.maximum(jnp.max(s_l, axis=-1, keepdims=True), jnp.max(s_c, axis=-1, keepdims=True))
            e_l = jnp.exp2(s_l - m)
            e_c = jnp.exp2(s_c - m)
            inv = 1.0 / (jnp.sum(e_l, axis=-1, keepdims=True) + jnp.sum(e_c, axis=-1, keepdims=True))
            probs.append((e_l, e_c, inv))
        w1 = probs[0][2]
        w2 = lam * probs[1][2]
        a_l = (probs[0][0] * w1 - probs[1][0] * w2).astype(BF16)
        a_c = (probs[0][1] * w1 - probs[1][1] * w2).astype(BF16)
        o = _dot(a_l, v) + _dot(a_c, vc)
        o_ref[i * DIFF_SUB:(i + 1) * DIFF_SUB, :] = _subln(o, gs_ref[...], lam_init).astype(o_ref.dtype)


def _diff_attention(p_all, cache_k, cache_v, lam_params, g_subln, lam_init, layer):
    steps = DEC_SEQ // TQ_DIFF
    q_blk0 = N_CTX_TOK // TQ_DIFF
    seq_blk0 = N_CTX_TOK // DEC_SEQ
    vec = pl.BlockSpec((1, HEAD_DIM), lambda b, h, j: (0, 0))
    cache_spec = pl.BlockSpec((None, None, PAST_LEN, LANES), lambda b, h, j: (b, layer, 0, h))
    return pl.pallas_call(
        functools.partial(_diff_attn_kernel, lam_init=lam_init),
        grid=(DEC_BATCH, DIFF_HEADS, steps),
        in_specs=[
            pl.BlockSpec((TQ_DIFF, LANES), lambda b, h, j: (q_blk0 + b * steps + j, COL_D_Q // LANES + h)),
            pl.BlockSpec((DEC_SEQ, LANES), lambda b, h, j: (seq_blk0 + b, COL_D_K // LANES + h)),
            pl.BlockSpec((DEC_SEQ, LANES), lambda b, h, j: (seq_blk0 + b, COL_D_V // LANES + h)),
            cache_spec, cache_spec, vec, vec, vec, vec,
            pl.BlockSpec((1, 2 * HEAD_DIM), lambda b, h, j: (0, 0)),
        ],
        out_specs=pl.BlockSpec((TQ_DIFF, LANES), lambda b, h, j: (b * steps + j, h)),
        out_shape=jax.ShapeDtypeStruct((N_LAT_TOK, DIFF_WIDTH), BF16),
        compiler_params=_cparams("parallel", "parallel", "arbitrary"),
        name="latent_diff_attention",
    )(p_all, p_all, p_all, cache_k, cache_v, *lam_params, g_subln)


def _attn_residual(i, x_ref, oc_ref, ona_ref, od_ref, ga_ref, w_ref, dst_ref):
    @pl.when(i < CTX_TILES)
    def _():
        dst_ref[...] = x_ref[...] + ga_ref[0] * _dot(oc_ref[...], w_ref[...])

    @pl.when(i >= CTX_TILES)
    def _():
        acc = _dot(ona_ref[...], w_ref[:NA_WIDTH, :]) + _dot(od_ref[...], w_ref[NA_WIDTH:, :])
        dst_ref[...] = x_ref[...] + ga_ref[0] * acc


def _attn_residual_specs():
    lat_idx = lambda i, *_: (jnp.maximum(i - CTX_TILES, 0), 0)
    return [
        pl.BlockSpec((TM, D_MODEL), lambda i, *_: (jnp.minimum(i, CTX_TILES - 1), 0)),
        pl.BlockSpec((TM, NA_WIDTH), lat_idx),
        pl.BlockSpec((TM, DIFF_WIDTH), lat_idx),
        _mod_spec(2),
        pl.BlockSpec((D_MODEL, D_MODEL), lambda i, *_: (0, 0)),
    ]


def _swiglu_tile(hb, wg, wu):
    g = _dot(hb, wg)
    u = _dot(hb, wu)
    return (g * jax.nn.sigmoid(g)) * u


def _ffn_kernel(x_ref, oc_ref, ona_ref, od_ref, ga1_ref, wo_ref, g_ref, sh_ref, sc_ref, ga_ref,
                wg_ref, wu_ref, wd_ref, y_ref, xmid_scr, h_scr, acc_scr):
    i = pl.program_id(0)
    f = pl.program_id(1)

    @pl.when(f == 0)
    def _():
        _attn_residual(i, x_ref, oc_ref, ona_ref, od_ref, ga1_ref, wo_ref, xmid_scr)
        h = _modulated_norm(xmid_scr[...], g_ref[...], sh_ref[0], sc_ref[0])
        h_scr[...] = h.astype(BF16)
        acc_scr[...] = jnp.zeros_like(acc_scr)

    a = _swiglu_tile(h_scr[...], wg_ref[...], wu_ref[...])
    acc_scr[...] += _dot(a.astype(BF16), wd_ref[...])

    @pl.when(f == pl.num_programs(1) - 1)
    def _():
        y_ref[...] = xmid_scr[...] + ga_ref[0] * acc_scr[...]


def _ffn_dense(x, o_ctx, o_na, o_d, w_out, g, mod, wg, wu, wd):
    tf = FF_TILE_DENSE
    return pl.pallas_call(
        _ffn_kernel,
        grid=(N_TILES, D_FF // tf),
        in_specs=[pl.BlockSpec((TM, D_MODEL), lambda i, f: (i, 0))] + _attn_residual_specs() + [
            pl.BlockSpec((1, D_MODEL), lambda i, f: (0, 0)),
            _mod_spec(3), _mod_spec(4), _mod_spec(5),
            pl.BlockSpec((D_MODEL, tf), lambda i, f: (0, f)),
            pl.BlockSpec((D_MODEL, tf), lambda i, f: (0, f)),
            pl.BlockSpec((tf, D_MODEL), lambda i, f: (f, 0)),
        ],
        out_specs=pl.BlockSpec((TM, D_MODEL), lambda i, f: (i, 0)),
        out_shape=jax.ShapeDtypeStruct((N_TOK, D_MODEL), F32),
        scratch_shapes=[pltpu.VMEM((TM, D_MODEL), F32), pltpu.VMEM((TM, D_MODEL), BF16),
                        pltpu.VMEM((TM, D_MODEL), F32)],
        compiler_params=_cparams("parallel", "arbitrary"),
        name="ffn_dense",
    )(x, o_ctx, o_na, o_d, mod, w_out, g, mod, mod, mod, wg, wu, wd)


ROUTE_E1, ROUTE_E2, ROUTE_G1, ROUTE_G2, ROUTE_R1, ROUTE_R2 = range(6)


def _router_kernel(x_ref, oc_ref, ona_ref, od_ref, ga1_ref, wo_ref, g_ref, sh_ref, sc_ref, wr_ref,
                   xmid_ref, h_ref, route_ref, cnt_ref, carry_scr):
    i = pl.program_id(0)

    @pl.when(i == 0)
    def _():
        carry_scr[...] = jnp.zeros_like(carry_scr)

    _attn_residual(i, x_ref, oc_ref, ona_ref, od_ref, ga1_ref, wo_ref, xmid_ref)
    h = _modulated_norm(xmid_ref[...], g_ref[...], sh_ref[0], sc_ref[0])
    h_ref[...] = h
    logits = jnp.dot(h, wr_ref[...], preferred_element_type=F32, precision=lax.Precision.HIGHEST)
    lane = lax.broadcasted_iota(I32, logits.shape, 1)
    logits = jnp.where(lane < N_EXPERTS, logits, MASK_VALUE)
    m1 = jnp.max(logits, axis=-1, keepdims=True)
    i1 = jnp.min(jnp.where(logits == m1, lane, LANES), axis=-1, keepdims=True)
    rest = jnp.where(lane == i1, MASK_VALUE, logits)
    m2 = jnp.max(rest, axis=-1, keepdims=True)
    i2 = jnp.min(jnp.where(rest == m2, lane, LANES), axis=-1, keepdims=True)
    e2 = jnp.exp(m2 - m1)
    g1 = 1.0 / (1.0 + e2)
    g2 = e2 / (1.0 + e2)

    hit1 = lane == i1
    hit2 = lane == i2
    onehot = jnp.where(hit1 | hit2, 1.0, 0.0)
    row = lax.broadcasted_iota(I32, (TM, TM), 0)
    col = lax.broadcasted_iota(I32, (TM, TM), 1)
    lower = jnp.where(row > col, 1.0, 0.0).astype(BF16)
    before = _dot(lower, onehot.astype(BF16)) + carry_scr[...]
    r1 = jnp.sum(jnp.where(hit1, before, 0.0), axis=-1, keepdims=True)
    r2 = jnp.sum(jnp.where(hit2, before, 0.0), axis=-1, keepdims=True)
    carry_scr[...] += jnp.sum(onehot, axis=0, keepdims=True)

    out = jnp.zeros(logits.shape, F32)
    for slot, val in ((ROUTE_E1, i1.astype(F32)), (ROUTE_E2, i2.astype(F32)), (ROUTE_G1, g1),
                      (ROUTE_G2, g2), (ROUTE_R1, r1), (ROUTE_R2, r2)):
        out = jnp.where(lane == slot, val, out)
    route_ref[...] = out

    @pl.when(i == pl.num_programs(0) - 1)
    def _():
        cnt_ref[...] = jnp.broadcast_to(carry_scr[...], cnt_ref.shape)


def _router(x, o_ctx, o_na, o_d, w_out, g, mod, w_router_pad):
    return pl.pallas_call(
        _router_kernel,
        grid=(N_TILES,),
        in_specs=[pl.BlockSpec((TM, D_MODEL), lambda i: (i, 0))] + _attn_residual_specs() + [
            pl.BlockSpec((1, D_MODEL), lambda i: (0, 0)),
            _mod_spec(3), _mod_spec(4),
            pl.BlockSpec((D_MODEL, LANES), lambda i: (0, 0)),
        ],
        out_specs=[pl.BlockSpec((TM, D_MODEL), lambda i: (i, 0)),
                   pl.BlockSpec((TM, D_MODEL), lambda i: (i, 0)),
                   pl.BlockSpec((TM, LANES), lambda i: (i, 0)),
                   pl.BlockSpec((8, LANES), lambda i: (0, 0))],
        out_shape=[jax.ShapeDtypeStruct((N_TOK, D_MODEL), F32),
                   jax.ShapeDtypeStruct((N_TOK, D_MODEL), F32),
                   jax.ShapeDtypeStruct((N_TOK, LANES), F32),
                   jax.ShapeDtypeStruct((8, LANES), F32)],
        scratch_shapes=[pltpu.VMEM((1, LANES), F32)],
        compiler_params=_cparams("arbitrary"),
        name="moe_router",
    )(x, o_ctx, o_na, o_d, mod, w_out, g, mod, mod, w_router_pad)


def _moe_plan(route, counts):
    cnt = counts[0, :N_EXPERTS].astype(I32)
    tiles = (cnt + TM_MOE - 1) // TM_MOE
    tile_end = jnp.cumsum(tiles)
    group_start = (tile_end - tiles) * TM_MOE
    e = route[:, ROUTE_E1:ROUTE_E2 + 1].astype(I32)
    r = route[:, ROUTE_R1:ROUTE_R2 + 1].astype(I32)
    slot = (group_start[e] + r).T.reshape(-1)
    tok = jnp.tile(jnp.arange(N_TOK, dtype=I32), 2)
    src_tok = jnp.zeros(((MOE_TILES + 2) * TM_MOE,), I32).at[slot].set(tok)
    n_valid = tile_end[-1:]
    t = jnp.arange(MOE_TILES, dtype=I32)
    tile_expert = jnp.minimum(jnp.sum((t[:, None] >= tile_end[None, :]).astype(I32), axis=1),
                              N_EXPERTS - 1)
    last_expert = tile_expert[jnp.maximum(n_valid[0] - 1, 0)]
    tile_expert = jnp.where(t < n_valid[0], tile_expert, last_expert)
    return tile_expert, n_valid, src_tok, slot


def _moe_kernel(te_ref, nv_ref, src_ref, h_hbm, *refs):
    wg_refs = refs[:MOE_W_SPLIT]
    wu_refs = refs[MOE_W_SPLIT:2 * MOE_W_SPLIT]
    wd_refs = refs[2 * MOE_W_SPLIT:3 * MOE_W_SPLIT]
    y_ref, hrow_scr, hb_scr, sems = refs[3 * MOE_W_SPLIT:]
    t = pl.program_id(0)
    f = pl.program_id(1)
    valid = t < nv_ref[0]
    last_t = pl.num_programs(0) - 1
    last_f = N_FF_MOE - 1

    def row_copy(tile, row, buf):
        tok = src_ref[tile * TM_MOE + row]
        return pltpu.make_async_copy(h_hbm.at[pl.ds(tok, 1), :], hrow_scr.at[buf, pl.ds(row, 1), :],
                                     sems.at[buf])

    def wait_tile(buf):
        pltpu.make_async_copy(h_hbm.at[pl.ds(0, TM_MOE), :],
                              hrow_scr.at[buf, pl.ds(0, TM_MOE), :], sems.at[buf]).wait()
        for k in range(TM_MOE, GATHER_ROWS):
            pltpu.make_async_copy(h_hbm.at[pl.ds(0, 1), :],
                                  hrow_scr.at[buf, pl.ds(k, 1), :], sems.at[buf]).wait()

    def issue_next_chunk():
        for j in range(GATHER_CHUNK):
            row_copy(t + 1, f * GATHER_CHUNK + j, (t + 1) % 2).start()

    @pl.when((t == 0) & (f == 0))
    def _():
        def issue(j, carry):
            row_copy(0, j, 0).start()
            return carry

        lax.fori_loop(0, GATHER_ROWS, issue, 0, unroll=7)

    @pl.when(f == 0)
    def _():
        wait_tile(t % 2)
        hb_scr[...] = hrow_scr[t % 2, :TM_MOE, :].astype(BF16)
        y_ref[...] = jnp.zeros_like(y_ref)

    @pl.when(valid)
    def _():
        issue_next_chunk()
        hb = hb_scr[...]
        acc = None
        for wg_ref, wu_ref, wd_ref in zip(wg_refs, wu_refs, wd_refs):
            a = _swiglu_tile(hb, wg_ref[...].astype(BF16), wu_ref[...].astype(BF16))
            d = _dot(a.astype(BF16), wd_ref[...].astype(BF16))
            acc = d if acc is None else acc + d
        y_ref[...] += acc

    @pl.when(jnp.logical_not(valid))
    def _():
        issue_next_chunk()

    @pl.when((t == last_t) & (f == last_f))
    def _():
        wait_tile((t + 1) % 2)


def _moe_experts(tile_expert, n_valid, src_tok, h, wg, wu, wd, layer):
    ts = FF_TILE_MOE // MOE_W_SPLIT
    buf_rows = -(-GATHER_ROWS // 8) * 8

    def f_eff(t, f, nv):
        return jnp.where(t < nv[0], f, N_FF_MOE - 1)

    grid_spec = pltpu.PrefetchScalarGridSpec(
        num_scalar_prefetch=3,
        grid=(MOE_TILES, N_FF_MOE),
        in_specs=[pl.BlockSpec(memory_space=pl.ANY)] + [
            pl.BlockSpec((None, None, D_MODEL, ts),
                         lambda t, f, te, nv, st, s=s: (layer, te[t], 0, MOE_W_SPLIT * f_eff(t, f, nv) + s))
            for _ in range(2) for s in range(MOE_W_SPLIT)
        ] + [
            pl.BlockSpec((None, None, ts, D_MODEL),
                         lambda t, f, te, nv, st, s=s: (layer, te[t], MOE_W_SPLIT * f_eff(t, f, nv) + s, 0))
            for s in range(MOE_W_SPLIT)
        ],
        out_specs=pl.BlockSpec((TM_MOE, D_MODEL), lambda t, f, te, nv, st: (t, 0)),
        scratch_shapes=[pltpu.VMEM((2, buf_rows, D_MODEL), F32), pltpu.VMEM((TM_MOE, D_MODEL), BF16),
                        pltpu.SemaphoreType.DMA((2,))],
    )
    return pl.pallas_call(
        _moe_kernel,
        grid_spec=grid_spec,
        out_shape=jax.ShapeDtypeStruct((MOE_TILES * TM_MOE, D_MODEL), F32),
        compiler_params=_cparams("arbitrary", "arbitrary"),
        name="moe_experts",
    )(tile_expert, n_valid, src_tok, h, *([wg] * MOE_W_SPLIT + [wu] * MOE_W_SPLIT + [wd] * MOE_W_SPLIT))


def _moe_combine_kernel(slot_ref, x_ref, route_ref, ga_ref, y_hbm, o_ref, rows_scr, sems):
    i = pl.program_id(0)

    def issue_tile(tile, buf):
        for c in range(2):
            base = c * N_TOK + tile * TM

            def issue(j, carry, base=base, c=c):
                s = slot_ref[base + j]
                pltpu.make_async_copy(y_hbm.at[pl.ds(s, 1), :], rows_scr.at[buf, c, pl.ds(j, 1), :],
                                      sems.at[buf]).start()
                return carry

            lax.fori_loop(0, TM, issue, 0, unroll=8)

    @pl.when(i == 0)
    def _():
        issue_tile(0, 0)

    @pl.when(i + 1 < pl.num_programs(0))
    def _():
        issue_tile(i + 1, (i + 1) % 2)

    buf = i % 2
    for c in range(2):
        pltpu.make_async_copy(y_hbm.at[pl.ds(0, TM), :], rows_scr.at[buf, c], sems.at[buf]).wait()
    route = route_ref[...]
    g1 = route[:, ROUTE_G1:ROUTE_G1 + 1]
    g2 = route[:, ROUTE_G2:ROUTE_G2 + 1]
    o_ref[...] = x_ref[...] + ga_ref[0] * (g1 * rows_scr[buf, 0] + g2 * rows_scr[buf, 1])


def _moe_combine(slot, x, route, mod, y):
    grid_spec = pltpu.PrefetchScalarGridSpec(
        num_scalar_prefetch=1,
        grid=(N_TILES,),
        in_specs=[
            pl.BlockSpec((TM, D_MODEL), lambda i, s: (i, 0)),
            pl.BlockSpec((TM, LANES), lambda i, s: (i, 0)),
            _mod_spec(5),
            pl.BlockSpec(memory_space=pl.ANY),
        ],
        out_specs=pl.BlockSpec((TM, D_MODEL), lambda i, s: (i, 0)),
        scratch_shapes=[pltpu.VMEM((2, 2, TM, D_MODEL), F32), pltpu.SemaphoreType.DMA((2,))],
    )
    return pl.pallas_call(
        _moe_combine_kernel,
        grid_spec=grid_spec,
        out_shape=jax.ShapeDtypeStruct((N_TOK, D_MODEL), F32),
        compiler_params=_cparams("arbitrary"),
        name="moe_combine",
    )(slot, x, route, mod, y)


def _final_norm_kernel(x_ref, g_ref, y_ref):
    x = x_ref[...]
    y_ref[...] = (x * lax.rsqrt(jnp.mean(x * x, axis=-1, keepdims=True) + EPS)) * g_ref[...]


def _final_norm(x, g, tile0, n_tok):
    return pl.pallas_call(
        _final_norm_kernel,
        grid=(n_tok // TM,),
        in_specs=[pl.BlockSpec((TM, D_MODEL), lambda i: (tile0 + i, 0)),
                  pl.BlockSpec((1, D_MODEL), lambda i: (0, 0))],
        out_specs=pl.BlockSpec((TM, D_MODEL), lambda i: (i, 0)),
        out_shape=jax.ShapeDtypeStruct((n_tok, D_MODEL), F32),
        compiler_params=_cparams("parallel"),
        name="final_norm",
    )(x, g)


def _rope_tables():
    t = jnp.arange(DEC_SEQ)
    row = (t // GRID_W).astype(F32)
    col = (t % GRID_W).astype(F32)
    half = HEAD_DIM // 2
    inv = 1.0 / (ROPE_THETA ** (jnp.arange(0, half, 2, dtype=F32) / half))
    ar = row[:, None] * inv[None]
    ac = col[:, None] * inv[None]
    ang = jnp.concatenate([ar, ar, ac, ac], axis=-1)
    ang = jnp.concatenate([ang, jnp.zeros((TM, HEAD_DIM), F32)], axis=0)
    cos = jnp.tile(jnp.cos(ang), (1, LANES // HEAD_DIM))
    sin = jnp.tile(jnp.sin(ang), (1, LANES // HEAD_DIM))
    first_half = (jnp.arange(LANES) % 32) < 16
    sin_a = jnp.where(first_half[None, :], -sin, 0.0)
    sin_b = jnp.where(first_half[None, :], 0.0, sin)
    return cos, sin_a, sin_b


def kernel(x_prompt, x_sample, c, cache_na_k, cache_na_v, cache_diff_k, cache_diff_v, c_ctx, w_ada, b_ada, g_mix, w_in, rpb, lam_q1, lam_k1, lam_q2, lam_k2, g_subln, w_out, g_ffn, w_ffn_gate, w_ffn_up, w_ffn_down, w_router, w_moe_gate, w_moe_up, w_moe_down, g_final):
    x = jnp.concatenate([x_prompt.reshape(N_CTX_TOK, D_MODEL), x_sample.reshape(N_LAT_TOK, D_MODEL)])

    cvec = jnp.zeros((MOD_ROWS, D_MODEL), F32).at[0].set(c_ctx).at[1:1 + DEC_BATCH].set(c)
    mod_all = _modulation(cvec, w_ada, b_ada).reshape(DEPTH, MOD_ROWS * 6, 1, D_MODEL)

    col = jnp.arange(IN_WIDTH)
    is_q = (col < COL_NA_K) | ((col >= COL_D_Q) & (col < COL_D_K))
    q_scale = jnp.where(is_q, ATTN_SCALE * LOG2E, 1.0).astype(F32)
    w_in_b = (w_in * q_scale[None, None, :]).astype(BF16)
    w_out_b = w_out.astype(BF16)
    w_fg, w_fu, w_fd = (w.astype(BF16) for w in (w_ffn_gate, w_ffn_up, w_ffn_down))
    w_router_pad = jnp.pad(w_router, ((0, 0), (0, 0), (0, LANES - N_EXPERTS)))

    rope_tabs = _rope_tables()
    bias_tabs = _na_bias_tables(rpb)
    cna_k = cache_na_k.reshape(DEC_BATCH, DEPTH, PAST_LEN, NA_WIDTH)
    cna_v = cache_na_v.reshape(DEC_BATCH, DEPTH, PAST_LEN, NA_WIDTH)
    cd_k = cache_diff_k.reshape(DEC_BATCH, DEPTH, PAST_LEN, DIFF_WIDTH)
    cd_v = cache_diff_v.reshape(DEC_BATCH, DEPTH, PAST_LEN, DIFF_WIDTH)

    kv_layers = []
    for l in range(DEPTH):
        lam_init = 0.8 - 0.6 * math.exp(-0.3 * l)
        mod = mod_all[l]
        g_mix_l = g_mix[l][None, :]
        g_ffn_l = g_ffn[l][None, :]
        g_sub_l = g_subln[l][None, :]
        lam_params = tuple(p[l][None, :] for p in (lam_q1, lam_k1, lam_q2, lam_k2))

        p_all, *kv_ctx = _inproj(x, g_mix_l, mod, w_in_b[l], rope_tabs)
        kv_layers.append(kv_ctx)
        o_ctx = _ctx_attention(p_all, lam_params, g_sub_l, lam_init)
        o_na = _na_attention(p_all, cna_k, cna_v, bias_tabs, l)
        o_d = _diff_attention(p_all, cd_k, cd_v, lam_params, g_sub_l, lam_init, l)

        i = l // 2
        if l % 2 == 0:
            x = _ffn_dense(x, o_ctx, o_na, o_d, w_out_b[l], g_ffn_l, mod, w_fg[i], w_fu[i], w_fd[i])
        else:
            x, h, route, counts = _router(x, o_ctx, o_na, o_d, w_out_b[l], g_ffn_l, mod, w_router_pad[i])
            tile_expert, n_valid, src_tok, slot = _moe_plan(route, counts)
            y = _moe_experts(tile_expert, n_valid, src_tok, h, w_moe_gate, w_moe_up, w_moe_down, i)
            x = _moe_combine(slot, x, route, mod, y)

    g_fin = g_final[None, :]
    y_prompt = _final_norm(x, g_fin, 0, N_CTX_TOK).reshape(BATCH, SEQ, D_MODEL)
    y_sample = _final_norm(x, g_fin, CTX_TILES, N_LAT_TOK).reshape(DEC_BATCH, DEC_SEQ, D_MODEL)

    kv = [jnp.stack([layer_kv[k] for layer_kv in kv_layers], axis=1) for k in range(4)]
    new_na_k = kv[0].reshape(BATCH, DEPTH, SEQ, NA_HEADS, HEAD_DIM)
    new_na_v = kv[1].reshape(BATCH, DEPTH, SEQ, NA_HEADS, HEAD_DIM)
    new_diff_k = kv[2].reshape(BATCH, DEPTH, SEQ, DIFF_HEADS, 2, HEAD_DIM)
    new_diff_v = kv[3].reshape(BATCH, DEPTH, SEQ, DIFF_HEADS, 2 * HEAD_DIM)
    return (y_prompt, y_sample, new_na_k, new_na_v, new_diff_k, new_diff_v)
```

```python
import functools
import math

import numpy as np
import jax
import jax.numpy as jnp
from jax import lax
from jax.experimental import pallas as pl
from jax.experimental.pallas import tpu as pltpu

F32 = jnp.float32
BF16 = jnp.bfloat16
I32 = jnp.int32

D_MODEL = 1024
DEPTH = 4
BATCH = 16
SEQ = 256
DEC_BATCH = 8
DEC_SEQ = 2048
PAST_LEN = 256
GRID_W = 64
GRID_ROWS = DEC_SEQ // GRID_W
HEAD_DIM = 64
NA_HEADS = 8
NA_WIDTH = 512
DIFF_HEADS = 4
DIFF_WIDTH = 512
IN_WIDTH = 3072
NA_WIN_H = 8
NA_WIN_W = 16
ROPE_THETA = 10000.0
D_FF = 2816
N_EXPERTS = 8
D_FF_EXPERT = 3584
EPS = 1e-6
SUBLN_EPS = 1e-5
ATTN_SCALE = HEAD_DIM ** -0.5

LANES = 128
ROW_CHUNKS = D_MODEL // LANES
N_CTX_TOK = BATCH * SEQ
N_LAT_TOK = DEC_BATCH * DEC_SEQ
N_TOK = N_CTX_TOK + N_LAT_TOK
MOD_ROWS = 16
MASK_VALUE = -1e30

COL_NA_Q, COL_NA_K, COL_NA_V = 0, 512, 1024
COL_D_Q, COL_D_K, COL_D_V = 1536, 2048, 2560

TM = 512
N_TILES = N_TOK // TM
CTX_TILES = N_CTX_TOK // TM
TILES_PER_SEQ = DEC_SEQ // TM
FF_TILE_DENSE = 1408
FF_TILE_MOE = 512
N_FF_MOE = D_FF_EXPERT // FF_TILE_MOE
TM_MOE = 1024
MOE_W_SPLIT = 2
TQ_DIFF = 512
DIFF_SUB = 128
NA_R = 4
NA_WIN_ROWS = 12
NA_STEPS = GRID_ROWS // NA_R
MOE_SLOTS = 2 * N_TOK
MOE_TILES = MOE_SLOTS // TM_MOE + N_EXPERTS
GATHER_CHUNK = -(-TM_MOE // N_FF_MOE)
GATHER_ROWS = GATHER_CHUNK * N_FF_MOE
LOG2E = 1.4426950408889634
VMEM_LIMIT = 56 * 1024 * 1024


def _cparams(*sem):
    return pltpu.CompilerParams(dimension_semantics=sem, vmem_limit_bytes=VMEM_LIMIT)


def _dot(a, b):
    return jnp.dot(a, b, preferred_element_type=F32)


def _dot_nt(a, b):
    return lax.dot_general(a, b, (((1,), (1,)), ((), ())), preferred_element_type=F32)


def _modulated_norm(x, g, shift, scale):
    xn = x * lax.rsqrt(jnp.mean(x * x, axis=-1, keepdims=True) + EPS)
    return (xn * g) * (1.0 + scale) + shift


def _mod_row(i):
    return jnp.maximum(i // TILES_PER_SEQ - CTX_TILES // TILES_PER_SEQ + 1, 0)


def _mod_spec(chunk):
    return pl.BlockSpec((1, 1, D_MODEL), lambda i, *_: (_mod_row(i) * 6 + chunk, 0, 0))


def _mod_kernel(c_ref, w_ref, b_ref, o_ref):
    cv = c_ref[...]
    s = cv * jax.nn.sigmoid(cv)
    o_ref[...] = jnp.dot(s, w_ref[...], preferred_element_type=F32,
                         precision=lax.Precision.HIGHEST) + b_ref[...]


def _modulation(cvec, w_ada, b_ada):
    tn = 1536
    n = 6 * D_MODEL
    return pl.pallas_call(
        _mod_kernel,
        grid=(DEPTH, n // tn),
        in_specs=[
            pl.BlockSpec((MOD_ROWS, D_MODEL), lambda l, j: (0, 0)),
            pl.BlockSpec((None, D_MODEL, tn), lambda l, j: (l, 0, j)),
            pl.BlockSpec((None, 1, tn), lambda l, j: (l, 0, j)),
        ],
        out_specs=pl.BlockSpec((None, MOD_ROWS, tn), lambda l, j: (l, 0, j)),
        out_shape=jax.ShapeDtypeStruct((DEPTH, MOD_ROWS, n), F32),
        compiler_params=_cparams("parallel", "parallel"),
        name="adaln_modulation",
    )(cvec, w_ada, b_ada.reshape(DEPTH, 1, n))


def _inproj_kernel(x_ref, g_ref, sh_ref, sc_ref, w_ref, cos_ref, sina_ref, sinb_ref, o_ref, *kv_refs):
    i = pl.program_id(0)
    h = _modulated_norm(x_ref[...], g_ref[...], sh_ref[0], sc_ref[0]).astype(BF16)
    chunk = 512
    for c in range(IN_WIDTH // chunk):
        col = c * chunk
        acc = _dot(h, w_ref[:, col:col + chunk])
        for k_i, src in enumerate((COL_NA_K, COL_NA_V, COL_D_K, COL_D_V)):
            if src == col:
                @pl.when(i < CTX_TILES)
                def _(acc=acc, k_i=k_i):
                    for b in range(TM // SEQ):
                        kv_refs[k_i][b] = acc[b * SEQ:(b + 1) * SEQ, :]
        if COL_D_Q <= col < COL_D_V:
            parts = []
            for j in range(chunk // LANES):
                blk = acc[:, j * LANES:(j + 1) * LANES]
                parts.append(blk * cos_ref[...]
                             + pltpu.roll(blk, LANES - 16, 1) * sina_ref[...]
                             + pltpu.roll(blk, 16, 1) * sinb_ref[...])
            acc = jnp.concatenate(parts, axis=1)
        o_ref[:, col:col + chunk] = acc.astype(o_ref.dtype)


def _inproj(x, g, mod, w_bf16, rope_tabs):
    rope_spec = pl.BlockSpec(
        (TM, LANES),
        lambda i: (jnp.where(i < CTX_TILES, TILES_PER_SEQ, (i - CTX_TILES) % TILES_PER_SEQ), 0))
    return pl.pallas_call(
        _inproj_kernel,
        grid=(N_TILES,),
        in_specs=[
            pl.BlockSpec((TM, D_MODEL), lambda i: (i, 0)),
            pl.BlockSpec((1, D_MODEL), lambda i: (0, 0)),
            _mod_spec(0), _mod_spec(1),
            pl.BlockSpec((D_MODEL, IN_WIDTH), lambda i: (0, 0)),
            rope_spec, rope_spec, rope_spec,
        ],
        out_specs=[pl.BlockSpec((TM, IN_WIDTH), lambda i: (i, 0))]
        + [pl.BlockSpec((TM // SEQ, SEQ, 512), lambda i: (jnp.minimum(i, CTX_TILES - 1), 0, 0))] * 4,
        out_shape=[jax.ShapeDtypeStruct((N_TOK, IN_WIDTH), BF16)]
        + [jax.ShapeDtypeStruct((BATCH, SEQ, 512), F32)] * 4,
        compiler_params=_cparams("arbitrary"),
        name="inproj",
    )(x, g, mod, mod, w_bf16, *rope_tabs)


def _lane_half_mask(shape, half):
    lane = lax.broadcasted_iota(I32, shape, len(shape) - 1)
    return (lane < HEAD_DIM) if half == 0 else (lane >= HEAD_DIM)


def _lambda_value(lq1, lk1, lq2, lk2, lam_init):
    a = jnp.sum(lq1 * lk1, axis=-1, keepdims=True)
    b = jnp.sum(lq2 * lk2, axis=-1, keepdims=True)
    return jnp.exp(a) - jnp.exp(b) + lam_init


def _subln(o, g, lam_init):
    on = o * lax.rsqrt(jnp.mean(o * o, axis=-1, keepdims=True) + SUBLN_EPS)
    return (on * g) * (1.0 - lam_init)


def _ctx_attn_kernel(p_ref, lq1, lk1, lq2, lk2, gs_ref, o_ref, *, lam_init):
    lam = _lambda_value(lq1[...], lk1[...], lq2[...], lk2[...], lam_init)
    for hp in range(NA_HEADS // 2):
        q = p_ref[:, COL_NA_Q + hp * LANES:COL_NA_Q + (hp + 1) * LANES]
        k = p_ref[:, COL_NA_K + hp * LANES:COL_NA_K + (hp + 1) * LANES]
        v = p_ref[:, COL_NA_V + hp * LANES:COL_NA_V + (hp + 1) * LANES]
        outs = []
        for half in range(2):
            qm = jnp.where(_lane_half_mask(q.shape, half), q, jnp.zeros_like(q))
            s = _dot_nt(qm, k)
            m = jnp.max(s, axis=-1, keepdims=True)
            e = jnp.exp2(s - m)
            inv = 1.0 / jnp.sum(e, axis=-1, keepdims=True)
            outs.append(_dot(e.astype(BF16), v) * inv)
        o = jnp.where(_lane_half_mask(outs[0].shape, 0), outs[0], outs[1])
        o_ref[:, hp * LANES:(hp + 1) * LANES] = o.astype(o_ref.dtype)
    for h in range(DIFF_HEADS):
        q = p_ref[:, COL_D_Q + h * LANES:COL_D_Q + (h + 1) * LANES]
        k = p_ref[:, COL_D_K + h * LANES:COL_D_K + (h + 1) * LANES]
        v = p_ref[:, COL_D_V + h * LANES:COL_D_V + (h + 1) * LANES]
        ps = []
        for half in range(2):
            qm = jnp.where(_lane_half_mask(q.shape, half), q, jnp.zeros_like(q))
            s = _dot_nt(qm, k)
            m = jnp.max(s, axis=-1, keepdims=True)
            e = jnp.exp2(s - m)
            ps.append(e / jnp.sum(e, axis=-1, keepdims=True))
        a = (ps[0] - lam * ps[1]).astype(BF16)
        o = _subln(_dot(a, v), gs_ref[...], lam_init)
        o_ref[:, NA_WIDTH + h * LANES:NA_WIDTH + (h + 1) * LANES] = o.astype(o_ref.dtype)


def _ctx_attention(p_all, lam_params, g_subln, lam_init):
    vec = pl.BlockSpec((1, HEAD_DIM), lambda b: (0, 0))
    return pl.pallas_call(
        functools.partial(_ctx_attn_kernel, lam_init=lam_init),
        grid=(BATCH,),
        in_specs=[pl.BlockSpec((SEQ, IN_WIDTH), lambda b: (b, 0)), vec, vec, vec, vec,
                  pl.BlockSpec((1, 2 * HEAD_DIM), lambda b: (0, 0))],
        out_specs=pl.BlockSpec((SEQ, D_MODEL), lambda b: (b, 0)),
        out_shape=jax.ShapeDtypeStruct((N_CTX_TOK, D_MODEL), BF16),
        compiler_params=_cparams("parallel"),
        name="ctx_attention",
    )(p_all, *lam_params, g_subln)


def _bias_table_kernel(rpb_ref, o_ref):
    dr_plan, ok_plan = _na_window_plan()
    shape = (GRID_W, LANES)
    qc = lax.broadcasted_iota(I32, shape, 0)
    lane = lax.broadcasted_iota(I32, shape, 1)
    kc = jnp.bitwise_and(lane, GRID_W - 1)
    cs = jnp.clip(qc - NA_WIN_W // 2, 0, GRID_W - NA_WIN_W)
    in_window = (kc >= cs) & (kc < cs + NA_WIN_W)
    low_half = lane < GRID_W
    masked = jnp.full(shape, MASK_VALUE, F32)

    pieces = {}

    def piece(dr, parity):
        if (dr, parity) not in pieces:
            base = jnp.broadcast_to(rpb_ref[dr:dr + 1, :], shape) * LOG2E
            shifted = pltpu.roll(base, 0, 1, stride=1, stride_axis=0)
            pieces[(dr, parity)] = pltpu.roll(shifted, (LANES - (NA_WIN_W - 1) + GRID_W * parity) % LANES, 1)
        return pieces[(dr, parity)]

    for t in range(3):
        for jr in range(NA_R):
            for p in range(NA_WIN_ROWS // 2):
                halves = []
                for parity in range(2):
                    i = 2 * p + parity
                    halves.append(piece(int(dr_plan[t, jr, i]), parity) if ok_plan[t, jr, i] else masked)
                blk = jnp.where(in_window, jnp.where(low_half, halves[0], halves[1]), MASK_VALUE)
                o_ref[t, jr * GRID_W:(jr + 1) * GRID_W, p * LANES:(p + 1) * LANES] = blk.astype(o_ref.dtype)


def _na_window_plan():
    dr = np.zeros((3, NA_R, NA_WIN_ROWS), np.int32)
    ok = np.zeros((3, NA_R, NA_WIN_ROWS), bool)
    for t, r0 in enumerate((0, NA_R, GRID_ROWS - NA_R)):
        lo = min(max(r0 - NA_WIN_H // 2, 0), GRID_ROWS - NA_WIN_ROWS)
        for jr in range(NA_R):
            r = r0 + jr
            rs = min(max(r - NA_WIN_H // 2, 0), GRID_ROWS - NA_WIN_H)
            for i in range(NA_WIN_ROWS):
                key_row = lo + i
                if rs <= key_row < rs + NA_WIN_H:
                    ok[t, jr, i] = True
                    dr[t, jr, i] = key_row - r + NA_WIN_H - 1
    return dr, ok


def _na_bias_tables(rpb):
    n_dr = 2 * NA_WIN_H - 1
    rpb_pad = jnp.pad(rpb.reshape(DEPTH * NA_HEADS, n_dr, 2 * NA_WIN_W - 1),
                      ((0, 0), (0, 16 - n_dr), (0, LANES - (2 * NA_WIN_W - 1))))
    return pl.pallas_call(
        _bias_table_kernel,
        grid=(DEPTH * NA_HEADS,),
        in_specs=[pl.BlockSpec((None, 16, LANES), lambda g: (g, 0, 0))],
        out_specs=pl.BlockSpec((None, 3, None, NA_R * GRID_W, NA_WIN_ROWS * GRID_W),
                               lambda g: (g // NA_HEADS, 0, g % NA_HEADS, 0, 0)),
        out_shape=jax.ShapeDtypeStruct((DEPTH, 3, NA_HEADS, NA_R * GRID_W, NA_WIN_ROWS * GRID_W), BF16),
        compiler_params=_cparams("parallel"),
        name="na_bias_table",
    )(rpb_pad)


def _na_attn_kernel(q_ref, k_ref, v_ref, kc_ref, vc_ref, bias_ref, o_ref):
    j = pl.program_id(1)
    lo = jnp.clip(j * NA_R - NA_WIN_H // 2, 0, GRID_ROWS - NA_WIN_ROWS)
    k0 = pl.multiple_of(lo * GRID_W, GRID_W)
    step_type = jnp.where(j == 0, 0, jnp.where(j == NA_STEPS - 1, 2, 1))
    n_keys = NA_WIN_ROWS * GRID_W

    def scores(head):
        lanes = slice((head // 2) * LANES, (head // 2 + 1) * LANES)
        q = q_ref[:, lanes]
        qm = jnp.where(_lane_half_mask(q.shape, head % 2), q, jnp.zeros_like(q))
        s_w = _dot_nt(qm, k_ref[pl.ds(k0, n_keys), lanes]) + bias_ref[step_type, head].astype(F32)
        s_c = _dot_nt(qm, kc_ref[:, lanes].astype(BF16))
        return s_w, s_c

    nxt = scores(0)
    outs = []
    for head in range(NA_HEADS):
        s_w, s_c = nxt
        if head + 1 < NA_HEADS:
            nxt = scores(head + 1)
        lanes = slice((head // 2) * LANES, (head // 2 + 1) * LANES)
        m = jnp.maximum(jnp.max(s_w, axis=-1, keepdims=True), jnp.max(s_c, axis=-1, keepdims=True))
        e_w = jnp.exp2(s_w - m)
        e_c = jnp.exp2(s_c - m)
        inv = 1.0 / (jnp.sum(e_w, axis=-1, keepdims=True) + jnp.sum(e_c, axis=-1, keepdims=True))
        pv = (_dot(e_w.astype(BF16), v_ref[pl.ds(k0, n_keys), lanes])
              + _dot(e_c.astype(BF16), vc_ref[:, lanes].astype(BF16)))
        outs.append(pv * inv)
        if head % 2 == 1:
            o = jnp.where(_lane_half_mask(outs[0].shape, 0), outs[0], outs[1])
            o_ref[:, lanes] = o.astype(o_ref.dtype)
            outs = []


def _na_attention(p_all, cache_k, cache_v, bias_tab, layer):
    q_rows = NA_R * GRID_W
    q_blk0 = N_CTX_TOK // q_rows
    seq_blk0 = N_CTX_TOK // DEC_SEQ
    cache_spec = pl.BlockSpec((None, None, PAST_LEN, NA_WIDTH), lambda b, j: (b, layer, 0, 0))
    return pl.pallas_call(
        _na_attn_kernel,
        grid=(DEC_BATCH, NA_STEPS),
        in_specs=[
            pl.BlockSpec((q_rows, NA_WIDTH), lambda b, j: (q_blk0 + b * NA_STEPS + j, COL_NA_Q // NA_WIDTH)),
            pl.BlockSpec((DEC_SEQ, NA_WIDTH), lambda b, j: (seq_blk0 + b, COL_NA_K // NA_WIDTH)),
            pl.BlockSpec((DEC_SEQ, NA_WIDTH), lambda b, j: (seq_blk0 + b, COL_NA_V // NA_WIDTH)),
            cache_spec, cache_spec,
            pl.BlockSpec((None, 3, NA_HEADS, q_rows, NA_WIN_ROWS * GRID_W), lambda b, j: (layer, 0, 0, 0, 0)),
        ],
        out_specs=pl.BlockSpec((q_rows, NA_WIDTH), lambda b, j: (b * NA_STEPS + j, 0)),
        out_shape=jax.ShapeDtypeStruct((N_LAT_TOK, NA_WIDTH), BF16),
        compiler_params=_cparams("parallel", "arbitrary"),
        name="latent_na_attention",
    )(p_all, p_all, p_all, cache_k, cache_v, bias_tab)


def _diff_attn_kernel(q_ref, k_ref, v_ref, kc_ref, vc_ref, lq1, lk1, lq2, lk2, gs_ref, o_ref,
                      *, lam_init):
    lam = _lambda_value(lq1[...], lk1[...], lq2[...], lk2[...], lam_init)
    k = k_ref[...]
    v = v_ref[...]
    kc = kc_ref[...].astype(BF16)
    vc = vc_ref[...].astype(BF16)
    n_sub = TQ_DIFF // DIFF_SUB

    def scores(i):
        q = q_ref[i * DIFF_SUB:(i + 1) * DIFF_SUB, :]
        out = []
        for half in range(2):
            qm = jnp.where(_lane_half_mask(q.shape, half), q, jnp.zeros_like(q))
            out.append((_dot_nt(qm, k), _dot_nt(qm, kc)))
        return out

    nxt = scores(0)
    for i in range(n_sub):
        cur = nxt
        if i + 1 < n_sub:
            nxt = scores(i + 1)
        probs = []
        for s_l, s_c in cur:
            m = jnp.maximum(jnp.max(s_l, axis=-1, keepdims=True), jnp.max(s_c, axis=-1, keepdims=True))
            e_l = jnp.exp2(s_l - m)
            e_c = jnp.exp2(s_c - m)
            inv = 1.0 / (jnp.sum(e_l, axis=-1, keepdims=True) + jnp.sum(e_c, axis=-1, keepdims=True))
            probs.append((e_l, e_c, inv))
        w1 = probs[0][2]
        w2 = lam * probs[1][2]
        a_l = (probs[0][0] * w1 - probs[1][0] * w2).astype(BF16)
        a_c = (probs[0][1] * w1 - probs[1][1] * w2).astype(BF16)
        o = _dot(a_l, v) + _dot(a_c, vc)
        o_ref[i * DIFF_SUB:(i + 1) * DIFF_SUB, :] = _subln(o, gs_ref[...], lam_init).astype(o_ref.dtype)


def _diff_attention(p_all, cache_k, cache_v, lam_params, g_subln, lam_init, layer):
    steps = DEC_SEQ // TQ_DIFF
    q_blk0 = N_CTX_TOK // TQ_DIFF
    seq_blk0 = N_CTX_TOK // DEC_SEQ
    vec = pl.BlockSpec((1, HEAD_DIM), lambda b, h, j: (0, 0))
    cache_spec = pl.BlockSpec((None, None, PAST_LEN, LANES), lambda b, h, j: (b, layer, 0, h))
    return pl.pallas_call(
        functools.partial(_diff_attn_kernel, lam_init=lam_init),
        grid=(DEC_BATCH, DIFF_HEADS, steps),
        in_specs=[
            pl.BlockSpec((TQ_DIFF, LANES), lambda b, h, j: (q_blk0 + b * steps + j, COL_D_Q // LANES + h)),
            pl.BlockSpec((DEC_SEQ, LANES), lambda b, h, j: (seq_blk0 + b, COL_D_K // LANES + h)),
            pl.BlockSpec((DEC_SEQ, LANES), lambda b, h, j: (seq_blk0 + b, COL_D_V // LANES + h)),
            cache_spec, cache_spec, vec, vec, vec, vec,
            pl.BlockSpec((1, 2 * HEAD_DIM), lambda b, h, j: (0, 0)),
        ],
        out_specs=pl.BlockSpec((TQ_DIFF, LANES), lambda b, h, j: (b * steps + j, h)),
        out_shape=jax.ShapeDtypeStruct((N_LAT_TOK, DIFF_WIDTH), BF16),
        compiler_params=_cparams("parallel", "parallel", "arbitrary"),
        name="latent_diff_attention",
    )(p_all, p_all, p_all, cache_k, cache_v, *lam_params, g_subln)


def _attn_residual(i, x_ref, oc_ref, ona_ref, od_ref, ga_ref, w_ref, dst_ref):
    @pl.when(i < CTX_TILES)
    def _():
        dst_ref[...] = x_ref[...] + ga_ref[0] * _dot(oc_ref[...], w_ref[...])

    @pl.when(i >= CTX_TILES)
    def _():
        acc = _dot(ona_ref[...], w_ref[:NA_WIDTH, :]) + _dot(od_ref[...], w_ref[NA_WIDTH:, :])
        dst_ref[...] = x_ref[...] + ga_ref[0] * acc


def _attn_residual_specs():
    lat_idx = lambda i, *_: (jnp.maximum(i - CTX_TILES, 0), 0)
    return [
        pl.BlockSpec((TM, D_MODEL), lambda i, *_: (jnp.minimum(i, CTX_TILES - 1), 0)),
        pl.BlockSpec((TM, NA_WIDTH), lat_idx),
        pl.BlockSpec((TM, DIFF_WIDTH), lat_idx),
        _mod_spec(2),
        pl.BlockSpec((D_MODEL, D_MODEL), lambda i, *_: (0, 0)),
    ]


def _swiglu_tile(hb, wg, wu):
    g = _dot(hb, wg)
    u = _dot(hb, wu)
    return (g * jax.nn.sigmoid(g)) * u


def _ffn_kernel(x_ref, oc_ref, ona_ref, od_ref, ga1_ref, wo_ref, g_ref, sh_ref, sc_ref, ga_ref,
                wg_ref, wu_ref, wd_ref, y_ref, xmid_scr, h_scr, acc_scr):
    i = pl.program_id(0)
    f = pl.program_id(1)

    @pl.when(f == 0)
    def _():
        _attn_residual(i, x_ref, oc_ref, ona_ref, od_ref, ga1_ref, wo_ref, xmid_scr)
        h = _modulated_norm(xmid_scr[...], g_ref[...], sh_ref[0], sc_ref[0])
        h_scr[...] = h.astype(BF16)
        acc_scr[...] = jnp.zeros_like(acc_scr)

    a = _swiglu_tile(h_scr[...], wg_ref[...], wu_ref[...])
    acc_scr[...] += _dot(a.astype(BF16), wd_ref[...])

    @pl.when(f == pl.num_programs(1) - 1)
    def _():
        y_ref[...] = xmid_scr[...] + ga_ref[0] * acc_scr[...]


def _ffn_dense(x, o_ctx, o_na, o_d, w_out, g, mod, wg, wu, wd):
    tf = FF_TILE_DENSE
    return pl.pallas_call(
        _ffn_kernel,
        grid=(N_TILES, D_FF // tf),
        in_specs=[pl.BlockSpec((TM, D_MODEL), lambda i, f: (i, 0))] + _attn_residual_specs() + [
            pl.BlockSpec((1, D_MODEL), lambda i, f: (0, 0)),
            _mod_spec(3), _mod_spec(4), _mod_spec(5),
            pl.BlockSpec((D_MODEL, tf), lambda i, f: (0, f)),
            pl.BlockSpec((D_MODEL, tf), lambda i, f: (0, f)),
            pl.BlockSpec((tf, D_MODEL), lambda i, f: (f, 0)),
        ],
        out_specs=pl.BlockSpec((TM, D_MODEL), lambda i, f: (i, 0)),
        out_shape=jax.ShapeDtypeStruct((N_TOK, D_MODEL), F32),
        scratch_shapes=[pltpu.VMEM((TM, D_MODEL), F32), pltpu.VMEM((TM, D_MODEL), BF16),
                        pltpu.VMEM((TM, D_MODEL), F32)],
        compiler_params=_cparams("parallel", "arbitrary"),
        name="ffn_dense",
    )(x, o_ctx, o_na, o_d, mod, w_out, g, mod, mod, mod, wg, wu, wd)


ROUTE_E1, ROUTE_E2, ROUTE_G1, ROUTE_G2, ROUTE_R1, ROUTE_R2 = range(6)


def _router_kernel(x_ref, oc_ref, ona_ref, od_ref, ga1_ref, wo_ref, g_ref, sh_ref, sc_ref, wr_ref,
                   xmid_ref, h_ref, route_ref, cnt_ref, carry_scr):
    i = pl.program_id(0)

    @pl.when(i == 0)
    def _():
        carry_scr[...] = jnp.zeros_like(carry_scr)

    _attn_residual(i, x_ref, oc_ref, ona_ref, od_ref, ga1_ref, wo_ref, xmid_ref)
    h = _modulated_norm(xmid_ref[...], g_ref[...], sh_ref[0], sc_ref[0])
    for c in range(ROW_CHUNKS):
        h_ref[pl.ds(c, TM, stride=ROW_CHUNKS), :] = h[:, c * LANES:(c + 1) * LANES]
    logits = jnp.dot(h, wr_ref[...], preferred_element_type=F32, precision=lax.Precision.HIGHEST)
    lane = lax.broadcasted_iota(I32, logits.shape, 1)
    logits = jnp.where(lane < N_EXPERTS, logits, MASK_VALUE)
    m1 = jnp.max(logits, axis=-1, keepdims=True)
    i1 = jnp.min(jnp.where(logits == m1, lane, LANES), axis=-1, keepdims=True)
    rest = jnp.where(lane == i1, MASK_VALUE, logits)
    m2 = jnp.max(rest, axis=-1, keepdims=True)
    i2 = jnp.min(jnp.where(rest == m2, lane, LANES), axis=-1, keepdims=True)
    e2 = jnp.exp(m2 - m1)
    g1 = 1.0 / (1.0 + e2)
    g2 = e2 / (1.0 + e2)

    hit1 = lane == i1
    hit2 = lane == i2
    onehot = jnp.where(hit1 | hit2, 1.0, 0.0)
    row = lax.broadcasted_iota(I32, (TM, TM), 0)
    col = lax.broadcasted_iota(I32, (TM, TM), 1)
    lower = jnp.where(row > col, 1.0, 0.0).astype(BF16)
    before = _dot(lower, onehot.astype(BF16)) + carry_scr[...]
    r1 = jnp.sum(jnp.where(hit1, before, 0.0), axis=-1, keepdims=True)
    r2 = jnp.sum(jnp.where(hit2, before, 0.0), axis=-1, keepdims=True)
    carry_scr[...] += jnp.sum(onehot, axis=0, keepdims=True)

    out = jnp.zeros(logits.shape, F32)
    for slot, val in ((ROUTE_E1, i1.astype(F32)), (ROUTE_E2, i2.astype(F32)), (ROUTE_G1, g1),
                      (ROUTE_G2, g2), (ROUTE_R1, r1), (ROUTE_R2, r2)):
        out = jnp.where(lane == slot, val, out)
    route_ref[...] = out

    @pl.when(i == pl.num_programs(0) - 1)
    def _():
        cnt_ref[...] = jnp.broadcast_to(carry_scr[...], cnt_ref.shape)


def _router(x, o_ctx, o_na, o_d, w_out, g, mod, w_router_pad):
    return pl.pallas_call(
        _router_kernel,
        grid=(N_TILES,),
        in_specs=[pl.BlockSpec((TM, D_MODEL), lambda i: (i, 0))] + _attn_residual_specs() + [
            pl.BlockSpec((1, D_MODEL), lambda i: (0, 0)),
            _mod_spec(3), _mod_spec(4),
            pl.BlockSpec((D_MODEL, LANES), lambda i: (0, 0)),
        ],
        out_specs=[pl.BlockSpec((TM, D_MODEL), lambda i: (i, 0)),
                   pl.BlockSpec((TM * ROW_CHUNKS, LANES), lambda i: (i, 0)),
                   pl.BlockSpec((TM, LANES), lambda i: (i, 0)),
                   pl.BlockSpec((8, LANES), lambda i: (0, 0))],
        out_shape=[jax.ShapeDtypeStruct((N_TOK, D_MODEL), F32),
                   jax.ShapeDtypeStruct((N_TOK * ROW_CHUNKS, LANES), F32),
                   jax.ShapeDtypeStruct((N_TOK, LANES), F32),
                   jax.ShapeDtypeStruct((8, LANES), F32)],
        scratch_shapes=[pltpu.VMEM((1, LANES), F32)],
        compiler_params=_cparams("arbitrary"),
        name="moe_router",
    )(x, o_ctx, o_na, o_d, mod, w_out, g, mod, mod, w_router_pad)


def _moe_plan(route, counts):
    cnt = counts[0, :N_EXPERTS].astype(I32)
    tiles = (cnt + TM_MOE - 1) // TM_MOE
    tile_end = jnp.cumsum(tiles)
    group_start = (tile_end - tiles) * TM_MOE
    e = route[:, ROUTE_E1:ROUTE_E2 + 1].astype(I32)
    r = route[:, ROUTE_R1:ROUTE_R2 + 1].astype(I32)
    slot = (group_start[e] + r).T.reshape(-1)
    tok = jnp.tile(jnp.arange(N_TOK, dtype=I32), 2)
    src_tok = jnp.zeros(((MOE_TILES + 2) * TM_MOE,), I32).at[slot].set(tok)
    n_valid = tile_end[-1:]
    t = jnp.arange(MOE_TILES, dtype=I32)
    tile_expert = jnp.minimum(jnp.sum((t[:, None] >= tile_end[None, :]).astype(I32), axis=1),
                              N_EXPERTS - 1)
    last_expert = tile_expert[jnp.maximum(n_valid[0] - 1, 0)]
    tile_expert = jnp.where(t < n_valid[0], tile_expert, last_expert)
    return tile_expert, n_valid, src_tok, slot


def _moe_kernel(te_ref, nv_ref, src_ref, h_hbm, *refs):
    wg_refs = refs[:MOE_W_SPLIT]
    wu_refs = refs[MOE_W_SPLIT:2 * MOE_W_SPLIT]
    wd_refs = refs[2 * MOE_W_SPLIT:3 * MOE_W_SPLIT]
    y_ref, hrow_scr, hb_scr, acc_scr, sems = refs[3 * MOE_W_SPLIT:]
    t = pl.program_id(0)
    f = pl.program_id(1)
    valid = t < nv_ref[0]
    last_t = pl.num_programs(0) - 1
    last_f = N_FF_MOE - 1

    def row_copy(tile, row, buf):
        tok = src_ref[tile * TM_MOE + row]
        return pltpu.make_async_copy(h_hbm.at[pl.ds(pl.multiple_of(tok * ROW_CHUNKS, ROW_CHUNKS), ROW_CHUNKS), :],
                                     hrow_scr.at[buf, pl.ds(pl.multiple_of(row * ROW_CHUNKS, ROW_CHUNKS), ROW_CHUNKS), :],
                                     sems.at[buf])

    def wait_tile(buf):
        pltpu.make_async_copy(h_hbm.at[pl.ds(0, GATHER_ROWS * ROW_CHUNKS), :],
                              hrow_scr.at[buf, pl.ds(0, GATHER_ROWS * ROW_CHUNKS), :], sems.at[buf]).wait()

    def issue_next_chunk():
        for j in range(GATHER_CHUNK):
            row_copy(t + 1, f * GATHER_CHUNK + j, (t + 1) % 2).start()

    @pl.when((t == 0) & (f == 0))
    def _():
        def issue(j, carry):
            row_copy(0, j, 0).start()
            return carry

        lax.fori_loop(0, GATHER_ROWS, issue, 0, unroll=7)

    @pl.when(f == 0)
    def _():
        wait_tile(t % 2)
        for c in range(ROW_CHUNKS):
            hb_scr[:, c * LANES:(c + 1) * LANES] = hrow_scr[t % 2, pl.ds(c, TM_MOE, stride=ROW_CHUNKS), :].astype(BF16)
        acc_scr[...] = jnp.zeros_like(acc_scr)

    @pl.when(valid)
    def _():
        issue_next_chunk()
        hb = hb_scr[...]
        acc = None
        for wg_ref, wu_ref, wd_ref in zip(wg_refs, wu_refs, wd_refs):
            a = _swiglu_tile(hb, wg_ref[...].astype(BF16), wu_ref[...].astype(BF16))
            d = _dot(a.astype(BF16), wd_ref[...].astype(BF16))
            acc = d if acc is None else acc + d
        acc_scr[...] += acc

    @pl.when(jnp.logical_not(valid))
    def _():
        issue_next_chunk()

    @pl.when(f == last_f)
    def _():
        for c in range(ROW_CHUNKS):
            y_ref[pl.ds(c, TM_MOE, stride=ROW_CHUNKS), :] = acc_scr[:, c * LANES:(c + 1) * LANES]

    @pl.when((t == last_t) & (f == last_f))
    def _():
        wait_tile((t + 1) % 2)


def _moe_experts(tile_expert, n_valid, src_tok, h, wg, wu, wd, layer):
    ts = FF_TILE_MOE // MOE_W_SPLIT
    buf_rows = -(-GATHER_ROWS // 8) * 8

    def f_eff(t, f, nv):
        return jnp.where(t < nv[0], f, N_FF_MOE - 1)

    grid_spec = pltpu.PrefetchScalarGridSpec(
        num_scalar_prefetch=3,
        grid=(MOE_TILES, N_FF_MOE),
        in_specs=[pl.BlockSpec(memory_space=pl.ANY)] + [
            pl.BlockSpec((None, None, D_MODEL, ts),
                         lambda t, f, te, nv, st, s=s: (layer, te[t], 0, MOE_W_SPLIT * f_eff(t, f, nv) + s))
            for _ in range(2) for s in range(MOE_W_SPLIT)
        ] + [
            pl.BlockSpec((None, None, ts, D_MODEL),
                         lambda t, f, te, nv, st, s=s: (layer, te[t], MOE_W_SPLIT * f_eff(t, f, nv) + s, 0))
            for s in range(MOE_W_SPLIT)
        ],
        out_specs=pl.BlockSpec((TM_MOE * ROW_CHUNKS, LANES), lambda t, f, te, nv, st: (t, 0)),
        scratch_shapes=[pltpu.VMEM((2, buf_rows * ROW_CHUNKS, LANES), F32), pltpu.VMEM((TM_MOE, D_MODEL), BF16),
                        pltpu.VMEM((TM_MOE, D_MODEL), F32), pltpu.SemaphoreType.DMA((2,))],
    )
    return pl.pallas_call(
        _moe_kernel,
        grid_spec=grid_spec,
        out_shape=jax.ShapeDtypeStruct((MOE_TILES * TM_MOE * ROW_CHUNKS, LANES), F32),
        compiler_params=_cparams("arbitrary", "arbitrary"),
        name="moe_experts",
    )(tile_expert, n_valid, src_tok, h, *([wg] * MOE_W_SPLIT + [wu] * MOE_W_SPLIT + [wd] * MOE_W_SPLIT))


def _moe_combine_kernel(slot_ref, x_ref, route_ref, ga_ref, y_hbm, o_ref, rows_scr, sems):
    i = pl.program_id(0)

    def issue_tile(tile, buf):
        for c in range(2):
            base = c * N_TOK + tile * TM

            def issue(j, carry, base=base, c=c):
                s = slot_ref[base + j]
                pltpu.make_async_copy(
                    y_hbm.at[pl.ds(pl.multiple_of(s * ROW_CHUNKS, ROW_CHUNKS), ROW_CHUNKS), :],
                    rows_scr.at[buf, c, pl.ds(pl.multiple_of(j * ROW_CHUNKS, ROW_CHUNKS), ROW_CHUNKS), :],
                    sems.at[buf]).start()
                return carry

            lax.fori_loop(0, TM, issue, 0, unroll=8)

    @pl.when(i == 0)
    def _():
        issue_tile(0, 0)

    @pl.when(i + 1 < pl.num_programs(0))
    def _():
        issue_tile(i + 1, (i + 1) % 2)

    buf = i % 2
    for c in range(2):
        pltpu.make_async_copy(y_hbm.at[pl.ds(0, TM * ROW_CHUNKS), :], rows_scr.at[buf, c], sems.at[buf]).wait()
    route = route_ref[...]
    g1 = route[:, ROUTE_G1:ROUTE_G1 + 1]
    g2 = route[:, ROUTE_G2:ROUTE_G2 + 1]
    for ch in range(ROW_CHUNKS):
        lanes = slice(ch * LANES, (ch + 1) * LANES)
        mix = (g1 * rows_scr[buf, 0, pl.ds(ch, TM, stride=ROW_CHUNKS), :]
               + g2 * rows_scr[buf, 1, pl.ds(ch, TM, stride=ROW_CHUNKS), :])
        o_ref[:, lanes] = x_ref[:, lanes] + ga_ref[0][:, lanes] * mix


def _moe_combine(slot, x, route, mod, y):
    grid_spec = pltpu.PrefetchScalarGridSpec(
        num_scalar_prefetch=1,
        grid=(N_TILES,),
        in_specs=[
            pl.BlockSpec((TM, D_MODEL), lambda i, s: (i, 0)),
            pl.BlockSpec((TM, LANES), lambda i, s: (i, 0)),
            _mod_spec(5),
            pl.BlockSpec(memory_space=pl.ANY),
        ],
        out_specs=pl.BlockSpec((TM, D_MODEL), lambda i, s: (i, 0)),
        scratch_shapes=[pltpu.VMEM((2, 2, TM * ROW_CHUNKS, LANES), F32), pltpu.SemaphoreType.DMA((2,))],
    )
    return pl.pallas_call(
        _moe_combine_kernel,
        grid_spec=grid_spec,
        out_shape=jax.ShapeDtypeStruct((N_TOK, D_MODEL), F32),
        compiler_params=_cparams("arbitrary"),
        name="moe_combine",
    )(slot, x, route, mod, y)


def _final_norm_kernel(x_ref, g_ref, y_ref):
    x = x_ref[...]
    y_ref[...] = (x * lax.rsqrt(jnp.mean(x * x, axis=-1, keepdims=True) + EPS)) * g_ref[...]


def _final_norm(x, g, tile0, n_tok):
    return pl.pallas_call(
        _final_norm_kernel,
        grid=(n_tok // TM,),
        in_specs=[pl.BlockSpec((TM, D_MODEL), lambda i: (tile0 + i, 0)),
                  pl.BlockSpec((1, D_MODEL), lambda i: (0, 0))],
        out_specs=pl.BlockSpec((TM, D_MODEL), lambda i: (i, 0)),
        out_shape=jax.ShapeDtypeStruct((n_tok, D_MODEL), F32),
        compiler_params=_cparams("parallel"),
        name="final_norm",
    )(x, g)


def _rope_tables():
    t = jnp.arange(DEC_SEQ)
    row = (t // GRID_W).astype(F32)
    col = (t % GRID_W).astype(F32)
    half = HEAD_DIM // 2
    inv = 1.0 / (ROPE_THETA ** (jnp.arange(0, half, 2, dtype=F32) / half))
    ar = row[:, None] * inv[None]
    ac = col[:, None] * inv[None]
    ang = jnp.concatenate([ar, ar, ac, ac], axis=-1)
    ang = jnp.concatenate([ang, jnp.zeros((TM, HEAD_DIM), F32)], axis=0)
    cos = jnp.tile(jnp.cos(ang), (1, LANES // HEAD_DIM))
    sin = jnp.tile(jnp.sin(ang), (1, LANES // HEAD_DIM))
    first_half = (jnp.arange(LANES) % 32) < 16
    sin_a = jnp.where(first_half[None, :], -sin, 0.0)
    sin_b = jnp.where(first_half[None, :], 0.0, sin)
    return cos, sin_a, sin_b


def kernel(x_prompt, x_sample, c, cache_na_k, cache_na_v, cache_diff_k, cache_diff_v, c_ctx, w_ada, b_ada, g_mix, w_in, rpb, lam_q1, lam_k1, lam_q2, lam_k2, g_subln, w_out, g_ffn, w_ffn_gate, w_ffn_up, w_ffn_down, w_router, w_moe_gate, w_moe_up, w_moe_down, g_final):
    x = jnp.concatenate([x_prompt.reshape(N_CTX_TOK, D_MODEL), x_sample.reshape(N_LAT_TOK, D_MODEL)])

    cvec = jnp.zeros((MOD_ROWS, D_MODEL), F32).at[0].set(c_ctx).at[1:1 + DEC_BATCH].set(c)
    mod_all = _modulation(cvec, w_ada, b_ada).reshape(DEPTH, MOD_ROWS * 6, 1, D_MODEL)

    col = jnp.arange(IN_WIDTH)
    is_q = (col < COL_NA_K) | ((col >= COL_D_Q) & (col < COL_D_K))
    q_scale = jnp.where(is_q, ATTN_SCALE * LOG2E, 1.0).astype(F32)
    w_in_b = (w_in * q_scale[None, None, :]).astype(BF16)
    w_out_b = w_out.astype(BF16)
    w_fg, w_fu, w_fd = (w.astype(BF16) for w in (w_ffn_gate, w_ffn_up, w_ffn_down))
    w_router_pad = jnp.pad(w_router, ((0, 0), (0, 0), (0, LANES - N_EXPERTS)))

    rope_tabs = _rope_tables()
    bias_tabs = _na_bias_tables(rpb)
    cna_k = cache_na_k.reshape(DEC_BATCH, DEPTH, PAST_LEN, NA_WIDTH)
    cna_v = cache_na_v.reshape(DEC_BATCH, DEPTH, PAST_LEN, NA_WIDTH)
    cd_k = cache_diff_k.reshape(DEC_BATCH, DEPTH, PAST_LEN, DIFF_WIDTH)
    cd_v = cache_diff_v.reshape(DEC_BATCH, DEPTH, PAST_LEN, DIFF_WIDTH)

    kv_layers = []
    for l in range(DEPTH):
        lam_init = 0.8 - 0.6 * math.exp(-0.3 * l)
        mod = mod_all[l]
        g_mix_l = g_mix[l][None, :]
        g_ffn_l = g_ffn[l][None, :]
        g_sub_l = g_subln[l][None, :]
        lam_params = tuple(p[l][None, :] for p in (lam_q1, lam_k1, lam_q2, lam_k2))

        p_all, *kv_ctx = _inproj(x, g_mix_l, mod, w_in_b[l], rope_tabs)
        kv_layers.append(kv_ctx)
        o_ctx = _ctx_attention(p_all, lam_params, g_sub_l, lam_init)
        o_na = _na_attention(p_all, cna_k, cna_v, bias_tabs, l)
        o_d = _diff_attention(p_all, cd_k, cd_v, lam_params, g_sub_l, lam_init, l)

        i = l // 2
        if l % 2 == 0:
            x = _ffn_dense(x, o_ctx, o_na, o_d, w_out_b[l], g_ffn_l, mod, w_fg[i], w_fu[i], w_fd[i])
        else:
            x, h, route, counts = _router(x, o_ctx, o_na, o_d, w_out_b[l], g_ffn_l, mod, w_router_pad[i])
            tile_expert, n_valid, src_tok, slot = _moe_plan(route, counts)
            y = _moe_experts(tile_expert, n_valid, src_tok, h, w_moe_gate, w_moe_up, w_moe_down, i)
            x = _moe_combine(slot, x, route, mod, y)

    g_fin = g_final[None, :]
    y_prompt = _final_norm(x, g_fin, 0, N_CTX_TOK).reshape(BATCH, SEQ, D_MODEL)
    y_sample = _final_norm(x, g_fin, CTX_TILES, N_LAT_TOK).reshape(DEC_BATCH, DEC_SEQ, D_MODEL)

    kv = [jnp.stack([layer_kv[k] for layer_kv in kv_layers], axis=1) for k in range(4)]
    new_na_k = kv[0].reshape(BATCH, DEPTH, SEQ, NA_HEADS, HEAD_DIM)
    new_na_v = kv[1].reshape(BATCH, DEPTH, SEQ, NA_HEADS, HEAD_DIM)
    new_diff_k = kv[2].reshape(BATCH, DEPTH, SEQ, DIFF_HEADS, 2, HEAD_DIM)
    new_diff_v = kv[3].reshape(BATCH, DEPTH, SEQ, DIFF_HEADS, 2 * HEAD_DIM)
    return (y_prompt, y_sample, new_na_k, new_na_v, new_diff_k, new_diff_v)
```

```python
import functools
import math

import numpy as np
import jax
import jax.numpy as jnp
from jax import lax
from jax.experimental import pallas as pl
from jax.experimental.pallas import tpu as pltpu

F32 = jnp.float32
BF16 = jnp.bfloat16
I32 = jnp.int32

D_MODEL = 1024
DEPTH = 4
BATCH = 16
SEQ = 256
DEC_BATCH = 8
DEC_SEQ = 2048
PAST_LEN = 256
GRID_W = 64
GRID_ROWS = DEC_SEQ // GRID_W
HEAD_DIM = 64
NA_HEADS = 8
NA_WIDTH = 512
DIFF_HEADS = 4
DIFF_WIDTH = 512
IN_WIDTH = 3072
NA_WIN_H = 8
NA_WIN_W = 16
ROPE_THETA = 10000.0
D_FF = 2816
N_EXPERTS = 8
D_FF_EXPERT = 3584
EPS = 1e-6
SUBLN_EPS = 1e-5
ATTN_SCALE = HEAD_DIM ** -0.5

LANES = 128
ROW_CHUNKS = D_MODEL // LANES
N_CTX_TOK = BATCH * SEQ
N_LAT_TOK = DEC_BATCH * DEC_SEQ
N_TOK = N_CTX_TOK + N_LAT_TOK
MOD_ROWS = 16
MASK_VALUE = -1e30

COL_NA_Q, COL_NA_K, COL_NA_V = 0, 512, 1024
COL_D_Q, COL_D_K, COL_D_V = 1536, 2048, 2560

TM = 512
N_TILES = N_TOK // TM
CTX_TILES = N_CTX_TOK // TM
TILES_PER_SEQ = DEC_SEQ // TM
FF_TILE_DENSE = 1408
FF_TILE_MOE = 512
N_FF_MOE = D_FF_EXPERT // FF_TILE_MOE
TM_MOE = 1024
MOE_W_SPLIT = 1
TQ_DIFF = 512
DIFF_SUB = 128
NA_R = 4
NA_WIN_ROWS = 12
NA_STEPS = GRID_ROWS // NA_R
MOE_SLOTS = 2 * N_TOK
MOE_TILES = MOE_SLOTS // TM_MOE + N_EXPERTS
GATHER_CHUNK = -(-TM_MOE // N_FF_MOE)
GATHER_ROWS = GATHER_CHUNK * N_FF_MOE
LOG2E = 1.4426950408889634
VMEM_LIMIT = 56 * 1024 * 1024


def _cparams(*sem):
    return pltpu.CompilerParams(dimension_semantics=sem, vmem_limit_bytes=VMEM_LIMIT)


def _dot(a, b):
    return jnp.dot(a, b, preferred_element_type=F32)


def _dot_nt(a, b):
    return lax.dot_general(a, b, (((1,), (1,)), ((), ())), preferred_element_type=F32)


def _modulated_norm(x, g, shift, scale):
    xn = x * lax.rsqrt(jnp.mean(x * x, axis=-1, keepdims=True) + EPS)
    return (xn * g) * (1.0 + scale) + shift


def _mod_row(i):
    return jnp.maximum(i // TILES_PER_SEQ - CTX_TILES // TILES_PER_SEQ + 1, 0)


def _mod_spec(chunk):
    return pl.BlockSpec((1, 1, D_MODEL), lambda i, *_: (_mod_row(i) * 6 + chunk, 0, 0))


def _mod_kernel(c_ref, w_ref, b_ref, o_ref):
    cv = c_ref[...]
    s = cv * jax.nn.sigmoid(cv)
    o_ref[...] = jnp.dot(s, w_ref[...], preferred_element_type=F32,
                         precision=lax.Precision.HIGHEST) + b_ref[...]


def _modulation(cvec, w_ada, b_ada):
    tn = 1536
    n = 6 * D_MODEL
    return pl.pallas_call(
        _mod_kernel,
        grid=(DEPTH, n // tn),
        in_specs=[
            pl.BlockSpec((MOD_ROWS, D_MODEL), lambda l, j: (0, 0)),
            pl.BlockSpec((None, D_MODEL, tn), lambda l, j: (l, 0, j)),
            pl.BlockSpec((None, 1, tn), lambda l, j: (l, 0, j)),
        ],
        out_specs=pl.BlockSpec((None, MOD_ROWS, tn), lambda l, j: (l, 0, j)),
        out_shape=jax.ShapeDtypeStruct((DEPTH, MOD_ROWS, n), F32),
        compiler_params=_cparams("parallel", "parallel"),
        name="adaln_modulation",
    )(cvec, w_ada, b_ada.reshape(DEPTH, 1, n))


def _inproj_kernel(x_ref, g_ref, sh_ref, sc_ref, w_ref, cos_ref, sina_ref, sinb_ref, o_ref, *kv_refs):
    i = pl.program_id(0)
    h = _modulated_norm(x_ref[...], g_ref[...], sh_ref[0], sc_ref[0]).astype(BF16)
    chunk = 512
    for c in range(IN_WIDTH // chunk):
        col = c * chunk
        acc = _dot(h, w_ref[:, col:col + chunk])
        for k_i, src in enumerate((COL_NA_K, COL_NA_V, COL_D_K, COL_D_V)):
            if src == col:
                @pl.when(i < CTX_TILES)
                def _(acc=acc, k_i=k_i):
                    for b in range(TM // SEQ):
                        kv_refs[k_i][b] = acc[b * SEQ:(b + 1) * SEQ, :]
        if COL_D_Q <= col < COL_D_V:
            parts = []
            for j in range(chunk // LANES):
                blk = acc[:, j * LANES:(j + 1) * LANES]
                parts.append(blk * cos_ref[...]
                             + pltpu.roll(blk, LANES - 16, 1) * sina_ref[...]
                             + pltpu.roll(blk, 16, 1) * sinb_ref[...])
            acc = jnp.concatenate(parts, axis=1)
        o_ref[:, col:col + chunk] = acc.astype(o_ref.dtype)


def _inproj(x, g, mod, w_bf16, rope_tabs):
    rope_spec = pl.BlockSpec(
        (TM, LANES),
        lambda i: (jnp.where(i < CTX_TILES, TILES_PER_SEQ, (i - CTX_TILES) % TILES_PER_SEQ), 0))
    return pl.pallas_call(
        _inproj_kernel,
        grid=(N_TILES,),
        in_specs=[
            pl.BlockSpec((TM, D_MODEL), lambda i: (i, 0)),
            pl.BlockSpec((1, D_MODEL), lambda i: (0, 0)),
            _mod_spec(0), _mod_spec(1),
            pl.BlockSpec((D_MODEL, IN_WIDTH), lambda i: (0, 0)),
            rope_spec, rope_spec, rope_spec,
        ],
        out_specs=[pl.BlockSpec((TM, IN_WIDTH), lambda i: (i, 0))]
        + [pl.BlockSpec((TM // SEQ, SEQ, 512), lambda i: (jnp.minimum(i, CTX_TILES - 1), 0, 0))] * 4,
        out_shape=[jax.ShapeDtypeStruct((N_TOK, IN_WIDTH), BF16)]
        + [jax.ShapeDtypeStruct((BATCH, SEQ, 512), F32)] * 4,
        compiler_params=_cparams("arbitrary"),
        name="inproj",
    )(x, g, mod, mod, w_bf16, *rope_tabs)


def _lane_half_mask(shape, half):
    lane = lax.broadcasted_iota(I32, shape, len(shape) - 1)
    return (lane < HEAD_DIM) if half == 0 else (lane >= HEAD_DIM)


def _lambda_value(lq1, lk1, lq2, lk2, lam_init):
    a = jnp.sum(lq1 * lk1, axis=-1, keepdims=True)
    b = jnp.sum(lq2 * lk2, axis=-1, keepdims=True)
    return jnp.exp(a) - jnp.exp(b) + lam_init


def _subln(o, g, lam_init):
    on = o * lax.rsqrt(jnp.mean(o * o, axis=-1, keepdims=True) + SUBLN_EPS)
    return (on * g) * (1.0 - lam_init)


def _ctx_attn_kernel(p_ref, lq1, lk1, lq2, lk2, gs_ref, o_ref, *, lam_init):
    lam = _lambda_value(lq1[...], lk1[...], lq2[...], lk2[...], lam_init)
    for hp in range(NA_HEADS // 2):
        q = p_ref[:, COL_NA_Q + hp * LANES:COL_NA_Q + (hp + 1) * LANES]
        k = p_ref[:, COL_NA_K + hp * LANES:COL_NA_K + (hp + 1) * LANES]
        v = p_ref[:, COL_NA_V + hp * LANES:COL_NA_V + (hp + 1) * LANES]
        outs = []
        for half in range(2):
            qm = jnp.where(_lane_half_mask(q.shape, half), q, jnp.zeros_like(q))
            s = _dot_nt(qm, k)
            m = jnp.max(s, axis=-1, keepdims=True)
            e = jnp.exp2(s - m)
            inv = 1.0 / jnp.sum(e, axis=-1, keepdims=True)
            outs.append(_dot(e.astype(BF16), v) * inv)
        o = jnp.where(_lane_half_mask(outs[0].shape, 0), outs[0], outs[1])
        o_ref[:, hp * LANES:(hp + 1) * LANES] = o.astype(o_ref.dtype)
    for h in range(DIFF_HEADS):
        q = p_ref[:, COL_D_Q + h * LANES:COL_D_Q + (h + 1) * LANES]
        k = p_ref[:, COL_D_K + h * LANES:COL_D_K + (h + 1) * LANES]
        v = p_ref[:, COL_D_V + h * LANES:COL_D_V + (h + 1) * LANES]
        ps = []
        for half in range(2):
            qm = jnp.where(_lane_half_mask(q.shape, half), q, jnp.zeros_like(q))
            s = _dot_nt(qm, k)
            m = jnp.max(s, axis=-1, keepdims=True)
            e = jnp.exp2(s - m)
            ps.append(e / jnp.sum(e, axis=-1, keepdims=True))
        a = (ps[0] - lam * ps[1]).astype(BF16)
        o = _subln(_dot(a, v), gs_ref[...], lam_init)
        o_ref[:, NA_WIDTH + h * LANES:NA_WIDTH + (h + 1) * LANES] = o.astype(o_ref.dtype)


def _ctx_attention(p_all, lam_params, g_subln, lam_init):
    vec = pl.BlockSpec((1, HEAD_DIM), lambda b: (0, 0))
    return pl.pallas_call(
        functools.partial(_ctx_attn_kernel, lam_init=lam_init),
        grid=(BATCH,),
        in_specs=[pl.BlockSpec((SEQ, IN_WIDTH), lambda b: (b, 0)), vec, vec, vec, vec,
                  pl.BlockSpec((1, 2 * HEAD_DIM), lambda b: (0, 0))],
        out_specs=pl.BlockSpec((SEQ, D_MODEL), lambda b: (b, 0)),
        out_shape=jax.ShapeDtypeStruct((N_CTX_TOK, D_MODEL), BF16),
        compiler_params=_cparams("parallel"),
        name="ctx_attention",
    )(p_all, *lam_params, g_subln)


def _bias_table_kernel(rpb_ref, o_ref):
    dr_plan, ok_plan = _na_window_plan()
    shape = (GRID_W, LANES)
    qc = lax.broadcasted_iota(I32, shape, 0)
    lane = lax.broadcasted_iota(I32, shape, 1)
    kc = jnp.bitwise_and(lane, GRID_W - 1)
    cs = jnp.clip(qc - NA_WIN_W // 2, 0, GRID_W - NA_WIN_W)
    in_window = (kc >= cs) & (kc < cs + NA_WIN_W)
    low_half = lane < GRID_W
    masked = jnp.full(shape, MASK_VALUE, F32)

    pieces = {}

    def piece(dr, parity):
        if (dr, parity) not in pieces:
            base = jnp.broadcast_to(rpb_ref[dr:dr + 1, :], shape) * LOG2E
            shifted = pltpu.roll(base, 0, 1, stride=1, stride_axis=0)
            pieces[(dr, parity)] = pltpu.roll(shifted, (LANES - (NA_WIN_W - 1) + GRID_W * parity) % LANES, 1)
        return pieces[(dr, parity)]

    for t in range(3):
        for jr in range(NA_R):
            for p in range(NA_WIN_ROWS // 2):
                halves = []
                for parity in range(2):
                    i = 2 * p + parity
                    halves.append(piece(int(dr_plan[t, jr, i]), parity) if ok_plan[t, jr, i] else masked)
                blk = jnp.where(in_window, jnp.where(low_half, halves[0], halves[1]), MASK_VALUE)
                o_ref[t, jr * GRID_W:(jr + 1) * GRID_W, p * LANES:(p + 1) * LANES] = blk.astype(o_ref.dtype)


def _na_window_plan():
    dr = np.zeros((3, NA_R, NA_WIN_ROWS), np.int32)
    ok = np.zeros((3, NA_R, NA_WIN_ROWS), bool)
    for t, r0 in enumerate((0, NA_R, GRID_ROWS - NA_R)):
        lo = min(max(r0 - NA_WIN_H // 2, 0), GRID_ROWS - NA_WIN_ROWS)
        for jr in range(NA_R):
            r = r0 + jr
            rs = min(max(r - NA_WIN_H // 2, 0), GRID_ROWS - NA_WIN_H)
            for i in range(NA_WIN_ROWS):
                key_row = lo + i
                if rs <= key_row < rs + NA_WIN_H:
                    ok[t, jr, i] = True
                    dr[t, jr, i] = key_row - r + NA_WIN_H - 1
    return dr, ok


def _na_bias_tables(rpb):
    n_dr = 2 * NA_WIN_H - 1
    rpb_pad = jnp.pad(rpb.reshape(DEPTH * NA_HEADS, n_dr, 2 * NA_WIN_W - 1),
                      ((0, 0), (0, 16 - n_dr), (0, LANES - (2 * NA_WIN_W - 1))))
    return pl.pallas_call(
        _bias_table_kernel,
        grid=(DEPTH * NA_HEADS,),
        in_specs=[pl.BlockSpec((None, 16, LANES), lambda g: (g, 0, 0))],
        out_specs=pl.BlockSpec((None, 3, None, NA_R * GRID_W, NA_WIN_ROWS * GRID_W),
                               lambda g: (g // NA_HEADS, 0, g % NA_HEADS, 0, 0)),
        out_shape=jax.ShapeDtypeStruct((DEPTH, 3, NA_HEADS, NA_R * GRID_W, NA_WIN_ROWS * GRID_W), BF16),
        compiler_params=_cparams("parallel"),
        name="na_bias_table",
    )(rpb_pad)


def _na_attn_kernel(q_ref, k_ref, v_ref, kc_ref, vc_ref, bias_ref, o_ref):
    j = pl.program_id(1)
    lo = jnp.clip(j * NA_R - NA_WIN_H // 2, 0, GRID_ROWS - NA_WIN_ROWS)
    k0 = pl.multiple_of(lo * GRID_W, GRID_W)
    step_type = jnp.where(j == 0, 0, jnp.where(j == NA_STEPS - 1, 2, 1))
    n_keys = NA_WIN_ROWS * GRID_W

    def scores(head):
        lanes = slice((head // 2) * LANES, (head // 2 + 1) * LANES)
        q = q_ref[:, lanes]
        qm = jnp.where(_lane_half_mask(q.shape, head % 2), q, jnp.zeros_like(q))
        s_w = _dot_nt(qm, k_ref[pl.ds(k0, n_keys), lanes]) + bias_ref[step_type, head].astype(F32)
        s_c = _dot_nt(qm, kc_ref[:, lanes].astype(BF16))
        return s_w, s_c

    nxt = scores(0)
    outs = []
    for head in range(NA_HEADS):
        s_w, s_c = nxt
        if head + 1 < NA_HEADS:
            nxt = scores(head + 1)
        lanes = slice((head // 2) * LANES, (head // 2 + 1) * LANES)
        m = jnp.maximum(jnp.max(s_w, axis=-1, keepdims=True), jnp.max(s_c, axis=-1, keepdims=True))
        e_w = jnp.exp2(s_w - m)
        e_c = jnp.exp2(s_c - m)
        inv = 1.0 / (jnp.sum(e_w, axis=-1, keepdims=True) + jnp.sum(e_c, axis=-1, keepdims=True))
        pv = (_dot(e_w.astype(BF16), v_ref[pl.ds(k0, n_keys), lanes])
              + _dot(e_c.astype(BF16), vc_ref[:, lanes].astype(BF16)))
        outs.append(pv * inv)
        if head % 2 == 1:
            o = jnp.where(_lane_half_mask(outs[0].shape, 0), outs[0], outs[1])
            o_ref[:, lanes] = o.astype(o_ref.dtype)
            outs = []


def _na_attention(p_all, cache_k, cache_v, bias_tab, layer):
    q_rows = NA_R * GRID_W
    q_blk0 = N_CTX_TOK // q_rows
    seq_blk0 = N_CTX_TOK // DEC_SEQ
    cache_spec = pl.BlockSpec((None, None, PAST_LEN, NA_WIDTH), lambda b, j: (b, layer, 0, 0))
    return pl.pallas_call(
        _na_attn_kernel,
        grid=(DEC_BATCH, NA_STEPS),
        in_specs=[
            pl.BlockSpec((q_rows, NA_WIDTH), lambda b, j: (q_blk0 + b * NA_STEPS + j, COL_NA_Q // NA_WIDTH)),
            pl.BlockSpec((DEC_SEQ, NA_WIDTH), lambda b, j: (seq_blk0 + b, COL_NA_K // NA_WIDTH)),
            pl.BlockSpec((DEC_SEQ, NA_WIDTH), lambda b, j: (seq_blk0 + b, COL_NA_V // NA_WIDTH)),
            cache_spec, cache_spec,
            pl.BlockSpec((None, 3, NA_HEADS, q_rows, NA_WIN_ROWS * GRID_W), lambda b, j: (layer, 0, 0, 0, 0)),
        ],
        out_specs=pl.BlockSpec((q_rows, NA_WIDTH), lambda b, j: (b * NA_STEPS + j, 0)),
        out_shape=jax.ShapeDtypeStruct((N_LAT_TOK, NA_WIDTH), BF16),
        compiler_params=_cparams("parallel", "arbitrary"),
        name="latent_na_attention",
    )(p_all, p_all, p_all, cache_k, cache_v, bias_tab)


def _diff_attn_kernel(q_ref, k_ref, v_ref, kc_ref, vc_ref, lq1, lk1, lq2, lk2, gs_ref, o_ref,
                      *, lam_init):
    lam = _lambda_value(lq1[...], lk1[...], lq2[...], lk2[...], lam_init)
    k = k_ref[...]
    v = v_ref[...]
    kc = kc_ref[...].astype(BF16)
    vc = vc_ref[...].astype(BF16)
    n_sub = TQ_DIFF // DIFF_SUB

    def scores(i):
        q = q_ref[i * DIFF_SUB:(i + 1) * DIFF_SUB, :]
        out = []
        for half in range(2):
            qm = jnp.where(_lane_half_mask(q.shape, half), q, jnp.zeros_like(q))
            out.append((_dot_nt(qm, k), _dot_nt(qm, kc)))
        return out

    nxt = scores(0)
    for i in range(n_sub):
        cur = nxt
        if i + 1 < n_sub:
            nxt = scores(i + 1)
        probs = []
        for s_l, s_c in cur:
            m = jnp.maximum(jnp.max(s_l, axis=-1, keepdims=True), jnp.max(s_c, axis=-1, keepdims=True))
            e_l = jnp.exp2(s_l - m)
            e_c = jnp.exp2(s_c - m)
            inv = 1.0 / (jnp.sum(e_l, axis=-1, keepdims=True) + jnp.sum(e_c, axis=-1, keepdims=True))
            probs.append((e_l, e_c, inv))
        r = lam * probs[1][2] / probs[0][2]
        a_l = (probs[0][0] - probs[1][0] * r).astype(BF16)
        a_c = (probs[0][1] - probs[1][1] * r).astype(BF16)
        o = (_dot(a_l, v) + _dot(a_c, vc)) * probs[0][2]
        o_ref[i * DIFF_SUB:(i + 1) * DIFF_SUB, :] = _subln(o, gs_ref[...], lam_init).astype(o_ref.dtype)


def _diff_attention(p_all, cache_k, cache_v, lam_params, g_subln, lam_init, layer):
    steps = DEC_SEQ // TQ_DIFF
    q_blk0 = N_CTX_TOK // TQ_DIFF
    seq_blk0 = N_CTX_TOK // DEC_SEQ
    vec = pl.BlockSpec((1, HEAD_DIM), lambda b, h, j: (0, 0))
    cache_spec = pl.BlockSpec((None, None, PAST_LEN, LANES), lambda b, h, j: (b, layer, 0, h))
    return pl.pallas_call(
        functools.partial(_diff_attn_kernel, lam_init=lam_init),
        grid=(DEC_BATCH, DIFF_HEADS, steps),
        in_specs=[
            pl.BlockSpec((TQ_DIFF, LANES), lambda b, h, j: (q_blk0 + b * steps + j, COL_D_Q // LANES + h)),
            pl.BlockSpec((DEC_SEQ, LANES), lambda b, h, j: (seq_blk0 + b, COL_D_K // LANES + h)),
            pl.BlockSpec((DEC_SEQ, LANES), lambda b, h, j: (seq_blk0 + b, COL_D_V // LANES + h)),
            cache_spec, cache_spec, vec, vec, vec, vec,
            pl.BlockSpec((1, 2 * HEAD_DIM), lambda b, h, j: (0, 0)),
        ],
        out_specs=pl.BlockSpec((TQ_DIFF, LANES), lambda b, h, j: (b * steps + j, h)),
        out_shape=jax.ShapeDtypeStruct((N_LAT_TOK, DIFF_WIDTH), BF16),
        compiler_params=_cparams("parallel", "parallel", "arbitrary"),
        name="latent_diff_attention",
    )(p_all, p_all, p_all, cache_k, cache_v, *lam_params, g_subln)


def _attn_residual(i, x_ref, oc_ref, ona_ref, od_ref, ga_ref, w_ref, dst_ref):
    @pl.when(i < CTX_TILES)
    def _():
        dst_ref[...] = x_ref[...] + ga_ref[0] * _dot(oc_ref[...], w_ref[...])

    @pl.when(i >= CTX_TILES)
    def _():
        acc = _dot(ona_ref[...], w_ref[:NA_WIDTH, :]) + _dot(od_ref[...], w_ref[NA_WIDTH:, :])
        dst_ref[...] = x_ref[...] + ga_ref[0] * acc


def _attn_residual_specs():
    lat_idx = lambda i, *_: (jnp.maximum(i - CTX_TILES, 0), 0)
    return [
        pl.BlockSpec((TM, D_MODEL), lambda i, *_: (jnp.minimum(i, CTX_TILES - 1), 0)),
        pl.BlockSpec((TM, NA_WIDTH), lat_idx),
        pl.BlockSpec((TM, DIFF_WIDTH), lat_idx),
        _mod_spec(2),
        pl.BlockSpec((D_MODEL, D_MODEL), lambda i, *_: (0, 0)),
    ]


def _swiglu_tile(hb, wg, wu):
    g = _dot(hb, wg)
    u = _dot(hb, wu)
    return (g * jax.nn.sigmoid(g)) * u


def _ffn_kernel(x_ref, oc_ref, ona_ref, od_ref, ga1_ref, wo_ref, g_ref, sh_ref, sc_ref, ga_ref,
                wg_ref, wu_ref, wd_ref, y_ref, xmid_scr, h_scr, acc_scr):
    i = pl.program_id(0)
    f = pl.program_id(1)

    @pl.when(f == 0)
    def _():
        _attn_residual(i, x_ref, oc_ref, ona_ref, od_ref, ga1_ref, wo_ref, xmid_scr)
        h = _modulated_norm(xmid_scr[...], g_ref[...], sh_ref[0], sc_ref[0])
        h_scr[...] = h.astype(BF16)
        acc_scr[...] = jnp.zeros_like(acc_scr)

    a = _swiglu_tile(h_scr[...], wg_ref[...], wu_ref[...])
    acc_scr[...] += _dot(a.astype(BF16), wd_ref[...])

    @pl.when(f == pl.num_programs(1) - 1)
    def _():
        y_ref[...] = xmid_scr[...] + ga_ref[0] * acc_scr[...]


def _ffn_dense(x, o_ctx, o_na, o_d, w_out, g, mod, wg, wu, wd):
    tf = FF_TILE_DENSE
    return pl.pallas_call(
        _ffn_kernel,
        grid=(N_TILES, D_FF // tf),
        in_specs=[pl.BlockSpec((TM, D_MODEL), lambda i, f: (i, 0))] + _attn_residual_specs() + [
            pl.BlockSpec((1, D_MODEL), lambda i, f: (0, 0)),
            _mod_spec(3), _mod_spec(4), _mod_spec(5),
            pl.BlockSpec((D_MODEL, tf), lambda i, f: (0, f)),
            pl.BlockSpec((D_MODEL, tf), lambda i, f: (0, f)),
            pl.BlockSpec((tf, D_MODEL), lambda i, f: (f, 0)),
        ],
        out_specs=pl.BlockSpec((TM, D_MODEL), lambda i, f: (i, 0)),
        out_shape=jax.ShapeDtypeStruct((N_TOK, D_MODEL), F32),
        scratch_shapes=[pltpu.VMEM((TM, D_MODEL), F32), pltpu.VMEM((TM, D_MODEL), BF16),
                        pltpu.VMEM((TM, D_MODEL), F32)],
        compiler_params=_cparams("parallel", "arbitrary"),
        name="ffn_dense",
    )(x, o_ctx, o_na, o_d, mod, w_out, g, mod, mod, mod, wg, wu, wd)


ROUTE_E1, ROUTE_E2, ROUTE_G1, ROUTE_G2, ROUTE_R1, ROUTE_R2 = range(6)


def _router_kernel(x_ref, oc_ref, ona_ref, od_ref, ga1_ref, wo_ref, g_ref, sh_ref, sc_ref, wr_ref,
                   xmid_ref, h_ref, route_ref, cnt_ref, carry_scr):
    i = pl.program_id(0)

    @pl.when(i == 0)
    def _():
        carry_scr[...] = jnp.zeros_like(carry_scr)

    _attn_residual(i, x_ref, oc_ref, ona_ref, od_ref, ga1_ref, wo_ref, xmid_ref)
    h = _modulated_norm(xmid_ref[...], g_ref[...], sh_ref[0], sc_ref[0])
    for c in range(ROW_CHUNKS):
        h_ref[pl.ds(c, TM, stride=ROW_CHUNKS), :] = h[:, c * LANES:(c + 1) * LANES]
    logits = jnp.dot(h, wr_ref[...], preferred_element_type=F32, precision=lax.Precision.HIGHEST)
    lane = lax.broadcasted_iota(I32, logits.shape, 1)
    logits = jnp.where(lane < N_EXPERTS, logits, MASK_VALUE)
    m1 = jnp.max(logits, axis=-1, keepdims=True)
    i1 = jnp.min(jnp.where(logits == m1, lane, LANES), axis=-1, keepdims=True)
    rest = jnp.where(lane == i1, MASK_VALUE, logits)
    m2 = jnp.max(rest, axis=-1, keepdims=True)
    i2 = jnp.min(jnp.where(rest == m2, lane, LANES), axis=-1, keepdims=True)
    e2 = jnp.exp(m2 - m1)
    g1 = 1.0 / (1.0 + e2)
    g2 = e2 / (1.0 + e2)

    hit1 = lane == i1
    hit2 = lane == i2
    onehot = jnp.where(hit1 | hit2, 1.0, 0.0)
    row = lax.broadcasted_iota(I32, (TM, TM), 0)
    col = lax.broadcasted_iota(I32, (TM, TM), 1)
    lower = jnp.where(row > col, 1.0, 0.0).astype(BF16)
    before = _dot(lower, onehot.astype(BF16)) + carry_scr[...]
    r1 = jnp.sum(jnp.where(hit1, before, 0.0), axis=-1, keepdims=True)
    r2 = jnp.sum(jnp.where(hit2, before, 0.0), axis=-1, keepdims=True)
    carry_scr[...] += jnp.sum(onehot, axis=0, keepdims=True)

    out = jnp.zeros(logits.shape, F32)
    for slot, val in ((ROUTE_E1, i1.astype(F32)), (ROUTE_E2, i2.astype(F32)), (ROUTE_G1, g1),
                      (ROUTE_G2, g2), (ROUTE_R1, r1), (ROUTE_R2, r2)):
        out = jnp.where(lane == slot, val, out)
    route_ref[...] = out

    @pl.when(i == pl.num_programs(0) - 1)
    def _():
        cnt_ref[...] = jnp.broadcast_to(carry_scr[...], cnt_ref.shape)


def _router(x, o_ctx, o_na, o_d, w_out, g, mod, w_router_pad):
    return pl.pallas_call(
        _router_kernel,
        grid=(N_TILES,),
        in_specs=[pl.BlockSpec((TM, D_MODEL), lambda i: (i, 0))] + _attn_residual_specs() + [
            pl.BlockSpec((1, D_MODEL), lambda i: (0, 0)),
            _mod_spec(3), _mod_spec(4),
            pl.BlockSpec((D_MODEL, LANES), lambda i: (0, 0)),
        ],
        out_specs=[pl.BlockSpec((TM, D_MODEL), lambda i: (i, 0)),
                   pl.BlockSpec((TM * ROW_CHUNKS, LANES), lambda i: (i, 0)),
                   pl.BlockSpec((TM, LANES), lambda i: (i, 0)),
                   pl.BlockSpec((8, LANES), lambda i: (0, 0))],
        out_shape=[jax.ShapeDtypeStruct((N_TOK, D_MODEL), F32),
                   jax.ShapeDtypeStruct((N_TOK * ROW_CHUNKS, LANES), F32),
                   jax.ShapeDtypeStruct((N_TOK, LANES), F32),
                   jax.ShapeDtypeStruct((8, LANES), F32)],
        scratch_shapes=[pltpu.VMEM((1, LANES), F32)],
        compiler_params=_cparams("arbitrary"),
        name="moe_router",
    )(x, o_ctx, o_na, o_d, mod, w_out, g, mod, mod, w_router_pad)


def _moe_plan(route, counts):
    cnt = counts[0, :N_EXPERTS].astype(I32)
    tiles = (cnt + TM_MOE - 1) // TM_MOE
    tile_end = jnp.cumsum(tiles)
    group_start = (tile_end - tiles) * TM_MOE
    e = route[:, ROUTE_E1:ROUTE_E2 + 1].astype(I32)
    r = route[:, ROUTE_R1:ROUTE_R2 + 1].astype(I32)
    slot = (group_start[e] + r).T.reshape(-1)
    tok = jnp.tile(jnp.arange(N_TOK, dtype=I32), 2)
    src_tok = jnp.zeros(((MOE_TILES + 2) * TM_MOE,), I32).at[slot].set(tok)
    n_valid = tile_end[-1:]
    t = jnp.arange(MOE_TILES, dtype=I32)
    tile_expert = jnp.minimum(jnp.sum((t[:, None] >= tile_end[None, :]).astype(I32), axis=1),
                              N_EXPERTS - 1)
    last_expert = tile_expert[jnp.maximum(n_valid[0] - 1, 0)]
    tile_expert = jnp.where(t < n_valid[0], tile_expert, last_expert)
    return tile_expert, n_valid, src_tok, slot


def _moe_kernel(te_ref, nv_ref, src_ref, h_hbm, *refs):
    wg_refs = refs[:MOE_W_SPLIT]
    wu_refs = refs[MOE_W_SPLIT:2 * MOE_W_SPLIT]
    wd_refs = refs[2 * MOE_W_SPLIT:3 * MOE_W_SPLIT]
    y_ref, hrow_scr, hb_scr, acc_scr, sems = refs[3 * MOE_W_SPLIT:]
    t = pl.program_id(0)
    f = pl.program_id(1)
    valid = t < nv_ref[0]
    last_t = pl.num_programs(0) - 1
    last_f = N_FF_MOE - 1

    def row_copy(tile, row, buf):
        tok = src_ref[tile * TM_MOE + row]
        return pltpu.make_async_copy(h_hbm.at[pl.ds(pl.multiple_of(tok * ROW_CHUNKS, ROW_CHUNKS), ROW_CHUNKS), :],
                                     hrow_scr.at[buf, pl.ds(pl.multiple_of(row * ROW_CHUNKS, ROW_CHUNKS), ROW_CHUNKS), :],
                                     sems.at[buf])

    def wait_tile(buf):
        pltpu.make_async_copy(h_hbm.at[pl.ds(0, GATHER_ROWS * ROW_CHUNKS), :],
                              hrow_scr.at[buf, pl.ds(0, GATHER_ROWS * ROW_CHUNKS), :], sems.at[buf]).wait()

    def issue_next_chunk():
        for j in range(GATHER_CHUNK):
            row_copy(t + 1, f * GATHER_CHUNK + j, (t + 1) % 2).start()

    @pl.when((t == 0) & (f == 0))
    def _():
        def issue(j, carry):
            row_copy(0, j, 0).start()
            return carry

        lax.fori_loop(0, GATHER_ROWS, issue, 0, unroll=7)

    @pl.when(f == 0)
    def _():
        wait_tile(t % 2)
        for c in range(ROW_CHUNKS):
            hb_scr[:, c * LANES:(c + 1) * LANES] = hrow_scr[t % 2, pl.ds(c, TM_MOE, stride=ROW_CHUNKS), :].astype(BF16)
        acc_scr[...] = jnp.zeros_like(acc_scr)

    @pl.when(valid)
    def _():
        issue_next_chunk()
        hb = hb_scr[...]
        acc = None
        for wg_ref, wu_ref, wd_ref in zip(wg_refs, wu_refs, wd_refs):
            a = _swiglu_tile(hb, wg_ref[...], wu_ref[...])
            d = _dot(a.astype(BF16), wd_ref[...])
            acc = d if acc is None else acc + d
        acc_scr[...] += acc

    @pl.when(jnp.logical_not(valid))
    def _():
        issue_next_chunk()

    @pl.when(f == last_f)
    def _():
        for c in range(ROW_CHUNKS):
            y_ref[pl.ds(c, TM_MOE, stride=ROW_CHUNKS), :] = acc_scr[:, c * LANES:(c + 1) * LANES]

    @pl.when((t == last_t) & (f == last_f))
    def _():
        wait_tile((t + 1) % 2)


def _moe_experts(tile_expert, n_valid, src_tok, h, wg, wu, wd, layer):
    ts = FF_TILE_MOE // MOE_W_SPLIT
    buf_rows = -(-GATHER_ROWS // 8) * 8

    def f_eff(t, f, nv):
        return jnp.where(t < nv[0], f, N_FF_MOE - 1)

    grid_spec = pltpu.PrefetchScalarGridSpec(
        num_scalar_prefetch=3,
        grid=(MOE_TILES, N_FF_MOE),
        in_specs=[pl.BlockSpec(memory_space=pl.ANY)] + [
            pl.BlockSpec((None, None, D_MODEL, ts),
                         lambda t, f, te, nv, st, s=s: (layer, te[t], 0, MOE_W_SPLIT * f_eff(t, f, nv) + s))
            for _ in range(2) for s in range(MOE_W_SPLIT)
        ] + [
            pl.BlockSpec((None, None, ts, D_MODEL),
                         lambda t, f, te, nv, st, s=s: (layer, te[t], MOE_W_SPLIT * f_eff(t, f, nv) + s, 0))
            for s in range(MOE_W_SPLIT)
        ],
        out_specs=pl.BlockSpec((TM_MOE * ROW_CHUNKS, LANES), lambda t, f, te, nv, st: (t, 0)),
        scratch_shapes=[pltpu.VMEM((2, buf_rows * ROW_CHUNKS, LANES), F32), pltpu.VMEM((TM_MOE, D_MODEL), BF16),
                        pltpu.VMEM((TM_MOE, D_MODEL), F32), pltpu.SemaphoreType.DMA((2,))],
    )
    return pl.pallas_call(
        _moe_kernel,
        grid_spec=grid_spec,
        out_shape=jax.ShapeDtypeStruct((MOE_TILES * TM_MOE * ROW_CHUNKS, LANES), F32),
        compiler_params=_cparams("arbitrary", "arbitrary"),
        name="moe_experts",
    )(tile_expert, n_valid, src_tok, h, *([wg] * MOE_W_SPLIT + [wu] * MOE_W_SPLIT + [wd] * MOE_W_SPLIT))


def _moe_combine_kernel(slot_ref, x_ref, route_ref, ga_ref, y_hbm, o_ref, rows_scr, sems):
    i = pl.program_id(0)

    def issue_tile(tile, buf):
        for c in range(2):
            base = c * N_TOK + tile * TM

            def issue(j, carry, base=base, c=c):
                s = slot_ref[base + j]
                pltpu.make_async_copy(
                    y_hbm.at[pl.ds(pl.multiple_of(s * ROW_CHUNKS, ROW_CHUNKS), ROW_CHUNKS), :],
                    rows_scr.at[buf, c, pl.ds(pl.multiple_of(j * ROW_CHUNKS, ROW_CHUNKS), ROW_CHUNKS), :],
                    sems.at[buf]).start()
                return carry

            lax.fori_loop(0, TM, issue, 0, unroll=8)

    @pl.when(i == 0)
    def _():
        issue_tile(0, 0)

    @pl.when(i + 1 < pl.num_programs(0))
    def _():
        issue_tile(i + 1, (i + 1) % 2)

    buf = i % 2
    for c in range(2):
        pltpu.make_async_copy(y_hbm.at[pl.ds(0, TM * ROW_CHUNKS), :], rows_scr.at[buf, c], sems.at[buf]).wait()
    route = route_ref[...]
    g1 = route[:, ROUTE_G1:ROUTE_G1 + 1]
    g2 = route[:, ROUTE_G2:ROUTE_G2 + 1]
    for ch in range(ROW_CHUNKS):
        lanes = slice(ch * LANES, (ch + 1) * LANES)
        mix = (g1 * rows_scr[buf, 0, pl.ds(ch, TM, stride=ROW_CHUNKS), :]
               + g2 * rows_scr[buf, 1, pl.ds(ch, TM, stride=ROW_CHUNKS), :])
        o_ref[:, lanes] = x_ref[:, lanes] + ga_ref[0][:, lanes] * mix


def _moe_combine(slot, x, route, mod, y):
    grid_spec = pltpu.PrefetchScalarGridSpec(
        num_scalar_prefetch=1,
        grid=(N_TILES,),
        in_specs=[
            pl.BlockSpec((TM, D_MODEL), lambda i, s: (i, 0)),
            pl.BlockSpec((TM, LANES), lambda i, s: (i, 0)),
            _mod_spec(5),
            pl.BlockSpec(memory_space=pl.ANY),
        ],
        out_specs=pl.BlockSpec((TM, D_MODEL), lambda i, s: (i, 0)),
        scratch_shapes=[pltpu.VMEM((2, 2, TM * ROW_CHUNKS, LANES), F32), pltpu.SemaphoreType.DMA((2,))],
    )
    return pl.pallas_call(
        _moe_combine_kernel,
        grid_spec=grid_spec,
        out_shape=jax.ShapeDtypeStruct((N_TOK, D_MODEL), F32),
        compiler_params=_cparams("arbitrary"),
        name="moe_combine",
    )(slot, x, route, mod, y)


def _final_norm_kernel(x_ref, g_ref, y_ref):
    x = x_ref[...]
    y_ref[...] = (x * lax.rsqrt(jnp.mean(x * x, axis=-1, keepdims=True) + EPS)) * g_ref[...]


def _final_norm(x, g, tile0, n_tok):
    return pl.pallas_call(
        _final_norm_kernel,
        grid=(n_tok // TM,),
        in_specs=[pl.BlockSpec((TM, D_MODEL), lambda i: (tile0 + i, 0)),
                  pl.BlockSpec((1, D_MODEL), lambda i: (0, 0))],
        out_specs=pl.BlockSpec((TM, D_MODEL), lambda i: (i, 0)),
        out_shape=jax.ShapeDtypeStruct((n_tok, D_MODEL), F32),
        compiler_params=_cparams("parallel"),
        name="final_norm",
    )(x, g)


def _rope_tables():
    t = jnp.arange(DEC_SEQ)
    row = (t // GRID_W).astype(F32)
    col = (t % GRID_W).astype(F32)
    half = HEAD_DIM // 2
    inv = 1.0 / (ROPE_THETA ** (jnp.arange(0, half, 2, dtype=F32) / half))
    ar = row[:, None] * inv[None]
    ac = col[:, None] * inv[None]
    ang = jnp.concatenate([ar, ar, ac, ac], axis=-1)
    ang = jnp.concatenate([ang, jnp.zeros((TM, HEAD_DIM), F32)], axis=0)
    cos = jnp.tile(jnp.cos(ang), (1, LANES // HEAD_DIM))
    sin = jnp.tile(jnp.sin(ang), (1, LANES // HEAD_DIM))
    first_half = (jnp.arange(LANES) % 32) < 16
    sin_a = jnp.where(first_half[None, :], -sin, 0.0)
    sin_b = jnp.where(first_half[None, :], 0.0, sin)
    return cos, sin_a, sin_b


def kernel(x_prompt, x_sample, c, cache_na_k, cache_na_v, cache_diff_k, cache_diff_v, c_ctx, w_ada, b_ada, g_mix, w_in, rpb, lam_q1, lam_k1, lam_q2, lam_k2, g_subln, w_out, g_ffn, w_ffn_gate, w_ffn_up, w_ffn_down, w_router, w_moe_gate, w_moe_up, w_moe_down, g_final):
    x = jnp.concatenate([x_prompt.reshape(N_CTX_TOK, D_MODEL), x_sample.reshape(N_LAT_TOK, D_MODEL)])

    cvec = jnp.zeros((MOD_ROWS, D_MODEL), F32).at[0].set(c_ctx).at[1:1 + DEC_BATCH].set(c)
    mod_all = _modulation(cvec, w_ada, b_ada).reshape(DEPTH, MOD_ROWS * 6, 1, D_MODEL)

    col = jnp.arange(IN_WIDTH)
    is_q = (col < COL_NA_K) | ((col >= COL_D_Q) & (col < COL_D_K))
    q_scale = jnp.where(is_q, ATTN_SCALE * LOG2E, 1.0).astype(F32)
    w_in_b = (w_in * q_scale[None, None, :]).astype(BF16)
    w_out_b = w_out.astype(BF16)
    w_fg, w_fu, w_fd = (w.astype(BF16) for w in (w_ffn_gate, w_ffn_up, w_ffn_down))
    w_mg, w_mu, w_md = (w.astype(BF16) for w in (w_moe_gate, w_moe_up, w_moe_down))
    w_router_pad = jnp.pad(w_router, ((0, 0), (0, 0), (0, LANES - N_EXPERTS)))

    rope_tabs = _rope_tables()
    bias_tabs = _na_bias_tables(rpb)
    cna_k = cache_na_k.reshape(DEC_BATCH, DEPTH, PAST_LEN, NA_WIDTH)
    cna_v = cache_na_v.reshape(DEC_BATCH, DEPTH, PAST_LEN, NA_WIDTH)
    cd_k = cache_diff_k.reshape(DEC_BATCH, DEPTH, PAST_LEN, DIFF_WIDTH)
    cd_v = cache_diff_v.reshape(DEC_BATCH, DEPTH, PAST_LEN, DIFF_WIDTH)

    kv_layers = []
    for l in range(DEPTH):
        lam_init = 0.8 - 0.6 * math.exp(-0.3 * l)
        mod = mod_all[l]
        g_mix_l = g_mix[l][None, :]
        g_ffn_l = g_ffn[l][None, :]
        g_sub_l = g_subln[l][None, :]
        lam_params = tuple(p[l][None, :] for p in (lam_q1, lam_k1, lam_q2, lam_k2))

        p_all, *kv_ctx = _inproj(x, g_mix_l, mod, w_in_b[l], rope_tabs)
        kv_layers.append(kv_ctx)
        o_ctx = _ctx_attention(p_all, lam_params, g_sub_l, lam_init)
        o_na = _na_attention(p_all, cna_k, cna_v, bias_tabs, l)
        o_d = _diff_attention(p_all, cd_k, cd_v, lam_params, g_sub_l, lam_init, l)

        i = l // 2
        if l % 2 == 0:
            x = _ffn_dense(x, o_ctx, o_na, o_d, w_out_b[l], g_ffn_l, mod, w_fg[i], w_fu[i], w_fd[i])
        else:
            x, h, route, counts = _router(x, o_ctx, o_na, o_d, w_out_b[l], g_ffn_l, mod, w_router_pad[i])
            tile_expert, n_valid, src_tok, slot = _moe_plan(route, counts)
            y = _moe_experts(tile_expert, n_valid, src_tok, h, w_mg, w_mu, w_md, i)
            x = _moe_combine(slot, x, route, mod, y)

    g_fin = g_final[None, :]
    y_prompt = _final_norm(x, g_fin, 0, N_CTX_TOK).reshape(BATCH, SEQ, D_MODEL)
    y_sample = _final_norm(x, g_fin, CTX_TILES, N_LAT_TOK).reshape(DEC_BATCH, DEC_SEQ, D_MODEL)

    kv = [jnp.stack([layer_kv[k] for layer_kv in kv_layers], axis=1) for k in range(4)]
    new_na_k = kv[0].reshape(BATCH, DEPTH, SEQ, NA_HEADS, HEAD_DIM)
    new_na_v = kv[1].reshape(BATCH, DEPTH, SEQ, NA_HEADS, HEAD_DIM)
    new_diff_k = kv[2].reshape(BATCH, DEPTH, SEQ, DIFF_HEADS, 2, HEAD_DIM)
    new_diff_v = kv[3].reshape(BATCH, DEPTH, SEQ, DIFF_HEADS, 2 * HEAD_DIM)
    return (y_prompt, y_sample, new_na_k, new_na_v, new_diff_k, new_diff_v)
```

```python
import functools
import math

import numpy as np
import jax
import jax.numpy as jnp
from jax import lax
from jax.experimental import pallas as pl
from jax.experimental.pallas import tpu as pltpu

F32 = jnp.float32
BF16 = jnp.bfloat16
I32 = jnp.int32

D_MODEL = 1024
DEPTH = 4
BATCH = 16
SEQ = 256
DEC_BATCH = 8
DEC_SEQ = 2048
PAST_LEN = 256
GRID_W = 64
GRID_ROWS = DEC_SEQ // GRID_W
HEAD_DIM = 64
NA_HEADS = 8
NA_WIDTH = 512
DIFF_HEADS = 4
DIFF_WIDTH = 512
IN_WIDTH = 3072
NA_WIN_H = 8
NA_WIN_W = 16
ROPE_THETA = 10000.0
D_FF = 2816
N_EXPERTS = 8
D_FF_EXPERT = 3584
EPS = 1e-6
SUBLN_EPS = 1e-5
ATTN_SCALE = HEAD_DIM ** -0.5

LANES = 128
ROW_CHUNKS = D_MODEL // LANES
N_CTX_TOK = BATCH * SEQ
N_LAT_TOK = DEC_BATCH * DEC_SEQ
N_TOK = N_CTX_TOK + N_LAT_TOK
MOD_ROWS = 16
MASK_VALUE = -1e30

COL_NA_Q, COL_NA_K, COL_NA_V = 0, 512, 1024
COL_D_Q, COL_D_K, COL_D_V = 1536, 2048, 2560

TM = 512
N_TILES = N_TOK // TM
CTX_TILES = N_CTX_TOK // TM
TILES_PER_SEQ = DEC_SEQ // TM
FF_TILE_DENSE = 1408
FF_TILE_MOE = 512
N_FF_MOE = D_FF_EXPERT // FF_TILE_MOE
TM_MOE = 1024
MOE_W_SPLIT = 1
TQ_DIFF = 512
DIFF_SUB = 128
NA_R = 4
NA_WIN_ROWS = 12
NA_STEPS = GRID_ROWS // NA_R
MOE_SLOTS = 2 * N_TOK
MOE_TILES = MOE_SLOTS // TM_MOE + N_EXPERTS
GATHER_CHUNK = -(-TM_MOE // N_FF_MOE)
GATHER_ROWS = GATHER_CHUNK * N_FF_MOE
ROW_DMA_PRIORITY = 1
LOG2E = 1.4426950408889634
VMEM_LIMIT = 56 * 1024 * 1024


def _cparams(*sem):
    return pltpu.CompilerParams(dimension_semantics=sem, vmem_limit_bytes=VMEM_LIMIT)


def _dot(a, b):
    return jnp.dot(a, b, preferred_element_type=F32)


def _dot_nt(a, b):
    return lax.dot_general(a, b, (((1,), (1,)), ((), ())), preferred_element_type=F32)


def _modulated_norm(x, g, shift, scale):
    xn = x * lax.rsqrt(jnp.mean(x * x, axis=-1, keepdims=True) + EPS)
    return (xn * g) * (1.0 + scale) + shift


def _mod_row(i):
    return jnp.maximum(i // TILES_PER_SEQ - CTX_TILES // TILES_PER_SEQ + 1, 0)


def _mod_spec(chunk):
    return pl.BlockSpec((1, 1, D_MODEL), lambda i, *_: (_mod_row(i) * 6 + chunk, 0, 0))


def _mod_kernel(c_ref, w_ref, b_ref, o_ref):
    cv = c_ref[...]
    s = cv * jax.nn.sigmoid(cv)
    o_ref[...] = jnp.dot(s, w_ref[...], preferred_element_type=F32,
                         precision=lax.Precision.HIGHEST) + b_ref[...]


def _modulation(cvec, w_ada, b_ada):
    tn = 1536
    n = 6 * D_MODEL
    return pl.pallas_call(
        _mod_kernel,
        grid=(DEPTH, n // tn),
        in_specs=[
            pl.BlockSpec((MOD_ROWS, D_MODEL), lambda l, j: (0, 0)),
            pl.BlockSpec((None, D_MODEL, tn), lambda l, j: (l, 0, j)),
            pl.BlockSpec((None, 1, tn), lambda l, j: (l, 0, j)),
        ],
        out_specs=pl.BlockSpec((None, MOD_ROWS, tn), lambda l, j: (l, 0, j)),
        out_shape=jax.ShapeDtypeStruct((DEPTH, MOD_ROWS, n), F32),
        compiler_params=_cparams("parallel", "parallel"),
        name="adaln_modulation",
    )(cvec, w_ada, b_ada.reshape(DEPTH, 1, n))


def _inproj_kernel(x_ref, g_ref, sh_ref, sc_ref, w_ref, cos_ref, sina_ref, sinb_ref, o_ref, *kv_refs):
    i = pl.program_id(0)
    h = _modulated_norm(x_ref[...], g_ref[...], sh_ref[0], sc_ref[0]).astype(BF16)
    chunk = 512
    for c in range(IN_WIDTH // chunk):
        col = c * chunk
        acc = _dot(h, w_ref[:, col:col + chunk])
        for k_i, src in enumerate((COL_NA_K, COL_NA_V, COL_D_K, COL_D_V)):
            if src == col:
                @pl.when(i < CTX_TILES)
                def _(acc=acc, k_i=k_i):
                    for b in range(TM // SEQ):
                        kv_refs[k_i][b] = acc[b * SEQ:(b + 1) * SEQ, :]
        if COL_D_Q <= col < COL_D_V:
            parts = []
            for j in range(chunk // LANES):
                blk = acc[:, j * LANES:(j + 1) * LANES]
                parts.append(blk * cos_ref[...]
                             + pltpu.roll(blk, LANES - 16, 1) * sina_ref[...]
                             + pltpu.roll(blk, 16, 1) * sinb_ref[...])
            acc = jnp.concatenate(parts, axis=1)
        o_ref[:, col:col + chunk] = acc.astype(o_ref.dtype)


def _inproj(x, g, mod, w_bf16, rope_tabs):
    rope_spec = pl.BlockSpec(
        (TM, LANES),
        lambda i: (jnp.where(i < CTX_TILES, TILES_PER_SEQ, (i - CTX_TILES) % TILES_PER_SEQ), 0))
    return pl.pallas_call(
        _inproj_kernel,
        grid=(N_TILES,),
        in_specs=[
            pl.BlockSpec((TM, D_MODEL), lambda i: (i, 0)),
            pl.BlockSpec((1, D_MODEL), lambda i: (0, 0)),
            _mod_spec(0), _mod_spec(1),
            pl.BlockSpec((D_MODEL, IN_WIDTH), lambda i: (0, 0)),
            rope_spec, rope_spec, rope_spec,
        ],
        out_specs=[pl.BlockSpec((TM, IN_WIDTH), lambda i: (i, 0))]
        + [pl.BlockSpec((TM // SEQ, SEQ, 512), lambda i: (jnp.minimum(i, CTX_TILES - 1), 0, 0))] * 4,
        out_shape=[jax.ShapeDtypeStruct((N_TOK, IN_WIDTH), BF16)]
        + [jax.ShapeDtypeStruct((BATCH, SEQ, 512), F32)] * 4,
        compiler_params=_cparams("arbitrary"),
        name="inproj",
    )(x, g, mod, mod, w_bf16, *rope_tabs)


def _lane_half_mask(shape, half):
    lane = lax.broadcasted_iota(I32, shape, len(shape) - 1)
    return (lane < HEAD_DIM) if half == 0 else (lane >= HEAD_DIM)


def _lambda_value(lq1, lk1, lq2, lk2, lam_init):
    a = jnp.sum(lq1 * lk1, axis=-1, keepdims=True)
    b = jnp.sum(lq2 * lk2, axis=-1, keepdims=True)
    return jnp.exp(a) - jnp.exp(b) + lam_init


def _subln(o, g, lam_init):
    on = o * lax.rsqrt(jnp.mean(o * o, axis=-1, keepdims=True) + SUBLN_EPS)
    return (on * g) * (1.0 - lam_init)


def _ctx_attn_kernel(p_ref, lq1, lk1, lq2, lk2, gs_ref, o_ref, *, lam_init):
    lam = _lambda_value(lq1[...], lk1[...], lq2[...], lk2[...], lam_init)
    for hp in range(NA_HEADS // 2):
        q = p_ref[:, COL_NA_Q + hp * LANES:COL_NA_Q + (hp + 1) * LANES]
        k = p_ref[:, COL_NA_K + hp * LANES:COL_NA_K + (hp + 1) * LANES]
        v = p_ref[:, COL_NA_V + hp * LANES:COL_NA_V + (hp + 1) * LANES]
        outs = []
        for half in range(2):
            qm = jnp.where(_lane_half_mask(q.shape, half), q, jnp.zeros_like(q))
            s = _dot_nt(qm, k)
            m = jnp.max(s, axis=-1, keepdims=True)
            e = jnp.exp2(s - m)
            inv = 1.0 / jnp.sum(e, axis=-1, keepdims=True)
            outs.append(_dot(e.astype(BF16), v) * inv)
        o = jnp.where(_lane_half_mask(outs[0].shape, 0), outs[0], outs[1])
        o_ref[:, hp * LANES:(hp + 1) * LANES] = o.astype(o_ref.dtype)
    for h in range(DIFF_HEADS):
        q = p_ref[:, COL_D_Q + h * LANES:COL_D_Q + (h + 1) * LANES]
        k = p_ref[:, COL_D_K + h * LANES:COL_D_K + (h + 1) * LANES]
        v = p_ref[:, COL_D_V + h * LANES:COL_D_V + (h + 1) * LANES]
        ps = []
        for half in range(2):
            qm = jnp.where(_lane_half_mask(q.shape, half), q, jnp.zeros_like(q))
            s = _dot_nt(qm, k)
            m = jnp.max(s, axis=-1, keepdims=True)
            e = jnp.exp2(s - m)
            ps.append(e / jnp.sum(e, axis=-1, keepdims=True))
        a = (ps[0] - lam * ps[1]).astype(BF16)
        o = _subln(_dot(a, v), gs_ref[...], lam_init)
        o_ref[:, NA_WIDTH + h * LANES:NA_WIDTH + (h + 1) * LANES] = o.astype(o_ref.dtype)


def _ctx_attention(p_all, lam_params, g_subln, lam_init):
    vec = pl.BlockSpec((1, HEAD_DIM), lambda b: (0, 0))
    return pl.pallas_call(
        functools.partial(_ctx_attn_kernel, lam_init=lam_init),
        grid=(BATCH,),
        in_specs=[pl.BlockSpec((SEQ, IN_WIDTH), lambda b: (b, 0)), vec, vec, vec, vec,
                  pl.BlockSpec((1, 2 * HEAD_DIM), lambda b: (0, 0))],
        out_specs=pl.BlockSpec((SEQ, D_MODEL), lambda b: (b, 0)),
        out_shape=jax.ShapeDtypeStruct((N_CTX_TOK, D_MODEL), BF16),
        compiler_params=_cparams("parallel"),
        name="ctx_attention",
    )(p_all, *lam_params, g_subln)


def _bias_table_kernel(rpb_ref, o_ref):
    dr_plan, ok_plan = _na_window_plan()
    shape = (GRID_W, LANES)
    qc = lax.broadcasted_iota(I32, shape, 0)
    lane = lax.broadcasted_iota(I32, shape, 1)
    kc = jnp.bitwise_and(lane, GRID_W - 1)
    cs = jnp.clip(qc - NA_WIN_W // 2, 0, GRID_W - NA_WIN_W)
    in_window = (kc >= cs) & (kc < cs + NA_WIN_W)
    low_half = lane < GRID_W
    masked = jnp.full(shape, MASK_VALUE, F32)

    pieces = {}

    def piece(dr, parity):
        if (dr, parity) not in pieces:
            base = jnp.broadcast_to(rpb_ref[dr:dr + 1, :], shape) * LOG2E
            shifted = pltpu.roll(base, 0, 1, stride=1, stride_axis=0)
            pieces[(dr, parity)] = pltpu.roll(shifted, (LANES - (NA_WIN_W - 1) + GRID_W * parity) % LANES, 1)
        return pieces[(dr, parity)]

    for t in range(3):
        for jr in range(NA_R):
            for p in range(NA_WIN_ROWS // 2):
                halves = []
                for parity in range(2):
                    i = 2 * p + parity
                    halves.append(piece(int(dr_plan[t, jr, i]), parity) if ok_plan[t, jr, i] else masked)
                blk = jnp.where(in_window, jnp.where(low_half, halves[0], halves[1]), MASK_VALUE)
                o_ref[t, jr * GRID_W:(jr + 1) * GRID_W, p * LANES:(p + 1) * LANES] = blk.astype(o_ref.dtype)


def _na_window_plan():
    dr = np.zeros((3, NA_R, NA_WIN_ROWS), np.int32)
    ok = np.zeros((3, NA_R, NA_WIN_ROWS), bool)
    for t, r0 in enumerate((0, NA_R, GRID_ROWS - NA_R)):
        lo = min(max(r0 - NA_WIN_H // 2, 0), GRID_ROWS - NA_WIN_ROWS)
        for jr in range(NA_R):
            r = r0 + jr
            rs = min(max(r - NA_WIN_H // 2, 0), GRID_ROWS - NA_WIN_H)
            for i in range(NA_WIN_ROWS):
                key_row = lo + i
                if rs <= key_row < rs + NA_WIN_H:
                    ok[t, jr, i] = True
                    dr[t, jr, i] = key_row - r + NA_WIN_H - 1
    return dr, ok


def _na_bias_tables(rpb):
    n_dr = 2 * NA_WIN_H - 1
    rpb_pad = jnp.pad(rpb.reshape(DEPTH * NA_HEADS, n_dr, 2 * NA_WIN_W - 1),
                      ((0, 0), (0, 16 - n_dr), (0, LANES - (2 * NA_WIN_W - 1))))
    return pl.pallas_call(
        _bias_table_kernel,
        grid=(DEPTH * NA_HEADS,),
        in_specs=[pl.BlockSpec((None, 16, LANES), lambda g: (g, 0, 0))],
        out_specs=pl.BlockSpec((None, 3, None, NA_R * GRID_W, NA_WIN_ROWS * GRID_W),
                               lambda g: (g // NA_HEADS, 0, g % NA_HEADS, 0, 0)),
        out_shape=jax.ShapeDtypeStruct((DEPTH, 3, NA_HEADS, NA_R * GRID_W, NA_WIN_ROWS * GRID_W), BF16),
        compiler_params=_cparams("parallel"),
        name="na_bias_table",
    )(rpb_pad)


def _na_attn_kernel(q_ref, k_ref, v_ref, kc_ref, vc_ref, bias_ref, o_ref):
    j = pl.program_id(1)
    lo = jnp.clip(j * NA_R - NA_WIN_H // 2, 0, GRID_ROWS - NA_WIN_ROWS)
    k0 = pl.multiple_of(lo * GRID_W, GRID_W)
    step_type = jnp.where(j == 0, 0, jnp.where(j == NA_STEPS - 1, 2, 1))
    n_keys = NA_WIN_ROWS * GRID_W

    def scores(head):
        lanes = slice((head // 2) * LANES, (head // 2 + 1) * LANES)
        q = q_ref[:, lanes]
        qm = jnp.where(_lane_half_mask(q.shape, head % 2), q, jnp.zeros_like(q))
        s_w = _dot_nt(qm, k_ref[pl.ds(k0, n_keys), lanes]) + bias_ref[step_type, head].astype(F32)
        s_c = _dot_nt(qm, kc_ref[:, lanes].astype(BF16))
        return s_w, s_c

    nxt = scores(0)
    outs = []
    for head in range(NA_HEADS):
        s_w, s_c = nxt
        if head + 1 < NA_HEADS:
            nxt = scores(head + 1)
        lanes = slice((head // 2) * LANES, (head // 2 + 1) * LANES)
        m = jnp.maximum(jnp.max(s_w, axis=-1, keepdims=True), jnp.max(s_c, axis=-1, keepdims=True))
        e_w = jnp.exp2(s_w - m)
        e_c = jnp.exp2(s_c - m)
        inv = 1.0 / (jnp.sum(e_w, axis=-1, keepdims=True) + jnp.sum(e_c, axis=-1, keepdims=True))
        pv = (_dot(e_w.astype(BF16), v_ref[pl.ds(k0, n_keys), lanes])
              + _dot(e_c.astype(BF16), vc_ref[:, lanes].astype(BF16)))
        outs.append(pv * inv)
        if head % 2 == 1:
            o = jnp.where(_lane_half_mask(outs[0].shape, 0), outs[0], outs[1])
            o_ref[:, lanes] = o.astype(o_ref.dtype)
            outs = []


def _na_attention(p_all, cache_k, cache_v, bias_tab, layer):
    q_rows = NA_R * GRID_W
    q_blk0 = N_CTX_TOK // q_rows
    seq_blk0 = N_CTX_TOK // DEC_SEQ
    cache_spec = pl.BlockSpec((None, None, PAST_LEN, NA_WIDTH), lambda b, j: (b, layer, 0, 0))
    return pl.pallas_call(
        _na_attn_kernel,
        grid=(DEC_BATCH, NA_STEPS),
        in_specs=[
            pl.BlockSpec((q_rows, NA_WIDTH), lambda b, j: (q_blk0 + b * NA_STEPS + j, COL_NA_Q // NA_WIDTH)),
            pl.BlockSpec((DEC_SEQ, NA_WIDTH), lambda b, j: (seq_blk0 + b, COL_NA_K // NA_WIDTH)),
            pl.BlockSpec((DEC_SEQ, NA_WIDTH), lambda b, j: (seq_blk0 + b, COL_NA_V // NA_WIDTH)),
            cache_spec, cache_spec,
            pl.BlockSpec((None, 3, NA_HEADS, q_rows, NA_WIN_ROWS * GRID_W), lambda b, j: (layer, 0, 0, 0, 0)),
        ],
        out_specs=pl.BlockSpec((q_rows, NA_WIDTH), lambda b, j: (b * NA_STEPS + j, 0)),
        out_shape=jax.ShapeDtypeStruct((N_LAT_TOK, NA_WIDTH), BF16),
        compiler_params=_cparams("parallel", "arbitrary"),
        name="latent_na_attention",
    )(p_all, p_all, p_all, cache_k, cache_v, bias_tab)


def _diff_attn_kernel(q_ref, k_ref, v_ref, kc_ref, vc_ref, lq1, lk1, lq2, lk2, gs_ref, o_ref,
                      *, lam_init):
    lam = _lambda_value(lq1[...], lk1[...], lq2[...], lk2[...], lam_init)
    k = k_ref[...]
    v = v_ref[...]
    kc = kc_ref[...].astype(BF16)
    vc = vc_ref[...].astype(BF16)
    n_sub = TQ_DIFF // DIFF_SUB

    def scores(i):
        q = q_ref[i * DIFF_SUB:(i + 1) * DIFF_SUB, :]
        out = []
        for half in range(2):
            qm = jnp.where(_lane_half_mask(q.shape, half), q, jnp.zeros_like(q))
            out.append((_dot_nt(qm, k), _dot_nt(qm, kc)))
        return out

    nxt = scores(0)
    for i in range(n_sub):
        cur = nxt
        if i + 1 < n_sub:
            nxt = scores(i + 1)
        probs = []
        for s_l, s_c in cur:
            m = jnp.maximum(jnp.max(s_l, axis=-1, keepdims=True), jnp.max(s_c, axis=-1, keepdims=True))
            e_l = jnp.exp2(s_l - m)
            e_c = jnp.exp2(s_c - m)
            inv = 1.0 / (jnp.sum(e_l, axis=-1, keepdims=True) + jnp.sum(e_c, axis=-1, keepdims=True))
            probs.append((e_l, e_c, inv))
        r = lam * probs[1][2] / probs[0][2]
        a_l = (probs[0][0] - probs[1][0] * r).astype(BF16)
        a_c = (probs[0][1] - probs[1][1] * r).astype(BF16)
        o = (_dot(a_l, v) + _dot(a_c, vc)) * probs[0][2]
        o_ref[i * DIFF_SUB:(i + 1) * DIFF_SUB, :] = _subln(o, gs_ref[...], lam_init).astype(o_ref.dtype)


def _diff_attention(p_all, cache_k, cache_v, lam_params, g_subln, lam_init, layer):
    steps = DEC_SEQ // TQ_DIFF
    q_blk0 = N_CTX_TOK // TQ_DIFF
    seq_blk0 = N_CTX_TOK // DEC_SEQ
    vec = pl.BlockSpec((1, HEAD_DIM), lambda b, h, j: (0, 0))
    cache_spec = pl.BlockSpec((None, None, PAST_LEN, LANES), lambda b, h, j: (b, layer, 0, h))
    return pl.pallas_call(
        functools.partial(_diff_attn_kernel, lam_init=lam_init),
        grid=(DEC_BATCH, DIFF_HEADS, steps),
        in_specs=[
            pl.BlockSpec((TQ_DIFF, LANES), lambda b, h, j: (q_blk0 + b * steps + j, COL_D_Q // LANES + h)),
            pl.BlockSpec((DEC_SEQ, LANES), lambda b, h, j: (seq_blk0 + b, COL_D_K // LANES + h)),
            pl.BlockSpec((DEC_SEQ, LANES), lambda b, h, j: (seq_blk0 + b, COL_D_V // LANES + h)),
            cache_spec, cache_spec, vec, vec, vec, vec,
            pl.BlockSpec((1, 2 * HEAD_DIM), lambda b, h, j: (0, 0)),
        ],
        out_specs=pl.BlockSpec((TQ_DIFF, LANES), lambda b, h, j: (b * steps + j, h)),
        out_shape=jax.ShapeDtypeStruct((N_LAT_TOK, DIFF_WIDTH), BF16),
        compiler_params=_cparams("parallel", "parallel", "arbitrary"),
        name="latent_diff_attention",
    )(p_all, p_all, p_all, cache_k, cache_v, *lam_params, g_subln)


def _attn_residual(i, x_ref, oc_ref, ona_ref, od_ref, ga_ref, w_ref, dst_ref):
    @pl.when(i < CTX_TILES)
    def _():
        dst_ref[...] = x_ref[...] + ga_ref[0] * _dot(oc_ref[...], w_ref[...])

    @pl.when(i >= CTX_TILES)
    def _():
        acc = _dot(ona_ref[...], w_ref[:NA_WIDTH, :]) + _dot(od_ref[...], w_ref[NA_WIDTH:, :])
        dst_ref[...] = x_ref[...] + ga_ref[0] * acc


def _attn_residual_specs():
    lat_idx = lambda i, *_: (jnp.maximum(i - CTX_TILES, 0), 0)
    return [
        pl.BlockSpec((TM, D_MODEL), lambda i, *_: (jnp.minimum(i, CTX_TILES - 1), 0)),
        pl.BlockSpec((TM, NA_WIDTH), lat_idx),
        pl.BlockSpec((TM, DIFF_WIDTH), lat_idx),
        _mod_spec(2),
        pl.BlockSpec((D_MODEL, D_MODEL), lambda i, *_: (0, 0)),
    ]


def _swiglu_tile(hb, wg, wu):
    g = _dot(hb, wg)
    u = _dot(hb, wu)
    return (g * jax.nn.sigmoid(g)) * u


def _ffn_kernel(x_ref, oc_ref, ona_ref, od_ref, ga1_ref, wo_ref, g_ref, sh_ref, sc_ref, ga_ref,
                wg_ref, wu_ref, wd_ref, y_ref, xmid_scr, h_scr, acc_scr):
    i = pl.program_id(0)
    f = pl.program_id(1)

    @pl.when(f == 0)
    def _():
        _attn_residual(i, x_ref, oc_ref, ona_ref, od_ref, ga1_ref, wo_ref, xmid_scr)
        h = _modulated_norm(xmid_scr[...], g_ref[...], sh_ref[0], sc_ref[0])
        h_scr[...] = h.astype(BF16)
        acc_scr[...] = jnp.zeros_like(acc_scr)

    a = _swiglu_tile(h_scr[...], wg_ref[...], wu_ref[...])
    acc_scr[...] += _dot(a.astype(BF16), wd_ref[...])

    @pl.when(f == pl.num_programs(1) - 1)
    def _():
        y_ref[...] = xmid_scr[...] + ga_ref[0] * acc_scr[...]


def _ffn_dense(x, o_ctx, o_na, o_d, w_out, g, mod, wg, wu, wd):
    tf = FF_TILE_DENSE
    return pl.pallas_call(
        _ffn_kernel,
        grid=(N_TILES, D_FF // tf),
        in_specs=[pl.BlockSpec((TM, D_MODEL), lambda i, f: (i, 0))] + _attn_residual_specs() + [
            pl.BlockSpec((1, D_MODEL), lambda i, f: (0, 0)),
            _mod_spec(3), _mod_spec(4), _mod_spec(5),
            pl.BlockSpec((D_MODEL, tf), lambda i, f: (0, f)),
            pl.BlockSpec((D_MODEL, tf), lambda i, f: (0, f)),
            pl.BlockSpec((tf, D_MODEL), lambda i, f: (f, 0)),
        ],
        out_specs=pl.BlockSpec((TM, D_MODEL), lambda i, f: (i, 0)),
        out_shape=jax.ShapeDtypeStruct((N_TOK, D_MODEL), F32),
        scratch_shapes=[pltpu.VMEM((TM, D_MODEL), F32), pltpu.VMEM((TM, D_MODEL), BF16),
                        pltpu.VMEM((TM, D_MODEL), F32)],
        compiler_params=_cparams("parallel", "arbitrary"),
        name="ffn_dense",
    )(x, o_ctx, o_na, o_d, mod, w_out, g, mod, mod, mod, wg, wu, wd)


ROUTE_E1, ROUTE_E2, ROUTE_G1, ROUTE_G2, ROUTE_R1, ROUTE_R2 = range(6)


def _router_kernel(x_ref, oc_ref, ona_ref, od_ref, ga1_ref, wo_ref, g_ref, sh_ref, sc_ref, wr_ref,
                   xmid_ref, h_ref, route_ref, cnt_ref, carry_scr):
    i = pl.program_id(0)

    @pl.when(i == 0)
    def _():
        carry_scr[...] = jnp.zeros_like(carry_scr)

    _attn_residual(i, x_ref, oc_ref, ona_ref, od_ref, ga1_ref, wo_ref, xmid_ref)
    h = _modulated_norm(xmid_ref[...], g_ref[...], sh_ref[0], sc_ref[0])
    for c in range(ROW_CHUNKS):
        h_ref[pl.ds(c, TM, stride=ROW_CHUNKS), :] = h[:, c * LANES:(c + 1) * LANES]
    logits = jnp.dot(h, wr_ref[...], preferred_element_type=F32, precision=lax.Precision.HIGHEST)
    lane = lax.broadcasted_iota(I32, logits.shape, 1)
    logits = jnp.where(lane < N_EXPERTS, logits, MASK_VALUE)
    m1 = jnp.max(logits, axis=-1, keepdims=True)
    i1 = jnp.min(jnp.where(logits == m1, lane, LANES), axis=-1, keepdims=True)
    rest = jnp.where(lane == i1, MASK_VALUE, logits)
    m2 = jnp.max(rest, axis=-1, keepdims=True)
    i2 = jnp.min(jnp.where(rest == m2, lane, LANES), axis=-1, keepdims=True)
    e2 = jnp.exp(m2 - m1)
    g1 = 1.0 / (1.0 + e2)
    g2 = e2 / (1.0 + e2)

    hit1 = lane == i1
    hit2 = lane == i2
    onehot = jnp.where(hit1 | hit2, 1.0, 0.0)
    row = lax.broadcasted_iota(I32, (TM, TM), 0)
    col = lax.broadcasted_iota(I32, (TM, TM), 1)
    lower = jnp.where(row > col, 1.0, 0.0).astype(BF16)
    before = _dot(lower, onehot.astype(BF16)) + carry_scr[...]
    r1 = jnp.sum(jnp.where(hit1, before, 0.0), axis=-1, keepdims=True)
    r2 = jnp.sum(jnp.where(hit2, before, 0.0), axis=-1, keepdims=True)
    carry_scr[...] += jnp.sum(onehot, axis=0, keepdims=True)

    out = jnp.zeros(logits.shape, F32)
    for slot, val in ((ROUTE_E1, i1.astype(F32)), (ROUTE_E2, i2.astype(F32)), (ROUTE_G1, g1),
                      (ROUTE_G2, g2), (ROUTE_R1, r1), (ROUTE_R2, r2)):
        out = jnp.where(lane == slot, val, out)
    route_ref[...] = out

    @pl.when(i == pl.num_programs(0) - 1)
    def _():
        cnt_ref[...] = jnp.broadcast_to(carry_scr[...], cnt_ref.shape)


def _router(x, o_ctx, o_na, o_d, w_out, g, mod, w_router_pad):
    return pl.pallas_call(
        _router_kernel,
        grid=(N_TILES,),
        in_specs=[pl.BlockSpec((TM, D_MODEL), lambda i: (i, 0))] + _attn_residual_specs() + [
            pl.BlockSpec((1, D_MODEL), lambda i: (0, 0)),
            _mod_spec(3), _mod_spec(4),
            pl.BlockSpec((D_MODEL, LANES), lambda i: (0, 0)),
        ],
        out_specs=[pl.BlockSpec((TM, D_MODEL), lambda i: (i, 0)),
                   pl.BlockSpec((TM * ROW_CHUNKS, LANES), lambda i: (i, 0)),
                   pl.BlockSpec((TM, LANES), lambda i: (i, 0)),
                   pl.BlockSpec((8, LANES), lambda i: (0, 0))],
        out_shape=[jax.ShapeDtypeStruct((N_TOK, D_MODEL), F32),
                   jax.ShapeDtypeStruct((N_TOK * ROW_CHUNKS, LANES), F32),
                   jax.ShapeDtypeStruct((N_TOK, LANES), F32),
                   jax.ShapeDtypeStruct((8, LANES), F32)],
        scratch_shapes=[pltpu.VMEM((1, LANES), F32)],
        compiler_params=_cparams("arbitrary"),
        name="moe_router",
    )(x, o_ctx, o_na, o_d, mod, w_out, g, mod, mod, w_router_pad)


def _moe_plan(route, counts):
    cnt = counts[0, :N_EXPERTS].astype(I32)
    tiles = (cnt + TM_MOE - 1) // TM_MOE
    tile_end = jnp.cumsum(tiles)
    group_start = (tile_end - tiles) * TM_MOE
    e = route[:, ROUTE_E1:ROUTE_E2 + 1].astype(I32)
    r = route[:, ROUTE_R1:ROUTE_R2 + 1].astype(I32)
    slot = (group_start[e] + r).T.reshape(-1)
    tok = jnp.tile(jnp.arange(N_TOK, dtype=I32), 2)
    src_tok = jnp.zeros(((MOE_TILES + 2) * TM_MOE,), I32).at[slot].set(tok)
    n_valid = tile_end[-1:]
    t = jnp.arange(MOE_TILES, dtype=I32)
    tile_expert = jnp.minimum(jnp.sum((t[:, None] >= tile_end[None, :]).astype(I32), axis=1),
                              N_EXPERTS - 1)
    last_expert = tile_expert[jnp.maximum(n_valid[0] - 1, 0)]
    tile_expert = jnp.where(t < n_valid[0], tile_expert, last_expert)
    return tile_expert, n_valid, src_tok, slot


def _moe_kernel(te_ref, nv_ref, src_ref, h_hbm, *refs):
    wg_refs = refs[:MOE_W_SPLIT]
    wu_refs = refs[MOE_W_SPLIT:2 * MOE_W_SPLIT]
    wd_refs = refs[2 * MOE_W_SPLIT:3 * MOE_W_SPLIT]
    y_ref, hrow_scr, hb_scr, acc_scr, sems = refs[3 * MOE_W_SPLIT:]
    t = pl.program_id(0)
    f = pl.program_id(1)
    valid = t < nv_ref[0]
    last_t = pl.num_programs(0) - 1
    last_f = N_FF_MOE - 1

    def row_copy(tile, row, buf):
        tok = src_ref[tile * TM_MOE + row]
        return pltpu.make_async_copy(h_hbm.at[pl.ds(pl.multiple_of(tok * ROW_CHUNKS, ROW_CHUNKS), ROW_CHUNKS), :],
                                     hrow_scr.at[buf, pl.ds(pl.multiple_of(row * ROW_CHUNKS, ROW_CHUNKS), ROW_CHUNKS), :],
                                     sems.at[buf])

    def wait_tile(buf):
        pltpu.make_async_copy(h_hbm.at[pl.ds(0, GATHER_ROWS * ROW_CHUNKS), :],
                              hrow_scr.at[buf, pl.ds(0, GATHER_ROWS * ROW_CHUNKS), :], sems.at[buf]).wait()

    def issue_next_chunk():
        for j in range(GATHER_CHUNK):
            row_copy(t + 1, f * GATHER_CHUNK + j, (t + 1) % 2).start(priority=ROW_DMA_PRIORITY)

    @pl.when((t == 0) & (f == 0))
    def _():
        def issue(j, carry):
            row_copy(0, j, 0).start(priority=ROW_DMA_PRIORITY)
            return carry

        lax.fori_loop(0, GATHER_ROWS, issue, 0, unroll=7)

    @pl.when(f == 0)
    def _():
        wait_tile(t % 2)
        for c in range(ROW_CHUNKS):
            hb_scr[:, c * LANES:(c + 1) * LANES] = hrow_scr[t % 2, pl.ds(c, TM_MOE, stride=ROW_CHUNKS), :].astype(BF16)
        acc_scr[...] = jnp.zeros_like(acc_scr)

    @pl.when(valid)
    def _():
        issue_next_chunk()
        hb = hb_scr[...]
        acc = None
        for wg_ref, wu_ref, wd_ref in zip(wg_refs, wu_refs, wd_refs):
            a = _swiglu_tile(hb, wg_ref[...].astype(BF16), wu_ref[...].astype(BF16))
            d = _dot(a.astype(BF16), wd_ref[...].astype(BF16))
            acc = d if acc is None else acc + d
        acc_scr[...] += acc

    @pl.when(jnp.logical_not(valid))
    def _():
        issue_next_chunk()

    @pl.when(f == last_f)
    def _():
        for c in range(ROW_CHUNKS):
            y_ref[pl.ds(c, TM_MOE, stride=ROW_CHUNKS), :] = acc_scr[:, c * LANES:(c + 1) * LANES]

    @pl.when((t == last_t) & (f == last_f))
    def _():
        wait_tile((t + 1) % 2)


def _moe_experts(tile_expert, n_valid, src_tok, h, wg, wu, wd, layer):
    ts = FF_TILE_MOE // MOE_W_SPLIT
    buf_rows = -(-GATHER_ROWS // 8) * 8

    def f_eff(t, f, nv):
        return jnp.where(t < nv[0], f, N_FF_MOE - 1)

    grid_spec = pltpu.PrefetchScalarGridSpec(
        num_scalar_prefetch=3,
        grid=(MOE_TILES, N_FF_MOE),
        in_specs=[pl.BlockSpec(memory_space=pl.ANY)] + [
            pl.BlockSpec((None, None, D_MODEL, ts),
                         lambda t, f, te, nv, st, s=s: (layer, te[t], 0, MOE_W_SPLIT * f_eff(t, f, nv) + s))
            for _ in range(2) for s in range(MOE_W_SPLIT)
        ] + [
            pl.BlockSpec((None, None, ts, D_MODEL),
                         lambda t, f, te, nv, st, s=s: (layer, te[t], MOE_W_SPLIT * f_eff(t, f, nv) + s, 0))
            for s in range(MOE_W_SPLIT)
        ],
        out_specs=pl.BlockSpec((TM_MOE * ROW_CHUNKS, LANES), lambda t, f, te, nv, st: (t, 0)),
        scratch_shapes=[pltpu.VMEM((2, buf_rows * ROW_CHUNKS, LANES), F32), pltpu.VMEM((TM_MOE, D_MODEL), BF16),
                        pltpu.VMEM((TM_MOE, D_MODEL), F32), pltpu.SemaphoreType.DMA((2,))],
    )
    return pl.pallas_call(
        _moe_kernel,
        grid_spec=grid_spec,
        out_shape=jax.ShapeDtypeStruct((MOE_TILES * TM_MOE * ROW_CHUNKS, LANES), F32),
        compiler_params=_cparams("arbitrary", "arbitrary"),
        name="moe_experts",
    )(tile_expert, n_valid, src_tok, h, *([wg] * MOE_W_SPLIT + [wu] * MOE_W_SPLIT + [wd] * MOE_W_SPLIT))


def _moe_combine_kernel(slot_ref, x_ref, route_ref, ga_ref, y_hbm, o_ref, rows_scr, sems):
    i = pl.program_id(0)

    def issue_tile(tile, buf):
        for c in range(2):
            base = c * N_TOK + tile * TM

            def issue(j, carry, base=base, c=c):
                s = slot_ref[base + j]
                pltpu.make_async_copy(
                    y_hbm.at[pl.ds(pl.multiple_of(s * ROW_CHUNKS, ROW_CHUNKS), ROW_CHUNKS), :],
                    rows_scr.at[buf, c, pl.ds(pl.multiple_of(j * ROW_CHUNKS, ROW_CHUNKS), ROW_CHUNKS), :],
                    sems.at[buf]).start(priority=ROW_DMA_PRIORITY)
                return carry

            lax.fori_loop(0, TM, issue, 0, unroll=8)

    @pl.when(i == 0)
    def _():
        issue_tile(0, 0)

    @pl.when(i + 1 < pl.num_programs(0))
    def _():
        issue_tile(i + 1, (i + 1) % 2)

    buf = i % 2
    for c in range(2):
        pltpu.make_async_copy(y_hbm.at[pl.ds(0, TM * ROW_CHUNKS), :], rows_scr.at[buf, c], sems.at[buf]).wait()
    route = route_ref[...]
    g1 = route[:, ROUTE_G1:ROUTE_G1 + 1]
    g2 = route[:, ROUTE_G2:ROUTE_G2 + 1]
    for ch in range(ROW_CHUNKS):
        lanes = slice(ch * LANES, (ch + 1) * LANES)
        mix = (g1 * rows_scr[buf, 0, pl.ds(ch, TM, stride=ROW_CHUNKS), :]
               + g2 * rows_scr[buf, 1, pl.ds(ch, TM, stride=ROW_CHUNKS), :])
        o_ref[:, lanes] = x_ref[:, lanes] + ga_ref[0][:, lanes] * mix


def _moe_combine(slot, x, route, mod, y):
    grid_spec = pltpu.PrefetchScalarGridSpec(
        num_scalar_prefetch=1,
        grid=(N_TILES,),
        in_specs=[
            pl.BlockSpec((TM, D_MODEL), lambda i, s: (i, 0)),
            pl.BlockSpec((TM, LANES), lambda i, s: (i, 0)),
            _mod_spec(5),
            pl.BlockSpec(memory_space=pl.ANY),
        ],
        out_specs=pl.BlockSpec((TM, D_MODEL), lambda i, s: (i, 0)),
        scratch_shapes=[pltpu.VMEM((2, 2, TM * ROW_CHUNKS, LANES), F32), pltpu.SemaphoreType.DMA((2,))],
    )
    return pl.pallas_call(
        _moe_combine_kernel,
        grid_spec=grid_spec,
        out_shape=jax.ShapeDtypeStruct((N_TOK, D_MODEL), F32),
        compiler_params=_cparams("arbitrary"),
        name="moe_combine",
    )(slot, x, route, mod, y)


def _final_norm_kernel(x_ref, g_ref, y_ref):
    x = x_ref[...]
    y_ref[...] = (x * lax.rsqrt(jnp.mean(x * x, axis=-1, keepdims=True) + EPS)) * g_ref[...]


def _final_norm(x, g, tile0, n_tok):
    return pl.pallas_call(
        _final_norm_kernel,
        grid=(n_tok // TM,),
        in_specs=[pl.BlockSpec((TM, D_MODEL), lambda i: (tile0 + i, 0)),
                  pl.BlockSpec((1, D_MODEL), lambda i: (0, 0))],
        out_specs=pl.BlockSpec((TM, D_MODEL), lambda i: (i, 0)),
        out_shape=jax.ShapeDtypeStruct((n_tok, D_MODEL), F32),
        compiler_params=_cparams("parallel"),
        name="final_norm",
    )(x, g)


def _rope_tables():
    t = jnp.arange(DEC_SEQ)
    row = (t // GRID_W).astype(F32)
    col = (t % GRID_W).astype(F32)
    half = HEAD_DIM // 2
    inv = 1.0 / (ROPE_THETA ** (jnp.arange(0, half, 2, dtype=F32) / half))
    ar = row[:, None] * inv[None]
    ac = col[:, None] * inv[None]
    ang = jnp.concatenate([ar, ar, ac, ac], axis=-1)
    ang = jnp.concatenate([ang, jnp.zeros((TM, HEAD_DIM), F32)], axis=0)
    cos = jnp.tile(jnp.cos(ang), (1, LANES // HEAD_DIM))
    sin = jnp.tile(jnp.sin(ang), (1, LANES // HEAD_DIM))
    first_half = (jnp.arange(LANES) % 32) < 16
    sin_a = jnp.where(first_half[None, :], -sin, 0.0)
    sin_b = jnp.where(first_half[None, :], 0.0, sin)
    return cos, sin_a, sin_b


def kernel(x_prompt, x_sample, c, cache_na_k, cache_na_v, cache_diff_k, cache_diff_v, c_ctx, w_ada, b_ada, g_mix, w_in, rpb, lam_q1, lam_k1, lam_q2, lam_k2, g_subln, w_out, g_ffn, w_ffn_gate, w_ffn_up, w_ffn_down, w_router, w_moe_gate, w_moe_up, w_moe_down, g_final):
    x = jnp.concatenate([x_prompt.reshape(N_CTX_TOK, D_MODEL), x_sample.reshape(N_LAT_TOK, D_MODEL)])

    cvec = jnp.zeros((MOD_ROWS, D_MODEL), F32).at[0].set(c_ctx).at[1:1 + DEC_BATCH].set(c)
    mod_all = _modulation(cvec, w_ada, b_ada).reshape(DEPTH, MOD_ROWS * 6, 1, D_MODEL)

    col = jnp.arange(IN_WIDTH)
    is_q = (col < COL_NA_K) | ((col >= COL_D_Q) & (col < COL_D_K))
    q_scale = jnp.where(is_q, ATTN_SCALE * LOG2E, 1.0).astype(F32)
    w_in_b = (w_in * q_scale[None, None, :]).astype(BF16)
    w_out_b = w_out.astype(BF16)
    w_fg, w_fu, w_fd = (w.astype(BF16) for w in (w_ffn_gate, w_ffn_up, w_ffn_down))
    w_router_pad = jnp.pad(w_router, ((0, 0), (0, 0), (0, LANES - N_EXPERTS)))

    rope_tabs = _rope_tables()
    bias_tabs = _na_bias_tables(rpb)
    cna_k = cache_na_k.reshape(DEC_BATCH, DEPTH, PAST_LEN, NA_WIDTH)
    cna_v = cache_na_v.reshape(DEC_BATCH, DEPTH, PAST_LEN, NA_WIDTH)
    cd_k = cache_diff_k.reshape(DEC_BATCH, DEPTH, PAST_LEN, DIFF_WIDTH)
    cd_v = cache_diff_v.reshape(DEC_BATCH, DEPTH, PAST_LEN, DIFF_WIDTH)

    kv_layers = []
    for l in range(DEPTH):
        lam_init = 0.8 - 0.6 * math.exp(-0.3 * l)
        mod = mod_all[l]
        g_mix_l = g_mix[l][None, :]
        g_ffn_l = g_ffn[l][None, :]
        g_sub_l = g_subln[l][None, :]
        lam_params = tuple(p[l][None, :] for p in (lam_q1, lam_k1, lam_q2, lam_k2))

        p_all, *kv_ctx = _inproj(x, g_mix_l, mod, w_in_b[l], rope_tabs)
        kv_layers.append(kv_ctx)
        o_ctx = _ctx_attention(p_all, lam_params, g_sub_l, lam_init)
        o_na = _na_attention(p_all, cna_k, cna_v, bias_tabs, l)
        o_d = _diff_attention(p_all, cd_k, cd_v, lam_params, g_sub_l, lam_init, l)

        i = l // 2
        if l % 2 == 0:
            x = _ffn_dense(x, o_ctx, o_na, o_d, w_out_b[l], g_ffn_l, mod, w_fg[i], w_fu[i], w_fd[i])
        else:
            x, h, route, counts = _router(x, o_ctx, o_na, o_d, w_out_b[l], g_ffn_l, mod, w_router_pad[i])
            tile_expert, n_valid, src_tok, slot = _moe_plan(route, counts)
            y = _moe_experts(tile_expert, n_valid, src_tok, h, w_moe_gate, w_moe_up, w_moe_down, i)
            x = _moe_combine(slot, x, route, mod, y)

    g_fin = g_final[None, :]
    y_prompt = _final_norm(x, g_fin, 0, N_CTX_TOK).reshape(BATCH, SEQ, D_MODEL)
    y_sample = _final_norm(x, g_fin, CTX_TILES, N_LAT_TOK).reshape(DEC_BATCH, DEC_SEQ, D_MODEL)

    kv = [jnp.stack([layer_kv[k] for layer_kv in kv_layers], axis=1) for k in range(4)]
    new_na_k = kv[0].reshape(BATCH, DEPTH, SEQ, NA_HEADS, HEAD_DIM)
    new_na_v = kv[1].reshape(BATCH, DEPTH, SEQ, NA_HEADS, HEAD_DIM)
    new_diff_k = kv[2].reshape(BATCH, DEPTH, SEQ, DIFF_HEADS, 2, HEAD_DIM)
    new_diff_v = kv[3].reshape(BATCH, DEPTH, SEQ, DIFF_HEADS, 2 * HEAD_DIM)
    return (y_prompt, y_sample, new_na_k, new_na_v, new_diff_k, new_diff_v)
```

```python
import functools
import math

import numpy as np
import jax
import jax.numpy as jnp
from jax import lax
from jax.experimental import pallas as pl
from jax.experimental.pallas import tpu as pltpu

F32 = jnp.float32
BF16 = jnp.bfloat16
I32 = jnp.int32

D_MODEL = 1024
DEPTH = 4
BATCH = 16
SEQ = 256
DEC_BATCH = 8
DEC_SEQ = 2048
PAST_LEN = 256
GRID_W = 64
GRID_ROWS = DEC_SEQ // GRID_W
HEAD_DIM = 64
NA_HEADS = 8
NA_WIDTH = 512
DIFF_HEADS = 4
DIFF_WIDTH = 512
IN_WIDTH = 3072
NA_WIN_H = 8
NA_WIN_W = 16
ROPE_THETA = 10000.0
D_FF = 2816
N_EXPERTS = 8
D_FF_EXPERT = 3584
EPS = 1e-6
SUBLN_EPS = 1e-5
ATTN_SCALE = HEAD_DIM ** -0.5

LANES = 128
ROW_CHUNKS = D_MODEL // LANES
N_CTX_TOK = BATCH * SEQ
N_LAT_TOK = DEC_BATCH * DEC_SEQ
N_TOK = N_CTX_TOK + N_LAT_TOK
MOD_ROWS = 16
MASK_VALUE = -1e30

COL_NA_Q, COL_NA_K, COL_NA_V = 0, 512, 1024
COL_D_Q, COL_D_K, COL_D_V = 1536, 2048, 2560

TM = 512
N_TILES = N_TOK // TM
CTX_TILES = N_CTX_TOK // TM
TILES_PER_SEQ = DEC_SEQ // TM
FF_TILE_DENSE = 1408
FF_TILE_MOE = 512
N_FF_MOE = D_FF_EXPERT // FF_TILE_MOE
TM_MOE = 1024
MOE_W_SPLIT = 1
TQ_DIFF = 512
DIFF_SUB = 128
NA_R = 4
NA_WIN_ROWS = 12
NA_STEPS = GRID_ROWS // NA_R
MOE_SLOTS = 2 * N_TOK
MOE_TILES = MOE_SLOTS // TM_MOE + N_EXPERTS
LOG2E = 1.4426950408889634
VMEM_LIMIT = 56 * 1024 * 1024


def _cparams(*sem):
    return pltpu.CompilerParams(dimension_semantics=sem, vmem_limit_bytes=VMEM_LIMIT)


def _dot(a, b):
    return jnp.dot(a, b, preferred_element_type=F32)


def _dot_nt(a, b):
    return lax.dot_general(a, b, (((1,), (1,)), ((), ())), preferred_element_type=F32)


def _modulated_norm(x, g, shift, scale):
    xn = x * lax.rsqrt(jnp.mean(x * x, axis=-1, keepdims=True) + EPS)
    return (xn * g) * (1.0 + scale) + shift


def _mod_row(i):
    return jnp.maximum(i // TILES_PER_SEQ - CTX_TILES // TILES_PER_SEQ + 1, 0)


def _mod_spec(chunk):
    return pl.BlockSpec((1, 1, D_MODEL), lambda i, *_: (_mod_row(i) * 6 + chunk, 0, 0))


def _mod_kernel(c_ref, w_ref, b_ref, o_ref):
    cv = c_ref[...]
    s = cv * jax.nn.sigmoid(cv)
    o_ref[...] = jnp.dot(s, w_ref[...], preferred_element_type=F32,
                         precision=lax.Precision.HIGHEST) + b_ref[...]


def _modulation(cvec, w_ada, b_ada):
    tn = 1536
    n = 6 * D_MODEL
    return pl.pallas_call(
        _mod_kernel,
        grid=(DEPTH, n // tn),
        in_specs=[
            pl.BlockSpec((MOD_ROWS, D_MODEL), lambda l, j: (0, 0)),
            pl.BlockSpec((None, D_MODEL, tn), lambda l, j: (l, 0, j)),
            pl.BlockSpec((None, 1, tn), lambda l, j: (l, 0, j)),
        ],
        out_specs=pl.BlockSpec((None, MOD_ROWS, tn), lambda l, j: (l, 0, j)),
        out_shape=jax.ShapeDtypeStruct((DEPTH, MOD_ROWS, n), F32),
        compiler_params=_cparams("parallel", "parallel"),
        name="adaln_modulation",
    )(cvec, w_ada, b_ada.reshape(DEPTH, 1, n))


def _inproj_kernel(x_ref, g_ref, sh_ref, sc_ref, w_ref, cos_ref, sina_ref, sinb_ref, o_ref, *kv_refs):
    i = pl.program_id(0)
    h = _modulated_norm(x_ref[...], g_ref[...], sh_ref[0], sc_ref[0]).astype(BF16)
    chunk = 512
    for c in range(IN_WIDTH // chunk):
        col = c * chunk
        acc = _dot(h, w_ref[:, col:col + chunk])
        for k_i, src in enumerate((COL_NA_K, COL_NA_V, COL_D_K, COL_D_V)):
            if src == col:
                @pl.when(i < CTX_TILES)
                def _(acc=acc, k_i=k_i):
                    for b in range(TM // SEQ):
                        kv_refs[k_i][b] = acc[b * SEQ:(b + 1) * SEQ, :]
        if COL_D_Q <= col < COL_D_V:
            parts = []
            for j in range(chunk // LANES):
                blk = acc[:, j * LANES:(j + 1) * LANES]
                parts.append(blk * cos_ref[...]
                             + pltpu.roll(blk, LANES - 16, 1) * sina_ref[...]
                             + pltpu.roll(blk, 16, 1) * sinb_ref[...])
            acc = jnp.concatenate(parts, axis=1)
        o_ref[:, col:col + chunk] = acc.astype(o_ref.dtype)


def _inproj(x, g, mod, w_bf16, rope_tabs):
    rope_spec = pl.BlockSpec(
        (TM, LANES),
        lambda i: (jnp.where(i < CTX_TILES, TILES_PER_SEQ, (i - CTX_TILES) % TILES_PER_SEQ), 0))
    return pl.pallas_call(
        _inproj_kernel,
        grid=(N_TILES,),
        in_specs=[
            pl.BlockSpec((TM, D_MODEL), lambda i: (i, 0)),
            pl.BlockSpec((1, D_MODEL), lambda i: (0, 0)),
            _mod_spec(0), _mod_spec(1),
            pl.BlockSpec((D_MODEL, IN_WIDTH), lambda i: (0, 0)),
            rope_spec, rope_spec, rope_spec,
        ],
        out_specs=[pl.BlockSpec((TM, IN_WIDTH), lambda i: (i, 0))]
        + [pl.BlockSpec((TM // SEQ, SEQ, 512), lambda i: (jnp.minimum(i, CTX_TILES - 1), 0, 0))] * 4,
        out_shape=[jax.ShapeDtypeStruct((N_TOK, IN_WIDTH), BF16)]
        + [jax.ShapeDtypeStruct((BATCH, SEQ, 512), F32)] * 4,
        compiler_params=_cparams("arbitrary"),
        name="inproj",
    )(x, g, mod, mod, w_bf16, *rope_tabs)


def _lane_half_mask(shape, half):
    lane = lax.broadcasted_iota(I32, shape, len(shape) - 1)
    return (lane < HEAD_DIM) if half == 0 else (lane >= HEAD_DIM)


def _lambda_value(lq1, lk1, lq2, lk2, lam_init):
    a = jnp.sum(lq1 * lk1, axis=-1, keepdims=True)
    b = jnp.sum(lq2 * lk2, axis=-1, keepdims=True)
    return jnp.exp(a) - jnp.exp(b) + lam_init


def _subln(o, g, lam_init):
    on = o * lax.rsqrt(jnp.mean(o * o, axis=-1, keepdims=True) + SUBLN_EPS)
    return (on * g) * (1.0 - lam_init)


def _ctx_attn_kernel(p_ref, lq1, lk1, lq2, lk2, gs_ref, o_ref, *, lam_init):
    lam = _lambda_value(lq1[...], lk1[...], lq2[...], lk2[...], lam_init)
    for hp in range(NA_HEADS // 2):
        q = p_ref[:, COL_NA_Q + hp * LANES:COL_NA_Q + (hp + 1) * LANES]
        k = p_ref[:, COL_NA_K + hp * LANES:COL_NA_K + (hp + 1) * LANES]
        v = p_ref[:, COL_NA_V + hp * LANES:COL_NA_V + (hp + 1) * LANES]
        outs = []
        for half in range(2):
            qm = jnp.where(_lane_half_mask(q.shape, half), q, jnp.zeros_like(q))
            s = _dot_nt(qm, k)
            m = jnp.max(s, axis=-1, keepdims=True)
            e = jnp.exp2(s - m)
            inv = 1.0 / jnp.sum(e, axis=-1, keepdims=True)
            outs.append(_dot(e.astype(BF16), v) * inv)
        o = jnp.where(_lane_half_mask(outs[0].shape, 0), outs[0], outs[1])
        o_ref[:, hp * LANES:(hp + 1) * LANES] = o.astype(o_ref.dtype)
    for h in range(DIFF_HEADS):
        q = p_ref[:, COL_D_Q + h * LANES:COL_D_Q + (h + 1) * LANES]
        k = p_ref[:, COL_D_K + h * LANES:COL_D_K + (h + 1) * LANES]
        v = p_ref[:, COL_D_V + h * LANES:COL_D_V + (h + 1) * LANES]
        ps = []
        for half in range(2):
            qm = jnp.where(_lane_half_mask(q.shape, half), q, jnp.zeros_like(q))
            s = _dot_nt(qm, k)
            m = jnp.max(s, axis=-1, keepdims=True)
            e = jnp.exp2(s - m)
            ps.append(e / jnp.sum(e, axis=-1, keepdims=True))
        a = (ps[0] - lam * ps[1]).astype(BF16)
        o = _subln(_dot(a, v), gs_ref[...], lam_init)
        o_ref[:, NA_WIDTH + h * LANES:NA_WIDTH + (h + 1) * LANES] = o.astype(o_ref.dtype)


def _ctx_attention(p_all, lam_params, g_subln, lam_init):
    vec = pl.BlockSpec((1, HEAD_DIM), lambda b: (0, 0))
    return pl.pallas_call(
        functools.partial(_ctx_attn_kernel, lam_init=lam_init),
        grid=(BATCH,),
        in_specs=[pl.BlockSpec((SEQ, IN_WIDTH), lambda b: (b, 0)), vec, vec, vec, vec,
                  pl.BlockSpec((1, 2 * HEAD_DIM), lambda b: (0, 0))],
        out_specs=pl.BlockSpec((SEQ, D_MODEL), lambda b: (b, 0)),
        out_shape=jax.ShapeDtypeStruct((N_CTX_TOK, D_MODEL), BF16),
        compiler_params=_cparams("parallel"),
        name="ctx_attention",
    )(p_all, *lam_params, g_subln)


def _bias_table_kernel(rpb_ref, o_ref):
    dr_plan, ok_plan = _na_window_plan()
    shape = (GRID_W, LANES)
    qc = lax.broadcasted_iota(I32, shape, 0)
    lane = lax.broadcasted_iota(I32, shape, 1)
    kc = jnp.bitwise_and(lane, GRID_W - 1)
    cs = jnp.clip(qc - NA_WIN_W // 2, 0, GRID_W - NA_WIN_W)
    in_window = (kc >= cs) & (kc < cs + NA_WIN_W)
    low_half = lane < GRID_W
    masked = jnp.full(shape, MASK_VALUE, F32)

    pieces = {}

    def piece(dr, parity):
        if (dr, parity) not in pieces:
            base = jnp.broadcast_to(rpb_ref[dr:dr + 1, :], shape) * LOG2E
            shifted = pltpu.roll(base, 0, 1, stride=1, stride_axis=0)
            pieces[(dr, parity)] = pltpu.roll(shifted, (LANES - (NA_WIN_W - 1) + GRID_W * parity) % LANES, 1)
        return pieces[(dr, parity)]

    for t in range(3):
        for jr in range(NA_R):
            for p in range(NA_WIN_ROWS // 2):
                halves = []
                for parity in range(2):
                    i = 2 * p + parity
                    halves.append(piece(int(dr_plan[t, jr, i]), parity) if ok_plan[t, jr, i] else masked)
                blk = jnp.where(in_window, jnp.where(low_half, halves[0], halves[1]), MASK_VALUE)
                o_ref[t, jr * GRID_W:(jr + 1) * GRID_W, p * LANES:(p + 1) * LANES] = blk.astype(o_ref.dtype)


def _na_window_plan():
    dr = np.zeros((3, NA_R, NA_WIN_ROWS), np.int32)
    ok = np.zeros((3, NA_R, NA_WIN_ROWS), bool)
    for t, r0 in enumerate((0, NA_R, GRID_ROWS - NA_R)):
        lo = min(max(r0 - NA_WIN_H // 2, 0), GRID_ROWS - NA_WIN_ROWS)
        for jr in range(NA_R):
            r = r0 + jr
            rs = min(max(r - NA_WIN_H // 2, 0), GRID_ROWS - NA_WIN_H)
            for i in range(NA_WIN_ROWS):
                key_row = lo + i
                if rs <= key_row < rs + NA_WIN_H:
                    ok[t, jr, i] = True
                    dr[t, jr, i] = key_row - r + NA_WIN_H - 1
    return dr, ok


def _na_bias_tables(rpb):
    n_dr = 2 * NA_WIN_H - 1
    rpb_pad = jnp.pad(rpb.reshape(DEPTH * NA_HEADS, n_dr, 2 * NA_WIN_W - 1),
                      ((0, 0), (0, 16 - n_dr), (0, LANES - (2 * NA_WIN_W - 1))))
    return pl.pallas_call(
        _bias_table_kernel,
        grid=(DEPTH * NA_HEADS,),
        in_specs=[pl.BlockSpec((None, 16, LANES), lambda g: (g, 0, 0))],
        out_specs=pl.BlockSpec((None, 3, None, NA_R * GRID_W, NA_WIN_ROWS * GRID_W),
                               lambda g: (g // NA_HEADS, 0, g % NA_HEADS, 0, 0)),
        out_shape=jax.ShapeDtypeStruct((DEPTH, 3, NA_HEADS, NA_R * GRID_W, NA_WIN_ROWS * GRID_W), BF16),
        compiler_params=_cparams("parallel"),
        name="na_bias_table",
    )(rpb_pad)


def _na_attn_kernel(q_ref, k_ref, v_ref, kc_ref, vc_ref, bias_ref, o_ref):
    j = pl.program_id(1)
    lo = jnp.clip(j * NA_R - NA_WIN_H // 2, 0, GRID_ROWS - NA_WIN_ROWS)
    k0 = pl.multiple_of(lo * GRID_W, GRID_W)
    step_type = jnp.where(j == 0, 0, jnp.where(j == NA_STEPS - 1, 2, 1))
    n_keys = NA_WIN_ROWS * GRID_W

    def scores(head):
        lanes = slice((head // 2) * LANES, (head // 2 + 1) * LANES)
        q = q_ref[:, lanes]
        qm = jnp.where(_lane_half_mask(q.shape, head % 2), q, jnp.zeros_like(q))
        s_w = _dot_nt(qm, k_ref[pl.ds(k0, n_keys), lanes]) + bias_ref[step_type, head].astype(F32)
        s_c = _dot_nt(qm, kc_ref[:, lanes].astype(BF16))
        return s_w, s_c

    nxt = scores(0)
    outs = []
    for head in range(NA_HEADS):
        s_w, s_c = nxt
        if head + 1 < NA_HEADS:
            nxt = scores(head + 1)
        lanes = slice((head // 2) * LANES, (head // 2 + 1) * LANES)
        m = jnp.maximum(jnp.max(s_w, axis=-1, keepdims=True), jnp.max(s_c, axis=-1, keepdims=True))
        e_w = jnp.exp2(s_w - m)
        e_c = jnp.exp2(s_c - m)
        inv = 1.0 / (jnp.sum(e_w, axis=-1, keepdims=True) + jnp.sum(e_c, axis=-1, keepdims=True))
        pv = (_dot(e_w.astype(BF16), v_ref[pl.ds(k0, n_keys), lanes])
              + _dot(e_c.astype(BF16), vc_ref[:, lanes].astype(BF16)))
        outs.append(pv * inv)
        if head % 2 == 1:
            o = jnp.where(_lane_half_mask(outs[0].shape, 0), outs[0], outs[1])
            o_ref[:, lanes] = o.astype(o_ref.dtype)
            outs = []


def _na_attention(p_all, cache_k, cache_v, bias_tab, layer):
    q_rows = NA_R * GRID_W
    q_blk0 = N_CTX_TOK // q_rows
    seq_blk0 = N_CTX_TOK // DEC_SEQ
    cache_spec = pl.BlockSpec((None, None, PAST_LEN, NA_WIDTH), lambda b, j: (b, layer, 0, 0))
    return pl.pallas_call(
        _na_attn_kernel,
        grid=(DEC_BATCH, NA_STEPS),
        in_specs=[
            pl.BlockSpec((q_rows, NA_WIDTH), lambda b, j: (q_blk0 + b * NA_STEPS + j, COL_NA_Q // NA_WIDTH)),
            pl.BlockSpec((DEC_SEQ, NA_WIDTH), lambda b, j: (seq_blk0 + b, COL_NA_K // NA_WIDTH)),
            pl.BlockSpec((DEC_SEQ, NA_WIDTH), lambda b, j: (seq_blk0 + b, COL_NA_V // NA_WIDTH)),
            cache_spec, cache_spec,
            pl.BlockSpec((None, 3, NA_HEADS, q_rows, NA_WIN_ROWS * GRID_W), lambda b, j: (layer, 0, 0, 0, 0)),
        ],
        out_specs=pl.BlockSpec((q_rows, NA_WIDTH), lambda b, j: (b * NA_STEPS + j, 0)),
        out_shape=jax.ShapeDtypeStruct((N_LAT_TOK, NA_WIDTH), BF16),
        compiler_params=_cparams("parallel", "arbitrary"),
        name="latent_na_attention",
    )(p_all, p_all, p_all, cache_k, cache_v, bias_tab)


def _diff_attn_kernel(q_ref, k_ref, v_ref, kc_ref, vc_ref, lq1, lk1, lq2, lk2, gs_ref, o_ref,
                      *, lam_init):
    lam = _lambda_value(lq1[...], lk1[...], lq2[...], lk2[...], lam_init)
    k = k_ref[...]
    v = v_ref[...]
    kc = kc_ref[...].astype(BF16)
    vc = vc_ref[...].astype(BF16)
    n_sub = TQ_DIFF // DIFF_SUB

    def scores(i):
        q = q_ref[i * DIFF_SUB:(i + 1) * DIFF_SUB, :]
        out = []
        for half in range(2):
            qm = jnp.where(_lane_half_mask(q.shape, half), q, jnp.zeros_like(q))
            out.append((_dot_nt(qm, k), _dot_nt(qm, kc)))
        return out

    nxt = scores(0)
    for i in range(n_sub):
        cur = nxt
        if i + 1 < n_sub:
            nxt = scores(i + 1)
        probs = []
        for s_l, s_c in cur:
            m = jnp.maximum(jnp.max(s_l, axis=-1, keepdims=True), jnp.max(s_c, axis=-1, keepdims=True))
            e_l = jnp.exp2(s_l - m)
            e_c = jnp.exp2(s_c - m)
            inv = 1.0 / (jnp.sum(e_l, axis=-1, keepdims=True) + jnp.sum(e_c, axis=-1, keepdims=True))
            probs.append((e_l, e_c, inv))
        r = lam * probs[1][2] / probs[0][2]
        a_l = (probs[0][0] - probs[1][0] * r).astype(BF16)
        a_c = (probs[0][1] - probs[1][1] * r).astype(BF16)
        o = (_dot(a_l, v) + _dot(a_c, vc)) * probs[0][2]
        o_ref[i * DIFF_SUB:(i + 1) * DIFF_SUB, :] = _subln(o, gs_ref[...], lam_init).astype(o_ref.dtype)


def _diff_attention(p_all, cache_k, cache_v, lam_params, g_subln, lam_init, layer):
    steps = DEC_SEQ // TQ_DIFF
    q_blk0 = N_CTX_TOK // TQ_DIFF
    seq_blk0 = N_CTX_TOK // DEC_SEQ
    vec = pl.BlockSpec((1, HEAD_DIM), lambda b, h, j: (0, 0))
    cache_spec = pl.BlockSpec((None, None, PAST_LEN, LANES), lambda b, h, j: (b, layer, 0, h))
    return pl.pallas_call(
        functools.partial(_diff_attn_kernel, lam_init=lam_init),
        grid=(DEC_BATCH, DIFF_HEADS, steps),
        in_specs=[
            pl.BlockSpec((TQ_DIFF, LANES), lambda b, h, j: (q_blk0 + b * steps + j, COL_D_Q // LANES + h)),
            pl.BlockSpec((DEC_SEQ, LANES), lambda b, h, j: (seq_blk0 + b, COL_D_K // LANES + h)),
            pl.BlockSpec((DEC_SEQ, LANES), lambda b, h, j: (seq_blk0 + b, COL_D_V // LANES + h)),
            cache_spec, cache_spec, vec, vec, vec, vec,
            pl.BlockSpec((1, 2 * HEAD_DIM), lambda b, h, j: (0, 0)),
        ],
        out_specs=pl.BlockSpec((TQ_DIFF, LANES), lambda b, h, j: (b * steps + j, h)),
        out_shape=jax.ShapeDtypeStruct((N_LAT_TOK, DIFF_WIDTH), BF16),
        compiler_params=_cparams("parallel", "parallel", "arbitrary"),
        name="latent_diff_attention",
    )(p_all, p_all, p_all, cache_k, cache_v, *lam_params, g_subln)


def _attn_residual(i, x_ref, oc_ref, ona_ref, od_ref, ga_ref, w_ref, dst_ref):
    @pl.when(i < CTX_TILES)
    def _():
        dst_ref[...] = x_ref[...] + ga_ref[0] * _dot(oc_ref[...], w_ref[...])

    @pl.when(i >= CTX_TILES)
    def _():
        acc = _dot(ona_ref[...], w_ref[:NA_WIDTH, :]) + _dot(od_ref[...], w_ref[NA_WIDTH:, :])
        dst_ref[...] = x_ref[...] + ga_ref[0] * acc


def _attn_residual_specs():
    lat_idx = lambda i, *_: (jnp.maximum(i - CTX_TILES, 0), 0)
    return [
        pl.BlockSpec((TM, D_MODEL), lambda i, *_: (jnp.minimum(i, CTX_TILES - 1), 0)),
        pl.BlockSpec((TM, NA_WIDTH), lat_idx),
        pl.BlockSpec((TM, DIFF_WIDTH), lat_idx),
        _mod_spec(2),
        pl.BlockSpec((D_MODEL, D_MODEL), lambda i, *_: (0, 0)),
    ]


def _swiglu_tile(hb, wg, wu):
    g = _dot(hb, wg)
    u = _dot(hb, wu)
    return (g * jax.nn.sigmoid(g)) * u


def _ffn_kernel(x_ref, oc_ref, ona_ref, od_ref, ga1_ref, wo_ref, g_ref, sh_ref, sc_ref, ga_ref,
                wg_ref, wu_ref, wd_ref, y_ref, xmid_scr, h_scr, acc_scr):
    i = pl.program_id(0)
    f = pl.program_id(1)

    @pl.when(f == 0)
    def _():
        _attn_residual(i, x_ref, oc_ref, ona_ref, od_ref, ga1_ref, wo_ref, xmid_scr)
        h = _modulated_norm(xmid_scr[...], g_ref[...], sh_ref[0], sc_ref[0])
        h_scr[...] = h.astype(BF16)
        acc_scr[...] = jnp.zeros_like(acc_scr)

    a = _swiglu_tile(h_scr[...], wg_ref[...], wu_ref[...])
    acc_scr[...] += _dot(a.astype(BF16), wd_ref[...])

    @pl.when(f == pl.num_programs(1) - 1)
    def _():
        y_ref[...] = xmid_scr[...] + ga_ref[0] * acc_scr[...]


def _ffn_dense(x, o_ctx, o_na, o_d, w_out, g, mod, wg, wu, wd):
    tf = FF_TILE_DENSE
    return pl.pallas_call(
        _ffn_kernel,
        grid=(N_TILES, D_FF // tf),
        in_specs=[pl.BlockSpec((TM, D_MODEL), lambda i, f: (i, 0))] + _attn_residual_specs() + [
            pl.BlockSpec((1, D_MODEL), lambda i, f: (0, 0)),
            _mod_spec(3), _mod_spec(4), _mod_spec(5),
            pl.BlockSpec((D_MODEL, tf), lambda i, f: (0, f)),
            pl.BlockSpec((D_MODEL, tf), lambda i, f: (0, f)),
            pl.BlockSpec((tf, D_MODEL), lambda i, f: (f, 0)),
        ],
        out_specs=pl.BlockSpec((TM, D_MODEL), lambda i, f: (i, 0)),
        out_shape=jax.ShapeDtypeStruct((N_TOK, D_MODEL), F32),
        scratch_shapes=[pltpu.VMEM((TM, D_MODEL), F32), pltpu.VMEM((TM, D_MODEL), BF16),
                        pltpu.VMEM((TM, D_MODEL), F32)],
        compiler_params=_cparams("parallel", "arbitrary"),
        name="ffn_dense",
    )(x, o_ctx, o_na, o_d, mod, w_out, g, mod, mod, mod, wg, wu, wd)


ROUTE_E1, ROUTE_E2, ROUTE_G1, ROUTE_G2, ROUTE_R1, ROUTE_R2 = range(6)


def _router_kernel(x_ref, oc_ref, ona_ref, od_ref, ga1_ref, wo_ref, g_ref, sh_ref, sc_ref, wr_ref,
                   xmid_ref, h_ref, route_ref, cnt_ref, carry_scr):
    i = pl.program_id(0)

    @pl.when(i == 0)
    def _():
        carry_scr[...] = jnp.zeros_like(carry_scr)

    _attn_residual(i, x_ref, oc_ref, ona_ref, od_ref, ga1_ref, wo_ref, xmid_ref)
    h = _modulated_norm(xmid_ref[...], g_ref[...], sh_ref[0], sc_ref[0])
    for c in range(ROW_CHUNKS):
        h_ref[pl.ds(c, TM, stride=ROW_CHUNKS), :] = h[:, c * LANES:(c + 1) * LANES]
    logits = jnp.dot(h, wr_ref[...], preferred_element_type=F32, precision=lax.Precision.HIGHEST)
    lane = lax.broadcasted_iota(I32, logits.shape, 1)
    logits = jnp.where(lane < N_EXPERTS, logits, MASK_VALUE)
    m1 = jnp.max(logits, axis=-1, keepdims=True)
    i1 = jnp.min(jnp.where(logits == m1, lane, LANES), axis=-1, keepdims=True)
    rest = jnp.where(lane == i1, MASK_VALUE, logits)
    m2 = jnp.max(rest, axis=-1, keepdims=True)
    i2 = jnp.min(jnp.where(rest == m2, lane, LANES), axis=-1, keepdims=True)
    e2 = jnp.exp(m2 - m1)
    g1 = 1.0 / (1.0 + e2)
    g2 = e2 / (1.0 + e2)

    hit1 = lane == i1
    hit2 = lane == i2
    onehot = jnp.where(hit1 | hit2, 1.0, 0.0)
    row = lax.broadcasted_iota(I32, (TM, TM), 0)
    col = lax.broadcasted_iota(I32, (TM, TM), 1)
    lower = jnp.where(row > col, 1.0, 0.0).astype(BF16)
    before = _dot(lower, onehot.astype(BF16)) + carry_scr[...]
    r1 = jnp.sum(jnp.where(hit1, before, 0.0), axis=-1, keepdims=True)
    r2 = jnp.sum(jnp.where(hit2, before, 0.0), axis=-1, keepdims=True)
    carry_scr[...] += jnp.sum(onehot, axis=0, keepdims=True)

    out = jnp.zeros(logits.shape, F32)
    for slot, val in ((ROUTE_E1, i1.astype(F32)), (ROUTE_E2, i2.astype(F32)), (ROUTE_G1, g1),
                      (ROUTE_G2, g2), (ROUTE_R1, r1), (ROUTE_R2, r2)):
        out = jnp.where(lane == slot, val, out)
    route_ref[...] = out

    @pl.when(i == pl.num_programs(0) - 1)
    def _():
        cnt_ref[...] = jnp.broadcast_to(carry_scr[...], cnt_ref.shape)


def _router(x, o_ctx, o_na, o_d, w_out, g, mod, w_router_pad):
    return pl.pallas_call(
        _router_kernel,
        grid=(N_TILES,),
        in_specs=[pl.BlockSpec((TM, D_MODEL), lambda i: (i, 0))] + _attn_residual_specs() + [
            pl.BlockSpec((1, D_MODEL), lambda i: (0, 0)),
            _mod_spec(3), _mod_spec(4),
            pl.BlockSpec((D_MODEL, LANES), lambda i: (0, 0)),
        ],
        out_specs=[pl.BlockSpec((TM, D_MODEL), lambda i: (i, 0)),
                   pl.BlockSpec((TM * ROW_CHUNKS, LANES), lambda i: (i, 0)),
                   pl.BlockSpec((TM, LANES), lambda i: (i, 0)),
                   pl.BlockSpec((8, LANES), lambda i: (0, 0))],
        out_shape=[jax.ShapeDtypeStruct((N_TOK, D_MODEL), F32),
                   jax.ShapeDtypeStruct((N_TOK * ROW_CHUNKS, LANES), F32),
                   jax.ShapeDtypeStruct((N_TOK, LANES), F32),
                   jax.ShapeDtypeStruct((8, LANES), F32)],
        scratch_shapes=[pltpu.VMEM((1, LANES), F32)],
        compiler_params=_cparams("arbitrary"),
        name="moe_router",
    )(x, o_ctx, o_na, o_d, mod, w_out, g, mod, mod, w_router_pad)


def _moe_plan(route, counts):
    cnt = counts[0, :N_EXPERTS].astype(I32)
    tiles = (cnt + TM_MOE - 1) // TM_MOE
    tile_end = jnp.cumsum(tiles)
    group_start = (tile_end - tiles) * TM_MOE
    e = route[:, ROUTE_E1:ROUTE_E2 + 1].astype(I32)
    r = route[:, ROUTE_R1:ROUTE_R2 + 1].astype(I32)
    slot = (group_start[e] + r).T.reshape(-1)
    tok = jnp.tile(jnp.arange(N_TOK, dtype=I32), 2)
    src_tok = jnp.zeros(((MOE_TILES + 2) * TM_MOE,), I32).at[slot].set(tok)
    n_valid = tile_end[-1:]
    t = jnp.arange(MOE_TILES, dtype=I32)
    tile_expert = jnp.minimum(jnp.sum((t[:, None] >= tile_end[None, :]).astype(I32), axis=1),
                              N_EXPERTS - 1)
    last_expert = tile_expert[jnp.maximum(n_valid[0] - 1, 0)]
    tile_expert = jnp.where(t < n_valid[0], tile_expert, last_expert)
    return tile_expert, n_valid, src_tok, slot


def _moe_kernel(te_ref, nv_ref, src_ref, h_hbm, *refs):
    wg_refs = refs[:MOE_W_SPLIT]
    wu_refs = refs[MOE_W_SPLIT:2 * MOE_W_SPLIT]
    wd_refs = refs[2 * MOE_W_SPLIT:3 * MOE_W_SPLIT]
    y_ref, hrow_scr, hb_scr, acc_scr, sems = refs[3 * MOE_W_SPLIT:]
    t = pl.program_id(0)
    f = pl.program_id(1)
    valid = t < nv_ref[0]
    last_t = pl.num_programs(0) - 1
    last_f = N_FF_MOE - 1

    def row_copy(tile, row, buf):
        tok = src_ref[tile * TM_MOE + row]
        return pltpu.make_async_copy(h_hbm.at[pl.ds(pl.multiple_of(tok * ROW_CHUNKS, ROW_CHUNKS), ROW_CHUNKS), :],
                                     hrow_scr.at[buf, pl.ds(pl.multiple_of(row * ROW_CHUNKS, ROW_CHUNKS), ROW_CHUNKS), :],
                                     sems.at[buf])

    def wait_tile(buf):
        pltpu.make_async_copy(h_hbm.at[pl.ds(0, TM_MOE * ROW_CHUNKS), :], hrow_scr.at[buf], sems.at[buf]).wait()

    def issue_tile(tile, buf):
        def issue(j, carry):
            row_copy(tile, j, buf).start()
            return carry

        lax.fori_loop(0, TM_MOE, issue, 0, unroll=8)

    @pl.when((t == 0) & (f == 0))
    def _():
        issue_tile(0, 0)

    @pl.when(f == 0)
    def _():
        issue_tile(t + 1, (t + 1) % 2)
        wait_tile(t % 2)
        for c in range(ROW_CHUNKS):
            hb_scr[:, c * LANES:(c + 1) * LANES] = hrow_scr[t % 2, pl.ds(c, TM_MOE, stride=ROW_CHUNKS), :].astype(BF16)
        acc_scr[...] = jnp.zeros_like(acc_scr)

    @pl.when(valid)
    def _():
        hb = hb_scr[...]
        acc = None
        for wg_ref, wu_ref, wd_ref in zip(wg_refs, wu_refs, wd_refs):
            a = _swiglu_tile(hb, wg_ref[...].astype(BF16), wu_ref[...].astype(BF16))
            d = _dot(a.astype(BF16), wd_ref[...].astype(BF16))
            acc = d if acc is None else acc + d
        acc_scr[...] += acc

    @pl.when(f == last_f)
    def _():
        for c in range(ROW_CHUNKS):
            y_ref[pl.ds(c, TM_MOE, stride=ROW_CHUNKS), :] = acc_scr[:, c * LANES:(c + 1) * LANES]

    @pl.when((t == last_t) & (f == last_f))
    def _():
        wait_tile((t + 1) % 2)


def _moe_experts(tile_expert, n_valid, src_tok, h, wg, wu, wd, layer):
    ts = FF_TILE_MOE // MOE_W_SPLIT

    def f_eff(t, f, nv):
        return jnp.where(t < nv[0], f, N_FF_MOE - 1)

    grid_spec = pltpu.PrefetchScalarGridSpec(
        num_scalar_prefetch=3,
        grid=(MOE_TILES, N_FF_MOE),
        in_specs=[pl.BlockSpec(memory_space=pl.ANY)] + [
            pl.BlockSpec((None, None, D_MODEL, ts),
                         lambda t, f, te, nv, st, s=s: (layer, te[t], 0, MOE_W_SPLIT * f_eff(t, f, nv) + s))
            for _ in range(2) for s in range(MOE_W_SPLIT)
        ] + [
            pl.BlockSpec((None, None, ts, D_MODEL),
                         lambda t, f, te, nv, st, s=s: (layer, te[t], MOE_W_SPLIT * f_eff(t, f, nv) + s, 0))
            for s in range(MOE_W_SPLIT)
        ],
        out_specs=pl.BlockSpec((TM_MOE * ROW_CHUNKS, LANES), lambda t, f, te, nv, st: (t, 0)),
        scratch_shapes=[pltpu.VMEM((2, TM_MOE * ROW_CHUNKS, LANES), F32), pltpu.VMEM((TM_MOE, D_MODEL), BF16),
                        pltpu.VMEM((TM_MOE, D_MODEL), F32), pltpu.SemaphoreType.DMA((2,))],
    )
    return pl.pallas_call(
        _moe_kernel,
        grid_spec=grid_spec,
        out_shape=jax.ShapeDtypeStruct((MOE_TILES * TM_MOE * ROW_CHUNKS, LANES), F32),
        compiler_params=_cparams("arbitrary", "arbitrary"),
        name="moe_experts",
    )(tile_expert, n_valid, src_tok, h, *([wg] * MOE_W_SPLIT + [wu] * MOE_W_SPLIT + [wd] * MOE_W_SPLIT))


def _moe_combine_kernel(slot_ref, x_ref, route_ref, ga_ref, y_hbm, o_ref, rows_scr, sems):
    i = pl.program_id(0)

    def issue_tile(tile, buf):
        for c in range(2):
            base = c * N_TOK + tile * TM

            def issue(j, carry, base=base, c=c):
                s = slot_ref[base + j]
                pltpu.make_async_copy(
                    y_hbm.at[pl.ds(pl.multiple_of(s * ROW_CHUNKS, ROW_CHUNKS), ROW_CHUNKS), :],
                    rows_scr.at[buf, c, pl.ds(pl.multiple_of(j * ROW_CHUNKS, ROW_CHUNKS), ROW_CHUNKS), :],
                    sems.at[buf]).start()
                return carry

            lax.fori_loop(0, TM, issue, 0, unroll=8)

    @pl.when(i == 0)
    def _():
        issue_tile(0, 0)

    @pl.when(i + 1 < pl.num_programs(0))
    def _():
        issue_tile(i + 1, (i + 1) % 2)

    buf = i % 2
    for c in range(2):
        pltpu.make_async_copy(y_hbm.at[pl.ds(0, TM * ROW_CHUNKS), :], rows_scr.at[buf, c], sems.at[buf]).wait()
    route = route_ref[...]
    g1 = route[:, ROUTE_G1:ROUTE_G1 + 1]
    g2 = route[:, ROUTE_G2:ROUTE_G2 + 1]
    for ch in range(ROW_CHUNKS):
        lanes = slice(ch * LANES, (ch + 1) * LANES)
        mix = (g1 * rows_scr[buf, 0, pl.ds(ch, TM, stride=ROW_CHUNKS), :]
               + g2 * rows_scr[buf, 1, pl.ds(ch, TM, stride=ROW_CHUNKS), :])
        o_ref[:, lanes] = x_ref[:, lanes] + ga_ref[0][:, lanes] * mix


def _moe_combine(slot, x, route, mod, y):
    grid_spec = pltpu.PrefetchScalarGridSpec(
        num_scalar_prefetch=1,
        grid=(N_TILES,),
        in_specs=[
            pl.BlockSpec((TM, D_MODEL), lambda i, s: (i, 0)),
            pl.BlockSpec((TM, LANES), lambda i, s: (i, 0)),
            _mod_spec(5),
            pl.BlockSpec(memory_space=pl.ANY),
        ],
        out_specs=pl.BlockSpec((TM, D_MODEL), lambda i, s: (i, 0)),
        scratch_shapes=[pltpu.VMEM((2, 2, TM * ROW_CHUNKS, LANES), F32), pltpu.SemaphoreType.DMA((2,))],
    )
    return pl.pallas_call(
        _moe_combine_kernel,
        grid_spec=grid_spec,
        out_shape=jax.ShapeDtypeStruct((N_TOK, D_MODEL), F32),
        compiler_params=_cparams("arbitrary"),
        name="moe_combine",
    )(slot, x, route, mod, y)


def _final_norm_kernel(x_ref, g_ref, y_ref):
    x = x_ref[...]
    y_ref[...] = (x * lax.rsqrt(jnp.mean(x * x, axis=-1, keepdims=True) + EPS)) * g_ref[...]


def _final_norm(x, g, tile0, n_tok):
    return pl.pallas_call(
        _final_norm_kernel,
        grid=(n_tok // TM,),
        in_specs=[pl.BlockSpec((TM, D_MODEL), lambda i: (tile0 + i, 0)),
                  pl.BlockSpec((1, D_MODEL), lambda i: (0, 0))],
        out_specs=pl.BlockSpec((TM, D_MODEL), lambda i: (i, 0)),
        out_shape=jax.ShapeDtypeStruct((n_tok, D_MODEL), F32),
        compiler_params=_cparams("parallel"),
        name="final_norm",
    )(x, g)


def _rope_tables():
    t = jnp.arange(DEC_SEQ)
    row = (t // GRID_W).astype(F32)
    col = (t % GRID_W).astype(F32)
    half = HEAD_DIM // 2
    inv = 1.0 / (ROPE_THETA ** (jnp.arange(0, half, 2, dtype=F32) / half))
    ar = row[:, None] * inv[None]
    ac = col[:, None] * inv[None]
    ang = jnp.concatenate([ar, ar, ac, ac], axis=-1)
    ang = jnp.concatenate([ang, jnp.zeros((TM, HEAD_DIM), F32)], axis=0)
    cos = jnp.tile(jnp.cos(ang), (1, LANES // HEAD_DIM))
    sin = jnp.tile(jnp.sin(ang), (1, LANES // HEAD_DIM))
    first_half = (jnp.arange(LANES) % 32) < 16
    sin_a = jnp.where(first_half[None, :], -sin, 0.0)
    sin_b = jnp.where(first_half[None, :], 0.0, sin)
    return cos, sin_a, sin_b


def kernel(x_prompt, x_sample, c, cache_na_k, cache_na_v, cache_diff_k, cache_diff_v, c_ctx, w_ada, b_ada, g_mix, w_in, rpb, lam_q1, lam_k1, lam_q2, lam_k2, g_subln, w_out, g_ffn, w_ffn_gate, w_ffn_up, w_ffn_down, w_router, w_moe_gate, w_moe_up, w_moe_down, g_final):
    x = jnp.concatenate([x_prompt.reshape(N_CTX_TOK, D_MODEL), x_sample.reshape(N_LAT_TOK, D_MODEL)])

    cvec = jnp.zeros((MOD_ROWS, D_MODEL), F32).at[0].set(c_ctx).at[1:1 + DEC_BATCH].set(c)
    mod_all = _modulation(cvec, w_ada, b_ada).reshape(DEPTH, MOD_ROWS * 6, 1, D_MODEL)

    col = jnp.arange(IN_WIDTH)
    is_q = (col < COL_NA_K) | ((col >= COL_D_Q) & (col < COL_D_K))
    q_scale = jnp.where(is_q, ATTN_SCALE * LOG2E, 1.0).astype(F32)
    w_in_b = (w_in * q_scale[None, None, :]).astype(BF16)
    w_out_b = w_out.astype(BF16)
    w_fg, w_fu, w_fd = (w.astype(BF16) for w in (w_ffn_gate, w_ffn_up, w_ffn_down))
    w_router_pad = jnp.pad(w_router, ((0, 0), (0, 0), (0, LANES - N_EXPERTS)))

    rope_tabs = _rope_tables()
    bias_tabs = _na_bias_tables(rpb)
    cna_k = cache_na_k.reshape(DEC_BATCH, DEPTH, PAST_LEN, NA_WIDTH)
    cna_v = cache_na_v.reshape(DEC_BATCH, DEPTH, PAST_LEN, NA_WIDTH)
    cd_k = cache_diff_k.reshape(DEC_BATCH, DEPTH, PAST_LEN, DIFF_WIDTH)
    cd_v = cache_diff_v.reshape(DEC_BATCH, DEPTH, PAST_LEN, DIFF_WIDTH)

    kv_layers = []
    for l in range(DEPTH):
        lam_init = 0.8 - 0.6 * math.exp(-0.3 * l)
        mod = mod_all[l]
        g_mix_l = g_mix[l][None, :]
        g_ffn_l = g_ffn[l][None, :]
        g_sub_l = g_subln[l][None, :]
        lam_params = tuple(p[l][None, :] for p in (lam_q1, lam_k1, lam_q2, lam_k2))

        p_all, *kv_ctx = _inproj(x, g_mix_l, mod, w_in_b[l], rope_tabs)
        kv_layers.append(kv_ctx)
        o_ctx = _ctx_attention(p_all, lam_params, g_sub_l, lam_init)
        o_na = _na_attention(p_all, cna_k, cna_v, bias_tabs, l)
        o_d = _diff_attention(p_all, cd_k, cd_v, lam_params, g_sub_l, lam_init, l)

        i = l // 2
        if l % 2 == 0:
            x = _ffn_dense(x, o_ctx, o_na, o_d, w_out_b[l], g_ffn_l, mod, w_fg[i], w_fu[i], w_fd[i])
        else:
            x, h, route, counts = _router(x, o_ctx, o_na, o_d, w_out_b[l], g_ffn_l, mod, w_router_pad[i])
            tile_expert, n_valid, src_tok, slot = _moe_plan(route, counts)
            y = _moe_experts(tile_expert, n_valid, src_tok, h, w_moe_gate, w_moe_up, w_moe_down, i)
            x = _moe_combine(slot, x, route, mod, y)

    g_fin = g_final[None, :]
    y_prompt = _final_norm(x, g_fin, 0, N_CTX_TOK).reshape(BATCH, SEQ, D_MODEL)
    y_sample = _final_norm(x, g_fin, CTX_TILES, N_LAT_TOK).reshape(DEC_BATCH, DEC_SEQ, D_MODEL)

    kv = [jnp.stack([layer_kv[k] for layer_kv in kv_layers], axis=1) for k in range(4)]
    new_na_k = kv[0].reshape(BATCH, DEPTH, SEQ, NA_HEADS, HEAD_DIM)
    new_na_v = kv[1].reshape(BATCH, DEPTH, SEQ, NA_HEADS, HEAD_DIM)
    new_diff_k = kv[2].reshape(BATCH, DEPTH, SEQ, DIFF_HEADS, 2, HEAD_DIM)
    new_diff_v = kv[3].reshape(BATCH, DEPTH, SEQ, DIFF_HEADS, 2 * HEAD_DIM)
    return (y_prompt, y_sample, new_na_k, new_na_v, new_diff_k, new_diff_v)
```

```python
import functools
import math

import numpy as np
import jax
import jax.numpy as jnp
from jax import lax
from jax.experimental import pallas as pl
from jax.experimental.pallas import tpu as pltpu

F32 = jnp.float32
BF16 = jnp.bfloat16
I32 = jnp.int32

D_MODEL = 1024
DEPTH = 4
BATCH = 16
SEQ = 256
DEC_BATCH = 8
DEC_SEQ = 2048
PAST_LEN = 256
GRID_W = 64
GRID_ROWS = DEC_SEQ // GRID_W
HEAD_DIM = 64
NA_HEADS = 8
NA_WIDTH = 512
DIFF_HEADS = 4
DIFF_WIDTH = 512
IN_WIDTH = 3072
NA_WIN_H = 8
NA_WIN_W = 16
ROPE_THETA = 10000.0
D_FF = 2816
N_EXPERTS = 8
D_FF_EXPERT = 3584
EPS = 1e-6
SUBLN_EPS = 1e-5
ATTN_SCALE = HEAD_DIM ** -0.5

LANES = 128
ROW_CHUNKS = D_MODEL // LANES
N_CTX_TOK = BATCH * SEQ
N_LAT_TOK = DEC_BATCH * DEC_SEQ
N_TOK = N_CTX_TOK + N_LAT_TOK
MOD_ROWS = 16
MASK_VALUE = -1e30

COL_NA_Q, COL_NA_K, COL_NA_V = 0, 512, 1024
COL_D_Q, COL_D_K, COL_D_V = 1536, 2048, 2560

TM = 512
N_TILES = N_TOK // TM
CTX_TILES = N_CTX_TOK // TM
TILES_PER_SEQ = DEC_SEQ // TM
FF_TILE_DENSE = 1408
FF_TILE_MOE = 512
N_FF_MOE = D_FF_EXPERT // FF_TILE_MOE
TM_MOE = 1024
MOE_W_SPLIT = 1
TQ_DIFF = 512
DIFF_SUB = 128
NA_R = 4
NA_WIN_ROWS = 12
NA_STEPS = GRID_ROWS // NA_R
MOE_SLOTS = 2 * N_TOK
MOE_TILES = MOE_SLOTS // TM_MOE + N_EXPERTS
GATHER_CHUNK = -(-TM_MOE // N_FF_MOE)
GATHER_ROWS = GATHER_CHUNK * N_FF_MOE
N_SRC_SLOTS = (MOE_TILES + 2) * TM_MOE
LOG2E = 1.4426950408889634
VMEM_LIMIT = 56 * 1024 * 1024


def _cparams(*sem):
    return pltpu.CompilerParams(dimension_semantics=sem, vmem_limit_bytes=VMEM_LIMIT)


def _dot(a, b):
    return jnp.dot(a, b, preferred_element_type=F32)


def _dot_nt(a, b):
    return lax.dot_general(a, b, (((1,), (1,)), ((), ())), preferred_element_type=F32)


def _modulated_norm(x, g, shift, scale):
    xn = x * lax.rsqrt(jnp.mean(x * x, axis=-1, keepdims=True) + EPS)
    return (xn * g) * (1.0 + scale) + shift


def _mod_row(i):
    return jnp.maximum(i // TILES_PER_SEQ - CTX_TILES // TILES_PER_SEQ + 1, 0)


def _mod_spec(chunk):
    return pl.BlockSpec((1, 1, D_MODEL), lambda i, *_: (_mod_row(i) * 6 + chunk, 0, 0))


def _mod_kernel(c_ref, w_ref, b_ref, o_ref):
    cv = c_ref[...]
    s = cv * jax.nn.sigmoid(cv)
    o_ref[...] = jnp.dot(s, w_ref[...], preferred_element_type=F32,
                         precision=lax.Precision.HIGHEST) + b_ref[...]


def _modulation(cvec, w_ada, b_ada):
    tn = 1536
    n = 6 * D_MODEL
    return pl.pallas_call(
        _mod_kernel,
        grid=(DEPTH, n // tn),
        in_specs=[
            pl.BlockSpec((MOD_ROWS, D_MODEL), lambda l, j: (0, 0)),
            pl.BlockSpec((None, D_MODEL, tn), lambda l, j: (l, 0, j)),
            pl.BlockSpec((None, 1, tn), lambda l, j: (l, 0, j)),
        ],
        out_specs=pl.BlockSpec((None, MOD_ROWS, tn), lambda l, j: (l, 0, j)),
        out_shape=jax.ShapeDtypeStruct((DEPTH, MOD_ROWS, n), F32),
        compiler_params=_cparams("parallel", "parallel"),
        name="adaln_modulation",
    )(cvec, w_ada, b_ada.reshape(DEPTH, 1, n))


def _inproj_kernel(x_ref, g_ref, sh_ref, sc_ref, w_ref, cos_ref, sina_ref, sinb_ref, o_ref, *kv_refs):
    i = pl.program_id(0)
    h = _modulated_norm(x_ref[...], g_ref[...], sh_ref[0], sc_ref[0]).astype(BF16)
    chunk = 512
    for c in range(IN_WIDTH // chunk):
        col = c * chunk
        acc = _dot(h, w_ref[:, col:col + chunk])
        for k_i, src in enumerate((COL_NA_K, COL_NA_V, COL_D_K, COL_D_V)):
            if src == col:
                @pl.when(i < CTX_TILES)
                def _(acc=acc, k_i=k_i):
                    for b in range(TM // SEQ):
                        kv_refs[k_i][b] = acc[b * SEQ:(b + 1) * SEQ, :]
        if COL_D_Q <= col < COL_D_V:
            parts = []
            for j in range(chunk // LANES):
                blk = acc[:, j * LANES:(j + 1) * LANES]
                parts.append(blk * cos_ref[...]
                             + pltpu.roll(blk, LANES - 16, 1) * sina_ref[...]
                             + pltpu.roll(blk, 16, 1) * sinb_ref[...])
            acc = jnp.concatenate(parts, axis=1)
        o_ref[:, col:col + chunk] = acc.astype(o_ref.dtype)


def _inproj(x, g, mod, w_bf16, rope_tabs):
    rope_spec = pl.BlockSpec(
        (TM, LANES),
        lambda i: (jnp.where(i < CTX_TILES, TILES_PER_SEQ, (i - CTX_TILES) % TILES_PER_SEQ), 0))
    return pl.pallas_call(
        _inproj_kernel,
        grid=(N_TILES,),
        in_specs=[
            pl.BlockSpec((TM, D_MODEL), lambda i: (i, 0)),
            pl.BlockSpec((1, D_MODEL), lambda i: (0, 0)),
            _mod_spec(0), _mod_spec(1),
            pl.BlockSpec((D_MODEL, IN_WIDTH), lambda i: (0, 0)),
            rope_spec, rope_spec, rope_spec,
        ],
        out_specs=[pl.BlockSpec((TM, IN_WIDTH), lambda i: (i, 0))]
        + [pl.BlockSpec((TM // SEQ, SEQ, 512), lambda i: (jnp.minimum(i, CTX_TILES - 1), 0, 0))] * 4,
        out_shape=[jax.ShapeDtypeStruct((N_TOK, IN_WIDTH), BF16)]
        + [jax.ShapeDtypeStruct((BATCH, SEQ, 512), F32)] * 4,
        compiler_params=_cparams("arbitrary"),
        name="inproj",
    )(x, g, mod, mod, w_bf16, *rope_tabs)


def _lane_half_mask(shape, half):
    lane = lax.broadcasted_iota(I32, shape, len(shape) - 1)
    return (lane < HEAD_DIM) if half == 0 else (lane >= HEAD_DIM)


def _lambda_value(lq1, lk1, lq2, lk2, lam_init):
    a = jnp.sum(lq1 * lk1, axis=-1, keepdims=True)
    b = jnp.sum(lq2 * lk2, axis=-1, keepdims=True)
    return jnp.exp(a) - jnp.exp(b) + lam_init


def _subln(o, g, lam_init):
    on = o * lax.rsqrt(jnp.mean(o * o, axis=-1, keepdims=True) + SUBLN_EPS)
    return (on * g) * (1.0 - lam_init)


def _ctx_attn_kernel(p_ref, lq1, lk1, lq2, lk2, gs_ref, o_ref, *, lam_init):
    lam = _lambda_value(lq1[...], lk1[...], lq2[...], lk2[...], lam_init)
    for hp in range(NA_HEADS // 2):
        q = p_ref[:, COL_NA_Q + hp * LANES:COL_NA_Q + (hp + 1) * LANES]
        k = p_ref[:, COL_NA_K + hp * LANES:COL_NA_K + (hp + 1) * LANES]
        v = p_ref[:, COL_NA_V + hp * LANES:COL_NA_V + (hp + 1) * LANES]
        outs = []
        for half in range(2):
            qm = jnp.where(_lane_half_mask(q.shape, half), q, jnp.zeros_like(q))
            s = _dot_nt(qm, k)
            m = jnp.max(s, axis=-1, keepdims=True)
            e = jnp.exp2(s - m)
            inv = 1.0 / jnp.sum(e, axis=-1, keepdims=True)
            outs.append(_dot(e.astype(BF16), v) * inv)
        o = jnp.where(_lane_half_mask(outs[0].shape, 0), outs[0], outs[1])
        o_ref[:, hp * LANES:(hp + 1) * LANES] = o.astype(o_ref.dtype)
    for h in range(DIFF_HEADS):
        q = p_ref[:, COL_D_Q + h * LANES:COL_D_Q + (h + 1) * LANES]
        k = p_ref[:, COL_D_K + h * LANES:COL_D_K + (h + 1) * LANES]
        v = p_ref[:, COL_D_V + h * LANES:COL_D_V + (h + 1) * LANES]
        ps = []
        for half in range(2):
            qm = jnp.where(_lane_half_mask(q.shape, half), q, jnp.zeros_like(q))
            s = _dot_nt(qm, k)
            m = jnp.max(s, axis=-1, keepdims=True)
            e = jnp.exp2(s - m)
            ps.append(e / jnp.sum(e, axis=-1, keepdims=True))
        a = (ps[0] - lam * ps[1]).astype(BF16)
        o = _subln(_dot(a, v), gs_ref[...], lam_init)
        o_ref[:, NA_WIDTH + h * LANES:NA_WIDTH + (h + 1) * LANES] = o.astype(o_ref.dtype)


def _ctx_attention(p_all, lam_params, g_subln, lam_init):
    vec = pl.BlockSpec((1, HEAD_DIM), lambda b: (0, 0))
    return pl.pallas_call(
        functools.partial(_ctx_attn_kernel, lam_init=lam_init),
        grid=(BATCH,),
        in_specs=[pl.BlockSpec((SEQ, IN_WIDTH), lambda b: (b, 0)), vec, vec, vec, vec,
                  pl.BlockSpec((1, 2 * HEAD_DIM), lambda b: (0, 0))],
        out_specs=pl.BlockSpec((SEQ, D_MODEL), lambda b: (b, 0)),
        out_shape=jax.ShapeDtypeStruct((N_CTX_TOK, D_MODEL), BF16),
        compiler_params=_cparams("parallel"),
        name="ctx_attention",
    )(p_all, *lam_params, g_subln)


def _bias_table_kernel(rpb_ref, o_ref):
    dr_plan, ok_plan = _na_window_plan()
    shape = (GRID_W, LANES)
    qc = lax.broadcasted_iota(I32, shape, 0)
    lane = lax.broadcasted_iota(I32, shape, 1)
    kc = jnp.bitwise_and(lane, GRID_W - 1)
    cs = jnp.clip(qc - NA_WIN_W // 2, 0, GRID_W - NA_WIN_W)
    in_window = (kc >= cs) & (kc < cs + NA_WIN_W)
    low_half = lane < GRID_W
    masked = jnp.full(shape, MASK_VALUE, F32)

    pieces = {}

    def piece(dr, parity):
        if (dr, parity) not in pieces:
            base = jnp.broadcast_to(rpb_ref[dr:dr + 1, :], shape) * LOG2E
            shifted = pltpu.roll(base, 0, 1, stride=1, stride_axis=0)
            pieces[(dr, parity)] = pltpu.roll(shifted, (LANES - (NA_WIN_W - 1) + GRID_W * parity) % LANES, 1)
        return pieces[(dr, parity)]

    for t in range(3):
        for jr in range(NA_R):
            for p in range(NA_WIN_ROWS // 2):
                halves = []
                for parity in range(2):
                    i = 2 * p + parity
                    halves.append(piece(int(dr_plan[t, jr, i]), parity) if ok_plan[t, jr, i] else masked)
                blk = jnp.where(in_window, jnp.where(low_half, halves[0], halves[1]), MASK_VALUE)
                o_ref[t, jr * GRID_W:(jr + 1) * GRID_W, p * LANES:(p + 1) * LANES] = blk.astype(o_ref.dtype)


def _na_window_plan():
    dr = np.zeros((3, NA_R, NA_WIN_ROWS), np.int32)
    ok = np.zeros((3, NA_R, NA_WIN_ROWS), bool)
    for t, r0 in enumerate((0, NA_R, GRID_ROWS - NA_R)):
        lo = min(max(r0 - NA_WIN_H // 2, 0), GRID_ROWS - NA_WIN_ROWS)
        for jr in range(NA_R):
            r = r0 + jr
            rs = min(max(r - NA_WIN_H // 2, 0), GRID_ROWS - NA_WIN_H)
            for i in range(NA_WIN_ROWS):
                key_row = lo + i
                if rs <= key_row < rs + NA_WIN_H:
                    ok[t, jr, i] = True
                    dr[t, jr, i] = key_row - r + NA_WIN_H - 1
    return dr, ok


def _na_bias_tables(rpb):
    n_dr = 2 * NA_WIN_H - 1
    rpb_pad = jnp.pad(rpb.reshape(DEPTH * NA_HEADS, n_dr, 2 * NA_WIN_W - 1),
                      ((0, 0), (0, 16 - n_dr), (0, LANES - (2 * NA_WIN_W - 1))))
    return pl.pallas_call(
        _bias_table_kernel,
        grid=(DEPTH * NA_HEADS,),
        in_specs=[pl.BlockSpec((None, 16, LANES), lambda g: (g, 0, 0))],
        out_specs=pl.BlockSpec((None, 3, None, NA_R * GRID_W, NA_WIN_ROWS * GRID_W),
                               lambda g: (g // NA_HEADS, 0, g % NA_HEADS, 0, 0)),
        out_shape=jax.ShapeDtypeStruct((DEPTH, 3, NA_HEADS, NA_R * GRID_W, NA_WIN_ROWS * GRID_W), BF16),
        compiler_params=_cparams("parallel"),
        name="na_bias_table",
    )(rpb_pad)


def _na_attn_kernel(q_ref, k_ref, v_ref, kc_ref, vc_ref, bias_ref, o_ref):
    j = pl.program_id(1)
    lo = jnp.clip(j * NA_R - NA_WIN_H // 2, 0, GRID_ROWS - NA_WIN_ROWS)
    k0 = pl.multiple_of(lo * GRID_W, GRID_W)
    step_type = jnp.where(j == 0, 0, jnp.where(j == NA_STEPS - 1, 2, 1))
    n_keys = NA_WIN_ROWS * GRID_W

    def scores(head):
        lanes = slice((head // 2) * LANES, (head // 2 + 1) * LANES)
        q = q_ref[:, lanes]
        qm = jnp.where(_lane_half_mask(q.shape, head % 2), q, jnp.zeros_like(q))
        s_w = _dot_nt(qm, k_ref[pl.ds(k0, n_keys), lanes]) + bias_ref[step_type, head].astype(F32)
        s_c = _dot_nt(qm, kc_ref[:, lanes].astype(BF16))
        return s_w, s_c

    nxt = scores(0)
    outs = []
    for head in range(NA_HEADS):
        s_w, s_c = nxt
        if head + 1 < NA_HEADS:
            nxt = scores(head + 1)
        lanes = slice((head // 2) * LANES, (head // 2 + 1) * LANES)
        m = jnp.maximum(jnp.max(s_w, axis=-1, keepdims=True), jnp.max(s_c, axis=-1, keepdims=True))
        e_w = jnp.exp2(s_w - m)
        e_c = jnp.exp2(s_c - m)
        inv = 1.0 / (jnp.sum(e_w, axis=-1, keepdims=True) + jnp.sum(e_c, axis=-1, keepdims=True))
        pv = (_dot(e_w.astype(BF16), v_ref[pl.ds(k0, n_keys), lanes])
              + _dot(e_c.astype(BF16), vc_ref[:, lanes].astype(BF16)))
        outs.append(pv * inv)
        if head % 2 == 1:
            o = jnp.where(_lane_half_mask(outs[0].shape, 0), outs[0], outs[1])
            o_ref[:, lanes] = o.astype(o_ref.dtype)
            outs = []


def _na_attention(p_all, cache_k, cache_v, bias_tab, layer):
    q_rows = NA_R * GRID_W
    q_blk0 = N_CTX_TOK // q_rows
    seq_blk0 = N_CTX_TOK // DEC_SEQ
    cache_spec = pl.BlockSpec((None, None, PAST_LEN, NA_WIDTH), lambda b, j: (b, layer, 0, 0))
    return pl.pallas_call(
        _na_attn_kernel,
        grid=(DEC_BATCH, NA_STEPS),
        in_specs=[
            pl.BlockSpec((q_rows, NA_WIDTH), lambda b, j: (q_blk0 + b * NA_STEPS + j, COL_NA_Q // NA_WIDTH)),
            pl.BlockSpec((DEC_SEQ, NA_WIDTH), lambda b, j: (seq_blk0 + b, COL_NA_K // NA_WIDTH)),
            pl.BlockSpec((DEC_SEQ, NA_WIDTH), lambda b, j: (seq_blk0 + b, COL_NA_V // NA_WIDTH)),
            cache_spec, cache_spec,
            pl.BlockSpec((None, 3, NA_HEADS, q_rows, NA_WIN_ROWS * GRID_W), lambda b, j: (layer, 0, 0, 0, 0)),
        ],
        out_specs=pl.BlockSpec((q_rows, NA_WIDTH), lambda b, j: (b * NA_STEPS + j, 0)),
        out_shape=jax.ShapeDtypeStruct((N_LAT_TOK, NA_WIDTH), BF16),
        compiler_params=_cparams("parallel", "arbitrary"),
        name="latent_na_attention",
    )(p_all, p_all, p_all, cache_k, cache_v, bias_tab)


def _diff_attn_kernel(q_ref, k_ref, v_ref, kc_ref, vc_ref, lq1, lk1, lq2, lk2, gs_ref, o_ref,
                      *, lam_init):
    lam = _lambda_value(lq1[...], lk1[...], lq2[...], lk2[...], lam_init)
    k = k_ref[...]
    v = v_ref[...]
    kc = kc_ref[...].astype(BF16)
    vc = vc_ref[...].astype(BF16)
    n_sub = TQ_DIFF // DIFF_SUB

    def scores(i):
        q = q_ref[i * DIFF_SUB:(i + 1) * DIFF_SUB, :]
        out = []
        for half in range(2):
            qm = jnp.where(_lane_half_mask(q.shape, half), q, jnp.zeros_like(q))
            out.append((_dot_nt(qm, k), _dot_nt(qm, kc)))
        return out

    nxt = scores(0)
    for i in range(n_sub):
        cur = nxt
        if i + 1 < n_sub:
            nxt = scores(i + 1)
        probs = []
        for s_l, s_c in cur:
            m = jnp.maximum(jnp.max(s_l, axis=-1, keepdims=True), jnp.max(s_c, axis=-1, keepdims=True))
            e_l = jnp.exp2(s_l - m)
            e_c = jnp.exp2(s_c - m)
            inv = 1.0 / (jnp.sum(e_l, axis=-1, keepdims=True) + jnp.sum(e_c, axis=-1, keepdims=True))
            probs.append((e_l, e_c, inv))
        r = lam * probs[1][2] / probs[0][2]
        a_l = (probs[0][0] - probs[1][0] * r).astype(BF16)
        a_c = (probs[0][1] - probs[1][1] * r).astype(BF16)
        o = (_dot(a_l, v) + _dot(a_c, vc)) * probs[0][2]
        o_ref[i * DIFF_SUB:(i + 1) * DIFF_SUB, :] = _subln(o, gs_ref[...], lam_init).astype(o_ref.dtype)


def _diff_attention(p_all, cache_k, cache_v, lam_params, g_subln, lam_init, layer):
    steps = DEC_SEQ // TQ_DIFF
    q_blk0 = N_CTX_TOK // TQ_DIFF
    seq_blk0 = N_CTX_TOK // DEC_SEQ
    vec = pl.BlockSpec((1, HEAD_DIM), lambda b, h, j: (0, 0))
    cache_spec = pl.BlockSpec((None, None, PAST_LEN, LANES), lambda b, h, j: (b, layer, 0, h))
    return pl.pallas_call(
        functools.partial(_diff_attn_kernel, lam_init=lam_init),
        grid=(DEC_BATCH, DIFF_HEADS, steps),
        in_specs=[
            pl.BlockSpec((TQ_DIFF, LANES), lambda b, h, j: (q_blk0 + b * steps + j, COL_D_Q // LANES + h)),
            pl.BlockSpec((DEC_SEQ, LANES), lambda b, h, j: (seq_blk0 + b, COL_D_K // LANES + h)),
            pl.BlockSpec((DEC_SEQ, LANES), lambda b, h, j: (seq_blk0 + b, COL_D_V // LANES + h)),
            cache_spec, cache_spec, vec, vec, vec, vec,
            pl.BlockSpec((1, 2 * HEAD_DIM), lambda b, h, j: (0, 0)),
        ],
        out_specs=pl.BlockSpec((TQ_DIFF, LANES), lambda b, h, j: (b * steps + j, h)),
        out_shape=jax.ShapeDtypeStruct((N_LAT_TOK, DIFF_WIDTH), BF16),
        compiler_params=_cparams("parallel", "parallel", "arbitrary"),
        name="latent_diff_attention",
    )(p_all, p_all, p_all, cache_k, cache_v, *lam_params, g_subln)


def _attn_residual(i, x_ref, oc_ref, ona_ref, od_ref, ga_ref, w_ref, dst_ref):
    @pl.when(i < CTX_TILES)
    def _():
        dst_ref[...] = x_ref[...] + ga_ref[0] * _dot(oc_ref[...], w_ref[...])

    @pl.when(i >= CTX_TILES)
    def _():
        acc = _dot(ona_ref[...], w_ref[:NA_WIDTH, :]) + _dot(od_ref[...], w_ref[NA_WIDTH:, :])
        dst_ref[...] = x_ref[...] + ga_ref[0] * acc


def _attn_residual_specs():
    lat_idx = lambda i, *_: (jnp.maximum(i - CTX_TILES, 0), 0)
    return [
        pl.BlockSpec((TM, D_MODEL), lambda i, *_: (jnp.minimum(i, CTX_TILES - 1), 0)),
        pl.BlockSpec((TM, NA_WIDTH), lat_idx),
        pl.BlockSpec((TM, DIFF_WIDTH), lat_idx),
        _mod_spec(2),
        pl.BlockSpec((D_MODEL, D_MODEL), lambda i, *_: (0, 0)),
    ]


def _swiglu_tile(hb, wg, wu):
    g = _dot(hb, wg)
    u = _dot(hb, wu)
    return (g * jax.nn.sigmoid(g)) * u


def _ffn_kernel(x_ref, oc_ref, ona_ref, od_ref, ga1_ref, wo_ref, g_ref, sh_ref, sc_ref, ga_ref,
                wg_ref, wu_ref, wd_ref, y_ref, xmid_scr, h_scr, acc_scr):
    i = pl.program_id(0)
    f = pl.program_id(1)

    @pl.when(f == 0)
    def _():
        _attn_residual(i, x_ref, oc_ref, ona_ref, od_ref, ga1_ref, wo_ref, xmid_scr)
        h = _modulated_norm(xmid_scr[...], g_ref[...], sh_ref[0], sc_ref[0])
        h_scr[...] = h.astype(BF16)
        acc_scr[...] = jnp.zeros_like(acc_scr)

    a = _swiglu_tile(h_scr[...], wg_ref[...], wu_ref[...])
    acc_scr[...] += _dot(a.astype(BF16), wd_ref[...])

    @pl.when(f == pl.num_programs(1) - 1)
    def _():
        y_ref[...] = xmid_scr[...] + ga_ref[0] * acc_scr[...]


def _ffn_dense(x, o_ctx, o_na, o_d, w_out, g, mod, wg, wu, wd):
    tf = FF_TILE_DENSE
    return pl.pallas_call(
        _ffn_kernel,
        grid=(N_TILES, D_FF // tf),
        in_specs=[pl.BlockSpec((TM, D_MODEL), lambda i, f: (i, 0))] + _attn_residual_specs() + [
            pl.BlockSpec((1, D_MODEL), lambda i, f: (0, 0)),
            _mod_spec(3), _mod_spec(4), _mod_spec(5),
            pl.BlockSpec((D_MODEL, tf), lambda i, f: (0, f)),
            pl.BlockSpec((D_MODEL, tf), lambda i, f: (0, f)),
            pl.BlockSpec((tf, D_MODEL), lambda i, f: (f, 0)),
        ],
        out_specs=pl.BlockSpec((TM, D_MODEL), lambda i, f: (i, 0)),
        out_shape=jax.ShapeDtypeStruct((N_TOK, D_MODEL), F32),
        scratch_shapes=[pltpu.VMEM((TM, D_MODEL), F32), pltpu.VMEM((TM, D_MODEL), BF16),
                        pltpu.VMEM((TM, D_MODEL), F32)],
        compiler_params=_cparams("parallel", "arbitrary"),
        name="ffn_dense",
    )(x, o_ctx, o_na, o_d, mod, w_out, g, mod, mod, mod, wg, wu, wd)


ROUTE_E1, ROUTE_E2, ROUTE_G1, ROUTE_G2, ROUTE_R1, ROUTE_R2 = range(6)


def _router_kernel(x_ref, oc_ref, ona_ref, od_ref, ga1_ref, wo_ref, g_ref, sh_ref, sc_ref, wr_ref,
                   xmid_ref, h_ref, route_ref, cnt_ref, carry_scr):
    i = pl.program_id(0)

    @pl.when(i == 0)
    def _():
        carry_scr[...] = jnp.zeros_like(carry_scr)

    _attn_residual(i, x_ref, oc_ref, ona_ref, od_ref, ga1_ref, wo_ref, xmid_ref)
    h = _modulated_norm(xmid_ref[...], g_ref[...], sh_ref[0], sc_ref[0])
    for c in range(ROW_CHUNKS):
        h_ref[pl.ds(c, TM, stride=ROW_CHUNKS), :] = h[:, c * LANES:(c + 1) * LANES]
    logits = jnp.dot(h, wr_ref[...], preferred_element_type=F32, precision=lax.Precision.HIGHEST)
    lane = lax.broadcasted_iota(I32, logits.shape, 1)
    logits = jnp.where(lane < N_EXPERTS, logits, MASK_VALUE)
    m1 = jnp.max(logits, axis=-1, keepdims=True)
    i1 = jnp.min(jnp.where(logits == m1, lane, LANES), axis=-1, keepdims=True)
    rest = jnp.where(lane == i1, MASK_VALUE, logits)
    m2 = jnp.max(rest, axis=-1, keepdims=True)
    i2 = jnp.min(jnp.where(rest == m2, lane, LANES), axis=-1, keepdims=True)
    e2 = jnp.exp(m2 - m1)
    g1 = 1.0 / (1.0 + e2)
    g2 = e2 / (1.0 + e2)

    hit1 = lane == i1
    hit2 = lane == i2
    onehot = jnp.where(hit1 | hit2, 1.0, 0.0)
    row = lax.broadcasted_iota(I32, (TM, TM), 0)
    col = lax.broadcasted_iota(I32, (TM, TM), 1)
    lower = jnp.where(row > col, 1.0, 0.0).astype(BF16)
    before = _dot(lower, onehot.astype(BF16)) + carry_scr[...]
    r1 = jnp.sum(jnp.where(hit1, before, 0.0), axis=-1, keepdims=True)
    r2 = jnp.sum(jnp.where(hit2, before, 0.0), axis=-1, keepdims=True)
    carry_scr[...] += jnp.sum(onehot, axis=0, keepdims=True)

    out = jnp.zeros(logits.shape, F32)
    for slot, val in ((ROUTE_E1, i1.astype(F32)), (ROUTE_E2, i2.astype(F32)), (ROUTE_G1, g1),
                      (ROUTE_G2, g2), (ROUTE_R1, r1), (ROUTE_R2, r2)):
        out = jnp.where(lane == slot, val, out)
    route_ref[...] = out

    @pl.when(i == pl.num_programs(0) - 1)
    def _():
        cnt_ref[...] = jnp.broadcast_to(carry_scr[...], cnt_ref.shape)


def _router(x, o_ctx, o_na, o_d, w_out, g, mod, w_router_pad):
    return pl.pallas_call(
        _router_kernel,
        grid=(N_TILES,),
        in_specs=[pl.BlockSpec((TM, D_MODEL), lambda i: (i, 0))] + _attn_residual_specs() + [
            pl.BlockSpec((1, D_MODEL), lambda i: (0, 0)),
            _mod_spec(3), _mod_spec(4),
            pl.BlockSpec((D_MODEL, LANES), lambda i: (0, 0)),
        ],
        out_specs=[pl.BlockSpec((TM, D_MODEL), lambda i: (i, 0)),
                   pl.BlockSpec((TM * ROW_CHUNKS, LANES), lambda i: (i, 0)),
                   pl.BlockSpec((TM, LANES), lambda i: (i, 0)),
                   pl.BlockSpec((8, LANES), lambda i: (0, 0))],
        out_shape=[jax.ShapeDtypeStruct((N_TOK, D_MODEL), F32),
                   jax.ShapeDtypeStruct((N_TOK * ROW_CHUNKS, LANES), F32),
                   jax.ShapeDtypeStruct((N_TOK, LANES), F32),
                   jax.ShapeDtypeStruct((8, LANES), F32)],
        scratch_shapes=[pltpu.VMEM((1, LANES), F32)],
        compiler_params=_cparams("arbitrary"),
        name="moe_router",
    )(x, o_ctx, o_na, o_d, mod, w_out, g, mod, mod, w_router_pad)


def _slot_sources_kernel(slot_ref, src_ref):
    def clear(j, carry):
        src_ref[j] = 0
        return carry

    lax.fori_loop(0, N_SRC_SLOTS, clear, 0, unroll=8)

    def put(j, carry):
        src_ref[slot_ref[j]] = jnp.where(j >= N_TOK, j - N_TOK, j)
        return carry

    lax.fori_loop(0, MOE_SLOTS, put, 0, unroll=8)


def _slot_sources(slot):
    return pl.pallas_call(
        _slot_sources_kernel,
        in_specs=[pl.BlockSpec(memory_space=pltpu.SMEM)],
        out_specs=pl.BlockSpec(memory_space=pltpu.SMEM),
        out_shape=jax.ShapeDtypeStruct((N_SRC_SLOTS,), I32),
        name="moe_slot_sources",
    )(slot)


def _moe_plan(route, counts):
    cnt = counts[0, :N_EXPERTS].astype(I32)
    tiles = (cnt + TM_MOE - 1) // TM_MOE
    tile_end = jnp.cumsum(tiles)
    group_start = (tile_end - tiles) * TM_MOE
    e = route[:, ROUTE_E1:ROUTE_E2 + 1].astype(I32)
    r = route[:, ROUTE_R1:ROUTE_R2 + 1].astype(I32)
    slot = (group_start[e] + r).T.reshape(-1)
    src_tok = _slot_sources(slot)
    n_valid = tile_end[-1:]
    t = jnp.arange(MOE_TILES, dtype=I32)
    tile_expert = jnp.minimum(jnp.sum((t[:, None] >= tile_end[None, :]).astype(I32), axis=1),
                              N_EXPERTS - 1)
    last_expert = tile_expert[jnp.maximum(n_valid[0] - 1, 0)]
    tile_expert = jnp.where(t < n_valid[0], tile_expert, last_expert)
    return tile_expert, n_valid, src_tok, slot


def _moe_kernel(te_ref, nv_ref, src_ref, h_hbm, *refs):
    wg_refs = refs[:MOE_W_SPLIT]
    wu_refs = refs[MOE_W_SPLIT:2 * MOE_W_SPLIT]
    wd_refs = refs[2 * MOE_W_SPLIT:3 * MOE_W_SPLIT]
    y_ref, hrow_scr, hb_scr, acc_scr, sems = refs[3 * MOE_W_SPLIT:]
    t = pl.program_id(0)
    f = pl.program_id(1)
    valid = t < nv_ref[0]
    last_t = pl.num_programs(0) - 1
    last_f = N_FF_MOE - 1

    def row_copy(tile, row, buf):
        tok = src_ref[tile * TM_MOE + row]
        return pltpu.make_async_copy(h_hbm.at[pl.ds(pl.multiple_of(tok * ROW_CHUNKS, ROW_CHUNKS), ROW_CHUNKS), :],
                                     hrow_scr.at[buf, pl.ds(pl.multiple_of(row * ROW_CHUNKS, ROW_CHUNKS), ROW_CHUNKS), :],
                                     sems.at[buf])

    def wait_tile(buf):
        pltpu.make_async_copy(h_hbm.at[pl.ds(0, GATHER_ROWS * ROW_CHUNKS), :],
                              hrow_scr.at[buf, pl.ds(0, GATHER_ROWS * ROW_CHUNKS), :], sems.at[buf]).wait()

    def issue_next_chunk():
        for j in range(GATHER_CHUNK):
            row_copy(t + 1, f * GATHER_CHUNK + j, (t + 1) % 2).start()

    @pl.when((t == 0) & (f == 0))
    def _():
        def issue(j, carry):
            row_copy(0, j, 0).start()
            return carry

        lax.fori_loop(0, GATHER_ROWS, issue, 0, unroll=7)

    @pl.when(f == 0)
    def _():
        wait_tile(t % 2)
        for c in range(ROW_CHUNKS):
            hb_scr[:, c * LANES:(c + 1) * LANES] = hrow_scr[t % 2, pl.ds(c, TM_MOE, stride=ROW_CHUNKS), :].astype(BF16)
        acc_scr[...] = jnp.zeros_like(acc_scr)

    @pl.when(valid)
    def _():
        issue_next_chunk()
        hb = hb_scr[...]
        acc = None
        for wg_ref, wu_ref, wd_ref in zip(wg_refs, wu_refs, wd_refs):
            a = _swiglu_tile(hb, wg_ref[...].astype(BF16), wu_ref[...].astype(BF16))
            d = _dot(a.astype(BF16), wd_ref[...].astype(BF16))
            acc = d if acc is None else acc + d
        acc_scr[...] += acc

    @pl.when(jnp.logical_not(valid))
    def _():
        issue_next_chunk()

    @pl.when(f == last_f)
    def _():
        for c in range(ROW_CHUNKS):
            y_ref[pl.ds(c, TM_MOE, stride=ROW_CHUNKS), :] = acc_scr[:, c * LANES:(c + 1) * LANES]

    @pl.when((t == last_t) & (f == last_f))
    def _():
        wait_tile((t + 1) % 2)


def _moe_experts(tile_expert, n_valid, src_tok, h, wg, wu, wd, layer):
    ts = FF_TILE_MOE // MOE_W_SPLIT
    buf_rows = -(-GATHER_ROWS // 8) * 8

    def f_eff(t, f, nv):
        return jnp.where(t < nv[0], f, N_FF_MOE - 1)

    grid_spec = pltpu.PrefetchScalarGridSpec(
        num_scalar_prefetch=3,
        grid=(MOE_TILES, N_FF_MOE),
        in_specs=[pl.BlockSpec(memory_space=pl.ANY)] + [
            pl.BlockSpec((None, None, D_MODEL, ts),
                         lambda t, f, te, nv, st, s=s: (layer, te[t], 0, MOE_W_SPLIT * f_eff(t, f, nv) + s))
            for _ in range(2) for s in range(MOE_W_SPLIT)
        ] + [
            pl.BlockSpec((None, None, ts, D_MODEL),
                         lambda t, f, te, nv, st, s=s: (layer, te[t], MOE_W_SPLIT * f_eff(t, f, nv) + s, 0))
            for s in range(MOE_W_SPLIT)
        ],
        out_specs=pl.BlockSpec((TM_MOE * ROW_CHUNKS, LANES), lambda t, f, te, nv, st: (t, 0)),
        scratch_shapes=[pltpu.VMEM((2, buf_rows * ROW_CHUNKS, LANES), F32), pltpu.VMEM((TM_MOE, D_MODEL), BF16),
                        pltpu.VMEM((TM_MOE, D_MODEL), F32), pltpu.SemaphoreType.DMA((2,))],
    )
    return pl.pallas_call(
        _moe_kernel,
        grid_spec=grid_spec,
        out_shape=jax.ShapeDtypeStruct((MOE_TILES * TM_MOE * ROW_CHUNKS, LANES), F32),
        compiler_params=_cparams("arbitrary", "arbitrary"),
        name="moe_experts",
    )(tile_expert, n_valid, src_tok, h, *([wg] * MOE_W_SPLIT + [wu] * MOE_W_SPLIT + [wd] * MOE_W_SPLIT))


def _moe_combine_kernel(slot_ref, x_ref, route_ref, ga_ref, y_hbm, o_ref, rows_scr, sems):
    i = pl.program_id(0)

    def issue_tile(tile, buf):
        for c in range(2):
            base = c * N_TOK + tile * TM

            def issue(j, carry, base=base, c=c):
                s = slot_ref[base + j]
                pltpu.make_async_copy(
                    y_hbm.at[pl.ds(pl.multiple_of(s * ROW_CHUNKS, ROW_CHUNKS), ROW_CHUNKS), :],
                    rows_scr.at[buf, c, pl.ds(pl.multiple_of(j * ROW_CHUNKS, ROW_CHUNKS), ROW_CHUNKS), :],
                    sems.at[buf]).start()
                return carry

            lax.fori_loop(0, TM, issue, 0, unroll=8)

    @pl.when(i == 0)
    def _():
        issue_tile(0, 0)

    @pl.when(i + 1 < pl.num_programs(0))
    def _():
        issue_tile(i + 1, (i + 1) % 2)

    buf = i % 2
    for c in range(2):
        pltpu.make_async_copy(y_hbm.at[pl.ds(0, TM * ROW_CHUNKS), :], rows_scr.at[buf, c], sems.at[buf]).wait()
    route = route_ref[...]
    g1 = route[:, ROUTE_G1:ROUTE_G1 + 1]
    g2 = route[:, ROUTE_G2:ROUTE_G2 + 1]
    for ch in range(ROW_CHUNKS):
        lanes = slice(ch * LANES, (ch + 1) * LANES)
        mix = (g1 * rows_scr[buf, 0, pl.ds(ch, TM, stride=ROW_CHUNKS), :]
               + g2 * rows_scr[buf, 1, pl.ds(ch, TM, stride=ROW_CHUNKS), :])
        o_ref[:, lanes] = x_ref[:, lanes] + ga_ref[0][:, lanes] * mix


def _moe_combine(slot, x, route, mod, y):
    grid_spec = pltpu.PrefetchScalarGridSpec(
        num_scalar_prefetch=1,
        grid=(N_TILES,),
        in_specs=[
            pl.BlockSpec((TM, D_MODEL), lambda i, s: (i, 0)),
            pl.BlockSpec((TM, LANES), lambda i, s: (i, 0)),
            _mod_spec(5),
            pl.BlockSpec(memory_space=pl.ANY),
        ],
        out_specs=pl.BlockSpec((TM, D_MODEL), lambda i, s: (i, 0)),
        scratch_shapes=[pltpu.VMEM((2, 2, TM * ROW_CHUNKS, LANES), F32), pltpu.SemaphoreType.DMA((2,))],
    )
    return pl.pallas_call(
        _moe_combine_kernel,
        grid_spec=grid_spec,
        out_shape=jax.ShapeDtypeStruct((N_TOK, D_MODEL), F32),
        compiler_params=_cparams("arbitrary"),
        name="moe_combine",
    )(slot, x, route, mod, y)


def _final_norm_kernel(x_ref, g_ref, y_ref):
    x = x_ref[...]
    y_ref[...] = (x * lax.rsqrt(jnp.mean(x * x, axis=-1, keepdims=True) + EPS)) * g_ref[...]


def _final_norm(x, g, tile0, n_tok):
    return pl.pallas_call(
        _final_norm_kernel,
        grid=(n_tok // TM,),
        in_specs=[pl.BlockSpec((TM, D_MODEL), lambda i: (tile0 + i, 0)),
                  pl.BlockSpec((1, D_MODEL), lambda i: (0, 0))],
        out_specs=pl.BlockSpec((TM, D_MODEL), lambda i: (i, 0)),
        out_shape=jax.ShapeDtypeStruct((n_tok, D_MODEL), F32),
        compiler_params=_cparams("parallel"),
        name="final_norm",
    )(x, g)


def _rope_tables():
    t = jnp.arange(DEC_SEQ)
    row = (t // GRID_W).astype(F32)
    col = (t % GRID_W).astype(F32)
    half = HEAD_DIM // 2
    inv = 1.0 / (ROPE_THETA ** (jnp.arange(0, half, 2, dtype=F32) / half))
    ar = row[:, None] * inv[None]
    ac = col[:, None] * inv[None]
    ang = jnp.concatenate([ar, ar, ac, ac], axis=-1)
    ang = jnp.concatenate([ang, jnp.zeros((TM, HEAD_DIM), F32)], axis=0)
    cos = jnp.tile(jnp.cos(ang), (1, LANES // HEAD_DIM))
    sin = jnp.tile(jnp.sin(ang), (1, LANES // HEAD_DIM))
    first_half = (jnp.arange(LANES) % 32) < 16
    sin_a = jnp.where(first_half[None, :], -sin, 0.0)
    sin_b = jnp.where(first_half[None, :], 0.0, sin)
    return cos, sin_a, sin_b


def kernel(x_prompt, x_sample, c, cache_na_k, cache_na_v, cache_diff_k, cache_diff_v, c_ctx, w_ada, b_ada, g_mix, w_in, rpb, lam_q1, lam_k1, lam_q2, lam_k2, g_subln, w_out, g_ffn, w_ffn_gate, w_ffn_up, w_ffn_down, w_router, w_moe_gate, w_moe_up, w_moe_down, g_final):
    x = jnp.concatenate([x_prompt.reshape(N_CTX_TOK, D_MODEL), x_sample.reshape(N_LAT_TOK, D_MODEL)])

    cvec = jnp.zeros((MOD_ROWS, D_MODEL), F32).at[0].set(c_ctx).at[1:1 + DEC_BATCH].set(c)
    mod_all = _modulation(cvec, w_ada, b_ada).reshape(DEPTH, MOD_ROWS * 6, 1, D_MODEL)

    col = jnp.arange(IN_WIDTH)
    is_q = (col < COL_NA_K) | ((col >= COL_D_Q) & (col < COL_D_K))
    q_scale = jnp.where(is_q, ATTN_SCALE * LOG2E, 1.0).astype(F32)
    w_in_b = (w_in * q_scale[None, None, :]).astype(BF16)
    w_out_b = w_out.astype(BF16)
    w_fg, w_fu, w_fd = (w.astype(BF16) for w in (w_ffn_gate, w_ffn_up, w_ffn_down))
    w_router_pad = jnp.pad(w_router, ((0, 0), (0, 0), (0, LANES - N_EXPERTS)))

    rope_tabs = _rope_tables()
    bias_tabs = _na_bias_tables(rpb)
    cna_k = cache_na_k.reshape(DEC_BATCH, DEPTH, PAST_LEN, NA_WIDTH)
    cna_v = cache_na_v.reshape(DEC_BATCH, DEPTH, PAST_LEN, NA_WIDTH)
    cd_k = cache_diff_k.reshape(DEC_BATCH, DEPTH, PAST_LEN, DIFF_WIDTH)
    cd_v = cache_diff_v.reshape(DEC_BATCH, DEPTH, PAST_LEN, DIFF_WIDTH)

    kv_layers = []
    for l in range(DEPTH):
        lam_init = 0.8 - 0.6 * math.exp(-0.3 * l)
        mod = mod_all[l]
        g_mix_l = g_mix[l][None, :]
        g_ffn_l = g_ffn[l][None, :]
        g_sub_l = g_subln[l][None, :]
        lam_params = tuple(p[l][None, :] for p in (lam_q1, lam_k1, lam_q2, lam_k2))

        p_all, *kv_ctx = _inproj(x, g_mix_l, mod, w_in_b[l], rope_tabs)
        kv_layers.append(kv_ctx)
        o_ctx = _ctx_attention(p_all, lam_params, g_sub_l, lam_init)
        o_na = _na_attention(p_all, cna_k, cna_v, bias_tabs, l)
        o_d = _diff_attention(p_all, cd_k, cd_v, lam_params, g_sub_l, lam_init, l)

        i = l // 2
        if l % 2 == 0:
            x = _ffn_dense(x, o_ctx, o_na, o_d, w_out_b[l], g_ffn_l, mod, w_fg[i], w_fu[i], w_fd[i])
        else:
            x, h, route, counts = _router(x, o_ctx, o_na, o_d, w_out_b[l], g_ffn_l, mod, w_router_pad[i])
            tile_expert, n_valid, src_tok, slot = _moe_plan(route, counts)
            y = _moe_experts(tile_expert, n_valid, src_tok, h, w_moe_gate, w_moe_up, w_moe_down, i)
            x = _moe_combine(slot, x, route, mod, y)

    g_fin = g_final[None, :]
    y_prompt = _final_norm(x, g_fin, 0, N_CTX_TOK).reshape(BATCH, SEQ, D_MODEL)
    y_sample = _final_norm(x, g_fin, CTX_TILES, N_LAT_TOK).reshape(DEC_BATCH, DEC_SEQ, D_MODEL)

    kv = [jnp.stack([layer_kv[k] for layer_kv in kv_layers], axis=1) for k in range(4)]
    new_na_k = kv[0].reshape(BATCH, DEPTH, SEQ, NA_HEADS, HEAD_DIM)
    new_na_v = kv[1].reshape(BATCH, DEPTH, SEQ, NA_HEADS, HEAD_DIM)
    new_diff_k = kv[2].reshape(BATCH, DEPTH, SEQ, DIFF_HEADS, 2, HEAD_DIM)
    new_diff_v = kv[3].reshape(BATCH, DEPTH, SEQ, DIFF_HEADS, 2 * HEAD_DIM)
    return (y_prompt, y_sample, new_na_k, new_na_v, new_diff_k, new_diff_v)
```

```python
import functools
import math

import numpy as np
import jax
import jax.numpy as jnp
from jax import lax
from jax.experimental import pallas as pl
from jax.experimental.pallas import tpu as pltpu

F32 = jnp.float32
BF16 = jnp.bfloat16
I32 = jnp.int32

D_MODEL = 1024
DEPTH = 4
BATCH = 16
SEQ = 256
DEC_BATCH = 8
DEC_SEQ = 2048
PAST_LEN = 256
GRID_W = 64
GRID_ROWS = DEC_SEQ // GRID_W
HEAD_DIM = 64
NA_HEADS = 8
NA_WIDTH = 512
DIFF_HEADS = 4
DIFF_WIDTH = 512
IN_WIDTH = 3072
NA_WIN_H = 8
NA_WIN_W = 16
ROPE_THETA = 10000.0
D_FF = 2816
N_EXPERTS = 8
D_FF_EXPERT = 3584
EPS = 1e-6
SUBLN_EPS = 1e-5
ATTN_SCALE = HEAD_DIM ** -0.5

LANES = 128
ROW_CHUNKS = D_MODEL // LANES
N_CTX_TOK = BATCH * SEQ
N_LAT_TOK = DEC_BATCH * DEC_SEQ
N_TOK = N_CTX_TOK + N_LAT_TOK
MOD_ROWS = 16
MASK_VALUE = -1e30

COL_NA_Q, COL_NA_K, COL_NA_V = 0, 512, 1024
COL_D_Q, COL_D_K, COL_D_V = 1536, 2048, 2560

TM = 512
N_TILES = N_TOK // TM
CTX_TILES = N_CTX_TOK // TM
TILES_PER_SEQ = DEC_SEQ // TM
FF_TILE_DENSE = 1408
FF_TILE_MOE = 512
N_FF_MOE = D_FF_EXPERT // FF_TILE_MOE
TM_MOE = 1024
MOE_W_SPLIT = 2
TQ_DIFF = 1024
DIFF_SUB = 128
NA_R = 4
NA_WIN_ROWS = 12
NA_STEPS = GRID_ROWS // NA_R
MOE_SLOTS = 2 * N_TOK
MOE_TILES = MOE_SLOTS // TM_MOE + N_EXPERTS
GATHER_CHUNK = -(-TM_MOE // N_FF_MOE)
GATHER_ROWS = GATHER_CHUNK * N_FF_MOE
N_SRC_SLOTS = (MOE_TILES + 2) * TM_MOE
LOG2E = 1.4426950408889634
VMEM_LIMIT = 56 * 1024 * 1024


def _cparams(*sem):
    return pltpu.CompilerParams(dimension_semantics=sem, vmem_limit_bytes=VMEM_LIMIT)


def _dot(a, b):
    return jnp.dot(a, b, preferred_element_type=F32)


def _dot_nt(a, b):
    return lax.dot_general(a, b, (((1,), (1,)), ((), ())), preferred_element_type=F32)


def _modulated_norm(x, g, shift, scale):
    xn = x * lax.rsqrt(jnp.mean(x * x, axis=-1, keepdims=True) + EPS)
    return (xn * g) * (1.0 + scale) + shift


def _mod_row(i):
    return jnp.maximum(i // TILES_PER_SEQ - CTX_TILES // TILES_PER_SEQ + 1, 0)


def _mod_spec(chunk):
    return pl.BlockSpec((1, 1, D_MODEL), lambda i, *_: (_mod_row(i) * 6 + chunk, 0, 0))


def _mod_kernel(c_ref, w_ref, b_ref, o_ref):
    cv = c_ref[...]
    s = cv * jax.nn.sigmoid(cv)
    o_ref[...] = jnp.dot(s, w_ref[...], preferred_element_type=F32,
                         precision=lax.Precision.HIGHEST) + b_ref[...]


def _modulation(cvec, w_ada, b_ada):
    tn = 1536
    n = 6 * D_MODEL
    return pl.pallas_call(
        _mod_kernel,
        grid=(DEPTH, n // tn),
        in_specs=[
            pl.BlockSpec((MOD_ROWS, D_MODEL), lambda l, j: (0, 0)),
            pl.BlockSpec((None, D_MODEL, tn), lambda l, j: (l, 0, j)),
            pl.BlockSpec((None, 1, tn), lambda l, j: (l, 0, j)),
        ],
        out_specs=pl.BlockSpec((None, MOD_ROWS, tn), lambda l, j: (l, 0, j)),
        out_shape=jax.ShapeDtypeStruct((DEPTH, MOD_ROWS, n), F32),
        compiler_params=_cparams("parallel", "parallel"),
        name="adaln_modulation",
    )(cvec, w_ada, b_ada.reshape(DEPTH, 1, n))


def _inproj_kernel(x_ref, g_ref, sh_ref, sc_ref, w_ref, cos_ref, sina_ref, sinb_ref, o_ref, *kv_refs):
    i = pl.program_id(0)
    h = _modulated_norm(x_ref[...], g_ref[...], sh_ref[0], sc_ref[0]).astype(BF16)
    chunk = 512
    for c in range(IN_WIDTH // chunk):
        col = c * chunk
        acc = _dot(h, w_ref[:, col:col + chunk])
        for k_i, src in enumerate((COL_NA_K, COL_NA_V, COL_D_K, COL_D_V)):
            if src == col:
                @pl.when(i < CTX_TILES)
                def _(acc=acc, k_i=k_i):
                    for b in range(TM // SEQ):
                        kv_refs[k_i][b] = acc[b * SEQ:(b + 1) * SEQ, :]
        if COL_D_Q <= col < COL_D_V:
            parts = []
            for j in range(chunk // LANES):
                blk = acc[:, j * LANES:(j + 1) * LANES]
                parts.append(blk * cos_ref[...]
                             + pltpu.roll(blk, LANES - 16, 1) * sina_ref[...]
                             + pltpu.roll(blk, 16, 1) * sinb_ref[...])
            acc = jnp.concatenate(parts, axis=1)
        o_ref[:, col:col + chunk] = acc.astype(o_ref.dtype)


def _inproj(x, g, mod, w_bf16, rope_tabs):
    rope_spec = pl.BlockSpec(
        (TM, LANES),
        lambda i: (jnp.where(i < CTX_TILES, TILES_PER_SEQ, (i - CTX_TILES) % TILES_PER_SEQ), 0))
    return pl.pallas_call(
        _inproj_kernel,
        grid=(N_TILES,),
        in_specs=[
            pl.BlockSpec((TM, D_MODEL), lambda i: (i, 0)),
            pl.BlockSpec((1, D_MODEL), lambda i: (0, 0)),
            _mod_spec(0), _mod_spec(1),
            pl.BlockSpec((D_MODEL, IN_WIDTH), lambda i: (0, 0)),
            rope_spec, rope_spec, rope_spec,
        ],
        out_specs=[pl.BlockSpec((TM, IN_WIDTH), lambda i: (i, 0))]
        + [pl.BlockSpec((TM // SEQ, SEQ, 512), lambda i: (jnp.minimum(i, CTX_TILES - 1), 0, 0))] * 4,
        out_shape=[jax.ShapeDtypeStruct((N_TOK, IN_WIDTH), BF16)]
        + [jax.ShapeDtypeStruct((BATCH, SEQ, 512), F32)] * 4,
        compiler_params=_cparams("arbitrary"),
        name="inproj",
    )(x, g, mod, mod, w_bf16, *rope_tabs)


def _lane_half_mask(shape, half):
    lane = lax.broadcasted_iota(I32, shape, len(shape) - 1)
    return (lane < HEAD_DIM) if half == 0 else (lane >= HEAD_DIM)


def _lambda_value(lq1, lk1, lq2, lk2, lam_init):
    a = jnp.sum(lq1 * lk1, axis=-1, keepdims=True)
    b = jnp.sum(lq2 * lk2, axis=-1, keepdims=True)
    return jnp.exp(a) - jnp.exp(b) + lam_init


def _subln(o, g, lam_init):
    on = o * lax.rsqrt(jnp.mean(o * o, axis=-1, keepdims=True) + SUBLN_EPS)
    return (on * g) * (1.0 - lam_init)


def _ctx_attn_kernel(p_ref, lq1, lk1, lq2, lk2, gs_ref, o_ref, *, lam_init):
    lam = _lambda_value(lq1[...], lk1[...], lq2[...], lk2[...], lam_init)
    for hp in range(NA_HEADS // 2):
        q = p_ref[:, COL_NA_Q + hp * LANES:COL_NA_Q + (hp + 1) * LANES]
        k = p_ref[:, COL_NA_K + hp * LANES:COL_NA_K + (hp + 1) * LANES]
        v = p_ref[:, COL_NA_V + hp * LANES:COL_NA_V + (hp + 1) * LANES]
        outs = []
        for half in range(2):
            qm = jnp.where(_lane_half_mask(q.shape, half), q, jnp.zeros_like(q))
            s = _dot_nt(qm, k)
            m = jnp.max(s, axis=-1, keepdims=True)
            e = jnp.exp2(s - m)
            inv = 1.0 / jnp.sum(e, axis=-1, keepdims=True)
            outs.append(_dot(e.astype(BF16), v) * inv)
        o = jnp.where(_lane_half_mask(outs[0].shape, 0), outs[0], outs[1])
        o_ref[:, hp * LANES:(hp + 1) * LANES] = o.astype(o_ref.dtype)
    for h in range(DIFF_HEADS):
        q = p_ref[:, COL_D_Q + h * LANES:COL_D_Q + (h + 1) * LANES]
        k = p_ref[:, COL_D_K + h * LANES:COL_D_K + (h + 1) * LANES]
        v = p_ref[:, COL_D_V + h * LANES:COL_D_V + (h + 1) * LANES]
        ps = []
        for half in range(2):
            qm = jnp.where(_lane_half_mask(q.shape, half), q, jnp.zeros_like(q))
            s = _dot_nt(qm, k)
            m = jnp.max(s, axis=-1, keepdims=True)
            e = jnp.exp2(s - m)
            ps.append(e / jnp.sum(e, axis=-1, keepdims=True))
        a = (ps[0] - lam * ps[1]).astype(BF16)
        o = _subln(_dot(a, v), gs_ref[...], lam_init)
        o_ref[:, NA_WIDTH + h * LANES:NA_WIDTH + (h + 1) * LANES] = o.astype(o_ref.dtype)


def _ctx_attention(p_all, lam_params, g_subln, lam_init):
    vec = pl.BlockSpec((1, HEAD_DIM), lambda b: (0, 0))
    return pl.pallas_call(
        functools.partial(_ctx_attn_kernel, lam_init=lam_init),
        grid=(BATCH,),
        in_specs=[pl.BlockSpec((SEQ, IN_WIDTH), lambda b: (b, 0)), vec, vec, vec, vec,
                  pl.BlockSpec((1, 2 * HEAD_DIM), lambda b: (0, 0))],
        out_specs=pl.BlockSpec((SEQ, D_MODEL), lambda b: (b, 0)),
        out_shape=jax.ShapeDtypeStruct((N_CTX_TOK, D_MODEL), BF16),
        compiler_params=_cparams("parallel"),
        name="ctx_attention",
    )(p_all, *lam_params, g_subln)


def _bias_table_kernel(rpb_ref, o_ref):
    dr_plan, ok_plan = _na_window_plan()
    shape = (GRID_W, LANES)
    qc = lax.broadcasted_iota(I32, shape, 0)
    lane = lax.broadcasted_iota(I32, shape, 1)
    kc = jnp.bitwise_and(lane, GRID_W - 1)
    cs = jnp.clip(qc - NA_WIN_W // 2, 0, GRID_W - NA_WIN_W)
    in_window = (kc >= cs) & (kc < cs + NA_WIN_W)
    low_half = lane < GRID_W
    masked = jnp.full(shape, MASK_VALUE, F32)

    pieces = {}

    def piece(dr, parity):
        if (dr, parity) not in pieces:
            base = jnp.broadcast_to(rpb_ref[dr:dr + 1, :], shape) * LOG2E
            shifted = pltpu.roll(base, 0, 1, stride=1, stride_axis=0)
            pieces[(dr, parity)] = pltpu.roll(shifted, (LANES - (NA_WIN_W - 1) + GRID_W * parity) % LANES, 1)
        return pieces[(dr, parity)]

    for t in range(3):
        for jr in range(NA_R):
            for p in range(NA_WIN_ROWS // 2):
                halves = []
                for parity in range(2):
                    i = 2 * p + parity
                    halves.append(piece(int(dr_plan[t, jr, i]), parity) if ok_plan[t, jr, i] else masked)
                blk = jnp.where(in_window, jnp.where(low_half, halves[0], halves[1]), MASK_VALUE)
                o_ref[t, jr * GRID_W:(jr + 1) * GRID_W, p * LANES:(p + 1) * LANES] = blk.astype(o_ref.dtype)


def _na_window_plan():
    dr = np.zeros((3, NA_R, NA_WIN_ROWS), np.int32)
    ok = np.zeros((3, NA_R, NA_WIN_ROWS), bool)
    for t, r0 in enumerate((0, NA_R, GRID_ROWS - NA_R)):
        lo = min(max(r0 - NA_WIN_H // 2, 0), GRID_ROWS - NA_WIN_ROWS)
        for jr in range(NA_R):
            r = r0 + jr
            rs = min(max(r - NA_WIN_H // 2, 0), GRID_ROWS - NA_WIN_H)
            for i in range(NA_WIN_ROWS):
                key_row = lo + i
                if rs <= key_row < rs + NA_WIN_H:
                    ok[t, jr, i] = True
                    dr[t, jr, i] = key_row - r + NA_WIN_H - 1
    return dr, ok


def _na_bias_tables(rpb):
    n_dr = 2 * NA_WIN_H - 1
    rpb_pad = jnp.pad(rpb.reshape(DEPTH * NA_HEADS, n_dr, 2 * NA_WIN_W - 1),
                      ((0, 0), (0, 16 - n_dr), (0, LANES - (2 * NA_WIN_W - 1))))
    return pl.pallas_call(
        _bias_table_kernel,
        grid=(DEPTH * NA_HEADS,),
        in_specs=[pl.BlockSpec((None, 16, LANES), lambda g: (g, 0, 0))],
        out_specs=pl.BlockSpec((None, 3, None, NA_R * GRID_W, NA_WIN_ROWS * GRID_W),
                               lambda g: (g // NA_HEADS, 0, g % NA_HEADS, 0, 0)),
        out_shape=jax.ShapeDtypeStruct((DEPTH, 3, NA_HEADS, NA_R * GRID_W, NA_WIN_ROWS * GRID_W), BF16),
        compiler_params=_cparams("parallel"),
        name="na_bias_table",
    )(rpb_pad)


def _na_attn_kernel(q_ref, k_ref, v_ref, kc_ref, vc_ref, bias_ref, o_ref):
    j = pl.program_id(1)
    lo = jnp.clip(j * NA_R - NA_WIN_H // 2, 0, GRID_ROWS - NA_WIN_ROWS)
    k0 = pl.multiple_of(lo * GRID_W, GRID_W)
    step_type = jnp.where(j == 0, 0, jnp.where(j == NA_STEPS - 1, 2, 1))
    n_keys = NA_WIN_ROWS * GRID_W

    def scores(head):
        lanes = slice((head // 2) * LANES, (head // 2 + 1) * LANES)
        q = q_ref[:, lanes]
        qm = jnp.where(_lane_half_mask(q.shape, head % 2), q, jnp.zeros_like(q))
        s_w = _dot_nt(qm, k_ref[pl.ds(k0, n_keys), lanes]) + bias_ref[step_type, head].astype(F32)
        s_c = _dot_nt(qm, kc_ref[:, lanes].astype(BF16))
        return s_w, s_c

    nxt = scores(0)
    outs = []
    for head in range(NA_HEADS):
        s_w, s_c = nxt
        if head + 1 < NA_HEADS:
            nxt = scores(head + 1)
        lanes = slice((head // 2) * LANES, (head // 2 + 1) * LANES)
        m = jnp.maximum(jnp.max(s_w, axis=-1, keepdims=True), jnp.max(s_c, axis=-1, keepdims=True))
        e_w = jnp.exp2(s_w - m)
        e_c = jnp.exp2(s_c - m)
        inv = 1.0 / (jnp.sum(e_w, axis=-1, keepdims=True) + jnp.sum(e_c, axis=-1, keepdims=True))
        pv = (_dot(e_w.astype(BF16), v_ref[pl.ds(k0, n_keys), lanes])
              + _dot(e_c.astype(BF16), vc_ref[:, lanes].astype(BF16)))
        outs.append(pv * inv)
        if head % 2 == 1:
            o = jnp.where(_lane_half_mask(outs[0].shape, 0), outs[0], outs[1])
            o_ref[:, lanes] = o.astype(o_ref.dtype)
            outs = []


def _na_attention(p_all, cache_k, cache_v, bias_tab, layer):
    q_rows = NA_R * GRID_W
    q_blk0 = N_CTX_TOK // q_rows
    seq_blk0 = N_CTX_TOK // DEC_SEQ
    cache_spec = pl.BlockSpec((None, None, PAST_LEN, NA_WIDTH), lambda b, j: (b, layer, 0, 0))
    return pl.pallas_call(
        _na_attn_kernel,
        grid=(DEC_BATCH, NA_STEPS),
        in_specs=[
            pl.BlockSpec((q_rows, NA_WIDTH), lambda b, j: (q_blk0 + b * NA_STEPS + j, COL_NA_Q // NA_WIDTH)),
            pl.BlockSpec((DEC_SEQ, NA_WIDTH), lambda b, j: (seq_blk0 + b, COL_NA_K // NA_WIDTH)),
            pl.BlockSpec((DEC_SEQ, NA_WIDTH), lambda b, j: (seq_blk0 + b, COL_NA_V // NA_WIDTH)),
            cache_spec, cache_spec,
            pl.BlockSpec((None, 3, NA_HEADS, q_rows, NA_WIN_ROWS * GRID_W), lambda b, j: (layer, 0, 0, 0, 0)),
        ],
        out_specs=pl.BlockSpec((q_rows, NA_WIDTH), lambda b, j: (b * NA_STEPS + j, 0)),
        out_shape=jax.ShapeDtypeStruct((N_LAT_TOK, NA_WIDTH), BF16),
        compiler_params=_cparams("parallel", "arbitrary"),
        name="latent_na_attention",
    )(p_all, p_all, p_all, cache_k, cache_v, bias_tab)


def _diff_attn_kernel(q_ref, k_ref, v_ref, kc_ref, vc_ref, lq1, lk1, lq2, lk2, gs_ref, o_ref,
                      *, lam_init):
    lam = _lambda_value(lq1[...], lk1[...], lq2[...], lk2[...], lam_init)
    k = k_ref[...]
    v = v_ref[...]
    kc = kc_ref[...].astype(BF16)
    vc = vc_ref[...].astype(BF16)
    n_sub = TQ_DIFF // DIFF_SUB

    def scores(i):
        q = q_ref[i * DIFF_SUB:(i + 1) * DIFF_SUB, :]
        out = []
        for half in range(2):
            qm = jnp.where(_lane_half_mask(q.shape, half), q, jnp.zeros_like(q))
            out.append((_dot_nt(qm, k), _dot_nt(qm, kc)))
        return out

    nxt = scores(0)
    for i in range(n_sub):
        cur = nxt
        if i + 1 < n_sub:
            nxt = scores(i + 1)
        probs = []
        for s_l, s_c in cur:
            m = jnp.maximum(jnp.max(s_l, axis=-1, keepdims=True), jnp.max(s_c, axis=-1, keepdims=True))
            e_l = jnp.exp2(s_l - m)
            e_c = jnp.exp2(s_c - m)
            inv = 1.0 / (jnp.sum(e_l, axis=-1, keepdims=True) + jnp.sum(e_c, axis=-1, keepdims=True))
            probs.append((e_l, e_c, inv))
        r = lam * probs[1][2] / probs[0][2]
        a_l = (probs[0][0] - probs[1][0] * r).astype(BF16)
        a_c = (probs[0][1] - probs[1][1] * r).astype(BF16)
        o = (_dot(a_l, v) + _dot(a_c, vc)) * probs[0][2]
        o_ref[i * DIFF_SUB:(i + 1) * DIFF_SUB, :] = _subln(o, gs_ref[...], lam_init).astype(o_ref.dtype)


def _diff_attention(p_all, cache_k, cache_v, lam_params, g_subln, lam_init, layer):
    steps = DEC_SEQ // TQ_DIFF
    q_blk0 = N_CTX_TOK // TQ_DIFF
    seq_blk0 = N_CTX_TOK // DEC_SEQ
    vec = pl.BlockSpec((1, HEAD_DIM), lambda b, h, j: (0, 0))
    cache_spec = pl.BlockSpec((None, None, PAST_LEN, LANES), lambda b, h, j: (b, layer, 0, h))
    return pl.pallas_call(
        functools.partial(_diff_attn_kernel, lam_init=lam_init),
        grid=(DEC_BATCH, DIFF_HEADS, steps),
        in_specs=[
            pl.BlockSpec((TQ_DIFF, LANES), lambda b, h, j: (q_blk0 + b * steps + j, COL_D_Q // LANES + h)),
            pl.BlockSpec((DEC_SEQ, LANES), lambda b, h, j: (seq_blk0 + b, COL_D_K // LANES + h)),
            pl.BlockSpec((DEC_SEQ, LANES), lambda b, h, j: (seq_blk0 + b, COL_D_V // LANES + h)),
            cache_spec, cache_spec, vec, vec, vec, vec,
            pl.BlockSpec((1, 2 * HEAD_DIM), lambda b, h, j: (0, 0)),
        ],
        out_specs=pl.BlockSpec((TQ_DIFF, LANES), lambda b, h, j: (b * steps + j, h)),
        out_shape=jax.ShapeDtypeStruct((N_LAT_TOK, DIFF_WIDTH), BF16),
        compiler_params=_cparams("parallel", "parallel", "arbitrary"),
        name="latent_diff_attention",
    )(p_all, p_all, p_all, cache_k, cache_v, *lam_params, g_subln)


def _attn_residual(i, x_ref, oc_ref, ona_ref, od_ref, ga_ref, w_ref, dst_ref):
    @pl.when(i < CTX_TILES)
    def _():
        dst_ref[...] = x_ref[...] + ga_ref[0] * _dot(oc_ref[...], w_ref[...])

    @pl.when(i >= CTX_TILES)
    def _():
        acc = _dot(ona_ref[...], w_ref[:NA_WIDTH, :]) + _dot(od_ref[...], w_ref[NA_WIDTH:, :])
        dst_ref[...] = x_ref[...] + ga_ref[0] * acc


def _attn_residual_specs():
    lat_idx = lambda i, *_: (jnp.maximum(i - CTX_TILES, 0), 0)
    return [
        pl.BlockSpec((TM, D_MODEL), lambda i, *_: (jnp.minimum(i, CTX_TILES - 1), 0)),
        pl.BlockSpec((TM, NA_WIDTH), lat_idx),
        pl.BlockSpec((TM, DIFF_WIDTH), lat_idx),
        _mod_spec(2),
        pl.BlockSpec((D_MODEL, D_MODEL), lambda i, *_: (0, 0)),
    ]


def _swiglu_tile(hb, wg, wu):
    g = _dot(hb, wg)
    u = _dot(hb, wu)
    return (g * jax.nn.sigmoid(g)) * u


def _ffn_kernel(x_ref, oc_ref, ona_ref, od_ref, ga1_ref, wo_ref, g_ref, sh_ref, sc_ref, ga_ref,
                wg_ref, wu_ref, wd_ref, y_ref, xmid_scr, h_scr, acc_scr):
    i = pl.program_id(0)
    f = pl.program_id(1)

    @pl.when(f == 0)
    def _():
        _attn_residual(i, x_ref, oc_ref, ona_ref, od_ref, ga1_ref, wo_ref, xmid_scr)
        h = _modulated_norm(xmid_scr[...], g_ref[...], sh_ref[0], sc_ref[0])
        h_scr[...] = h.astype(BF16)
        acc_scr[...] = jnp.zeros_like(acc_scr)

    a = _swiglu_tile(h_scr[...], wg_ref[...], wu_ref[...])
    acc_scr[...] += _dot(a.astype(BF16), wd_ref[...])

    @pl.when(f == pl.num_programs(1) - 1)
    def _():
        y_ref[...] = xmid_scr[...] + ga_ref[0] * acc_scr[...]


def _ffn_dense(x, o_ctx, o_na, o_d, w_out, g, mod, wg, wu, wd):
    tf = FF_TILE_DENSE
    return pl.pallas_call(
        _ffn_kernel,
        grid=(N_TILES, D_FF // tf),
        in_specs=[pl.BlockSpec((TM, D_MODEL), lambda i, f: (i, 0))] + _attn_residual_specs() + [
            pl.BlockSpec((1, D_MODEL), lambda i, f: (0, 0)),
            _mod_spec(3), _mod_spec(4), _mod_spec(5),
            pl.BlockSpec((D_MODEL, tf), lambda i, f: (0, f)),
            pl.BlockSpec((D_MODEL, tf), lambda i, f: (0, f)),
            pl.BlockSpec((tf, D_MODEL), lambda i, f: (f, 0)),
        ],
        out_specs=pl.BlockSpec((TM, D_MODEL), lambda i, f: (i, 0)),
        out_shape=jax.ShapeDtypeStruct((N_TOK, D_MODEL), F32),
        scratch_shapes=[pltpu.VMEM((TM, D_MODEL), F32), pltpu.VMEM((TM, D_MODEL), BF16),
                        pltpu.VMEM((TM, D_MODEL), F32)],
        compiler_params=_cparams("parallel", "arbitrary"),
        name="ffn_dense",
    )(x, o_ctx, o_na, o_d, mod, w_out, g, mod, mod, mod, wg, wu, wd)


ROUTE_E1, ROUTE_E2, ROUTE_G1, ROUTE_G2, ROUTE_R1, ROUTE_R2 = range(6)


def _router_kernel(x_ref, oc_ref, ona_ref, od_ref, ga1_ref, wo_ref, g_ref, sh_ref, sc_ref, wr_ref,
                   xmid_ref, h_ref, route_ref, cnt_ref, carry_scr):
    i = pl.program_id(0)

    @pl.when(i == 0)
    def _():
        carry_scr[...] = jnp.zeros_like(carry_scr)

    _attn_residual(i, x_ref, oc_ref, ona_ref, od_ref, ga1_ref, wo_ref, xmid_ref)
    h = _modulated_norm(xmid_ref[...], g_ref[...], sh_ref[0], sc_ref[0])
    for c in range(ROW_CHUNKS):
        h_ref[pl.ds(c, TM, stride=ROW_CHUNKS), :] = h[:, c * LANES:(c + 1) * LANES]
    logits = jnp.dot(h, wr_ref[...], preferred_element_type=F32, precision=lax.Precision.HIGHEST)
    lane = lax.broadcasted_iota(I32, logits.shape, 1)
    logits = jnp.where(lane < N_EXPERTS, logits, MASK_VALUE)
    m1 = jnp.max(logits, axis=-1, keepdims=True)
    i1 = jnp.min(jnp.where(logits == m1, lane, LANES), axis=-1, keepdims=True)
    rest = jnp.where(lane == i1, MASK_VALUE, logits)
    m2 = jnp.max(rest, axis=-1, keepdims=True)
    i2 = jnp.min(jnp.where(rest == m2, lane, LANES), axis=-1, keepdims=True)
    e2 = jnp.exp(m2 - m1)
    g1 = 1.0 / (1.0 + e2)
    g2 = e2 / (1.0 + e2)

    hit1 = lane == i1
    hit2 = lane == i2
    onehot = jnp.where(hit1 | hit2, 1.0, 0.0)
    row = lax.broadcasted_iota(I32, (TM, TM), 0)
    col = lax.broadcasted_iota(I32, (TM, TM), 1)
    lower = jnp.where(row > col, 1.0, 0.0).astype(BF16)
    before = _dot(lower, onehot.astype(BF16)) + carry_scr[...]
    r1 = jnp.sum(jnp.where(hit1, before, 0.0), axis=-1, keepdims=True)
    r2 = jnp.sum(jnp.where(hit2, before, 0.0), axis=-1, keepdims=True)
    carry_scr[...] += jnp.sum(onehot, axis=0, keepdims=True)

    out = jnp.zeros(logits.shape, F32)
    for slot, val in ((ROUTE_E1, i1.astype(F32)), (ROUTE_E2, i2.astype(F32)), (ROUTE_G1, g1),
                      (ROUTE_G2, g2), (ROUTE_R1, r1), (ROUTE_R2, r2)):
        out = jnp.where(lane == slot, val, out)
    route_ref[...] = out

    @pl.when(i == pl.num_programs(0) - 1)
    def _():
        cnt_ref[...] = jnp.broadcast_to(carry_scr[...], cnt_ref.shape)


def _router(x, o_ctx, o_na, o_d, w_out, g, mod, w_router_pad):
    return pl.pallas_call(
        _router_kernel,
        grid=(N_TILES,),
        in_specs=[pl.BlockSpec((TM, D_MODEL), lambda i: (i, 0))] + _attn_residual_specs() + [
            pl.BlockSpec((1, D_MODEL), lambda i: (0, 0)),
            _mod_spec(3), _mod_spec(4),
            pl.BlockSpec((D_MODEL, LANES), lambda i: (0, 0)),
        ],
        out_specs=[pl.BlockSpec((TM, D_MODEL), lambda i: (i, 0)),
                   pl.BlockSpec((TM * ROW_CHUNKS, LANES), lambda i: (i, 0)),
                   pl.BlockSpec((TM, LANES), lambda i: (i, 0)),
                   pl.BlockSpec((8, LANES), lambda i: (0, 0))],
        out_shape=[jax.ShapeDtypeStruct((N_TOK, D_MODEL), F32),
                   jax.ShapeDtypeStruct((N_TOK * ROW_CHUNKS, LANES), F32),
                   jax.ShapeDtypeStruct((N_TOK, LANES), F32),
                   jax.ShapeDtypeStruct((8, LANES), F32)],
        scratch_shapes=[pltpu.VMEM((1, LANES), F32)],
        compiler_params=_cparams("arbitrary"),
        name="moe_router",
    )(x, o_ctx, o_na, o_d, mod, w_out, g, mod, mod, w_router_pad)


def _moe_plan(route, counts):
    cnt = counts[0, :N_EXPERTS].astype(I32)
    tiles = (cnt + TM_MOE - 1) // TM_MOE
    tile_end = jnp.cumsum(tiles)
    group_start = (tile_end - tiles) * TM_MOE
    e = route[:, ROUTE_E1:ROUTE_E2 + 1].astype(I32)
    r = route[:, ROUTE_R1:ROUTE_R2 + 1].astype(I32)
    slot = (group_start[e] + r).T.reshape(-1)
    tok = jnp.tile(jnp.arange(N_TOK, dtype=I32), 2)
    src_tok = jnp.zeros((N_SRC_SLOTS,), I32).at[slot].set(tok)
    n_valid = tile_end[-1:]
    t = jnp.arange(MOE_TILES, dtype=I32)
    tile_expert = jnp.minimum(jnp.sum((t[:, None] >= tile_end[None, :]).astype(I32), axis=1),
                              N_EXPERTS - 1)
    last_expert = tile_expert[jnp.maximum(n_valid[0] - 1, 0)]
    tile_expert = jnp.where(t < n_valid[0], tile_expert, last_expert)
    return tile_expert, n_valid, src_tok, slot


def _moe_kernel(te_ref, nv_ref, src_ref, h_hbm, *refs):
    wg_refs = refs[:MOE_W_SPLIT]
    wu_refs = refs[MOE_W_SPLIT:2 * MOE_W_SPLIT]
    wd_refs = refs[2 * MOE_W_SPLIT:3 * MOE_W_SPLIT]
    y_ref, hrow_scr, hb_scr, acc_scr, sems = refs[3 * MOE_W_SPLIT:]
    t = pl.program_id(0)
    f = pl.program_id(1)
    valid = t < nv_ref[0]
    last_t = pl.num_programs(0) - 1
    last_f = N_FF_MOE - 1

    def row_copy(tile, row, buf):
        tok = src_ref[tile * TM_MOE + row]
        return pltpu.make_async_copy(h_hbm.at[pl.ds(pl.multiple_of(tok * ROW_CHUNKS, ROW_CHUNKS), ROW_CHUNKS), :],
                                     hrow_scr.at[buf, pl.ds(pl.multiple_of(row * ROW_CHUNKS, ROW_CHUNKS), ROW_CHUNKS), :],
                                     sems.at[buf])

    def wait_tile(buf):
        pltpu.make_async_copy(h_hbm.at[pl.ds(0, GATHER_ROWS * ROW_CHUNKS), :],
                              hrow_scr.at[buf, pl.ds(0, GATHER_ROWS * ROW_CHUNKS), :], sems.at[buf]).wait()

    def issue_next_chunk():
        for j in range(GATHER_CHUNK):
            row_copy(t + 1, f * GATHER_CHUNK + j, (t + 1) % 2).start()

    @pl.when((t == 0) & (f == 0))
    def _():
        def issue(j, carry):
            row_copy(0, j, 0).start()
            return carry

        lax.fori_loop(0, GATHER_ROWS, issue, 0, unroll=7)

    @pl.when(f == 0)
    def _():
        wait_tile(t % 2)
        for c in range(ROW_CHUNKS):
            hb_scr[:, c * LANES:(c + 1) * LANES] = hrow_scr[t % 2, pl.ds(c, TM_MOE, stride=ROW_CHUNKS), :].astype(BF16)
        acc_scr[...] = jnp.zeros_like(acc_scr)

    @pl.when(valid)
    def _():
        issue_next_chunk()
        hb = hb_scr[...]
        acc = None
        for wg_ref, wu_ref, wd_ref in zip(wg_refs, wu_refs, wd_refs):
            a = _swiglu_tile(hb, wg_ref[...].astype(BF16), wu_ref[...].astype(BF16))
            d = _dot(a.astype(BF16), wd_ref[...].astype(BF16))
            acc = d if acc is None else acc + d
        acc_scr[...] += acc

    @pl.when(jnp.logical_not(valid))
    def _():
        issue_next_chunk()

    @pl.when(f == last_f)
    def _():
        for c in range(ROW_CHUNKS):
            y_ref[pl.ds(c, TM_MOE, stride=ROW_CHUNKS), :] = acc_scr[:, c * LANES:(c + 1) * LANES]

    @pl.when((t == last_t) & (f == last_f))
    def _():
        wait_tile((t + 1) % 2)


def _moe_experts(tile_expert, n_valid, src_tok, h, wg, wu, wd, layer):
    ts = FF_TILE_MOE // MOE_W_SPLIT
    buf_rows = -(-GATHER_ROWS // 8) * 8

    def f_eff(t, f, nv):
        return jnp.where(t < nv[0], f, N_FF_MOE - 1)

    grid_spec = pltpu.PrefetchScalarGridSpec(
        num_scalar_prefetch=3,
        grid=(MOE_TILES, N_FF_MOE),
        in_specs=[pl.BlockSpec(memory_space=pl.ANY)] + [
            pl.BlockSpec((None, None, D_MODEL, ts),
                         lambda t, f, te, nv, st, s=s: (layer, te[t], 0, MOE_W_SPLIT * f_eff(t, f, nv) + s))
            for _ in range(2) for s in range(MOE_W_SPLIT)
        ] + [
            pl.BlockSpec((None, None, ts, D_MODEL),
                         lambda t, f, te, nv, st, s=s: (layer, te[t], MOE_W_SPLIT * f_eff(t, f, nv) + s, 0))
            for s in range(MOE_W_SPLIT)
        ],
        out_specs=pl.BlockSpec((TM_MOE * ROW_CHUNKS, LANES), lambda t, f, te, nv, st: (t, 0)),
        scratch_shapes=[pltpu.VMEM((2, buf_rows * ROW_CHUNKS, LANES), F32), pltpu.VMEM((TM_MOE, D_MODEL), BF16),
                        pltpu.VMEM((TM_MOE, D_MODEL), F32), pltpu.SemaphoreType.DMA((2,))],
    )
    return pl.pallas_call(
        _moe_kernel,
        grid_spec=grid_spec,
        out_shape=jax.ShapeDtypeStruct((MOE_TILES * TM_MOE * ROW_CHUNKS, LANES), F32),
        compiler_params=_cparams("arbitrary", "arbitrary"),
        name="moe_experts",
    )(tile_expert, n_valid, src_tok, h, *([wg] * MOE_W_SPLIT + [wu] * MOE_W_SPLIT + [wd] * MOE_W_SPLIT))


def _moe_combine_kernel(slot_ref, x_ref, route_ref, ga_ref, y_hbm, o_ref, rows_scr, sems):
    i = pl.program_id(0)

    def issue_tile(tile, buf):
        for c in range(2):
            base = c * N_TOK + tile * TM

            def issue(j, carry, base=base, c=c):
                s = slot_ref[base + j]
                pltpu.make_async_copy(
                    y_hbm.at[pl.ds(pl.multiple_of(s * ROW_CHUNKS, ROW_CHUNKS), ROW_CHUNKS), :],
                    rows_scr.at[buf, c, pl.ds(pl.multiple_of(j * ROW_CHUNKS, ROW_CHUNKS), ROW_CHUNKS), :],
                    sems.at[buf]).start()
                return carry

            lax.fori_loop(0, TM, issue, 0, unroll=8)

    @pl.when(i == 0)
    def _():
        issue_tile(0, 0)

    @pl.when(i + 1 < pl.num_programs(0))
    def _():
        issue_tile(i + 1, (i + 1) % 2)

    buf = i % 2
    for c in range(2):
        pltpu.make_async_copy(y_hbm.at[pl.ds(0, TM * ROW_CHUNKS), :], rows_scr.at[buf, c], sems.at[buf]).wait()
    route = route_ref[...]
    g1 = route[:, ROUTE_G1:ROUTE_G1 + 1]
    g2 = route[:, ROUTE_G2:ROUTE_G2 + 1]
    for ch in range(ROW_CHUNKS):
        lanes = slice(ch * LANES, (ch + 1) * LANES)
        mix = (g1 * rows_scr[buf, 0, pl.ds(ch, TM, stride=ROW_CHUNKS), :]
               + g2 * rows_scr[buf, 1, pl.ds(ch, TM, stride=ROW_CHUNKS), :])
        o_ref[:, lanes] = x_ref[:, lanes] + ga_ref[0][:, lanes] * mix


def _moe_combine(slot, x, route, mod, y):
    grid_spec = pltpu.PrefetchScalarGridSpec(
        num_scalar_prefetch=1,
        grid=(N_TILES,),
        in_specs=[
            pl.BlockSpec((TM, D_MODEL), lambda i, s: (i, 0)),
            pl.BlockSpec((TM, LANES), lambda i, s: (i, 0)),
            _mod_spec(5),
            pl.BlockSpec(memory_space=pl.ANY),
        ],
        out_specs=pl.BlockSpec((TM, D_MODEL), lambda i, s: (i, 0)),
        scratch_shapes=[pltpu.VMEM((2, 2, TM * ROW_CHUNKS, LANES), F32), pltpu.SemaphoreType.DMA((2,))],
    )
    return pl.pallas_call(
        _moe_combine_kernel,
        grid_spec=grid_spec,
        out_shape=jax.ShapeDtypeStruct((N_TOK, D_MODEL), F32),
        compiler_params=_cparams("arbitrary"),
        name="moe_combine",
    )(slot, x, route, mod, y)


def _final_norm_kernel(x_ref, g_ref, y_ref):
    x = x_ref[...]
    y_ref[...] = (x * lax.rsqrt(jnp.mean(x * x, axis=-1, keepdims=True) + EPS)) * g_ref[...]


def _final_norm(x, g, tile0, n_tok):
    return pl.pallas_call(
        _final_norm_kernel,
        grid=(n_tok // TM,),
        in_specs=[pl.BlockSpec((TM, D_MODEL), lambda i: (tile0 + i, 0)),
                  pl.BlockSpec((1, D_MODEL), lambda i: (0, 0))],
        out_specs=pl.BlockSpec((TM, D_MODEL), lambda i: (i, 0)),
        out_shape=jax.ShapeDtypeStruct((n_tok, D_MODEL), F32),
        compiler_params=_cparams("parallel"),
        name="final_norm",
    )(x, g)


def _rope_tables():
    t = jnp.arange(DEC_SEQ)
    row = (t // GRID_W).astype(F32)
    col = (t % GRID_W).astype(F32)
    half = HEAD_DIM // 2
    inv = 1.0 / (ROPE_THETA ** (jnp.arange(0, half, 2, dtype=F32) / half))
    ar = row[:, None] * inv[None]
    ac = col[:, None] * inv[None]
    ang = jnp.concatenate([ar, ar, ac, ac], axis=-1)
    ang = jnp.concatenate([ang, jnp.zeros((TM, HEAD_DIM), F32)], axis=0)
    cos = jnp.tile(jnp.cos(ang), (1, LANES // HEAD_DIM))
    sin = jnp.tile(jnp.sin(ang), (1, LANES // HEAD_DIM))
    first_half = (jnp.arange(LANES) % 32) < 16
    sin_a = jnp.where(first_half[None, :], -sin, 0.0)
    sin_b = jnp.where(first_half[None, :], 0.0, sin)
    return cos, sin_a, sin_b


def kernel(x_prompt, x_sample, c, cache_na_k, cache_na_v, cache_diff_k, cache_diff_v, c_ctx, w_ada, b_ada, g_mix, w_in, rpb, lam_q1, lam_k1, lam_q2, lam_k2, g_subln, w_out, g_ffn, w_ffn_gate, w_ffn_up, w_ffn_down, w_router, w_moe_gate, w_moe_up, w_moe_down, g_final):
    x = jnp.concatenate([x_prompt.reshape(N_CTX_TOK, D_MODEL), x_sample.reshape(N_LAT_TOK, D_MODEL)])

    cvec = jnp.zeros((MOD_ROWS, D_MODEL), F32).at[0].set(c_ctx).at[1:1 + DEC_BATCH].set(c)
    mod_all = _modulation(cvec, w_ada, b_ada).reshape(DEPTH, MOD_ROWS * 6, 1, D_MODEL)

    col = jnp.arange(IN_WIDTH)
    is_q = (col < COL_NA_K) | ((col >= COL_D_Q) & (col < COL_D_K))
    q_scale = jnp.where(is_q, ATTN_SCALE * LOG2E, 1.0).astype(F32)
    w_in_b = (w_in * q_scale[None, None, :]).astype(BF16)
    w_out_b = w_out.astype(BF16)
    w_fg, w_fu, w_fd = (w.astype(BF16) for w in (w_ffn_gate, w_ffn_up, w_ffn_down))
    w_router_pad = jnp.pad(w_router, ((0, 0), (0, 0), (0, LANES - N_EXPERTS)))

    rope_tabs = _rope_tables()
    bias_tabs = _na_bias_tables(rpb)
    cna_k = cache_na_k.reshape(DEC_BATCH, DEPTH, PAST_LEN, NA_WIDTH)
    cna_v = cache_na_v.reshape(DEC_BATCH, DEPTH, PAST_LEN, NA_WIDTH)
    cd_k = cache_diff_k.reshape(DEC_BATCH, DEPTH, PAST_LEN, DIFF_WIDTH)
    cd_v = cache_diff_v.reshape(DEC_BATCH, DEPTH, PAST_LEN, DIFF_WIDTH)

    kv_layers = []
    for l in range(DEPTH):
        lam_init = 0.8 - 0.6 * math.exp(-0.3 * l)
        mod = mod_all[l]
        g_mix_l = g_mix[l][None, :]
        g_ffn_l = g_ffn[l][None, :]
        g_sub_l = g_subln[l][None, :]
        lam_params = tuple(p[l][None, :] for p in (lam_q1, lam_k1, lam_q2, lam_k2))

        p_all, *kv_ctx = _inproj(x, g_mix_l, mod, w_in_b[l], rope_tabs)
        kv_layers.append(kv_ctx)
        o_ctx = _ctx_attention(p_all, lam_params, g_sub_l, lam_init)
        o_na = _na_attention(p_all, cna_k, cna_v, bias_tabs, l)
        o_d = _diff_attention(p_all, cd_k, cd_v, lam_params, g_sub_l, lam_init, l)

        i = l // 2
        if l % 2 == 0:
            x = _ffn_dense(x, o_ctx, o_na, o_d, w_out_b[l], g_ffn_l, mod, w_fg[i], w_fu[i], w_fd[i])
        else:
            x, h, route, counts = _router(x, o_ctx, o_na, o_d, w_out_b[l], g_ffn_l, mod, w_router_pad[i])
            tile_expert, n_valid, src_tok, slot = _moe_plan(route, counts)
            y = _moe_experts(tile_expert, n_valid, src_tok, h, w_moe_gate, w_moe_up, w_moe_down, i)
            x = _moe_combine(slot, x, route, mod, y)

    g_fin = g_final[None, :]
    y_prompt = _final_norm(x, g_fin, 0, N_CTX_TOK).reshape(BATCH, SEQ, D_MODEL)
    y_sample = _final_norm(x, g_fin, CTX_TILES, N_LAT_TOK).reshape(DEC_BATCH, DEC_SEQ, D_MODEL)

    kv = [jnp.stack([layer_kv[k] for layer_kv in kv_layers], axis=1) for k in range(4)]
    new_na_k = kv[0].reshape(BATCH, DEPTH, SEQ, NA_HEADS, HEAD_DIM)
    new_na_v = kv[1].reshape(BATCH, DEPTH, SEQ, NA_HEADS, HEAD_DIM)
    new_diff_k = kv[2].reshape(BATCH, DEPTH, SEQ, DIFF_HEADS, 2, HEAD_DIM)
    new_diff_v = kv[3].reshape(BATCH, DEPTH, SEQ, DIFF_HEADS, 2 * HEAD_DIM)
    return (y_prompt, y_sample, new_na_k, new_na_v, new_diff_k, new_diff_v)
```

```python
import functools
import math

import numpy as np
import jax
import jax.numpy as jnp
from jax import lax
from jax.experimental import pallas as pl
from jax.experimental.pallas import tpu as pltpu

F32 = jnp.float32
BF16 = jnp.bfloat16
I32 = jnp.int32

D_MODEL = 1024
DEPTH = 4
BATCH = 16
SEQ = 256
DEC_BATCH = 8
DEC_SEQ = 2048
PAST_LEN = 256
GRID_W = 64
GRID_ROWS = DEC_SEQ // GRID_W
HEAD_DIM = 64
NA_HEADS = 8
NA_WIDTH = 512
DIFF_HEADS = 4
DIFF_WIDTH = 512
IN_WIDTH = 3072
NA_WIN_H = 8
NA_WIN_W = 16
ROPE_THETA = 10000.0
D_FF = 2816
N_EXPERTS = 8
D_FF_EXPERT = 3584
EPS = 1e-6
SUBLN_EPS = 1e-5
ATTN_SCALE = HEAD_DIM ** -0.5

LANES = 128
ROW_CHUNKS = D_MODEL // LANES
N_CTX_TOK = BATCH * SEQ
N_LAT_TOK = DEC_BATCH * DEC_SEQ
N_TOK = N_CTX_TOK + N_LAT_TOK
MOD_ROWS = 16
MASK_VALUE = -1e30

COL_NA_Q, COL_NA_K, COL_NA_V = 0, 512, 1024
COL_D_Q, COL_D_K, COL_D_V = 1536, 2048, 2560

TM = 512
N_TILES = N_TOK // TM
CTX_TILES = N_CTX_TOK // TM
TILES_PER_SEQ = DEC_SEQ // TM
FF_TILE_DENSE = 1408
FF_TILE_MOE = 512
N_FF_MOE = D_FF_EXPERT // FF_TILE_MOE
TM_MOE = 1024
MOE_W_SPLIT = 2
TQ_DIFF = 2048
DIFF_SUB = 128
NA_R = 4
NA_WIN_ROWS = 12
NA_STEPS = GRID_ROWS // NA_R
NA_BLOCKS = 2
MOE_SLOTS = 2 * N_TOK
MOE_TILES = MOE_SLOTS // TM_MOE + N_EXPERTS
GATHER_CHUNK = -(-TM_MOE // N_FF_MOE)
GATHER_ROWS = GATHER_CHUNK * N_FF_MOE
N_SRC_SLOTS = (MOE_TILES + 2) * TM_MOE
LOG2E = 1.4426950408889634
VMEM_LIMIT = 56 * 1024 * 1024


def _cparams(*sem):
    return pltpu.CompilerParams(dimension_semantics=sem, vmem_limit_bytes=VMEM_LIMIT)


def _dot(a, b):
    return jnp.dot(a, b, preferred_element_type=F32)


def _dot_nt(a, b):
    return lax.dot_general(a, b, (((1,), (1,)), ((), ())), preferred_element_type=F32)


def _modulated_norm(x, g, shift, scale):
    xn = x * lax.rsqrt(jnp.mean(x * x, axis=-1, keepdims=True) + EPS)
    return (xn * g) * (1.0 + scale) + shift


def _mod_row(i):
    return jnp.maximum(i // TILES_PER_SEQ - CTX_TILES // TILES_PER_SEQ + 1, 0)


def _mod_spec(chunk):
    return pl.BlockSpec((1, 1, D_MODEL), lambda i, *_: (_mod_row(i) * 6 + chunk, 0, 0))


def _mod_kernel(c_ref, w_ref, b_ref, o_ref):
    cv = c_ref[...]
    s = cv * jax.nn.sigmoid(cv)
    o_ref[...] = jnp.dot(s, w_ref[...], preferred_element_type=F32,
                         precision=lax.Precision.HIGHEST) + b_ref[...]


def _modulation(cvec, w_ada, b_ada):
    tn = 1536
    n = 6 * D_MODEL
    return pl.pallas_call(
        _mod_kernel,
        grid=(DEPTH, n // tn),
        in_specs=[
            pl.BlockSpec((MOD_ROWS, D_MODEL), lambda l, j: (0, 0)),
            pl.BlockSpec((None, D_MODEL, tn), lambda l, j: (l, 0, j)),
            pl.BlockSpec((None, 1, tn), lambda l, j: (l, 0, j)),
        ],
        out_specs=pl.BlockSpec((None, MOD_ROWS, tn), lambda l, j: (l, 0, j)),
        out_shape=jax.ShapeDtypeStruct((DEPTH, MOD_ROWS, n), F32),
        compiler_params=_cparams("parallel", "parallel"),
        name="adaln_modulation",
    )(cvec, w_ada, b_ada.reshape(DEPTH, 1, n))


def _inproj_kernel(x_ref, g_ref, sh_ref, sc_ref, w_ref, cos_ref, sina_ref, sinb_ref, o_ref, *kv_refs):
    i = pl.program_id(0)
    h = _modulated_norm(x_ref[...], g_ref[...], sh_ref[0], sc_ref[0]).astype(BF16)
    chunk = 512
    for c in range(IN_WIDTH // chunk):
        col = c * chunk
        acc = _dot(h, w_ref[:, col:col + chunk])
        for k_i, src in enumerate((COL_NA_K, COL_NA_V, COL_D_K, COL_D_V)):
            if src == col:
                @pl.when(i < CTX_TILES)
                def _(acc=acc, k_i=k_i):
                    for b in range(TM // SEQ):
                        kv_refs[k_i][b] = acc[b * SEQ:(b + 1) * SEQ, :]
        if COL_D_Q <= col < COL_D_V:
            parts = []
            for j in range(chunk // LANES):
                blk = acc[:, j * LANES:(j + 1) * LANES]
                parts.append(blk * cos_ref[...]
                             + pltpu.roll(blk, LANES - 16, 1) * sina_ref[...]
                             + pltpu.roll(blk, 16, 1) * sinb_ref[...])
            acc = jnp.concatenate(parts, axis=1)
        o_ref[:, col:col + chunk] = acc.astype(o_ref.dtype)


def _inproj(x, g, mod, w_bf16, rope_tabs):
    rope_spec = pl.BlockSpec(
        (TM, LANES),
        lambda i: (jnp.where(i < CTX_TILES, TILES_PER_SEQ, (i - CTX_TILES) % TILES_PER_SEQ), 0))
    return pl.pallas_call(
        _inproj_kernel,
        grid=(N_TILES,),
        in_specs=[
            pl.BlockSpec((TM, D_MODEL), lambda i: (i, 0)),
            pl.BlockSpec((1, D_MODEL), lambda i: (0, 0)),
            _mod_spec(0), _mod_spec(1),
            pl.BlockSpec((D_MODEL, IN_WIDTH), lambda i: (0, 0)),
            rope_spec, rope_spec, rope_spec,
        ],
        out_specs=[pl.BlockSpec((TM, IN_WIDTH), lambda i: (i, 0))]
        + [pl.BlockSpec((TM // SEQ, SEQ, 512), lambda i: (jnp.minimum(i, CTX_TILES - 1), 0, 0))] * 4,
        out_shape=[jax.ShapeDtypeStruct((N_TOK, IN_WIDTH), BF16)]
        + [jax.ShapeDtypeStruct((BATCH, SEQ, 512), F32)] * 4,
        compiler_params=_cparams("arbitrary"),
        name="inproj",
    )(x, g, mod, mod, w_bf16, *rope_tabs)


def _lane_half_mask(shape, half):
    lane = lax.broadcasted_iota(I32, shape, len(shape) - 1)
    return (lane < HEAD_DIM) if half == 0 else (lane >= HEAD_DIM)


def _lambda_value(lq1, lk1, lq2, lk2, lam_init):
    a = jnp.sum(lq1 * lk1, axis=-1, keepdims=True)
    b = jnp.sum(lq2 * lk2, axis=-1, keepdims=True)
    return jnp.exp(a) - jnp.exp(b) + lam_init


def _subln(o, g, lam_init):
    on = o * lax.rsqrt(jnp.mean(o * o, axis=-1, keepdims=True) + SUBLN_EPS)
    return (on * g) * (1.0 - lam_init)


def _ctx_attn_kernel(p_ref, lq1, lk1, lq2, lk2, gs_ref, o_ref, *, lam_init):
    lam = _lambda_value(lq1[...], lk1[...], lq2[...], lk2[...], lam_init)
    for hp in range(NA_HEADS // 2):
        q = p_ref[:, COL_NA_Q + hp * LANES:COL_NA_Q + (hp + 1) * LANES]
        k = p_ref[:, COL_NA_K + hp * LANES:COL_NA_K + (hp + 1) * LANES]
        v = p_ref[:, COL_NA_V + hp * LANES:COL_NA_V + (hp + 1) * LANES]
        outs = []
        for half in range(2):
            qm = jnp.where(_lane_half_mask(q.shape, half), q, jnp.zeros_like(q))
            s = _dot_nt(qm, k)
            m = jnp.max(s, axis=-1, keepdims=True)
            e = jnp.exp2(s - m)
            inv = 1.0 / jnp.sum(e, axis=-1, keepdims=True)
            outs.append(_dot(e.astype(BF16), v) * inv)
        o = jnp.where(_lane_half_mask(outs[0].shape, 0), outs[0], outs[1])
        o_ref[:, hp * LANES:(hp + 1) * LANES] = o.astype(o_ref.dtype)
    for h in range(DIFF_HEADS):
        q = p_ref[:, COL_D_Q + h * LANES:COL_D_Q + (h + 1) * LANES]
        k = p_ref[:, COL_D_K + h * LANES:COL_D_K + (h + 1) * LANES]
        v = p_ref[:, COL_D_V + h * LANES:COL_D_V + (h + 1) * LANES]
        ps = []
        for half in range(2):
            qm = jnp.where(_lane_half_mask(q.shape, half), q, jnp.zeros_like(q))
            s = _dot_nt(qm, k)
            m = jnp.max(s, axis=-1, keepdims=True)
            e = jnp.exp2(s - m)
            ps.append(e / jnp.sum(e, axis=-1, keepdims=True))
        a = (ps[0] - lam * ps[1]).astype(BF16)
        o = _subln(_dot(a, v), gs_ref[...], lam_init)
        o_ref[:, NA_WIDTH + h * LANES:NA_WIDTH + (h + 1) * LANES] = o.astype(o_ref.dtype)


def _ctx_attention(p_all, lam_params, g_subln, lam_init):
    vec = pl.BlockSpec((1, HEAD_DIM), lambda b: (0, 0))
    return pl.pallas_call(
        functools.partial(_ctx_attn_kernel, lam_init=lam_init),
        grid=(BATCH,),
        in_specs=[pl.BlockSpec((SEQ, IN_WIDTH), lambda b: (b, 0)), vec, vec, vec, vec,
                  pl.BlockSpec((1, 2 * HEAD_DIM), lambda b: (0, 0))],
        out_specs=pl.BlockSpec((SEQ, D_MODEL), lambda b: (b, 0)),
        out_shape=jax.ShapeDtypeStruct((N_CTX_TOK, D_MODEL), BF16),
        compiler_params=_cparams("parallel"),
        name="ctx_attention",
    )(p_all, *lam_params, g_subln)


def _bias_table_kernel(rpb_ref, o_ref):
    dr_plan, ok_plan = _na_window_plan()
    shape = (GRID_W, LANES)
    qc = lax.broadcasted_iota(I32, shape, 0)
    lane = lax.broadcasted_iota(I32, shape, 1)
    kc = jnp.bitwise_and(lane, GRID_W - 1)
    cs = jnp.clip(qc - NA_WIN_W // 2, 0, GRID_W - NA_WIN_W)
    in_window = (kc >= cs) & (kc < cs + NA_WIN_W)
    low_half = lane < GRID_W
    masked = jnp.full(shape, MASK_VALUE, F32)

    pieces = {}

    def piece(dr, parity):
        if (dr, parity) not in pieces:
            base = jnp.broadcast_to(rpb_ref[dr:dr + 1, :], shape) * LOG2E
            shifted = pltpu.roll(base, 0, 1, stride=1, stride_axis=0)
            pieces[(dr, parity)] = pltpu.roll(shifted, (LANES - (NA_WIN_W - 1) + GRID_W * parity) % LANES, 1)
        return pieces[(dr, parity)]

    for t in range(3):
        for jr in range(NA_R):
            for p in range(NA_WIN_ROWS // 2):
                halves = []
                for parity in range(2):
                    i = 2 * p + parity
                    halves.append(piece(int(dr_plan[t, jr, i]), parity) if ok_plan[t, jr, i] else masked)
                blk = jnp.where(in_window, jnp.where(low_half, halves[0], halves[1]), MASK_VALUE)
                o_ref[t, jr * GRID_W:(jr + 1) * GRID_W, p * LANES:(p + 1) * LANES] = blk.astype(o_ref.dtype)


def _na_window_plan():
    dr = np.zeros((3, NA_R, NA_WIN_ROWS), np.int32)
    ok = np.zeros((3, NA_R, NA_WIN_ROWS), bool)
    for t, r0 in enumerate((0, NA_R, GRID_ROWS - NA_R)):
        lo = min(max(r0 - NA_WIN_H // 2, 0), GRID_ROWS - NA_WIN_ROWS)
        for jr in range(NA_R):
            r = r0 + jr
            rs = min(max(r - NA_WIN_H // 2, 0), GRID_ROWS - NA_WIN_H)
            for i in range(NA_WIN_ROWS):
                key_row = lo + i
                if rs <= key_row < rs + NA_WIN_H:
                    ok[t, jr, i] = True
                    dr[t, jr, i] = key_row - r + NA_WIN_H - 1
    return dr, ok


def _na_bias_tables(rpb):
    n_dr = 2 * NA_WIN_H - 1
    rpb_pad = jnp.pad(rpb.reshape(DEPTH * NA_HEADS, n_dr, 2 * NA_WIN_W - 1),
                      ((0, 0), (0, 16 - n_dr), (0, LANES - (2 * NA_WIN_W - 1))))
    return pl.pallas_call(
        _bias_table_kernel,
        grid=(DEPTH * NA_HEADS,),
        in_specs=[pl.BlockSpec((None, 16, LANES), lambda g: (g, 0, 0))],
        out_specs=pl.BlockSpec((None, 3, None, NA_R * GRID_W, NA_WIN_ROWS * GRID_W),
                               lambda g: (g // NA_HEADS, 0, g % NA_HEADS, 0, 0)),
        out_shape=jax.ShapeDtypeStruct((DEPTH, 3, NA_HEADS, NA_R * GRID_W, NA_WIN_ROWS * GRID_W), BF16),
        compiler_params=_cparams("parallel"),
        name="na_bias_table",
    )(rpb_pad)


def _na_attn_kernel(q_ref, k_ref, v_ref, kc_ref, vc_ref, bias_ref, o_ref):
    q_rows = NA_R * GRID_W
    n_keys = NA_WIN_ROWS * GRID_W
    plans = []
    for blk in range(NA_BLOCKS):
        j = pl.program_id(1) * NA_BLOCKS + blk
        lo = jnp.clip(j * NA_R - NA_WIN_H // 2, 0, GRID_ROWS - NA_WIN_ROWS)
        k0 = pl.multiple_of(lo * GRID_W, GRID_W)
        step_type = jnp.where(j == 0, 0, jnp.where(j == NA_STEPS - 1, 2, 1))
        plans.append((k0, step_type))
    units = [(blk, head) for blk in range(NA_BLOCKS) for head in range(NA_HEADS)]

    def scores(unit):
        blk, head = unit
        k0, step_type = plans[blk]
        lanes = slice((head // 2) * LANES, (head // 2 + 1) * LANES)
        q = q_ref[blk * q_rows:(blk + 1) * q_rows, lanes]
        qm = jnp.where(_lane_half_mask(q.shape, head % 2), q, jnp.zeros_like(q))
        s_w = _dot_nt(qm, k_ref[pl.ds(k0, n_keys), lanes]) + bias_ref[step_type, head].astype(F32)
        s_c = _dot_nt(qm, kc_ref[:, lanes].astype(BF16))
        return s_w, s_c

    nxt = scores(units[0])
    outs = []
    for n, (blk, head) in enumerate(units):
        s_w, s_c = nxt
        if n + 1 < len(units):
            nxt = scores(units[n + 1])
        k0, _ = plans[blk]
        lanes = slice((head // 2) * LANES, (head // 2 + 1) * LANES)
        m = jnp.maximum(jnp.max(s_w, axis=-1, keepdims=True), jnp.max(s_c, axis=-1, keepdims=True))
        e_w = jnp.exp2(s_w - m)
        e_c = jnp.exp2(s_c - m)
        inv = 1.0 / (jnp.sum(e_w, axis=-1, keepdims=True) + jnp.sum(e_c, axis=-1, keepdims=True))
        pv = (_dot(e_w.astype(BF16), v_ref[pl.ds(k0, n_keys), lanes])
              + _dot(e_c.astype(BF16), vc_ref[:, lanes].astype(BF16)))
        outs.append(pv * inv)
        if head % 2 == 1:
            o = jnp.where(_lane_half_mask(outs[0].shape, 0), outs[0], outs[1])
            o_ref[blk * q_rows:(blk + 1) * q_rows, lanes] = o.astype(o_ref.dtype)
            outs = []


def _na_attention(p_all, cache_k, cache_v, bias_tab, layer):
    q_rows = NA_BLOCKS * NA_R * GRID_W
    steps = NA_STEPS // NA_BLOCKS
    q_blk0 = N_CTX_TOK // q_rows
    seq_blk0 = N_CTX_TOK // DEC_SEQ
    cache_spec = pl.BlockSpec((None, None, PAST_LEN, NA_WIDTH), lambda b, j: (b, layer, 0, 0))
    return pl.pallas_call(
        _na_attn_kernel,
        grid=(DEC_BATCH, steps),
        in_specs=[
            pl.BlockSpec((q_rows, NA_WIDTH), lambda b, j: (q_blk0 + b * steps + j, COL_NA_Q // NA_WIDTH)),
            pl.BlockSpec((DEC_SEQ, NA_WIDTH), lambda b, j: (seq_blk0 + b, COL_NA_K // NA_WIDTH)),
            pl.BlockSpec((DEC_SEQ, NA_WIDTH), lambda b, j: (seq_blk0 + b, COL_NA_V // NA_WIDTH)),
            cache_spec, cache_spec,
            pl.BlockSpec((None, 3, NA_HEADS, NA_R * GRID_W, NA_WIN_ROWS * GRID_W), lambda b, j: (layer, 0, 0, 0, 0)),
        ],
        out_specs=pl.BlockSpec((q_rows, NA_WIDTH), lambda b, j: (b * steps + j, 0)),
        out_shape=jax.ShapeDtypeStruct((N_LAT_TOK, NA_WIDTH), BF16),
        compiler_params=_cparams("parallel", "arbitrary"),
        name="latent_na_attention",
    )(p_all, p_all, p_all, cache_k, cache_v, bias_tab)


def _diff_attn_kernel(q_ref, k_ref, v_ref, kc_ref, vc_ref, lq1, lk1, lq2, lk2, gs_ref, o_ref,
                      *, lam_init):
    lam = _lambda_value(lq1[...], lk1[...], lq2[...], lk2[...], lam_init)
    k = k_ref[...]
    v = v_ref[...]
    kc = kc_ref[...].astype(BF16)
    vc = vc_ref[...].astype(BF16)
    n_sub = TQ_DIFF // DIFF_SUB

    def scores(i):
        q = q_ref[i * DIFF_SUB:(i + 1) * DIFF_SUB, :]
        out = []
        for half in range(2):
            qm = jnp.where(_lane_half_mask(q.shape, half), q, jnp.zeros_like(q))
            out.append((_dot_nt(qm, k), _dot_nt(qm, kc)))
        return out

    nxt = scores(0)
    for i in range(n_sub):
        cur = nxt
        if i + 1 < n_sub:
            nxt = scores(i + 1)
        probs = []
        for s_l, s_c in cur:
            m = jnp.maximum(jnp.max(s_l, axis=-1, keepdims=True), jnp.max(s_c, axis=-1, keepdims=True))
            e_l = jnp.exp2(s_l - m)
            e_c = jnp.exp2(s_c - m)
            inv = 1.0 / (jnp.sum(e_l, axis=-1, keepdims=True) + jnp.sum(e_c, axis=-1, keepdims=True))
            probs.append((e_l, e_c, inv))
        r = lam * probs[1][2] / probs[0][2]
        a_l = (probs[0][0] - probs[1][0] * r).astype(BF16)
        a_c = (probs[0][1] - probs[1][1] * r).astype(BF16)
        o = (_dot(a_l, v) + _dot(a_c, vc)) * probs[0][2]
        o_ref[i * DIFF_SUB:(i + 1) * DIFF_SUB, :] = _subln(o, gs_ref[...], lam_init).astype(o_ref.dtype)


def _diff_attention(p_all, cache_k, cache_v, lam_params, g_subln, lam_init, layer):
    steps = DEC_SEQ // TQ_DIFF
    q_blk0 = N_CTX_TOK // TQ_DIFF
    seq_blk0 = N_CTX_TOK // DEC_SEQ
    vec = pl.BlockSpec((1, HEAD_DIM), lambda b, h, j: (0, 0))
    cache_spec = pl.BlockSpec((None, None, PAST_LEN, LANES), lambda b, h, j: (b, layer, 0, h))
    return pl.pallas_call(
        functools.partial(_diff_attn_kernel, lam_init=lam_init),
        grid=(DEC_BATCH, DIFF_HEADS, steps),
        in_specs=[
            pl.BlockSpec((TQ_DIFF, LANES), lambda b, h, j: (q_blk0 + b * steps + j, COL_D_Q // LANES + h)),
            pl.BlockSpec((DEC_SEQ, LANES), lambda b, h, j: (seq_blk0 + b, COL_D_K // LANES + h)),
            pl.BlockSpec((DEC_SEQ, LANES), lambda b, h, j: (seq_blk0 + b, COL_D_V // LANES + h)),
            cache_spec, cache_spec, vec, vec, vec, vec,
            pl.BlockSpec((1, 2 * HEAD_DIM), lambda b, h, j: (0, 0)),
        ],
        out_specs=pl.BlockSpec((TQ_DIFF, LANES), lambda b, h, j: (b * steps + j, h)),
        out_shape=jax.ShapeDtypeStruct((N_LAT_TOK, DIFF_WIDTH), BF16),
        compiler_params=_cparams("parallel", "parallel", "arbitrary"),
        name="latent_diff_attention",
    )(p_all, p_all, p_all, cache_k, cache_v, *lam_params, g_subln)


def _attn_residual(i, x_ref, oc_ref, ona_ref, od_ref, ga_ref, w_ref, dst_ref):
    @pl.when(i < CTX_TILES)
    def _():
        dst_ref[...] = x_ref[...] + ga_ref[0] * _dot(oc_ref[...], w_ref[...])

    @pl.when(i >= CTX_TILES)
    def _():
        acc = _dot(ona_ref[...], w_ref[:NA_WIDTH, :]) + _dot(od_ref[...], w_ref[NA_WIDTH:, :])
        dst_ref[...] = x_ref[...] + ga_ref[0] * acc


def _attn_residual_specs():
    lat_idx = lambda i, *_: (jnp.maximum(i - CTX_TILES, 0), 0)
    return [
        pl.BlockSpec((TM, D_MODEL), lambda i, *_: (jnp.minimum(i, CTX_TILES - 1), 0)),
        pl.BlockSpec((TM, NA_WIDTH), lat_idx),
        pl.BlockSpec((TM, DIFF_WIDTH), lat_idx),
        _mod_spec(2),
        pl.BlockSpec((D_MODEL, D_MODEL), lambda i, *_: (0, 0)),
    ]


def _swiglu_tile(hb, wg, wu):
    g = _dot(hb, wg)
    u = _dot(hb, wu)
    return (g * jax.nn.sigmoid(g)) * u


def _ffn_kernel(x_ref, oc_ref, ona_ref, od_ref, ga1_ref, wo_ref, g_ref, sh_ref, sc_ref, ga_ref,
                wg_ref, wu_ref, wd_ref, y_ref, xmid_scr, h_scr, acc_scr):
    i = pl.program_id(0)
    f = pl.program_id(1)

    @pl.when(f == 0)
    def _():
        _attn_residual(i, x_ref, oc_ref, ona_ref, od_ref, ga1_ref, wo_ref, xmid_scr)
        h = _modulated_norm(xmid_scr[...], g_ref[...], sh_ref[0], sc_ref[0])
        h_scr[...] = h.astype(BF16)
        acc_scr[...] = jnp.zeros_like(acc_scr)

    a = _swiglu_tile(h_scr[...], wg_ref[...], wu_ref[...])
    acc_scr[...] += _dot(a.astype(BF16), wd_ref[...])

    @pl.when(f == pl.num_programs(1) - 1)
    def _():
        y_ref[...] = xmid_scr[...] + ga_ref[0] * acc_scr[...]


def _ffn_dense(x, o_ctx, o_na, o_d, w_out, g, mod, wg, wu, wd):
    tf = FF_TILE_DENSE
    return pl.pallas_call(
        _ffn_kernel,
        grid=(N_TILES, D_FF // tf),
        in_specs=[pl.BlockSpec((TM, D_MODEL), lambda i, f: (i, 0))] + _attn_residual_specs() + [
            pl.BlockSpec((1, D_MODEL), lambda i, f: (0, 0)),
            _mod_spec(3), _mod_spec(4), _mod_spec(5),
            pl.BlockSpec((D_MODEL, tf), lambda i, f: (0, f)),
            pl.BlockSpec((D_MODEL, tf), lambda i, f: (0, f)),
            pl.BlockSpec((tf, D_MODEL), lambda i, f: (f, 0)),
        ],
        out_specs=pl.BlockSpec((TM, D_MODEL), lambda i, f: (i, 0)),
        out_shape=jax.ShapeDtypeStruct((N_TOK, D_MODEL), F32),
        scratch_shapes=[pltpu.VMEM((TM, D_MODEL), F32), pltpu.VMEM((TM, D_MODEL), BF16),
                        pltpu.VMEM((TM, D_MODEL), F32)],
        compiler_params=_cparams("parallel", "arbitrary"),
        name="ffn_dense",
    )(x, o_ctx, o_na, o_d, mod, w_out, g, mod, mod, mod, wg, wu, wd)


ROUTE_E1, ROUTE_E2, ROUTE_G1, ROUTE_G2, ROUTE_R1, ROUTE_R2 = range(6)


def _router_kernel(x_ref, oc_ref, ona_ref, od_ref, ga1_ref, wo_ref, g_ref, sh_ref, sc_ref, wr_ref,
                   xmid_ref, h_ref, route_ref, cnt_ref, carry_scr):
    i = pl.program_id(0)

    @pl.when(i == 0)
    def _():
        carry_scr[...] = jnp.zeros_like(carry_scr)

    _attn_residual(i, x_ref, oc_ref, ona_ref, od_ref, ga1_ref, wo_ref, xmid_ref)
    h = _modulated_norm(xmid_ref[...], g_ref[...], sh_ref[0], sc_ref[0])
    for c in range(ROW_CHUNKS):
        h_ref[pl.ds(c, TM, stride=ROW_CHUNKS), :] = h[:, c * LANES:(c + 1) * LANES]
    logits = jnp.dot(h, wr_ref[...], preferred_element_type=F32, precision=lax.Precision.HIGHEST)
    lane = lax.broadcasted_iota(I32, logits.shape, 1)
    logits = jnp.where(lane < N_EXPERTS, logits, MASK_VALUE)
    m1 = jnp.max(logits, axis=-1, keepdims=True)
    i1 = jnp.min(jnp.where(logits == m1, lane, LANES), axis=-1, keepdims=True)
    rest = jnp.where(lane == i1, MASK_VALUE, logits)
    m2 = jnp.max(rest, axis=-1, keepdims=True)
    i2 = jnp.min(jnp.where(rest == m2, lane, LANES), axis=-1, keepdims=True)
    e2 = jnp.exp(m2 - m1)
    g1 = 1.0 / (1.0 + e2)
    g2 = e2 / (1.0 + e2)

    hit1 = lane == i1
    hit2 = lane == i2
    onehot = jnp.where(hit1 | hit2, 1.0, 0.0)
    row = lax.broadcasted_iota(I32, (TM, TM), 0)
    col = lax.broadcasted_iota(I32, (TM, TM), 1)
    lower = jnp.where(row > col, 1.0, 0.0).astype(BF16)
    before = _dot(lower, onehot.astype(BF16)) + carry_scr[...]
    r1 = jnp.sum(jnp.where(hit1, before, 0.0), axis=-1, keepdims=True)
    r2 = jnp.sum(jnp.where(hit2, before, 0.0), axis=-1, keepdims=True)
    carry_scr[...] += jnp.sum(onehot, axis=0, keepdims=True)

    out = jnp.zeros(logits.shape, F32)
    for slot, val in ((ROUTE_E1, i1.astype(F32)), (ROUTE_E2, i2.astype(F32)), (ROUTE_G1, g1),
                      (ROUTE_G2, g2), (ROUTE_R1, r1), (ROUTE_R2, r2)):
        out = jnp.where(lane == slot, val, out)
    route_ref[...] = out

    @pl.when(i == pl.num_programs(0) - 1)
    def _():
        cnt_ref[...] = jnp.broadcast_to(carry_scr[...], cnt_ref.shape)


def _router(x, o_ctx, o_na, o_d, w_out, g, mod, w_router_pad):
    return pl.pallas_call(
        _router_kernel,
        grid=(N_TILES,),
        in_specs=[pl.BlockSpec((TM, D_MODEL), lambda i: (i, 0))] + _attn_residual_specs() + [
            pl.BlockSpec((1, D_MODEL), lambda i: (0, 0)),
            _mod_spec(3), _mod_spec(4),
            pl.BlockSpec((D_MODEL, LANES), lambda i: (0, 0)),
        ],
        out_specs=[pl.BlockSpec((TM, D_MODEL), lambda i: (i, 0)),
                   pl.BlockSpec((TM * ROW_CHUNKS, LANES), lambda i: (i, 0)),
                   pl.BlockSpec((TM, LANES), lambda i: (i, 0)),
                   pl.BlockSpec((8, LANES), lambda i: (0, 0))],
        out_shape=[jax.ShapeDtypeStruct((N_TOK, D_MODEL), F32),
                   jax.ShapeDtypeStruct((N_TOK * ROW_CHUNKS, LANES), F32),
                   jax.ShapeDtypeStruct((N_TOK, LANES), F32),
                   jax.ShapeDtypeStruct((8, LANES), F32)],
        scratch_shapes=[pltpu.VMEM((1, LANES), F32)],
        compiler_params=_cparams("arbitrary"),
        name="moe_router",
    )(x, o_ctx, o_na, o_d, mod, w_out, g, mod, mod, w_router_pad)


def _moe_plan(route, counts):
    cnt = counts[0, :N_EXPERTS].astype(I32)
    tiles = (cnt + TM_MOE - 1) // TM_MOE
    tile_end = jnp.cumsum(tiles)
    group_start = (tile_end - tiles) * TM_MOE
    e = route[:, ROUTE_E1:ROUTE_E2 + 1].astype(I32)
    r = route[:, ROUTE_R1:ROUTE_R2 + 1].astype(I32)
    slot = (group_start[e] + r).T.reshape(-1)
    tok = jnp.tile(jnp.arange(N_TOK, dtype=I32), 2)
    src_tok = jnp.zeros((N_SRC_SLOTS,), I32).at[slot].set(tok)
    n_valid = tile_end[-1:]
    t = jnp.arange(MOE_TILES, dtype=I32)
    tile_expert = jnp.minimum(jnp.sum((t[:, None] >= tile_end[None, :]).astype(I32), axis=1),
                              N_EXPERTS - 1)
    last_expert = tile_expert[jnp.maximum(n_valid[0] - 1, 0)]
    tile_expert = jnp.where(t < n_valid[0], tile_expert, last_expert)
    return tile_expert, n_valid, src_tok, slot


def _moe_kernel(te_ref, nv_ref, src_ref, h_hbm, *refs):
    wg_refs = refs[:MOE_W_SPLIT]
    wu_refs = refs[MOE_W_SPLIT:2 * MOE_W_SPLIT]
    wd_refs = refs[2 * MOE_W_SPLIT:3 * MOE_W_SPLIT]
    y_ref, hrow_scr, hb_scr, acc_scr, sems = refs[3 * MOE_W_SPLIT:]
    t = pl.program_id(0)
    f = pl.program_id(1)
    valid = t < nv_ref[0]
    last_t = pl.num_programs(0) - 1
    last_f = N_FF_MOE - 1

    def row_copy(tile, row, buf):
        tok = src_ref[tile * TM_MOE + row]
        return pltpu.make_async_copy(h_hbm.at[pl.ds(pl.multiple_of(tok * ROW_CHUNKS, ROW_CHUNKS), ROW_CHUNKS), :],
                                     hrow_scr.at[buf, pl.ds(pl.multiple_of(row * ROW_CHUNKS, ROW_CHUNKS), ROW_CHUNKS), :],
                                     sems.at[buf])

    def wait_tile(buf):
        pltpu.make_async_copy(h_hbm.at[pl.ds(0, GATHER_ROWS * ROW_CHUNKS), :],
                              hrow_scr.at[buf, pl.ds(0, GATHER_ROWS * ROW_CHUNKS), :], sems.at[buf]).wait()

    def issue_next_chunk():
        for j in range(GATHER_CHUNK):
            row_copy(t + 1, f * GATHER_CHUNK + j, (t + 1) % 2).start()

    @pl.when((t == 0) & (f == 0))
    def _():
        def issue(j, carry):
            row_copy(0, j, 0).start()
            return carry

        lax.fori_loop(0, GATHER_ROWS, issue, 0, unroll=7)

    @pl.when(f == 0)
    def _():
        wait_tile(t % 2)
        for c in range(ROW_CHUNKS):
            hb_scr[:, c * LANES:(c + 1) * LANES] = hrow_scr[t % 2, pl.ds(c, TM_MOE, stride=ROW_CHUNKS), :].astype(BF16)
        acc_scr[...] = jnp.zeros_like(acc_scr)

    @pl.when(valid)
    def _():
        issue_next_chunk()
        hb = hb_scr[...]
        acc = None
        for wg_ref, wu_ref, wd_ref in zip(wg_refs, wu_refs, wd_refs):
            a = _swiglu_tile(hb, wg_ref[...].astype(BF16), wu_ref[...].astype(BF16))
            d = _dot(a.astype(BF16), wd_ref[...].astype(BF16))
            acc = d if acc is None else acc + d
        acc_scr[...] += acc

    @pl.when(jnp.logical_not(valid))
    def _():
        issue_next_chunk()

    @pl.when(f == last_f)
    def _():
        for c in range(ROW_CHUNKS):
            y_ref[pl.ds(c, TM_MOE, stride=ROW_CHUNKS), :] = acc_scr[:, c * LANES:(c + 1) * LANES]

    @pl.when((t == last_t) & (f == last_f))
    def _():
        wait_tile((t + 1) % 2)


def _moe_experts(tile_expert, n_valid, src_tok, h, wg, wu, wd, layer):
    ts = FF_TILE_MOE // MOE_W_SPLIT
    buf_rows = -(-GATHER_ROWS // 8) * 8

    def f_eff(t, f, nv):
        return jnp.where(t < nv[0], f, N_FF_MOE - 1)

    grid_spec = pltpu.PrefetchScalarGridSpec(
        num_scalar_prefetch=3,
        grid=(MOE_TILES, N_FF_MOE),
        in_specs=[pl.BlockSpec(memory_space=pl.ANY)] + [
            pl.BlockSpec((None, None, D_MODEL, ts),
                         lambda t, f, te, nv, st, s=s: (layer, te[t], 0, MOE_W_SPLIT * f_eff(t, f, nv) + s))
            for _ in range(2) for s in range(MOE_W_SPLIT)
        ] + [
            pl.BlockSpec((None, None, ts, D_MODEL),
                         lambda t, f, te, nv, st, s=s: (layer, te[t], MOE_W_SPLIT * f_eff(t, f, nv) + s, 0))
            for s in range(MOE_W_SPLIT)
        ],
        out_specs=pl.BlockSpec((TM_MOE * ROW_CHUNKS, LANES), lambda t, f, te, nv, st: (t, 0)),
        scratch_shapes=[pltpu.VMEM((2, buf_rows * ROW_CHUNKS, LANES), F32), pltpu.VMEM((TM_MOE, D_MODEL), BF16),
                        pltpu.VMEM((TM_MOE, D_MODEL), F32), pltpu.SemaphoreType.DMA((2,))],
    )
    return pl.pallas_call(
        _moe_kernel,
        grid_spec=grid_spec,
        out_shape=jax.ShapeDtypeStruct((MOE_TILES * TM_MOE * ROW_CHUNKS, LANES), F32),
        compiler_params=_cparams("arbitrary", "arbitrary"),
        name="moe_experts",
    )(tile_expert, n_valid, src_tok, h, *([wg] * MOE_W_SPLIT + [wu] * MOE_W_SPLIT + [wd] * MOE_W_SPLIT))


def _moe_combine_kernel(slot_ref, x_ref, route_ref, ga_ref, y_hbm, o_ref, rows_scr, sems):
    i = pl.program_id(0)

    def issue_tile(tile, buf):
        for c in range(2):
            base = c * N_TOK + tile * TM

            def issue(j, carry, base=base, c=c):
                s = slot_ref[base + j]
                pltpu.make_async_copy(
                    y_hbm.at[pl.ds(pl.multiple_of(s * ROW_CHUNKS, ROW_CHUNKS), ROW_CHUNKS), :],
                    rows_scr.at[buf, c, pl.ds(pl.multiple_of(j * ROW_CHUNKS, ROW_CHUNKS), ROW_CHUNKS), :],
                    sems.at[buf]).start()
                return carry

            lax.fori_loop(0, TM, issue, 0, unroll=8)

    @pl.when(i == 0)
    def _():
        issue_tile(0, 0)

    @pl.when(i + 1 < pl.num_programs(0))
    def _():
        issue_tile(i + 1, (i + 1) % 2)

    buf = i % 2
    for c in range(2):
        pltpu.make_async_copy(y_hbm.at[pl.ds(0, TM * ROW_CHUNKS), :], rows_scr.at[buf, c], sems.at[buf]).wait()
    route = route_ref[...]
    g1 = route[:, ROUTE_G1:ROUTE_G1 + 1]
    g2 = route[:, ROUTE_G2:ROUTE_G2 + 1]
    for ch in range(ROW_CHUNKS):
        lanes = slice(ch * LANES, (ch + 1) * LANES)
        mix = (g1 * rows_scr[buf, 0, pl.ds(ch, TM, stride=ROW_CHUNKS), :]
               + g2 * rows_scr[buf, 1, pl.ds(ch, TM, stride=ROW_CHUNKS), :])
        o_ref[:, lanes] = x_ref[:, lanes] + ga_ref[0][:, lanes] * mix


def _moe_combine(slot, x, route, mod, y):
    grid_spec = pltpu.PrefetchScalarGridSpec(
        num_scalar_prefetch=1,
        grid=(N_TILES,),
        in_specs=[
            pl.BlockSpec((TM, D_MODEL), lambda i, s: (i, 0)),
            pl.BlockSpec((TM, LANES), lambda i, s: (i, 0)),
            _mod_spec(5),
            pl.BlockSpec(memory_space=pl.ANY),
        ],
        out_specs=pl.BlockSpec((TM, D_MODEL), lambda i, s: (i, 0)),
        scratch_shapes=[pltpu.VMEM((2, 2, TM * ROW_CHUNKS, LANES), F32), pltpu.SemaphoreType.DMA((2,))],
    )
    return pl.pallas_call(
        _moe_combine_kernel,
        grid_spec=grid_spec,
        out_shape=jax.ShapeDtypeStruct((N_TOK, D_MODEL), F32),
        compiler_params=_cparams("arbitrary"),
        name="moe_combine",
    )(slot, x, route, mod, y)


def _final_norm_kernel(x_ref, g_ref, y_ref):
    x = x_ref[...]
    y_ref[...] = (x * lax.rsqrt(jnp.mean(x * x, axis=-1, keepdims=True) + EPS)) * g_ref[...]


def _final_norm(x, g, tile0, n_tok):
    return pl.pallas_call(
        _final_norm_kernel,
        grid=(n_tok // TM,),
        in_specs=[pl.BlockSpec((TM, D_MODEL), lambda i: (tile0 + i, 0)),
                  pl.BlockSpec((1, D_MODEL), lambda i: (0, 0))],
        out_specs=pl.BlockSpec((TM, D_MODEL), lambda i: (i, 0)),
        out_shape=jax.ShapeDtypeStruct((n_tok, D_MODEL), F32),
        compiler_params=_cparams("parallel"),
        name="final_norm",
    )(x, g)


def _rope_tables():
    t = jnp.arange(DEC_SEQ)
    row = (t // GRID_W).astype(F32)
    col = (t % GRID_W).astype(F32)
    half = HEAD_DIM // 2
    inv = 1.0 / (ROPE_THETA ** (jnp.arange(0, half, 2, dtype=F32) / half))
    ar = row[:, None] * inv[None]
    ac = col[:, None] * inv[None]
    ang = jnp.concatenate([ar, ar, ac, ac], axis=-1)
    ang = jnp.concatenate([ang, jnp.zeros((TM, HEAD_DIM), F32)], axis=0)
    cos = jnp.tile(jnp.cos(ang), (1, LANES // HEAD_DIM))
    sin = jnp.tile(jnp.sin(ang), (1, LANES // HEAD_DIM))
    first_half = (jnp.arange(LANES) % 32) < 16
    sin_a = jnp.where(first_half[None, :], -sin, 0.0)
    sin_b = jnp.where(first_half[None, :], 0.0, sin)
    return cos, sin_a, sin_b


def kernel(x_prompt, x_sample, c, cache_na_k, cache_na_v, cache_diff_k, cache_diff_v, c_ctx, w_ada, b_ada, g_mix, w_in, rpb, lam_q1, lam_k1, lam_q2, lam_k2, g_subln, w_out, g_ffn, w_ffn_gate, w_ffn_up, w_ffn_down, w_router, w_moe_gate, w_moe_up, w_moe_down, g_final):
    x = jnp.concatenate([x_prompt.reshape(N_CTX_TOK, D_MODEL), x_sample.reshape(N_LAT_TOK, D_MODEL)])

    cvec = jnp.zeros((MOD_ROWS, D_MODEL), F32).at[0].set(c_ctx).at[1:1 + DEC_BATCH].set(c)
    mod_all = _modulation(cvec, w_ada, b_ada).reshape(DEPTH, MOD_ROWS * 6, 1, D_MODEL)

    col = jnp.arange(IN_WIDTH)
    is_q = (col < COL_NA_K) | ((col >= COL_D_Q) & (col < COL_D_K))
    q_scale = jnp.where(is_q, ATTN_SCALE * LOG2E, 1.0).astype(F32)
    w_in_b = (w_in * q_scale[None, None, :]).astype(BF16)
    w_out_b = w_out.astype(BF16)
    w_fg, w_fu, w_fd = (w.astype(BF16) for w in (w_ffn_gate, w_ffn_up, w_ffn_down))
    w_router_pad = jnp.pad(w_router, ((0, 0), (0, 0), (0, LANES - N_EXPERTS)))

    rope_tabs = _rope_tables()
    bias_tabs = _na_bias_tables(rpb)
    cna_k = cache_na_k.reshape(DEC_BATCH, DEPTH, PAST_LEN, NA_WIDTH)
    cna_v = cache_na_v.reshape(DEC_BATCH, DEPTH, PAST_LEN, NA_WIDTH)
    cd_k = cache_diff_k.reshape(DEC_BATCH, DEPTH, PAST_LEN, DIFF_WIDTH)
    cd_v = cache_diff_v.reshape(DEC_BATCH, DEPTH, PAST_LEN, DIFF_WIDTH)

    kv_layers = []
    for l in range(DEPTH):
        lam_init = 0.8 - 0.6 * math.exp(-0.3 * l)
        mod = mod_all[l]
        g_mix_l = g_mix[l][None, :]
        g_ffn_l = g_ffn[l][None, :]
        g_sub_l = g_subln[l][None, :]
        lam_params = tuple(p[l][None, :] for p in (lam_q1, lam_k1, lam_q2, lam_k2))

        p_all, *kv_ctx = _inproj(x, g_mix_l, mod, w_in_b[l], rope_tabs)
        kv_layers.append(kv_ctx)
        o_ctx = _ctx_attention(p_all, lam_params, g_sub_l, lam_init)
        o_na = _na_attention(p_all, cna_k, cna_v, bias_tabs, l)
        o_d = _diff_attention(p_all, cd_k, cd_v, lam_params, g_sub_l, lam_init, l)

        i = l // 2
        if l % 2 == 0:
            x = _ffn_dense(x, o_ctx, o_na, o_d, w_out_b[l], g_ffn_l, mod, w_fg[i], w_fu[i], w_fd[i])
        else:
            x, h, route, counts = _router(x, o_ctx, o_na, o_d, w_out_b[l], g_ffn_l, mod, w_router_pad[i])
            tile_expert, n_valid, src_tok, slot = _moe_plan(route, counts)
            y = _moe_experts(tile_expert, n_valid, src_tok, h, w_moe_gate, w_moe_up, w_moe_down, i)
            x = _moe_combine(slot, x, route, mod, y)

    g_fin = g_final[None, :]
    y_prompt = _final_norm(x, g_fin, 0, N_CTX_TOK).reshape(BATCH, SEQ, D_MODEL)
    y_sample = _final_norm(x, g_fin, CTX_TILES, N_LAT_TOK).reshape(DEC_BATCH, DEC_SEQ, D_MODEL)

    kv = [jnp.stack([layer_kv[k] for layer_kv in kv_layers], axis=1) for k in range(4)]
    new_na_k = kv[0].reshape(BATCH, DEPTH, SEQ, NA_HEADS, HEAD_DIM)
    new_na_v = kv[1].reshape(BATCH, DEPTH, SEQ, NA_HEADS, HEAD_DIM)
    new_diff_k = kv[2].reshape(BATCH, DEPTH, SEQ, DIFF_HEADS, 2, HEAD_DIM)
    new_diff_v = kv[3].reshape(BATCH, DEPTH, SEQ, DIFF_HEADS, 2 * HEAD_DIM)
    return (y_prompt, y_sample, new_na_k, new_na_v, new_diff_k, new_diff_v)
```

```python
import functools
import math

import numpy as np
import jax
import jax.numpy as jnp
from jax import lax
from jax.experimental import pallas as pl
from jax.experimental.pallas import tpu as pltpu

F32 = jnp.float32
BF16 = jnp.bfloat16
I32 = jnp.int32

D_MODEL = 1024
DEPTH = 4
BATCH = 16
SEQ = 256
DEC_BATCH = 8
DEC_SEQ = 2048
PAST_LEN = 256
GRID_W = 64
GRID_ROWS = DEC_SEQ // GRID_W
HEAD_DIM = 64
NA_HEADS = 8
NA_WIDTH = 512
DIFF_HEADS = 4
DIFF_WIDTH = 512
IN_WIDTH = 3072
NA_WIN_H = 8
NA_WIN_W = 16
ROPE_THETA = 10000.0
D_FF = 2816
N_EXPERTS = 8
D_FF_EXPERT = 3584
EPS = 1e-6
SUBLN_EPS = 1e-5
ATTN_SCALE = HEAD_DIM ** -0.5

LANES = 128
ROW_CHUNKS = D_MODEL // LANES
N_CTX_TOK = BATCH * SEQ
N_LAT_TOK = DEC_BATCH * DEC_SEQ
N_TOK = N_CTX_TOK + N_LAT_TOK
MOD_ROWS = 16
MASK_VALUE = -1e30

COL_NA_Q, COL_NA_K, COL_NA_V = 0, 512, 1024
COL_D_Q, COL_D_K, COL_D_V = 1536, 2048, 2560

TM = 512
N_TILES = N_TOK // TM
CTX_TILES = N_CTX_TOK // TM
TILES_PER_SEQ = DEC_SEQ // TM
FF_TILE_DENSE = 1408
FF_TILE_MOE = 512
N_FF_MOE = D_FF_EXPERT // FF_TILE_MOE
TM_MOE = 1024
MOE_W_SPLIT = 2
TQ_DIFF = 2048
DIFF_SUB = 128
NA_R = 4
NA_WIN_ROWS = 12
NA_STEPS = GRID_ROWS // NA_R
NA_BLOCKS = 2
MOE_SLOTS = 2 * N_TOK
MOE_TILES = MOE_SLOTS // TM_MOE + N_EXPERTS
GATHER_CHUNK = -(-TM_MOE // N_FF_MOE)
GATHER_ROWS = GATHER_CHUNK * N_FF_MOE
N_SRC_SLOTS = (MOE_TILES + 2) * TM_MOE
LOG2E = 1.4426950408889634
VMEM_LIMIT = 56 * 1024 * 1024


def _cparams(*sem):
    return pltpu.CompilerParams(dimension_semantics=sem, vmem_limit_bytes=VMEM_LIMIT)


def _dot(a, b):
    return jnp.dot(a, b, preferred_element_type=F32)


def _dot_nt(a, b):
    return lax.dot_general(a, b, (((1,), (1,)), ((), ())), preferred_element_type=F32)


def _modulated_norm(x, g, shift, scale):
    xn = x * lax.rsqrt(jnp.mean(x * x, axis=-1, keepdims=True) + EPS)
    return (xn * g) * (1.0 + scale) + shift


def _mod_row(i):
    return jnp.maximum(i // TILES_PER_SEQ - CTX_TILES // TILES_PER_SEQ + 1, 0)


def _mod_spec(chunk):
    return pl.BlockSpec((1, 1, D_MODEL), lambda i, *_: (_mod_row(i) * 6 + chunk, 0, 0))


def _mod_kernel(c_ref, w_ref, b_ref, o_ref):
    cv = c_ref[...]
    s = cv * jax.nn.sigmoid(cv)
    o_ref[...] = jnp.dot(s, w_ref[...], preferred_element_type=F32,
                         precision=lax.Precision.HIGHEST) + b_ref[...]


def _modulation(cvec, w_ada, b_ada):
    tn = 1536
    n = 6 * D_MODEL
    return pl.pallas_call(
        _mod_kernel,
        grid=(DEPTH, n // tn),
        in_specs=[
            pl.BlockSpec((MOD_ROWS, D_MODEL), lambda l, j: (0, 0)),
            pl.BlockSpec((None, D_MODEL, tn), lambda l, j: (l, 0, j)),
            pl.BlockSpec((None, 1, tn), lambda l, j: (l, 0, j)),
        ],
        out_specs=pl.BlockSpec((None, MOD_ROWS, tn), lambda l, j: (l, 0, j)),
        out_shape=jax.ShapeDtypeStruct((DEPTH, MOD_ROWS, n), F32),
        compiler_params=_cparams("parallel", "parallel"),
        name="adaln_modulation",
    )(cvec, w_ada, b_ada.reshape(DEPTH, 1, n))


def _inproj_kernel(x_ref, g_ref, sh_ref, sc_ref, w_ref, cos_ref, sina_ref, sinb_ref, o_ref, *kv_refs):
    _inproj_body(x_ref[...], g_ref, sh_ref, sc_ref, w_ref, cos_ref, sina_ref, sinb_ref, o_ref, kv_refs)


def _inproj_body(x, g_ref, sh_ref, sc_ref, w_ref, cos_ref, sina_ref, sinb_ref, o_ref, kv_refs):
    i = pl.program_id(0)
    h = _modulated_norm(x, g_ref[...], sh_ref[0], sc_ref[0]).astype(BF16)
    chunk = 512
    for c in range(IN_WIDTH // chunk):
        col = c * chunk
        acc = _dot(h, w_ref[:, col:col + chunk])
        for k_i, src in enumerate((COL_NA_K, COL_NA_V, COL_D_K, COL_D_V)):
            if src == col:
                @pl.when(i < CTX_TILES)
                def _(acc=acc, k_i=k_i):
                    for b in range(TM // SEQ):
                        kv_refs[k_i][b] = acc[b * SEQ:(b + 1) * SEQ, :]
        if COL_D_Q <= col < COL_D_V:
            parts = []
            for j in range(chunk // LANES):
                blk = acc[:, j * LANES:(j + 1) * LANES]
                parts.append(blk * cos_ref[...]
                             + pltpu.roll(blk, LANES - 16, 1) * sina_ref[...]
                             + pltpu.roll(blk, 16, 1) * sinb_ref[...])
            acc = jnp.concatenate(parts, axis=1)
        o_ref[:, col:col + chunk] = acc.astype(o_ref.dtype)


def _inproj(x, g, mod, w_bf16, rope_tabs):
    rope_spec = pl.BlockSpec(
        (TM, LANES),
        lambda i: (jnp.where(i < CTX_TILES, TILES_PER_SEQ, (i - CTX_TILES) % TILES_PER_SEQ), 0))
    return pl.pallas_call(
        _inproj_kernel,
        grid=(N_TILES,),
        in_specs=[
            pl.BlockSpec((TM, D_MODEL), lambda i: (i, 0)),
            pl.BlockSpec((1, D_MODEL), lambda i: (0, 0)),
            _mod_spec(0), _mod_spec(1),
            pl.BlockSpec((D_MODEL, IN_WIDTH), lambda i: (0, 0)),
            rope_spec, rope_spec, rope_spec,
        ],
        out_specs=[pl.BlockSpec((TM, IN_WIDTH), lambda i: (i, 0))]
        + [pl.BlockSpec((TM // SEQ, SEQ, 512), lambda i: (jnp.minimum(i, CTX_TILES - 1), 0, 0))] * 4,
        out_shape=[jax.ShapeDtypeStruct((N_TOK, IN_WIDTH), BF16)]
        + [jax.ShapeDtypeStruct((BATCH, SEQ, 512), F32)] * 4,
        compiler_params=_cparams("arbitrary"),
        name="inproj",
    )(x, g, mod, mod, w_bf16, *rope_tabs)


def _combine_rows(rows_scr, buf, route, x_ref, ga_ref, dst_ref):
    g1 = route[:, ROUTE_G1:ROUTE_G1 + 1]
    g2 = route[:, ROUTE_G2:ROUTE_G2 + 1]
    for ch in range(ROW_CHUNKS):
        lanes = slice(ch * LANES, (ch + 1) * LANES)
        mix = (g1 * rows_scr[buf, 0, pl.ds(ch, TM, stride=ROW_CHUNKS), :]
               + g2 * rows_scr[buf, 1, pl.ds(ch, TM, stride=ROW_CHUNKS), :])
        dst_ref[:, lanes] = x_ref[:, lanes] + ga_ref[0][:, lanes] * mix


def _inproj_combine_kernel(slot_ref, xmid_ref, route_ref, ga_ref, y_hbm, g_ref, sh_ref, sc_ref, w_ref,
                           cos_ref, sina_ref, sinb_ref, x_ref, o_ref, *rest):
    kv_refs = rest[:4]
    rows_scr, sems = rest[4:]
    i = pl.program_id(0)

    def row_copy(tile, c, j, buf):
        s = slot_ref[c * N_TOK + tile * TM + j]
        return pltpu.make_async_copy(
            y_hbm.at[pl.ds(pl.multiple_of(s * ROW_CHUNKS, ROW_CHUNKS), ROW_CHUNKS), :],
            rows_scr.at[buf, c, pl.ds(pl.multiple_of(j * ROW_CHUNKS, ROW_CHUNKS), ROW_CHUNKS), :],
            sems.at[buf])

    def wait_tile(buf):
        for c in range(2):
            pltpu.make_async_copy(y_hbm.at[pl.ds(0, TM * ROW_CHUNKS), :], rows_scr.at[buf, c], sems.at[buf]).wait()

    @pl.when(i == 0)
    def _():
        for c in range(2):
            def issue(j, carry, c=c):
                row_copy(0, c, j, 0).start()
                return carry

            lax.fori_loop(0, TM, issue, 0, unroll=8)

    wait_tile(i % 2)
    for c in range(2):
        for j in range(TM):
            row_copy(i + 1, c, j, (i + 1) % 2).start()
    _combine_rows(rows_scr, i % 2, route_ref[...], xmid_ref, ga_ref, x_ref)
    _inproj_body(x_ref[...], g_ref, sh_ref, sc_ref, w_ref, cos_ref, sina_ref, sinb_ref, o_ref, kv_refs)

    @pl.when(i == pl.num_programs(0) - 1)
    def _():
        wait_tile((i + 1) % 2)


def _inproj_combine(slot, x_mid, route, mod_prev, y, g, mod, w_bf16, rope_tabs):
    rope_spec = pl.BlockSpec(
        (TM, LANES),
        lambda i, s: (jnp.where(i < CTX_TILES, TILES_PER_SEQ, (i - CTX_TILES) % TILES_PER_SEQ), 0))
    slot_pad = jnp.concatenate([slot, jnp.zeros((TM,), I32)])
    grid_spec = pltpu.PrefetchScalarGridSpec(
        num_scalar_prefetch=1,
        grid=(N_TILES,),
        in_specs=[
            pl.BlockSpec((TM, D_MODEL), lambda i, s: (i, 0)),
            pl.BlockSpec((TM, LANES), lambda i, s: (i, 0)),
            _mod_spec(5),
            pl.BlockSpec(memory_space=pl.ANY),
            pl.BlockSpec((1, D_MODEL), lambda i, s: (0, 0)),
            _mod_spec(0), _mod_spec(1),
            pl.BlockSpec((D_MODEL, IN_WIDTH), lambda i, s: (0, 0)),
            rope_spec, rope_spec, rope_spec,
        ],
        out_specs=[pl.BlockSpec((TM, D_MODEL), lambda i, s: (i, 0)),
                   pl.BlockSpec((TM, IN_WIDTH), lambda i, s: (i, 0))]
        + [pl.BlockSpec((TM // SEQ, SEQ, 512), lambda i, s: (jnp.minimum(i, CTX_TILES - 1), 0, 0))] * 4,
        scratch_shapes=[pltpu.VMEM((2, 2, TM * ROW_CHUNKS, LANES), F32), pltpu.SemaphoreType.DMA((2,))],
    )
    return pl.pallas_call(
        _inproj_combine_kernel,
        grid_spec=grid_spec,
        out_shape=[jax.ShapeDtypeStruct((N_TOK, D_MODEL), F32), jax.ShapeDtypeStruct((N_TOK, IN_WIDTH), BF16)]
        + [jax.ShapeDtypeStruct((BATCH, SEQ, 512), F32)] * 4,
        compiler_params=_cparams("arbitrary"),
        name="combine_inproj",
    )(slot_pad, x_mid, route, mod_prev, y, g, mod, mod, w_bf16, *rope_tabs)


def _lane_half_mask(shape, half):
    lane = lax.broadcasted_iota(I32, shape, len(shape) - 1)
    return (lane < HEAD_DIM) if half == 0 else (lane >= HEAD_DIM)


def _lambda_value(lq1, lk1, lq2, lk2, lam_init):
    a = jnp.sum(lq1 * lk1, axis=-1, keepdims=True)
    b = jnp.sum(lq2 * lk2, axis=-1, keepdims=True)
    return jnp.exp(a) - jnp.exp(b) + lam_init


def _subln(o, g, lam_init):
    on = o * lax.rsqrt(jnp.mean(o * o, axis=-1, keepdims=True) + SUBLN_EPS)
    return (on * g) * (1.0 - lam_init)


def _ctx_attn_kernel(p_ref, lq1, lk1, lq2, lk2, gs_ref, o_ref, *, lam_init):
    lam = _lambda_value(lq1[...], lk1[...], lq2[...], lk2[...], lam_init)
    for hp in range(NA_HEADS // 2):
        q = p_ref[:, COL_NA_Q + hp * LANES:COL_NA_Q + (hp + 1) * LANES]
        k = p_ref[:, COL_NA_K + hp * LANES:COL_NA_K + (hp + 1) * LANES]
        v = p_ref[:, COL_NA_V + hp * LANES:COL_NA_V + (hp + 1) * LANES]
        outs = []
        for half in range(2):
            qm = jnp.where(_lane_half_mask(q.shape, half), q, jnp.zeros_like(q))
            s = _dot_nt(qm, k)
            m = jnp.max(s, axis=-1, keepdims=True)
            e = jnp.exp2(s - m)
            inv = 1.0 / jnp.sum(e, axis=-1, keepdims=True)
            outs.append(_dot(e.astype(BF16), v) * inv)
        o = jnp.where(_lane_half_mask(outs[0].shape, 0), outs[0], outs[1])
        o_ref[:, hp * LANES:(hp + 1) * LANES] = o.astype(o_ref.dtype)
    for h in range(DIFF_HEADS):
        q = p_ref[:, COL_D_Q + h * LANES:COL_D_Q + (h + 1) * LANES]
        k = p_ref[:, COL_D_K + h * LANES:COL_D_K + (h + 1) * LANES]
        v = p_ref[:, COL_D_V + h * LANES:COL_D_V + (h + 1) * LANES]
        ps = []
        for half in range(2):
            qm = jnp.where(_lane_half_mask(q.shape, half), q, jnp.zeros_like(q))
            s = _dot_nt(qm, k)
            m = jnp.max(s, axis=-1, keepdims=True)
            e = jnp.exp2(s - m)
            ps.append(e / jnp.sum(e, axis=-1, keepdims=True))
        a = (ps[0] - lam * ps[1]).astype(BF16)
        o = _subln(_dot(a, v), gs_ref[...], lam_init)
        o_ref[:, NA_WIDTH + h * LANES:NA_WIDTH + (h + 1) * LANES] = o.astype(o_ref.dtype)


def _ctx_attention(p_all, lam_params, g_subln, lam_init):
    vec = pl.BlockSpec((1, HEAD_DIM), lambda b: (0, 0))
    return pl.pallas_call(
        functools.partial(_ctx_attn_kernel, lam_init=lam_init),
        grid=(BATCH,),
        in_specs=[pl.BlockSpec((SEQ, IN_WIDTH), lambda b: (b, 0)), vec, vec, vec, vec,
                  pl.BlockSpec((1, 2 * HEAD_DIM), lambda b: (0, 0))],
        out_specs=pl.BlockSpec((SEQ, D_MODEL), lambda b: (b, 0)),
        out_shape=jax.ShapeDtypeStruct((N_CTX_TOK, D_MODEL), BF16),
        compiler_params=_cparams("parallel"),
        name="ctx_attention",
    )(p_all, *lam_params, g_subln)


def _bias_table_kernel(rpb_ref, o_ref):
    dr_plan, ok_plan = _na_window_plan()
    shape = (GRID_W, LANES)
    qc = lax.broadcasted_iota(I32, shape, 0)
    lane = lax.broadcasted_iota(I32, shape, 1)
    kc = jnp.bitwise_and(lane, GRID_W - 1)
    cs = jnp.clip(qc - NA_WIN_W // 2, 0, GRID_W - NA_WIN_W)
    in_window = (kc >= cs) & (kc < cs + NA_WIN_W)
    low_half = lane < GRID_W
    masked = jnp.full(shape, MASK_VALUE, F32)

    pieces = {}

    def piece(dr, parity):
        if (dr, parity) not in pieces:
            base = jnp.broadcast_to(rpb_ref[dr:dr + 1, :], shape) * LOG2E
            shifted = pltpu.roll(base, 0, 1, stride=1, stride_axis=0)
            pieces[(dr, parity)] = pltpu.roll(shifted, (LANES - (NA_WIN_W - 1) + GRID_W * parity) % LANES, 1)
        return pieces[(dr, parity)]

    for t in range(3):
        for jr in range(NA_R):
            for p in range(NA_WIN_ROWS // 2):
                halves = []
                for parity in range(2):
                    i = 2 * p + parity
                    halves.append(piece(int(dr_plan[t, jr, i]), parity) if ok_plan[t, jr, i] else masked)
                blk = jnp.where(in_window, jnp.where(low_half, halves[0], halves[1]), MASK_VALUE)
                o_ref[t, jr * GRID_W:(jr + 1) * GRID_W, p * LANES:(p + 1) * LANES] = blk.astype(o_ref.dtype)


def _na_window_plan():
    dr = np.zeros((3, NA_R, NA_WIN_ROWS), np.int32)
    ok = np.zeros((3, NA_R, NA_WIN_ROWS), bool)
    for t, r0 in enumerate((0, NA_R, GRID_ROWS - NA_R)):
        lo = min(max(r0 - NA_WIN_H // 2, 0), GRID_ROWS - NA_WIN_ROWS)
        for jr in range(NA_R):
            r = r0 + jr
            rs = min(max(r - NA_WIN_H // 2, 0), GRID_ROWS - NA_WIN_H)
            for i in range(NA_WIN_ROWS):
                key_row = lo + i
                if rs <= key_row < rs + NA_WIN_H:
                    ok[t, jr, i] = True
                    dr[t, jr, i] = key_row - r + NA_WIN_H - 1
    return dr, ok


def _na_bias_tables(rpb):
    n_dr = 2 * NA_WIN_H - 1
    rpb_pad = jnp.pad(rpb.reshape(DEPTH * NA_HEADS, n_dr, 2 * NA_WIN_W - 1),
                      ((0, 0), (0, 16 - n_dr), (0, LANES - (2 * NA_WIN_W - 1))))
    return pl.pallas_call(
        _bias_table_kernel,
        grid=(DEPTH * NA_HEADS,),
        in_specs=[pl.BlockSpec((None, 16, LANES), lambda g: (g, 0, 0))],
        out_specs=pl.BlockSpec((None, 3, None, NA_R * GRID_W, NA_WIN_ROWS * GRID_W),
                               lambda g: (g // NA_HEADS, 0, g % NA_HEADS, 0, 0)),
        out_shape=jax.ShapeDtypeStruct((DEPTH, 3, NA_HEADS, NA_R * GRID_W, NA_WIN_ROWS * GRID_W), BF16),
        compiler_params=_cparams("parallel"),
        name="na_bias_table",
    )(rpb_pad)


def _na_attn_kernel(q_ref, k_ref, v_ref, kc_ref, vc_ref, bias_ref, o_ref):
    q_rows = NA_R * GRID_W
    n_keys = NA_WIN_ROWS * GRID_W
    plans = []
    for blk in range(NA_BLOCKS):
        j = pl.program_id(1) * NA_BLOCKS + blk
        lo = jnp.clip(j * NA_R - NA_WIN_H // 2, 0, GRID_ROWS - NA_WIN_ROWS)
        k0 = pl.multiple_of(lo * GRID_W, GRID_W)
        step_type = jnp.where(j == 0, 0, jnp.where(j == NA_STEPS - 1, 2, 1))
        plans.append((k0, step_type))
    units = [(blk, head) for blk in range(NA_BLOCKS) for head in range(NA_HEADS)]

    def scores(unit):
        blk, head = unit
        k0, step_type = plans[blk]
        lanes = slice((head // 2) * LANES, (head // 2 + 1) * LANES)
        q = q_ref[blk * q_rows:(blk + 1) * q_rows, lanes]
        qm = jnp.where(_lane_half_mask(q.shape, head % 2), q, jnp.zeros_like(q))
        s_w = _dot_nt(qm, k_ref[pl.ds(k0, n_keys), lanes]) + bias_ref[step_type, head].astype(F32)
        s_c = _dot_nt(qm, kc_ref[:, lanes].astype(BF16))
        return s_w, s_c

    nxt = scores(units[0])
    outs = []
    for n, (blk, head) in enumerate(units):
        s_w, s_c = nxt
        if n + 1 < len(units):
            nxt = scores(units[n + 1])
        k0, _ = plans[blk]
        lanes = slice((head // 2) * LANES, (head // 2 + 1) * LANES)
        m = jnp.maximum(jnp.max(s_w, axis=-1, keepdims=True), jnp.max(s_c, axis=-1, keepdims=True))
        e_w = jnp.exp2(s_w - m)
        e_c = jnp.exp2(s_c - m)
        inv = 1.0 / (jnp.sum(e_w, axis=-1, keepdims=True) + jnp.sum(e_c, axis=-1, keepdims=True))
        pv = (_dot(e_w.astype(BF16), v_ref[pl.ds(k0, n_keys), lanes])
              + _dot(e_c.astype(BF16), vc_ref[:, lanes].astype(BF16)))
        outs.append(pv * inv)
        if head % 2 == 1:
            o = jnp.where(_lane_half_mask(outs[0].shape, 0), outs[0], outs[1])
            o_ref[blk * q_rows:(blk + 1) * q_rows, lanes] = o.astype(o_ref.dtype)
            outs = []


def _na_attention(p_all, cache_k, cache_v, bias_tab, layer):
    q_rows = NA_BLOCKS * NA_R * GRID_W
    steps = NA_STEPS // NA_BLOCKS
    q_blk0 = N_CTX_TOK // q_rows
    seq_blk0 = N_CTX_TOK // DEC_SEQ
    cache_spec = pl.BlockSpec((None, None, PAST_LEN, NA_WIDTH), lambda b, j: (b, layer, 0, 0))
    return pl.pallas_call(
        _na_attn_kernel,
        grid=(DEC_BATCH, steps),
        in_specs=[
            pl.BlockSpec((q_rows, NA_WIDTH), lambda b, j: (q_blk0 + b * steps + j, COL_NA_Q // NA_WIDTH)),
            pl.BlockSpec((DEC_SEQ, NA_WIDTH), lambda b, j: (seq_blk0 + b, COL_NA_K // NA_WIDTH)),
            pl.BlockSpec((DEC_SEQ, NA_WIDTH), lambda b, j: (seq_blk0 + b, COL_NA_V // NA_WIDTH)),
            cache_spec, cache_spec,
            pl.BlockSpec((None, 3, NA_HEADS, NA_R * GRID_W, NA_WIN_ROWS * GRID_W), lambda b, j: (layer, 0, 0, 0, 0)),
        ],
        out_specs=pl.BlockSpec((q_rows, NA_WIDTH), lambda b, j: (b * steps + j, 0)),
        out_shape=jax.ShapeDtypeStruct((N_LAT_TOK, NA_WIDTH), BF16),
        compiler_params=_cparams("parallel", "arbitrary"),
        name="latent_na_attention",
    )(p_all, p_all, p_all, cache_k, cache_v, bias_tab)


def _diff_attn_kernel(q_ref, k_ref, v_ref, kc_ref, vc_ref, lq1, lk1, lq2, lk2, gs_ref, o_ref,
                      *, lam_init):
    lam = _lambda_value(lq1[...], lk1[...], lq2[...], lk2[...], lam_init)
    k = k_ref[...]
    v = v_ref[...]
    kc = kc_ref[...].astype(BF16)
    vc = vc_ref[...].astype(BF16)
    n_sub = TQ_DIFF // DIFF_SUB

    def scores(i):
        q = q_ref[i * DIFF_SUB:(i + 1) * DIFF_SUB, :]
        out = []
        for half in range(2):
            qm = jnp.where(_lane_half_mask(q.shape, half), q, jnp.zeros_like(q))
            out.append((_dot_nt(qm, k), _dot_nt(qm, kc)))
        return out

    nxt = scores(0)
    for i in range(n_sub):
        cur = nxt
        if i + 1 < n_sub:
            nxt = scores(i + 1)
        probs = []
        for s_l, s_c in cur:
            m = jnp.maximum(jnp.max(s_l, axis=-1, keepdims=True), jnp.max(s_c, axis=-1, keepdims=True))
            e_l = jnp.exp2(s_l - m)
            e_c = jnp.exp2(s_c - m)
            inv = 1.0 / (jnp.sum(e_l, axis=-1, keepdims=True) + jnp.sum(e_c, axis=-1, keepdims=True))
            probs.append((e_l, e_c, inv))
        r = lam * probs[1][2] / probs[0][2]
        a_l = (probs[0][0] - probs[1][0] * r).astype(BF16)
        a_c = (probs[0][1] - probs[1][1] * r).astype(BF16)
        o = (_dot(a_l, v) + _dot(a_c, vc)) * probs[0][2]
        o_ref[i * DIFF_SUB:(i + 1) * DIFF_SUB, :] = _subln(o, gs_ref[...], lam_init).astype(o_ref.dtype)


def _diff_attention(p_all, cache_k, cache_v, lam_params, g_subln, lam_init, layer):
    steps = DEC_SEQ // TQ_DIFF
    q_blk0 = N_CTX_TOK // TQ_DIFF
    seq_blk0 = N_CTX_TOK // DEC_SEQ
    vec = pl.BlockSpec((1, HEAD_DIM), lambda b, h, j: (0, 0))
    cache_spec = pl.BlockSpec((None, None, PAST_LEN, LANES), lambda b, h, j: (b, layer, 0, h))
    return pl.pallas_call(
        functools.partial(_diff_attn_kernel, lam_init=lam_init),
        grid=(DEC_BATCH, DIFF_HEADS, steps),
        in_specs=[
            pl.BlockSpec((TQ_DIFF, LANES), lambda b, h, j: (q_blk0 + b * steps + j, COL_D_Q // LANES + h)),
            pl.BlockSpec((DEC_SEQ, LANES), lambda b, h, j: (seq_blk0 + b, COL_D_K // LANES + h)),
            pl.BlockSpec((DEC_SEQ, LANES), lambda b, h, j: (seq_blk0 + b, COL_D_V // LANES + h)),
            cache_spec, cache_spec, vec, vec, vec, vec,
            pl.BlockSpec((1, 2 * HEAD_DIM), lambda b, h, j: (0, 0)),
        ],
        out_specs=pl.BlockSpec((TQ_DIFF, LANES), lambda b, h, j: (b * steps + j, h)),
        out_shape=jax.ShapeDtypeStruct((N_LAT_TOK, DIFF_WIDTH), BF16),
        compiler_params=_cparams("parallel", "parallel", "arbitrary"),
        name="latent_diff_attention",
    )(p_all, p_all, p_all, cache_k, cache_v, *lam_params, g_subln)


def _attn_residual(i, x_ref, oc_ref, ona_ref, od_ref, ga_ref, w_ref, dst_ref):
    @pl.when(i < CTX_TILES)
    def _():
        dst_ref[...] = x_ref[...] + ga_ref[0] * _dot(oc_ref[...], w_ref[...])

    @pl.when(i >= CTX_TILES)
    def _():
        acc = _dot(ona_ref[...], w_ref[:NA_WIDTH, :]) + _dot(od_ref[...], w_ref[NA_WIDTH:, :])
        dst_ref[...] = x_ref[...] + ga_ref[0] * acc


def _attn_residual_specs():
    lat_idx = lambda i, *_: (jnp.maximum(i - CTX_TILES, 0), 0)
    return [
        pl.BlockSpec((TM, D_MODEL), lambda i, *_: (jnp.minimum(i, CTX_TILES - 1), 0)),
        pl.BlockSpec((TM, NA_WIDTH), lat_idx),
        pl.BlockSpec((TM, DIFF_WIDTH), lat_idx),
        _mod_spec(2),
        pl.BlockSpec((D_MODEL, D_MODEL), lambda i, *_: (0, 0)),
    ]


def _swiglu_tile(hb, wg, wu):
    g = _dot(hb, wg)
    u = _dot(hb, wu)
    return (g * jax.nn.sigmoid(g)) * u


def _ffn_kernel(x_ref, oc_ref, ona_ref, od_ref, ga1_ref, wo_ref, g_ref, sh_ref, sc_ref, ga_ref,
                wg_ref, wu_ref, wd_ref, y_ref, xmid_scr, h_scr, acc_scr):
    i = pl.program_id(0)
    f = pl.program_id(1)

    @pl.when(f == 0)
    def _():
        _attn_residual(i, x_ref, oc_ref, ona_ref, od_ref, ga1_ref, wo_ref, xmid_scr)
        h = _modulated_norm(xmid_scr[...], g_ref[...], sh_ref[0], sc_ref[0])
        h_scr[...] = h.astype(BF16)
        acc_scr[...] = jnp.zeros_like(acc_scr)

    a = _swiglu_tile(h_scr[...], wg_ref[...], wu_ref[...])
    acc_scr[...] += _dot(a.astype(BF16), wd_ref[...])

    @pl.when(f == pl.num_programs(1) - 1)
    def _():
        y_ref[...] = xmid_scr[...] + ga_ref[0] * acc_scr[...]


def _ffn_dense(x, o_ctx, o_na, o_d, w_out, g, mod, wg, wu, wd):
    tf = FF_TILE_DENSE
    return pl.pallas_call(
        _ffn_kernel,
        grid=(N_TILES, D_FF // tf),
        in_specs=[pl.BlockSpec((TM, D_MODEL), lambda i, f: (i, 0))] + _attn_residual_specs() + [
            pl.BlockSpec((1, D_MODEL), lambda i, f: (0, 0)),
            _mod_spec(3), _mod_spec(4), _mod_spec(5),
            pl.BlockSpec((D_MODEL, tf), lambda i, f: (0, f)),
            pl.BlockSpec((D_MODEL, tf), lambda i, f: (0, f)),
            pl.BlockSpec((tf, D_MODEL), lambda i, f: (f, 0)),
        ],
        out_specs=pl.BlockSpec((TM, D_MODEL), lambda i, f: (i, 0)),
        out_shape=jax.ShapeDtypeStruct((N_TOK, D_MODEL), F32),
        scratch_shapes=[pltpu.VMEM((TM, D_MODEL), F32), pltpu.VMEM((TM, D_MODEL), BF16),
                        pltpu.VMEM((TM, D_MODEL), F32)],
        compiler_params=_cparams("parallel", "arbitrary"),
        name="ffn_dense",
    )(x, o_ctx, o_na, o_d, mod, w_out, g, mod, mod, mod, wg, wu, wd)


ROUTE_E1, ROUTE_E2, ROUTE_G1, ROUTE_G2, ROUTE_R1, ROUTE_R2 = range(6)


def _router_kernel(x_ref, oc_ref, ona_ref, od_ref, ga1_ref, wo_ref, g_ref, sh_ref, sc_ref, wr_ref,
                   xmid_ref, h_ref, route_ref, cnt_ref, carry_scr):
    i = pl.program_id(0)

    @pl.when(i == 0)
    def _():
        carry_scr[...] = jnp.zeros_like(carry_scr)

    _attn_residual(i, x_ref, oc_ref, ona_ref, od_ref, ga1_ref, wo_ref, xmid_ref)
    h = _modulated_norm(xmid_ref[...], g_ref[...], sh_ref[0], sc_ref[0])
    for c in range(ROW_CHUNKS):
        h_ref[pl.ds(c, TM, stride=ROW_CHUNKS), :] = h[:, c * LANES:(c + 1) * LANES]
    logits = jnp.dot(h, wr_ref[...], preferred_element_type=F32, precision=lax.Precision.HIGHEST)
    lane = lax.broadcasted_iota(I32, logits.shape, 1)
    logits = jnp.where(lane < N_EXPERTS, logits, MASK_VALUE)
    m1 = jnp.max(logits, axis=-1, keepdims=True)
    i1 = jnp.min(jnp.where(logits == m1, lane, LANES), axis=-1, keepdims=True)
    rest = jnp.where(lane == i1, MASK_VALUE, logits)
    m2 = jnp.max(rest, axis=-1, keepdims=True)
    i2 = jnp.min(jnp.where(rest == m2, lane, LANES), axis=-1, keepdims=True)
    e2 = jnp.exp(m2 - m1)
    g1 = 1.0 / (1.0 + e2)
    g2 = e2 / (1.0 + e2)

    hit1 = lane == i1
    hit2 = lane == i2
    onehot = jnp.where(hit1 | hit2, 1.0, 0.0)
    row = lax.broadcasted_iota(I32, (TM, TM), 0)
    col = lax.broadcasted_iota(I32, (TM, TM), 1)
    lower = jnp.where(row > col, 1.0, 0.0).astype(BF16)
    before = _dot(lower, onehot.astype(BF16)) + carry_scr[...]
    r1 = jnp.sum(jnp.where(hit1, before, 0.0), axis=-1, keepdims=True)
    r2 = jnp.sum(jnp.where(hit2, before, 0.0), axis=-1, keepdims=True)
    carry_scr[...] += jnp.sum(onehot, axis=0, keepdims=True)

    out = jnp.zeros(logits.shape, F32)
    for slot, val in ((ROUTE_E1, i1.astype(F32)), (ROUTE_E2, i2.astype(F32)), (ROUTE_G1, g1),
                      (ROUTE_G2, g2), (ROUTE_R1, r1), (ROUTE_R2, r2)):
        out = jnp.where(lane == slot, val, out)
    route_ref[...] = out

    @pl.when(i == pl.num_programs(0) - 1)
    def _():
        cnt_ref[...] = jnp.broadcast_to(carry_scr[...], cnt_ref.shape)


def _router(x, o_ctx, o_na, o_d, w_out, g, mod, w_router_pad):
    return pl.pallas_call(
        _router_kernel,
        grid=(N_TILES,),
        in_specs=[pl.BlockSpec((TM, D_MODEL), lambda i: (i, 0))] + _attn_residual_specs() + [
            pl.BlockSpec((1, D_MODEL), lambda i: (0, 0)),
            _mod_spec(3), _mod_spec(4),
            pl.BlockSpec((D_MODEL, LANES), lambda i: (0, 0)),
        ],
        out_specs=[pl.BlockSpec((TM, D_MODEL), lambda i: (i, 0)),
                   pl.BlockSpec((TM * ROW_CHUNKS, LANES), lambda i: (i, 0)),
                   pl.BlockSpec((TM, LANES), lambda i: (i, 0)),
                   pl.BlockSpec((8, LANES), lambda i: (0, 0))],
        out_shape=[jax.ShapeDtypeStruct((N_TOK, D_MODEL), F32),
                   jax.ShapeDtypeStruct((N_TOK * ROW_CHUNKS, LANES), F32),
                   jax.ShapeDtypeStruct((N_TOK, LANES), F32),
                   jax.ShapeDtypeStruct((8, LANES), F32)],
        scratch_shapes=[pltpu.VMEM((1, LANES), F32)],
        compiler_params=_cparams("arbitrary"),
        name="moe_router",
    )(x, o_ctx, o_na, o_d, mod, w_out, g, mod, mod, w_router_pad)


def _moe_plan(route, counts):
    cnt = counts[0, :N_EXPERTS].astype(I32)
    tiles = (cnt + TM_MOE - 1) // TM_MOE
    tile_end = jnp.cumsum(tiles)
    group_start = (tile_end - tiles) * TM_MOE
    e = route[:, ROUTE_E1:ROUTE_E2 + 1].astype(I32)
    r = route[:, ROUTE_R1:ROUTE_R2 + 1].astype(I32)
    slot = (group_start[e] + r).T.reshape(-1)
    tok = jnp.tile(jnp.arange(N_TOK, dtype=I32), 2)
    src_tok = jnp.zeros((N_SRC_SLOTS,), I32).at[slot].set(tok)
    n_valid = tile_end[-1:]
    t = jnp.arange(MOE_TILES, dtype=I32)
    tile_expert = jnp.minimum(jnp.sum((t[:, None] >= tile_end[None, :]).astype(I32), axis=1),
                              N_EXPERTS - 1)
    last_expert = tile_expert[jnp.maximum(n_valid[0] - 1, 0)]
    tile_expert = jnp.where(t < n_valid[0], tile_expert, last_expert)
    return tile_expert, n_valid, src_tok, slot


def _moe_kernel(te_ref, nv_ref, src_ref, h_hbm, *refs):
    wg_refs = refs[:MOE_W_SPLIT]
    wu_refs = refs[MOE_W_SPLIT:2 * MOE_W_SPLIT]
    wd_refs = refs[2 * MOE_W_SPLIT:3 * MOE_W_SPLIT]
    y_ref, hrow_scr, hb_scr, acc_scr, sems = refs[3 * MOE_W_SPLIT:]
    t = pl.program_id(0)
    f = pl.program_id(1)
    valid = t < nv_ref[0]
    last_t = pl.num_programs(0) - 1
    last_f = N_FF_MOE - 1

    def row_copy(tile, row, buf):
        tok = src_ref[tile * TM_MOE + row]
        return pltpu.make_async_copy(h_hbm.at[pl.ds(pl.multiple_of(tok * ROW_CHUNKS, ROW_CHUNKS), ROW_CHUNKS), :],
                                     hrow_scr.at[buf, pl.ds(pl.multiple_of(row * ROW_CHUNKS, ROW_CHUNKS), ROW_CHUNKS), :],
                                     sems.at[buf])

    def wait_tile(buf):
        pltpu.make_async_copy(h_hbm.at[pl.ds(0, GATHER_ROWS * ROW_CHUNKS), :],
                              hrow_scr.at[buf, pl.ds(0, GATHER_ROWS * ROW_CHUNKS), :], sems.at[buf]).wait()

    def issue_next_chunk():
        for j in range(GATHER_CHUNK):
            row_copy(t + 1, f * GATHER_CHUNK + j, (t + 1) % 2).start()

    @pl.when((t == 0) & (f == 0))
    def _():
        def issue(j, carry):
            row_copy(0, j, 0).start()
            return carry

        lax.fori_loop(0, GATHER_ROWS, issue, 0, unroll=7)

    @pl.when(f == 0)
    def _():
        wait_tile(t % 2)
        for c in range(ROW_CHUNKS):
            hb_scr[:, c * LANES:(c + 1) * LANES] = hrow_scr[t % 2, pl.ds(c, TM_MOE, stride=ROW_CHUNKS), :].astype(BF16)
        acc_scr[...] = jnp.zeros_like(acc_scr)

    @pl.when(valid)
    def _():
        issue_next_chunk()
        hb = hb_scr[...]
        acc = None
        for wg_ref, wu_ref, wd_ref in zip(wg_refs, wu_refs, wd_refs):
            a = _swiglu_tile(hb, wg_ref[...].astype(BF16), wu_ref[...].astype(BF16))
            d = _dot(a.astype(BF16), wd_ref[...].astype(BF16))
            acc = d if acc is None else acc + d
        acc_scr[...] += acc

    @pl.when(jnp.logical_not(valid))
    def _():
        issue_next_chunk()

    @pl.when(f == last_f)
    def _():
        for c in range(ROW_CHUNKS):
            y_ref[pl.ds(c, TM_MOE, stride=ROW_CHUNKS), :] = acc_scr[:, c * LANES:(c + 1) * LANES]

    @pl.when((t == last_t) & (f == last_f))
    def _():
        wait_tile((t + 1) % 2)


def _moe_experts(tile_expert, n_valid, src_tok, h, wg, wu, wd, layer):
    ts = FF_TILE_MOE // MOE_W_SPLIT
    buf_rows = -(-GATHER_ROWS // 8) * 8

    def f_eff(t, f, nv):
        return jnp.where(t < nv[0], f, N_FF_MOE - 1)

    grid_spec = pltpu.PrefetchScalarGridSpec(
        num_scalar_prefetch=3,
        grid=(MOE_TILES, N_FF_MOE),
        in_specs=[pl.BlockSpec(memory_space=pl.ANY)] + [
            pl.BlockSpec((None, None, D_MODEL, ts),
                         lambda t, f, te, nv, st, s=s: (layer, te[t], 0, MOE_W_SPLIT * f_eff(t, f, nv) + s))
            for _ in range(2) for s in range(MOE_W_SPLIT)
        ] + [
            pl.BlockSpec((None, None, ts, D_MODEL),
                         lambda t, f, te, nv, st, s=s: (layer, te[t], MOE_W_SPLIT * f_eff(t, f, nv) + s, 0))
            for s in range(MOE_W_SPLIT)
        ],
        out_specs=pl.BlockSpec((TM_MOE * ROW_CHUNKS, LANES), lambda t, f, te, nv, st: (t, 0)),
        scratch_shapes=[pltpu.VMEM((2, buf_rows * ROW_CHUNKS, LANES), F32), pltpu.VMEM((TM_MOE, D_MODEL), BF16),
                        pltpu.VMEM((TM_MOE, D_MODEL), F32), pltpu.SemaphoreType.DMA((2,))],
    )
    return pl.pallas_call(
        _moe_kernel,
        grid_spec=grid_spec,
        out_shape=jax.ShapeDtypeStruct((MOE_TILES * TM_MOE * ROW_CHUNKS, LANES), F32),
        compiler_params=_cparams("arbitrary", "arbitrary"),
        name="moe_experts",
    )(tile_expert, n_valid, src_tok, h, *([wg] * MOE_W_SPLIT + [wu] * MOE_W_SPLIT + [wd] * MOE_W_SPLIT))


def _moe_combine_kernel(slot_ref, x_ref, route_ref, ga_ref, y_hbm, o_ref, rows_scr, sems):
    i = pl.program_id(0)

    def issue_tile(tile, buf):
        for c in range(2):
            base = c * N_TOK + tile * TM

            def issue(j, carry, base=base, c=c):
                s = slot_ref[base + j]
                pltpu.make_async_copy(
                    y_hbm.at[pl.ds(pl.multiple_of(s * ROW_CHUNKS, ROW_CHUNKS), ROW_CHUNKS), :],
                    rows_scr.at[buf, c, pl.ds(pl.multiple_of(j * ROW_CHUNKS, ROW_CHUNKS), ROW_CHUNKS), :],
                    sems.at[buf]).start()
                return carry

            lax.fori_loop(0, TM, issue, 0, unroll=8)

    @pl.when(i == 0)
    def _():
        issue_tile(0, 0)

    @pl.when(i + 1 < pl.num_programs(0))
    def _():
        issue_tile(i + 1, (i + 1) % 2)

    buf = i % 2
    for c in range(2):
        pltpu.make_async_copy(y_hbm.at[pl.ds(0, TM * ROW_CHUNKS), :], rows_scr.at[buf, c], sems.at[buf]).wait()
    route = route_ref[...]
    g1 = route[:, ROUTE_G1:ROUTE_G1 + 1]
    g2 = route[:, ROUTE_G2:ROUTE_G2 + 1]
    for ch in range(ROW_CHUNKS):
        lanes = slice(ch * LANES, (ch + 1) * LANES)
        mix = (g1 * rows_scr[buf, 0, pl.ds(ch, TM, stride=ROW_CHUNKS), :]
               + g2 * rows_scr[buf, 1, pl.ds(ch, TM, stride=ROW_CHUNKS), :])
        o_ref[:, lanes] = x_ref[:, lanes] + ga_ref[0][:, lanes] * mix


def _moe_combine(slot, x, route, mod, y):
    grid_spec = pltpu.PrefetchScalarGridSpec(
        num_scalar_prefetch=1,
        grid=(N_TILES,),
        in_specs=[
            pl.BlockSpec((TM, D_MODEL), lambda i, s: (i, 0)),
            pl.BlockSpec((TM, LANES), lambda i, s: (i, 0)),
            _mod_spec(5),
            pl.BlockSpec(memory_space=pl.ANY),
        ],
        out_specs=pl.BlockSpec((TM, D_MODEL), lambda i, s: (i, 0)),
        scratch_shapes=[pltpu.VMEM((2, 2, TM * ROW_CHUNKS, LANES), F32), pltpu.SemaphoreType.DMA((2,))],
    )
    return pl.pallas_call(
        _moe_combine_kernel,
        grid_spec=grid_spec,
        out_shape=jax.ShapeDtypeStruct((N_TOK, D_MODEL), F32),
        compiler_params=_cparams("arbitrary"),
        name="moe_combine",
    )(slot, x, route, mod, y)


def _final_norm_kernel(x_ref, g_ref, y_ref):
    x = x_ref[...]
    y_ref[...] = (x * lax.rsqrt(jnp.mean(x * x, axis=-1, keepdims=True) + EPS)) * g_ref[...]


def _final_norm(x, g, tile0, n_tok):
    return pl.pallas_call(
        _final_norm_kernel,
        grid=(n_tok // TM,),
        in_specs=[pl.BlockSpec((TM, D_MODEL), lambda i: (tile0 + i, 0)),
                  pl.BlockSpec((1, D_MODEL), lambda i: (0, 0))],
        out_specs=pl.BlockSpec((TM, D_MODEL), lambda i: (i, 0)),
        out_shape=jax.ShapeDtypeStruct((n_tok, D_MODEL), F32),
        compiler_params=_cparams("parallel"),
        name="final_norm",
    )(x, g)


def _rope_tables():
    t = jnp.arange(DEC_SEQ)
    row = (t // GRID_W).astype(F32)
    col = (t % GRID_W).astype(F32)
    half = HEAD_DIM // 2
    inv = 1.0 / (ROPE_THETA ** (jnp.arange(0, half, 2, dtype=F32) / half))
    ar = row[:, None] * inv[None]
    ac = col[:, None] * inv[None]
    ang = jnp.concatenate([ar, ar, ac, ac], axis=-1)
    ang = jnp.concatenate([ang, jnp.zeros((TM, HEAD_DIM), F32)], axis=0)
    cos = jnp.tile(jnp.cos(ang), (1, LANES // HEAD_DIM))
    sin = jnp.tile(jnp.sin(ang), (1, LANES // HEAD_DIM))
    first_half = (jnp.arange(LANES) % 32) < 16
    sin_a = jnp.where(first_half[None, :], -sin, 0.0)
    sin_b = jnp.where(first_half[None, :], 0.0, sin)
    return cos, sin_a, sin_b


def kernel(x_prompt, x_sample, c, cache_na_k, cache_na_v, cache_diff_k, cache_diff_v, c_ctx, w_ada, b_ada, g_mix, w_in, rpb, lam_q1, lam_k1, lam_q2, lam_k2, g_subln, w_out, g_ffn, w_ffn_gate, w_ffn_up, w_ffn_down, w_router, w_moe_gate, w_moe_up, w_moe_down, g_final):
    x = jnp.concatenate([x_prompt.reshape(N_CTX_TOK, D_MODEL), x_sample.reshape(N_LAT_TOK, D_MODEL)])

    cvec = jnp.zeros((MOD_ROWS, D_MODEL), F32).at[0].set(c_ctx).at[1:1 + DEC_BATCH].set(c)
    mod_all = _modulation(cvec, w_ada, b_ada).reshape(DEPTH, MOD_ROWS * 6, 1, D_MODEL)

    col = jnp.arange(IN_WIDTH)
    is_q = (col < COL_NA_K) | ((col >= COL_D_Q) & (col < COL_D_K))
    q_scale = jnp.where(is_q, ATTN_SCALE * LOG2E, 1.0).astype(F32)
    w_in_b = (w_in * q_scale[None, None, :]).astype(BF16)
    w_out_b = w_out.astype(BF16)
    w_fg, w_fu, w_fd = (w.astype(BF16) for w in (w_ffn_gate, w_ffn_up, w_ffn_down))
    w_router_pad = jnp.pad(w_router, ((0, 0), (0, 0), (0, LANES - N_EXPERTS)))

    rope_tabs = _rope_tables()
    bias_tabs = _na_bias_tables(rpb)
    cna_k = cache_na_k.reshape(DEC_BATCH, DEPTH, PAST_LEN, NA_WIDTH)
    cna_v = cache_na_v.reshape(DEC_BATCH, DEPTH, PAST_LEN, NA_WIDTH)
    cd_k = cache_diff_k.reshape(DEC_BATCH, DEPTH, PAST_LEN, DIFF_WIDTH)
    cd_v = cache_diff_v.reshape(DEC_BATCH, DEPTH, PAST_LEN, DIFF_WIDTH)

    kv_layers = []
    pending_combine = None
    for l in range(DEPTH):
        lam_init = 0.8 - 0.6 * math.exp(-0.3 * l)
        mod = mod_all[l]
        g_mix_l = g_mix[l][None, :]
        g_ffn_l = g_ffn[l][None, :]
        g_sub_l = g_subln[l][None, :]
        lam_params = tuple(p[l][None, :] for p in (lam_q1, lam_k1, lam_q2, lam_k2))

        if pending_combine is None:
            p_all, *kv_ctx = _inproj(x, g_mix_l, mod, w_in_b[l], rope_tabs)
        else:
            x, p_all, *kv_ctx = _inproj_combine(*pending_combine, g_mix_l, mod, w_in_b[l], rope_tabs)
            pending_combine = None
        kv_layers.append(kv_ctx)
        o_ctx = _ctx_attention(p_all, lam_params, g_sub_l, lam_init)
        o_na = _na_attention(p_all, cna_k, cna_v, bias_tabs, l)
        o_d = _diff_attention(p_all, cd_k, cd_v, lam_params, g_sub_l, lam_init, l)

        i = l // 2
        if l % 2 == 0:
            x = _ffn_dense(x, o_ctx, o_na, o_d, w_out_b[l], g_ffn_l, mod, w_fg[i], w_fu[i], w_fd[i])
        else:
            x, h, route, counts = _router(x, o_ctx, o_na, o_d, w_out_b[l], g_ffn_l, mod, w_router_pad[i])
            tile_expert, n_valid, src_tok, slot = _moe_plan(route, counts)
            y = _moe_experts(tile_expert, n_valid, src_tok, h, w_moe_gate, w_moe_up, w_moe_down, i)
            if l + 1 < DEPTH:
                pending_combine = (slot, x, route, mod, y)
            else:
                x = _moe_combine(slot, x, route, mod, y)

    g_fin = g_final[None, :]
    y_prompt = _final_norm(x, g_fin, 0, N_CTX_TOK).reshape(BATCH, SEQ, D_MODEL)
    y_sample = _final_norm(x, g_fin, CTX_TILES, N_LAT_TOK).reshape(DEC_BATCH, DEC_SEQ, D_MODEL)

    kv = [jnp.stack([layer_kv[k] for layer_kv in kv_layers], axis=1) for k in range(4)]
    new_na_k = kv[0].reshape(BATCH, DEPTH, SEQ, NA_HEADS, HEAD_DIM)
    new_na_v = kv[1].reshape(BATCH, DEPTH, SEQ, NA_HEADS, HEAD_DIM)
    new_diff_k = kv[2].reshape(BATCH, DEPTH, SEQ, DIFF_HEADS, 2, HEAD_DIM)
    new_diff_v = kv[3].reshape(BATCH, DEPTH, SEQ, DIFF_HEADS, 2 * HEAD_DIM)
    return (y_prompt, y_sample, new_na_k, new_na_v, new_diff_k, new_diff_v)
```

```python
import functools
import math

import numpy as np
import jax
import jax.numpy as jnp
from jax import lax
from jax.experimental import pallas as pl
from jax.experimental.pallas import tpu as pltpu

F32 = jnp.float32
BF16 = jnp.bfloat16
I32 = jnp.int32

D_MODEL = 1024
DEPTH = 4
BATCH = 16
SEQ = 256
DEC_BATCH = 8
DEC_SEQ = 2048
PAST_LEN = 256
GRID_W = 64
GRID_ROWS = DEC_SEQ // GRID_W
HEAD_DIM = 64
NA_HEADS = 8
NA_WIDTH = 512
DIFF_HEADS = 4
DIFF_WIDTH = 512
IN_WIDTH = 3072
NA_WIN_H = 8
NA_WIN_W = 16
ROPE_THETA = 10000.0
D_FF = 2816
N_EXPERTS = 8
D_FF_EXPERT = 3584
EPS = 1e-6
SUBLN_EPS = 1e-5
ATTN_SCALE = HEAD_DIM ** -0.5

LANES = 128
ROW_CHUNKS = D_MODEL // LANES
N_CTX_TOK = BATCH * SEQ
N_LAT_TOK = DEC_BATCH * DEC_SEQ
N_TOK = N_CTX_TOK + N_LAT_TOK
MOD_ROWS = 16
MASK_VALUE = -1e30

COL_NA_Q, COL_NA_K, COL_NA_V = 0, 512, 1024
COL_D_Q, COL_D_K, COL_D_V = 1536, 2048, 2560

TM = 512
N_TILES = N_TOK // TM
CTX_TILES = N_CTX_TOK // TM
TILES_PER_SEQ = DEC_SEQ // TM
FF_TILE_DENSE = 1408
FF_TILE_MOE = 512
N_FF_MOE = D_FF_EXPERT // FF_TILE_MOE
TM_MOE = 1024
MOE_W_SPLIT = 2
TQ_DIFF = 2048
DIFF_SUB = 128
NA_R = 4
NA_WIN_ROWS = 12
NA_STEPS = GRID_ROWS // NA_R
NA_BLOCKS = 2
MOE_SLOTS = 2 * N_TOK
MOE_TILES = MOE_SLOTS // TM_MOE + N_EXPERTS
GATHER_CHUNK = -(-TM_MOE // N_FF_MOE)
GATHER_ROWS = GATHER_CHUNK * N_FF_MOE
N_SRC_SLOTS = (MOE_TILES + 2) * TM_MOE
LOG2E = 1.4426950408889634
VMEM_LIMIT = 56 * 1024 * 1024


def _cparams(*sem):
    return pltpu.CompilerParams(dimension_semantics=sem, vmem_limit_bytes=VMEM_LIMIT)


def _dot(a, b):
    return jnp.dot(a, b, preferred_element_type=F32)


def _dot_nt(a, b):
    return lax.dot_general(a, b, (((1,), (1,)), ((), ())), preferred_element_type=F32)


def _modulated_norm(x, g, shift, scale):
    xn = x * lax.rsqrt(jnp.mean(x * x, axis=-1, keepdims=True) + EPS)
    return (xn * g) * (1.0 + scale) + shift


def _mod_row(i):
    return jnp.maximum(i // TILES_PER_SEQ - CTX_TILES // TILES_PER_SEQ + 1, 0)


def _mod_spec(chunk):
    return pl.BlockSpec((1, 1, D_MODEL), lambda i, *_: (_mod_row(i) * 6 + chunk, 0, 0))


def _mod_kernel(c_ref, w_ref, b_ref, o_ref):
    cv = c_ref[...]
    s = cv * jax.nn.sigmoid(cv)
    o_ref[...] = jnp.dot(s, w_ref[...], preferred_element_type=F32,
                         precision=lax.Precision.HIGHEST) + b_ref[...]


def _modulation(cvec, w_ada, b_ada):
    tn = 1536
    n = 6 * D_MODEL
    return pl.pallas_call(
        _mod_kernel,
        grid=(DEPTH, n // tn),
        in_specs=[
            pl.BlockSpec((MOD_ROWS, D_MODEL), lambda l, j: (0, 0)),
            pl.BlockSpec((None, D_MODEL, tn), lambda l, j: (l, 0, j)),
            pl.BlockSpec((None, 1, tn), lambda l, j: (l, 0, j)),
        ],
        out_specs=pl.BlockSpec((None, MOD_ROWS, tn), lambda l, j: (l, 0, j)),
        out_shape=jax.ShapeDtypeStruct((DEPTH, MOD_ROWS, n), F32),
        compiler_params=_cparams("parallel", "parallel"),
        name="adaln_modulation",
    )(cvec, w_ada, b_ada.reshape(DEPTH, 1, n))


def _inproj_kernel(x_ref, g_ref, sh_ref, sc_ref, w_ref, cos_ref, sina_ref, sinb_ref, o_ref, *kv_refs):
    i = pl.program_id(0)
    h = _modulated_norm(x_ref[...], g_ref[...], sh_ref[0], sc_ref[0]).astype(BF16)
    chunk = 512
    for c in range(IN_WIDTH // chunk):
        col = c * chunk
        acc = _dot(h, w_ref[:, col:col + chunk])
        for k_i, src in enumerate((COL_NA_K, COL_NA_V, COL_D_K, COL_D_V)):
            if src == col:
                @pl.when(i < CTX_TILES)
                def _(acc=acc, k_i=k_i):
                    for b in range(TM // SEQ):
                        kv_refs[k_i][b] = acc[b * SEQ:(b + 1) * SEQ, :]
        if COL_D_Q <= col < COL_D_V:
            parts = []
            for j in range(chunk // LANES):
                blk = acc[:, j * LANES:(j + 1) * LANES]
                parts.append(blk * cos_ref[...]
                             + pltpu.roll(blk, LANES - 16, 1) * sina_ref[...]
                             + pltpu.roll(blk, 16, 1) * sinb_ref[...])
            acc = jnp.concatenate(parts, axis=1)
        o_ref[:, col:col + chunk] = acc.astype(o_ref.dtype)


def _inproj(x, g, mod, w_bf16, rope_tabs):
    rope_spec = pl.BlockSpec(
        (TM, LANES),
        lambda i: (jnp.where(i < CTX_TILES, TILES_PER_SEQ, (i - CTX_TILES) % TILES_PER_SEQ), 0))
    return pl.pallas_call(
        _inproj_kernel,
        grid=(N_TILES,),
        in_specs=[
            pl.BlockSpec((TM, D_MODEL), lambda i: (i, 0)),
            pl.BlockSpec((1, D_MODEL), lambda i: (0, 0)),
            _mod_spec(0), _mod_spec(1),
            pl.BlockSpec((D_MODEL, IN_WIDTH), lambda i: (0, 0)),
            rope_spec, rope_spec, rope_spec,
        ],
        out_specs=[pl.BlockSpec((TM, IN_WIDTH), lambda i: (i, 0))]
        + [pl.BlockSpec((TM // SEQ, SEQ, 512), lambda i: (jnp.minimum(i, CTX_TILES - 1), 0, 0))] * 4,
        out_shape=[jax.ShapeDtypeStruct((N_TOK, IN_WIDTH), BF16)]
        + [jax.ShapeDtypeStruct((BATCH, SEQ, 512), F32)] * 4,
        compiler_params=_cparams("arbitrary"),
        name="inproj",
    )(x, g, mod, mod, w_bf16, *rope_tabs)


def _lane_half_mask(shape, half):
    lane = lax.broadcasted_iota(I32, shape, len(shape) - 1)
    return (lane < HEAD_DIM) if half == 0 else (lane >= HEAD_DIM)


def _lambda_value(lq1, lk1, lq2, lk2, lam_init):
    a = jnp.sum(lq1 * lk1, axis=-1, keepdims=True)
    b = jnp.sum(lq2 * lk2, axis=-1, keepdims=True)
    return jnp.exp(a) - jnp.exp(b) + lam_init


def _subln(o, g, lam_init):
    on = o * lax.rsqrt(jnp.mean(o * o, axis=-1, keepdims=True) + SUBLN_EPS)
    return (on * g) * (1.0 - lam_init)


def _ctx_attn_kernel(p_ref, lq1, lk1, lq2, lk2, gs_ref, o_ref, *, lam_init):
    lam = _lambda_value(lq1[...], lk1[...], lq2[...], lk2[...], lam_init)
    for hp in range(NA_HEADS // 2):
        q = p_ref[:, COL_NA_Q + hp * LANES:COL_NA_Q + (hp + 1) * LANES]
        k = p_ref[:, COL_NA_K + hp * LANES:COL_NA_K + (hp + 1) * LANES]
        v = p_ref[:, COL_NA_V + hp * LANES:COL_NA_V + (hp + 1) * LANES]
        outs = []
        for half in range(2):
            qm = jnp.where(_lane_half_mask(q.shape, half), q, jnp.zeros_like(q))
            s = _dot_nt(qm, k)
            m = jnp.max(s, axis=-1, keepdims=True)
            e = jnp.exp2(s - m)
            inv = 1.0 / jnp.sum(e, axis=-1, keepdims=True)
            outs.append(_dot(e.astype(BF16), v) * inv)
        o = jnp.where(_lane_half_mask(outs[0].shape, 0), outs[0], outs[1])
        o_ref[:, hp * LANES:(hp + 1) * LANES] = o.astype(o_ref.dtype)
    for h in range(DIFF_HEADS):
        q = p_ref[:, COL_D_Q + h * LANES:COL_D_Q + (h + 1) * LANES]
        k = p_ref[:, COL_D_K + h * LANES:COL_D_K + (h + 1) * LANES]
        v = p_ref[:, COL_D_V + h * LANES:COL_D_V + (h + 1) * LANES]
        ps = []
        for half in range(2):
            qm = jnp.where(_lane_half_mask(q.shape, half), q, jnp.zeros_like(q))
            s = _dot_nt(qm, k)
            m = jnp.max(s, axis=-1, keepdims=True)
            e = jnp.exp2(s - m)
            ps.append(e / jnp.sum(e, axis=-1, keepdims=True))
        a = (ps[0] - lam * ps[1]).astype(BF16)
        o = _subln(_dot(a, v), gs_ref[...], lam_init)
        o_ref[:, NA_WIDTH + h * LANES:NA_WIDTH + (h + 1) * LANES] = o.astype(o_ref.dtype)


def _ctx_attention(p_all, lam_params, g_subln, lam_init):
    vec = pl.BlockSpec((1, HEAD_DIM), lambda b: (0, 0))
    return pl.pallas_call(
        functools.partial(_ctx_attn_kernel, lam_init=lam_init),
        grid=(BATCH,),
        in_specs=[pl.BlockSpec((SEQ, IN_WIDTH), lambda b: (b, 0)), vec, vec, vec, vec,
                  pl.BlockSpec((1, 2 * HEAD_DIM), lambda b: (0, 0))],
        out_specs=pl.BlockSpec((SEQ, D_MODEL), lambda b: (b, 0)),
        out_shape=jax.ShapeDtypeStruct((N_CTX_TOK, D_MODEL), BF16),
        compiler_params=_cparams("parallel"),
        name="ctx_attention",
    )(p_all, *lam_params, g_subln)


def _bias_table_kernel(rpb_ref, o_ref):
    dr_plan, ok_plan = _na_window_plan()
    shape = (GRID_W, LANES)
    qc = lax.broadcasted_iota(I32, shape, 0)
    lane = lax.broadcasted_iota(I32, shape, 1)
    kc = jnp.bitwise_and(lane, GRID_W - 1)
    cs = jnp.clip(qc - NA_WIN_W // 2, 0, GRID_W - NA_WIN_W)
    in_window = (kc >= cs) & (kc < cs + NA_WIN_W)
    low_half = lane < GRID_W
    masked = jnp.full(shape, MASK_VALUE, F32)

    pieces = {}

    def piece(dr, parity):
        if (dr, parity) not in pieces:
            base = jnp.broadcast_to(rpb_ref[dr:dr + 1, :], shape) * LOG2E
            shifted = pltpu.roll(base, 0, 1, stride=1, stride_axis=0)
            pieces[(dr, parity)] = pltpu.roll(shifted, (LANES - (NA_WIN_W - 1) + GRID_W * parity) % LANES, 1)
        return pieces[(dr, parity)]

    for t in range(3):
        for jr in range(NA_R):
            for p in range(NA_WIN_ROWS // 2):
                halves = []
                for parity in range(2):
                    i = 2 * p + parity
                    halves.append(piece(int(dr_plan[t, jr, i]), parity) if ok_plan[t, jr, i] else masked)
                blk = jnp.where(in_window, jnp.where(low_half, halves[0], halves[1]), MASK_VALUE)
                o_ref[t, jr * GRID_W:(jr + 1) * GRID_W, p * LANES:(p + 1) * LANES] = blk.astype(o_ref.dtype)


def _na_window_plan():
    dr = np.zeros((3, NA_R, NA_WIN_ROWS), np.int32)
    ok = np.zeros((3, NA_R, NA_WIN_ROWS), bool)
    for t, r0 in enumerate((0, NA_R, GRID_ROWS - NA_R)):
        lo = min(max(r0 - NA_WIN_H // 2, 0), GRID_ROWS - NA_WIN_ROWS)
        for jr in range(NA_R):
            r = r0 + jr
            rs = min(max(r - NA_WIN_H // 2, 0), GRID_ROWS - NA_WIN_H)
            for i in range(NA_WIN_ROWS):
                key_row = lo + i
                if rs <= key_row < rs + NA_WIN_H:
                    ok[t, jr, i] = True
                    dr[t, jr, i] = key_row - r + NA_WIN_H - 1
    return dr, ok


def _na_bias_tables(rpb):
    n_dr = 2 * NA_WIN_H - 1
    rpb_pad = jnp.pad(rpb.reshape(DEPTH * NA_HEADS, n_dr, 2 * NA_WIN_W - 1),
                      ((0, 0), (0, 16 - n_dr), (0, LANES - (2 * NA_WIN_W - 1))))
    return pl.pallas_call(
        _bias_table_kernel,
        grid=(DEPTH * NA_HEADS,),
        in_specs=[pl.BlockSpec((None, 16, LANES), lambda g: (g, 0, 0))],
        out_specs=pl.BlockSpec((None, 3, None, NA_R * GRID_W, NA_WIN_ROWS * GRID_W),
                               lambda g: (g // NA_HEADS, 0, g % NA_HEADS, 0, 0)),
        out_shape=jax.ShapeDtypeStruct((DEPTH, 3, NA_HEADS, NA_R * GRID_W, NA_WIN_ROWS * GRID_W), BF16),
        compiler_params=_cparams("parallel"),
        name="na_bias_table",
    )(rpb_pad)


def _na_attn_kernel(q_ref, k_ref, v_ref, kc_ref, vc_ref, bias_ref, o_ref):
    q_rows = NA_R * GRID_W
    n_keys = NA_WIN_ROWS * GRID_W
    plans = []
    for blk in range(NA_BLOCKS):
        j = pl.program_id(1) * NA_BLOCKS + blk
        lo = jnp.clip(j * NA_R - NA_WIN_H // 2, 0, GRID_ROWS - NA_WIN_ROWS)
        k0 = pl.multiple_of(lo * GRID_W, GRID_W)
        step_type = jnp.where(j == 0, 0, jnp.where(j == NA_STEPS - 1, 2, 1))
        plans.append((k0, step_type))
    units = [(blk, head) for blk in range(NA_BLOCKS) for head in range(NA_HEADS)]

    def scores(unit):
        blk, head = unit
        k0, step_type = plans[blk]
        lanes = slice((head // 2) * LANES, (head // 2 + 1) * LANES)
        q = q_ref[blk * q_rows:(blk + 1) * q_rows, lanes]
        qm = jnp.where(_lane_half_mask(q.shape, head % 2), q, jnp.zeros_like(q))
        s_w = _dot_nt(qm, k_ref[pl.ds(k0, n_keys), lanes]) + bias_ref[step_type, head].astype(F32)
        s_c = _dot_nt(qm, kc_ref[:, lanes].astype(BF16))
        return s_w, s_c

    nxt = scores(units[0])
    outs = []
    for n, (blk, head) in enumerate(units):
        s_w, s_c = nxt
        if n + 1 < len(units):
            nxt = scores(units[n + 1])
        k0, _ = plans[blk]
        lanes = slice((head // 2) * LANES, (head // 2 + 1) * LANES)
        m = jnp.maximum(jnp.max(s_w, axis=-1, keepdims=True), jnp.max(s_c, axis=-1, keepdims=True))
        e_w = jnp.exp2(s_w - m)
        e_c = jnp.exp2(s_c - m)
        inv = 1.0 / (jnp.sum(e_w, axis=-1, keepdims=True) + jnp.sum(e_c, axis=-1, keepdims=True))
        pv = (_dot(e_w.astype(BF16), v_ref[pl.ds(k0, n_keys), lanes])
              + _dot(e_c.astype(BF16), vc_ref[:, lanes].astype(BF16)))
        outs.append(pv * inv)
        if head % 2 == 1:
            o = jnp.where(_lane_half_mask(outs[0].shape, 0), outs[0], outs[1])
            o_ref[blk * q_rows:(blk + 1) * q_rows, lanes] = o.astype(o_ref.dtype)
            outs = []


def _na_attention(p_all, cache_k, cache_v, bias_tab, layer):
    q_rows = NA_BLOCKS * NA_R * GRID_W
    steps = NA_STEPS // NA_BLOCKS
    q_blk0 = N_CTX_TOK // q_rows
    seq_blk0 = N_CTX_TOK // DEC_SEQ
    cache_spec = pl.BlockSpec((None, None, PAST_LEN, NA_WIDTH), lambda b, j: (b, layer, 0, 0))
    return pl.pallas_call(
        _na_attn_kernel,
        grid=(DEC_BATCH, steps),
        in_specs=[
            pl.BlockSpec((q_rows, NA_WIDTH), lambda b, j: (q_blk0 + b * steps + j, COL_NA_Q // NA_WIDTH)),
            pl.BlockSpec((DEC_SEQ, NA_WIDTH), lambda b, j: (seq_blk0 + b, COL_NA_K // NA_WIDTH)),
            pl.BlockSpec((DEC_SEQ, NA_WIDTH), lambda b, j: (seq_blk0 + b, COL_NA_V // NA_WIDTH)),
            cache_spec, cache_spec,
            pl.BlockSpec((None, 3, NA_HEADS, NA_R * GRID_W, NA_WIN_ROWS * GRID_W), lambda b, j: (layer, 0, 0, 0, 0)),
        ],
        out_specs=pl.BlockSpec((q_rows, NA_WIDTH), lambda b, j: (b * steps + j, 0)),
        out_shape=jax.ShapeDtypeStruct((N_LAT_TOK, NA_WIDTH), BF16),
        compiler_params=_cparams("parallel", "arbitrary"),
        name="latent_na_attention",
    )(p_all, p_all, p_all, cache_k, cache_v, bias_tab)


def _diff_attn_kernel(q_ref, k_ref, v_ref, kc_ref, vc_ref, lq1, lk1, lq2, lk2, gs_ref, o_ref,
                      *, lam_init):
    lam = _lambda_value(lq1[...], lk1[...], lq2[...], lk2[...], lam_init)
    k = k_ref[...]
    v = v_ref[...]
    kc = kc_ref[...].astype(BF16)
    vc = vc_ref[...].astype(BF16)
    n_sub = TQ_DIFF // DIFF_SUB

    def scores(i):
        q = q_ref[i * DIFF_SUB:(i + 1) * DIFF_SUB, :]
        out = []
        for half in range(2):
            qm = jnp.where(_lane_half_mask(q.shape, half), q, jnp.zeros_like(q))
            out.append((_dot_nt(qm, k), _dot_nt(qm, kc)))
        return out

    nxt = scores(0)
    for i in range(n_sub):
        cur = nxt
        if i + 1 < n_sub:
            nxt = scores(i + 1)
        probs = []
        for s_l, s_c in cur:
            m = jnp.maximum(jnp.max(s_l, axis=-1, keepdims=True), jnp.max(s_c, axis=-1, keepdims=True))
            e_l = jnp.exp2(s_l - m)
            e_c = jnp.exp2(s_c - m)
            inv = 1.0 / (jnp.sum(e_l, axis=-1, keepdims=True) + jnp.sum(e_c, axis=-1, keepdims=True))
            probs.append((e_l, e_c, inv))
        r = lam * probs[1][2] / probs[0][2]
        a_l = (probs[0][0] - probs[1][0] * r).astype(BF16)
        a_c = (probs[0][1] - probs[1][1] * r).astype(BF16)
        o = (_dot(a_l, v) + _dot(a_c, vc)) * probs[0][2]
        o_ref[i * DIFF_SUB:(i + 1) * DIFF_SUB, :] = _subln(o, gs_ref[...], lam_init).astype(o_ref.dtype)


def _diff_attention(p_all, cache_k, cache_v, lam_params, g_subln, lam_init, layer):
    steps = DEC_SEQ // TQ_DIFF
    q_blk0 = N_CTX_TOK // TQ_DIFF
    seq_blk0 = N_CTX_TOK // DEC_SEQ
    vec = pl.BlockSpec((1, HEAD_DIM), lambda b, h, j: (0, 0))
    cache_spec = pl.BlockSpec((None, None, PAST_LEN, LANES), lambda b, h, j: (b, layer, 0, h))
    return pl.pallas_call(
        functools.partial(_diff_attn_kernel, lam_init=lam_init),
        grid=(DEC_BATCH, DIFF_HEADS, steps),
        in_specs=[
            pl.BlockSpec((TQ_DIFF, LANES), lambda b, h, j: (q_blk0 + b * steps + j, COL_D_Q // LANES + h)),
            pl.BlockSpec((DEC_SEQ, LANES), lambda b, h, j: (seq_blk0 + b, COL_D_K // LANES + h)),
            pl.BlockSpec((DEC_SEQ, LANES), lambda b, h, j: (seq_blk0 + b, COL_D_V // LANES + h)),
            cache_spec, cache_spec, vec, vec, vec, vec,
            pl.BlockSpec((1, 2 * HEAD_DIM), lambda b, h, j: (0, 0)),
        ],
        out_specs=pl.BlockSpec((TQ_DIFF, LANES), lambda b, h, j: (b * steps + j, h)),
        out_shape=jax.ShapeDtypeStruct((N_LAT_TOK, DIFF_WIDTH), BF16),
        compiler_params=_cparams("parallel", "parallel", "arbitrary"),
        name="latent_diff_attention",
    )(p_all, p_all, p_all, cache_k, cache_v, *lam_params, g_subln)


def _attn_residual(i, x_ref, oc_ref, ona_ref, od_ref, ga_ref, w_ref, dst_ref):
    @pl.when(i < CTX_TILES)
    def _():
        dst_ref[...] = x_ref[...] + ga_ref[0] * _dot(oc_ref[...], w_ref[...])

    @pl.when(i >= CTX_TILES)
    def _():
        acc = _dot(ona_ref[...], w_ref[:NA_WIDTH, :]) + _dot(od_ref[...], w_ref[NA_WIDTH:, :])
        dst_ref[...] = x_ref[...] + ga_ref[0] * acc


def _attn_residual_specs():
    lat_idx = lambda i, *_: (jnp.maximum(i - CTX_TILES, 0), 0)
    return [
        pl.BlockSpec((TM, D_MODEL), lambda i, *_: (jnp.minimum(i, CTX_TILES - 1), 0)),
        pl.BlockSpec((TM, NA_WIDTH), lat_idx),
        pl.BlockSpec((TM, DIFF_WIDTH), lat_idx),
        _mod_spec(2),
        pl.BlockSpec((D_MODEL, D_MODEL), lambda i, *_: (0, 0)),
    ]


def _swiglu_tile(hb, wg, wu):
    g = _dot(hb, wg)
    u = _dot(hb, wu)
    return (g * jax.nn.sigmoid(g)) * u


def _ffn_kernel(x_ref, oc_ref, ona_ref, od_ref, ga1_ref, wo_ref, g_ref, sh_ref, sc_ref, ga_ref,
                wg_ref, wu_ref, wd_ref, y_ref, xmid_scr, h_scr, acc_scr):
    i = pl.program_id(0)
    f = pl.program_id(1)

    @pl.when(f == 0)
    def _():
        _attn_residual(i, x_ref, oc_ref, ona_ref, od_ref, ga1_ref, wo_ref, xmid_scr)
        h = _modulated_norm(xmid_scr[...], g_ref[...], sh_ref[0], sc_ref[0])
        h_scr[...] = h.astype(BF16)
        acc_scr[...] = jnp.zeros_like(acc_scr)

    a = _swiglu_tile(h_scr[...], wg_ref[...], wu_ref[...])
    acc_scr[...] += _dot(a.astype(BF16), wd_ref[...])

    @pl.when(f == pl.num_programs(1) - 1)
    def _():
        y_ref[...] = xmid_scr[...] + ga_ref[0] * acc_scr[...]


def _ffn_dense(x, o_ctx, o_na, o_d, w_out, g, mod, wg, wu, wd):
    tf = FF_TILE_DENSE
    return pl.pallas_call(
        _ffn_kernel,
        grid=(N_TILES, D_FF // tf),
        in_specs=[pl.BlockSpec((TM, D_MODEL), lambda i, f: (i, 0))] + _attn_residual_specs() + [
            pl.BlockSpec((1, D_MODEL), lambda i, f: (0, 0)),
            _mod_spec(3), _mod_spec(4), _mod_spec(5),
            pl.BlockSpec((D_MODEL, tf), lambda i, f: (0, f)),
            pl.BlockSpec((D_MODEL, tf), lambda i, f: (0, f)),
            pl.BlockSpec((tf, D_MODEL), lambda i, f: (f, 0)),
        ],
        out_specs=pl.BlockSpec((TM, D_MODEL), lambda i, f: (i, 0)),
        out_shape=jax.ShapeDtypeStruct((N_TOK, D_MODEL), F32),
        scratch_shapes=[pltpu.VMEM((TM, D_MODEL), F32), pltpu.VMEM((TM, D_MODEL), BF16),
                        pltpu.VMEM((TM, D_MODEL), F32)],
        compiler_params=_cparams("parallel", "arbitrary"),
        name="ffn_dense",
    )(x, o_ctx, o_na, o_d, mod, w_out, g, mod, mod, mod, wg, wu, wd)


ROUTE_E1, ROUTE_E2, ROUTE_G1, ROUTE_G2, ROUTE_R1, ROUTE_R2 = range(6)


def _router_kernel(x_ref, oc_ref, ona_ref, od_ref, ga1_ref, wo_ref, g_ref, sh_ref, sc_ref, wr_ref,
                   xmid_ref, h_ref, route_ref, cnt_ref, carry_scr):
    i = pl.program_id(0)

    @pl.when(i == 0)
    def _():
        carry_scr[...] = jnp.zeros_like(carry_scr)

    _attn_residual(i, x_ref, oc_ref, ona_ref, od_ref, ga1_ref, wo_ref, xmid_ref)
    h = _modulated_norm(xmid_ref[...], g_ref[...], sh_ref[0], sc_ref[0])
    for c in range(ROW_CHUNKS):
        h_ref[pl.ds(c, TM, stride=ROW_CHUNKS), :] = h[:, c * LANES:(c + 1) * LANES]
    logits = jnp.dot(h, wr_ref[...], preferred_element_type=F32, precision=lax.Precision.HIGHEST)
    lane = lax.broadcasted_iota(I32, logits.shape, 1)
    logits = jnp.where(lane < N_EXPERTS, logits, MASK_VALUE)
    m1 = jnp.max(logits, axis=-1, keepdims=True)
    i1 = jnp.min(jnp.where(logits == m1, lane, LANES), axis=-1, keepdims=True)
    rest = jnp.where(lane == i1, MASK_VALUE, logits)
    m2 = jnp.max(rest, axis=-1, keepdims=True)
    i2 = jnp.min(jnp.where(rest == m2, lane, LANES), axis=-1, keepdims=True)
    e2 = jnp.exp(m2 - m1)
    g1 = 1.0 / (1.0 + e2)
    g2 = e2 / (1.0 + e2)

    hit1 = lane == i1
    hit2 = lane == i2
    onehot = jnp.where(hit1 | hit2, 1.0, 0.0)
    row = lax.broadcasted_iota(I32, (TM, TM), 0)
    col = lax.broadcasted_iota(I32, (TM, TM), 1)
    lower = jnp.where(row > col, 1.0, 0.0).astype(BF16)
    before = _dot(lower, onehot.astype(BF16)) + carry_scr[...]
    r1 = jnp.sum(jnp.where(hit1, before, 0.0), axis=-1, keepdims=True)
    r2 = jnp.sum(jnp.where(hit2, before, 0.0), axis=-1, keepdims=True)
    carry_scr[...] += jnp.sum(onehot, axis=0, keepdims=True)

    out = jnp.zeros(logits.shape, F32)
    for slot, val in ((ROUTE_E1, i1.astype(F32)), (ROUTE_E2, i2.astype(F32)), (ROUTE_G1, g1),
                      (ROUTE_G2, g2), (ROUTE_R1, r1), (ROUTE_R2, r2)):
        out = jnp.where(lane == slot, val, out)
    route_ref[...] = out

    @pl.when(i == pl.num_programs(0) - 1)
    def _():
        cnt_ref[...] = jnp.broadcast_to(carry_scr[...], cnt_ref.shape)


def _router(x, o_ctx, o_na, o_d, w_out, g, mod, w_router_pad):
    return pl.pallas_call(
        _router_kernel,
        grid=(N_TILES,),
        in_specs=[pl.BlockSpec((TM, D_MODEL), lambda i: (i, 0))] + _attn_residual_specs() + [
            pl.BlockSpec((1, D_MODEL), lambda i: (0, 0)),
            _mod_spec(3), _mod_spec(4),
            pl.BlockSpec((D_MODEL, LANES), lambda i: (0, 0)),
        ],
        out_specs=[pl.BlockSpec((TM, D_MODEL), lambda i: (i, 0)),
                   pl.BlockSpec((TM * ROW_CHUNKS, LANES), lambda i: (i, 0)),
                   pl.BlockSpec((TM, LANES), lambda i: (i, 0)),
                   pl.BlockSpec((8, LANES), lambda i: (0, 0))],
        out_shape=[jax.ShapeDtypeStruct((N_TOK, D_MODEL), F32),
                   jax.ShapeDtypeStruct((N_TOK * ROW_CHUNKS, LANES), F32),
                   jax.ShapeDtypeStruct((N_TOK, LANES), F32),
                   jax.ShapeDtypeStruct((8, LANES), F32)],
        scratch_shapes=[pltpu.VMEM((1, LANES), F32)],
        compiler_params=_cparams("arbitrary"),
        name="moe_router",
    )(x, o_ctx, o_na, o_d, mod, w_out, g, mod, mod, w_router_pad)


def _moe_plan(route, counts):
    cnt = counts[0, :N_EXPERTS].astype(I32)
    tiles = (cnt + TM_MOE - 1) // TM_MOE
    tile_end = jnp.cumsum(tiles)
    group_start = (tile_end - tiles) * TM_MOE
    e = route[:, ROUTE_E1:ROUTE_E2 + 1].astype(I32)
    r = route[:, ROUTE_R1:ROUTE_R2 + 1].astype(I32)
    slot = (group_start[e] + r).T.reshape(-1)
    tok = jnp.tile(jnp.arange(N_TOK, dtype=I32), 2)
    src_tok = jnp.zeros((N_SRC_SLOTS,), I32).at[slot].set(tok)
    n_valid = tile_end[-1:]
    t = jnp.arange(MOE_TILES, dtype=I32)
    tile_expert = jnp.minimum(jnp.sum((t[:, None] >= tile_end[None, :]).astype(I32), axis=1),
                              N_EXPERTS - 1)
    last_expert = tile_expert[jnp.maximum(n_valid[0] - 1, 0)]
    tile_expert = jnp.where(t < n_valid[0], tile_expert, last_expert)
    return tile_expert, n_valid, src_tok, slot


def _moe_kernel(te_ref, nv_ref, src_ref, h_hbm, *refs):
    wg_refs = refs[:MOE_W_SPLIT]
    wu_refs = refs[MOE_W_SPLIT:2 * MOE_W_SPLIT]
    wd_refs = refs[2 * MOE_W_SPLIT:3 * MOE_W_SPLIT]
    y_ref, hrow_scr, hb_scr, acc_scr, sems = refs[3 * MOE_W_SPLIT:]
    t = pl.program_id(0)
    f = pl.program_id(1)
    valid = t < nv_ref[0]
    last_t = pl.num_programs(0) - 1
    last_f = N_FF_MOE - 1

    def row_copy(tile, row, buf):
        tok = src_ref[tile * TM_MOE + row]
        return pltpu.make_async_copy(h_hbm.at[pl.ds(pl.multiple_of(tok * ROW_CHUNKS, ROW_CHUNKS), ROW_CHUNKS), :],
                                     hrow_scr.at[buf, pl.ds(pl.multiple_of(row * ROW_CHUNKS, ROW_CHUNKS), ROW_CHUNKS), :],
                                     sems.at[buf])

    def wait_tile(buf):
        pltpu.make_async_copy(h_hbm.at[pl.ds(0, GATHER_ROWS * ROW_CHUNKS), :],
                              hrow_scr.at[buf, pl.ds(0, GATHER_ROWS * ROW_CHUNKS), :], sems.at[buf]).wait()

    def issue_next_chunk():
        for j in range(GATHER_CHUNK):
            row_copy(t + 1, f * GATHER_CHUNK + j, (t + 1) % 2).start()

    @pl.when((t == 0) & (f == 0))
    def _():
        def issue(j, carry):
            row_copy(0, j, 0).start()
            return carry

        lax.fori_loop(0, GATHER_ROWS, issue, 0, unroll=7)

    @pl.when(f == 0)
    def _():
        wait_tile(t % 2)
        for c in range(ROW_CHUNKS):
            hb_scr[:, c * LANES:(c + 1) * LANES] = hrow_scr[t % 2, pl.ds(c, TM_MOE, stride=ROW_CHUNKS), :].astype(BF16)
        acc_scr[...] = jnp.zeros_like(acc_scr)

    @pl.when(valid)
    def _():
        issue_next_chunk()
        hb = hb_scr[...]
        acc = None
        for wg_ref, wu_ref, wd_ref in zip(wg_refs, wu_refs, wd_refs):
            a = _swiglu_tile(hb, wg_ref[...].astype(BF16), wu_ref[...].astype(BF16))
            d = _dot(a.astype(BF16), wd_ref[...].astype(BF16))
            acc = d if acc is None else acc + d
        acc_scr[...] += acc

    @pl.when(jnp.logical_not(valid))
    def _():
        issue_next_chunk()

    @pl.when(f == last_f)
    def _():
        for c in range(ROW_CHUNKS):
            y_ref[pl.ds(c, TM_MOE, stride=ROW_CHUNKS), :] = acc_scr[:, c * LANES:(c + 1) * LANES]

    @pl.when((t == last_t) & (f == last_f))
    def _():
        wait_tile((t + 1) % 2)


def _moe_experts(tile_expert, n_valid, src_tok, h, wg, wu, wd, layer):
    ts = FF_TILE_MOE // MOE_W_SPLIT
    buf_rows = -(-GATHER_ROWS // 8) * 8

    def f_eff(t, f, nv):
        return jnp.where(t < nv[0], f, N_FF_MOE - 1)

    grid_spec = pltpu.PrefetchScalarGridSpec(
        num_scalar_prefetch=3,
        grid=(MOE_TILES, N_FF_MOE),
        in_specs=[pl.BlockSpec(memory_space=pl.ANY)] + [
            pl.BlockSpec((None, None, D_MODEL, ts),
                         lambda t, f, te, nv, st, s=s: (layer, te[t], 0, MOE_W_SPLIT * f_eff(t, f, nv) + s))
            for _ in range(2) for s in range(MOE_W_SPLIT)
        ] + [
            pl.BlockSpec((None, None, ts, D_MODEL),
                         lambda t, f, te, nv, st, s=s: (layer, te[t], MOE_W_SPLIT * f_eff(t, f, nv) + s, 0))
            for s in range(MOE_W_SPLIT)
        ],
        out_specs=pl.BlockSpec((TM_MOE * ROW_CHUNKS, LANES), lambda t, f, te, nv, st: (t, 0)),
        scratch_shapes=[pltpu.VMEM((2, buf_rows * ROW_CHUNKS, LANES), F32), pltpu.VMEM((TM_MOE, D_MODEL), BF16),
                        pltpu.VMEM((TM_MOE, D_MODEL), F32), pltpu.SemaphoreType.DMA((2,))],
    )
    return pl.pallas_call(
        _moe_kernel,
        grid_spec=grid_spec,
        out_shape=jax.ShapeDtypeStruct((MOE_TILES * TM_MOE * ROW_CHUNKS, LANES), F32),
        compiler_params=_cparams("arbitrary", "arbitrary"),
        name="moe_experts",
    )(tile_expert, n_valid, src_tok, h, *([wg] * MOE_W_SPLIT + [wu] * MOE_W_SPLIT + [wd] * MOE_W_SPLIT))


def _moe_combine_kernel(slot_ref, x_ref, route_ref, ga_ref, *refs, final):
    if final:
        gf_ref, y_hbm, yp_ref, ys_ref, rows_scr, o_ref, sems = refs
    else:
        y_hbm, o_ref, rows_scr, sems = refs
    i = pl.program_id(0)

    def issue_tile(tile, buf):
        for c in range(2):
            base = c * N_TOK + tile * TM

            def issue(j, carry, base=base, c=c):
                s = slot_ref[base + j]
                pltpu.make_async_copy(
                    y_hbm.at[pl.ds(pl.multiple_of(s * ROW_CHUNKS, ROW_CHUNKS), ROW_CHUNKS), :],
                    rows_scr.at[buf, c, pl.ds(pl.multiple_of(j * ROW_CHUNKS, ROW_CHUNKS), ROW_CHUNKS), :],
                    sems.at[buf]).start()
                return carry

            lax.fori_loop(0, TM, issue, 0, unroll=8)

    @pl.when(i == 0)
    def _():
        issue_tile(0, 0)

    @pl.when(i + 1 < pl.num_programs(0))
    def _():
        issue_tile(i + 1, (i + 1) % 2)

    buf = i % 2
    for c in range(2):
        pltpu.make_async_copy(y_hbm.at[pl.ds(0, TM * ROW_CHUNKS), :], rows_scr.at[buf, c], sems.at[buf]).wait()
    route = route_ref[...]
    g1 = route[:, ROUTE_G1:ROUTE_G1 + 1]
    g2 = route[:, ROUTE_G2:ROUTE_G2 + 1]
    for ch in range(ROW_CHUNKS):
        lanes = slice(ch * LANES, (ch + 1) * LANES)
        mix = (g1 * rows_scr[buf, 0, pl.ds(ch, TM, stride=ROW_CHUNKS), :]
               + g2 * rows_scr[buf, 1, pl.ds(ch, TM, stride=ROW_CHUNKS), :])
        o_ref[:, lanes] = x_ref[:, lanes] + ga_ref[0][:, lanes] * mix

    if final:
        xn = o_ref[...]
        out = (xn * lax.rsqrt(jnp.mean(xn * xn, axis=-1, keepdims=True) + EPS)) * gf_ref[...]

        @pl.when(i < CTX_TILES)
        def _():
            yp_ref[...] = out

        @pl.when(i >= CTX_TILES)
        def _():
            ys_ref[...] = out


def _moe_combine(slot, x, route, mod, y, g_final=None):
    final = g_final is not None
    rows = pltpu.VMEM((2, 2, TM * ROW_CHUNKS, LANES), F32)
    tile = pl.BlockSpec((TM, D_MODEL), lambda i, s: (i, 0))
    in_specs = [tile, pl.BlockSpec((TM, LANES), lambda i, s: (i, 0)), _mod_spec(5)]
    args = [slot, x, route, mod]
    if final:
        in_specs.append(pl.BlockSpec((1, D_MODEL), lambda i, s: (0, 0)))
        args.append(g_final)
        out_specs = [pl.BlockSpec((TM, D_MODEL), lambda i, s: (jnp.minimum(i, CTX_TILES - 1), 0)),
                     pl.BlockSpec((TM, D_MODEL), lambda i, s: (jnp.maximum(i - CTX_TILES, 0), 0))]
        out_shape = [jax.ShapeDtypeStruct((N_CTX_TOK, D_MODEL), F32),
                     jax.ShapeDtypeStruct((N_LAT_TOK, D_MODEL), F32)]
        scratch = [rows, pltpu.VMEM((TM, D_MODEL), F32), pltpu.SemaphoreType.DMA((2,))]
    else:
        out_specs = tile
        out_shape = jax.ShapeDtypeStruct((N_TOK, D_MODEL), F32)
        scratch = [rows, pltpu.SemaphoreType.DMA((2,))]
    in_specs.append(pl.BlockSpec(memory_space=pl.ANY))
    args.append(y)
    grid_spec = pltpu.PrefetchScalarGridSpec(
        num_scalar_prefetch=1, grid=(N_TILES,), in_specs=in_specs, out_specs=out_specs, scratch_shapes=scratch)
    return pl.pallas_call(
        functools.partial(_moe_combine_kernel, final=final),
        grid_spec=grid_spec,
        out_shape=out_shape,
        compiler_params=_cparams("arbitrary"),
        name="moe_combine_final" if final else "moe_combine",
    )(*args)


def _final_norm_kernel(x_ref, g_ref, y_ref):
    x = x_ref[...]
    y_ref[...] = (x * lax.rsqrt(jnp.mean(x * x, axis=-1, keepdims=True) + EPS)) * g_ref[...]


def _final_norm(x, g, tile0, n_tok):
    return pl.pallas_call(
        _final_norm_kernel,
        grid=(n_tok // TM,),
        in_specs=[pl.BlockSpec((TM, D_MODEL), lambda i: (tile0 + i, 0)),
                  pl.BlockSpec((1, D_MODEL), lambda i: (0, 0))],
        out_specs=pl.BlockSpec((TM, D_MODEL), lambda i: (i, 0)),
        out_shape=jax.ShapeDtypeStruct((n_tok, D_MODEL), F32),
        compiler_params=_cparams("parallel"),
        name="final_norm",
    )(x, g)


def _rope_tables():
    t = jnp.arange(DEC_SEQ)
    row = (t // GRID_W).astype(F32)
    col = (t % GRID_W).astype(F32)
    half = HEAD_DIM // 2
    inv = 1.0 / (ROPE_THETA ** (jnp.arange(0, half, 2, dtype=F32) / half))
    ar = row[:, None] * inv[None]
    ac = col[:, None] * inv[None]
    ang = jnp.concatenate([ar, ar, ac, ac], axis=-1)
    ang = jnp.concatenate([ang, jnp.zeros((TM, HEAD_DIM), F32)], axis=0)
    cos = jnp.tile(jnp.cos(ang), (1, LANES // HEAD_DIM))
    sin = jnp.tile(jnp.sin(ang), (1, LANES // HEAD_DIM))
    first_half = (jnp.arange(LANES) % 32) < 16
    sin_a = jnp.where(first_half[None, :], -sin, 0.0)
    sin_b = jnp.where(first_half[None, :], 0.0, sin)
    return cos, sin_a, sin_b


def kernel(x_prompt, x_sample, c, cache_na_k, cache_na_v, cache_diff_k, cache_diff_v, c_ctx, w_ada, b_ada, g_mix, w_in, rpb, lam_q1, lam_k1, lam_q2, lam_k2, g_subln, w_out, g_ffn, w_ffn_gate, w_ffn_up, w_ffn_down, w_router, w_moe_gate, w_moe_up, w_moe_down, g_final):
    x = jnp.concatenate([x_prompt.reshape(N_CTX_TOK, D_MODEL), x_sample.reshape(N_LAT_TOK, D_MODEL)])

    cvec = jnp.zeros((MOD_ROWS, D_MODEL), F32).at[0].set(c_ctx).at[1:1 + DEC_BATCH].set(c)
    mod_all = _modulation(cvec, w_ada, b_ada).reshape(DEPTH, MOD_ROWS * 6, 1, D_MODEL)

    col = jnp.arange(IN_WIDTH)
    is_q = (col < COL_NA_K) | ((col >= COL_D_Q) & (col < COL_D_K))
    q_scale = jnp.where(is_q, ATTN_SCALE * LOG2E, 1.0).astype(F32)
    w_in_b = (w_in * q_scale[None, None, :]).astype(BF16)
    w_out_b = w_out.astype(BF16)
    w_fg, w_fu, w_fd = (w.astype(BF16) for w in (w_ffn_gate, w_ffn_up, w_ffn_down))
    w_router_pad = jnp.pad(w_router, ((0, 0), (0, 0), (0, LANES - N_EXPERTS)))

    rope_tabs = _rope_tables()
    bias_tabs = _na_bias_tables(rpb)
    cna_k = cache_na_k.reshape(DEC_BATCH, DEPTH, PAST_LEN, NA_WIDTH)
    cna_v = cache_na_v.reshape(DEC_BATCH, DEPTH, PAST_LEN, NA_WIDTH)
    cd_k = cache_diff_k.reshape(DEC_BATCH, DEPTH, PAST_LEN, DIFF_WIDTH)
    cd_v = cache_diff_v.reshape(DEC_BATCH, DEPTH, PAST_LEN, DIFF_WIDTH)

    kv_layers = []
    g_fin = g_final[None, :]
    final_rows = None
    for l in range(DEPTH):
        lam_init = 0.8 - 0.6 * math.exp(-0.3 * l)
        mod = mod_all[l]
        g_mix_l = g_mix[l][None, :]
        g_ffn_l = g_ffn[l][None, :]
        g_sub_l = g_subln[l][None, :]
        lam_params = tuple(p[l][None, :] for p in (lam_q1, lam_k1, lam_q2, lam_k2))

        p_all, *kv_ctx = _inproj(x, g_mix_l, mod, w_in_b[l], rope_tabs)
        kv_layers.append(kv_ctx)
        o_ctx = _ctx_attention(p_all, lam_params, g_sub_l, lam_init)
        o_na = _na_attention(p_all, cna_k, cna_v, bias_tabs, l)
        o_d = _diff_attention(p_all, cd_k, cd_v, lam_params, g_sub_l, lam_init, l)

        i = l // 2
        if l % 2 == 0:
            x = _ffn_dense(x, o_ctx, o_na, o_d, w_out_b[l], g_ffn_l, mod, w_fg[i], w_fu[i], w_fd[i])
        else:
            x, h, route, counts = _router(x, o_ctx, o_na, o_d, w_out_b[l], g_ffn_l, mod, w_router_pad[i])
            tile_expert, n_valid, src_tok, slot = _moe_plan(route, counts)
            y = _moe_experts(tile_expert, n_valid, src_tok, h, w_moe_gate, w_moe_up, w_moe_down, i)
            if l == DEPTH - 1:
                final_rows = _moe_combine(slot, x, route, mod, y, g_fin)
            else:
                x = _moe_combine(slot, x, route, mod, y)

    if final_rows is None:
        final_rows = (_final_norm(x, g_fin, 0, N_CTX_TOK), _final_norm(x, g_fin, CTX_TILES, N_LAT_TOK))
    y_prompt = final_rows[0].reshape(BATCH, SEQ, D_MODEL)
    y_sample = final_rows[1].reshape(DEC_BATCH, DEC_SEQ, D_MODEL)

    kv = [jnp.stack([layer_kv[k] for layer_kv in kv_layers], axis=1) for k in range(4)]
    new_na_k = kv[0].reshape(BATCH, DEPTH, SEQ, NA_HEADS, HEAD_DIM)
    new_na_v = kv[1].reshape(BATCH, DEPTH, SEQ, NA_HEADS, HEAD_DIM)
    new_diff_k = kv[2].reshape(BATCH, DEPTH, SEQ, DIFF_HEADS, 2, HEAD_DIM)
    new_diff_v = kv[3].reshape(BATCH, DEPTH, SEQ, DIFF_HEADS, 2 * HEAD_DIM)
    return (y_prompt, y_sample, new_na_k, new_na_v, new_diff_k, new_diff_v)
```
